```python
import math
import jax, jax.numpy as jnp
from jax import lax
import numpy as np

D_MODEL = 2048
BATCH = 2
SEQ = 4096
DEPTH = 4
DEC_BATCH = 8
DEC_SEQ = 1
PAST_LEN = 16384
PAGE_SIZE = 128

N_EVEN = (DEPTH + 1) // 2
N_ODD = DEPTH // 2
RMS_EPS = 1e-6
NEG = -1e30
D_FF = 4 * D_MODEL

GDN_KH = 4
GDN_VH = 8
GDN_DK = 128
GDN_DV = 128
GDN_CONV = 4
GDN_CHUNK = 64
GDN_QK = GDN_KH * GDN_DK
GDN_V = GDN_VH * GDN_DV
GDN_CONV_CH = 2 * GDN_QK + GDN_V

NSA_H = 8
NSA_G = 2
NSA_HG = NSA_H // NSA_G
NSA_D = 128
NSA_BLK = 64
NSA_TOPK = 15
NSA_WIN = 512
NSA_QB = 128
NSA_KV = NSA_G * NSA_D

REL_BUCKETS = 32
REL_EXACT = 16
REL_MAX_DIST = 2048

ML_H = 8
ML_DK = 128
ML_DV = 256
ML_CHUNK = 64

EVEN_WIDTHS = (GDN_QK, GDN_QK, GDN_V, GDN_V, GDN_VH, GDN_VH, NSA_H * NSA_D,
               NSA_KV, NSA_KV, NSA_KV, NSA_KV, NSA_KV, NSA_KV, NSA_H * 3)
EVEN_IN = 2 * GDN_QK + 2 * GDN_V + 2 * GDN_VH + NSA_H * NSA_D + 6 * NSA_KV + 3 * NSA_H
EVEN_MIX = GDN_V + NSA_H * NSA_D
ODD_WIDTHS = (ML_H * ML_DK, ML_H * ML_DK, ML_H * ML_DV, ML_H * ML_DV, ML_H, ML_H)
ODD_IN = 2 * ML_H * ML_DK + 2 * ML_H * ML_DV + 2 * ML_H
ODD_MIX = ML_H * ML_DV

kernel_name = 'hybrid_gdn_nsa_mlstm_decode_step'


def _split(x, widths):
    offs = [int(o) for o in np.cumsum(widths)[:-1]]
    return jnp.split(x, offs, axis=-1)


def _rms(x, g):
    xf = x.astype(jnp.float32)
    y = xf * lax.rsqrt(jnp.mean(xf * xf, axis=-1, keepdims=True) + RMS_EPS)
    return (y * g.astype(jnp.float32)).astype(x.dtype)


def _mlp(xn, w1, w2):
    h = jnp.square(jax.nn.relu(xn @ w1))
    return h @ w2


def _causal_conv(xin, w):
    c = xin.shape[-1]
    return lax.conv_general_dilated(xin, w[:, None, :].astype(xin.dtype), window_strides=(1,),
                                    padding='VALID', dimension_numbers=('NWC', 'WIO', 'NWC'),
                                    feature_group_count=c)


def _chunks(t, c, pad, pad_val=0.0):
    t = t.astype(jnp.float32)
    if pad:
        t = jnp.pad(t, [(0, 0), (0, pad)] + [(0, 0)] * (t.ndim - 2), constant_values=pad_val)
    b, tp = t.shape[:2]
    t = t.reshape(b, tp // c, c, *t.shape[2:])
    return jnp.swapaxes(jnp.moveaxis(t, 1, 0), 2, 3)


def _unchunk(o, t_len):
    n, b, h, c, d = o.shape
    return jnp.transpose(o, (1, 0, 3, 2, 4)).reshape(b, n * c, h, d)[:, :t_len]


def _gdn_chunked(q, k, v, g, beta, s0):
    b, t_len, h, dk = q.shape
    dv = v.shape[-1]
    c = min(GDN_CHUNK, t_len)
    pad = (-t_len) % c
    tri_s = jnp.tril(jnp.ones((c, c), bool), -1)
    tri_i = jnp.tril(jnp.ones((c, c), bool), 0)
    eye = jnp.eye(c, dtype=jnp.float32)

    def body(s, inp):
        qc, kc, vc, gc, bc = inp
        gcum = jnp.cumsum(gc, axis=-1)
        diff = gcum[..., :, None] - gcum[..., None, :]
        eg = jnp.exp(gcum)
        lmat = bc[..., :, None] * jnp.einsum('bhid,bhjd->bhij', kc, kc) * jnp.exp(jnp.where(tri_s, diff, -jnp.inf))
        rhs = jnp.concatenate([bc[..., None] * vc, (bc * eg)[..., None] * kc], axis=-1)
        sol = lax.linalg.triangular_solve(eye + lmat, rhs, left_side=True, lower=True)
        u = sol[..., :dv] - jnp.einsum('bhcd,bhde->bhce', sol[..., dv:], s)
        att = jnp.einsum('bhid,bhjd->bhij', qc, kc) * jnp.exp(jnp.where(tri_i, diff, -jnp.inf))
        o = eg[..., None] * jnp.einsum('bhcd,bhde->bhce', qc, s) + jnp.einsum('bhij,bhje->bhie', att, u)
        gl = gcum[..., -1]
        s = jnp.exp(gl)[..., None, None] * s + jnp.einsum('bhcd,bhce->bhde', kc * jnp.exp(gl[..., None] - gcum)[..., None], u)
        return s, o

    s, o = lax.scan(body, s0.astype(jnp.float32),
                    (_chunks(q, c, pad), _chunks(k, c, pad), _chunks(v, c, pad), _chunks(g, c, pad), _chunks(beta, c, pad)))
    return _unchunk(o, t_len).astype(v.dtype), s.astype(s0.dtype)


def _gdn(parts, conv_w, a_log, dt_bias, norm_w, conv_prev, s0):
    q, k, v, z, a, bgate = parts
    b, t_len = q.shape[:2]
    xin = jnp.concatenate([conv_prev.astype(q.dtype), jnp.concatenate([q, k, v], axis=-1)], axis=1)
    c = jax.nn.silu(_causal_conv(xin, conv_w))
    cq, ck, cv = _split(c, (GDN_QK, GDN_QK, GDN_V))

    def l2(t):
        t = t.reshape(b, t_len, GDN_KH, GDN_DK).astype(jnp.float32)
        return t * lax.rsqrt(jnp.sum(t * t, axis=-1, keepdims=True) + 1e-6)

    rep = GDN_VH // GDN_KH
    qh = jnp.repeat(l2(cq) * (GDN_DK ** -0.5), rep, axis=2)
    kh = jnp.repeat(l2(ck), rep, axis=2)
    vh = cv.reshape(b, t_len, GDN_VH, GDN_DV)
    beta = jax.nn.sigmoid(bgate.astype(jnp.float32))
    g = -jnp.exp(a_log.astype(jnp.float32)) * jax.nn.softplus(a.astype(jnp.float32) + dt_bias.astype(jnp.float32))
    o, s = _gdn_chunked(qh, kh, vh, g, beta, s0)
    o = _rms(o, norm_w) * jax.nn.silu(z.reshape(b, t_len, GDN_VH, GDN_DV))
    return o.reshape(b, t_len, GDN_V), xin[:, -(GDN_CONV - 1):], s


def _bucket(dist):
    n = jnp.maximum(dist, 0)
    nf = jnp.maximum(n, REL_EXACT).astype(jnp.float32)
    large = REL_EXACT + (jnp.log(nf / REL_EXACT) / math.log(REL_MAX_DIST / REL_EXACT)
                         * (REL_BUCKETS - REL_EXACT)).astype(jnp.int32)
    large = jnp.minimum(large, REL_BUCKETS - 1)
    return jnp.where(n < REL_EXACT, n, large)


def _tok_bias(dist, tab):
    t_len, n = dist.shape
    bias = tab.astype(jnp.float32)[_bucket(dist)]
    return bias.reshape(t_len, n, NSA_G, NSA_HG).transpose(0, 2, 3, 1)


def _compress(rows, w):
    b, t_len = rows.shape[:2]
    r = rows.reshape(b, t_len // NSA_BLK, NSA_BLK, NSA_G, NSA_D)
    return jnp.einsum('bnlgd,lgd->bngd', r, w)


def _nsa_core(q, qpos, kc_b, vc_b, fetch, kw, vw, kwpos, tab):
    b, t_len = q.shape[:2]
    f32 = jnp.float32
    nb = kc_b.shape[1]
    blk = jnp.arange(nb, dtype=jnp.int32)
    dist_c = qpos[:, None] - (blk * NSA_BLK + NSA_BLK - 1)[None, :]
    ok_c = (dist_c >= 0)[:, None, None, :]
    lg_c = jnp.einsum('btghd,bngd->btghn', q, kc_b).astype(f32) + _tok_bias(dist_c, tab)
    p_c = jax.nn.softmax(jnp.where(ok_c, lg_c, NEG), axis=-1) * ok_c
    o_c = jnp.einsum('btghn,bngd->btghd', p_c.astype(vc_b.dtype), vc_b)
    cur = qpos // NSA_BLK
    cand = (blk[None, :] < cur[:, None])[None, :, None, :]
    score = jnp.where(cand, p_c.sum(axis=3), -1.0)
    _, idx = lax.top_k(score, min(NSA_TOPK, nb))
    cur_b = jnp.broadcast_to(cur[None, :, None, None], (b, t_len, NSA_G, 1))
    idx_all = jnp.concatenate([idx, cur_b], axis=-1)
    ok_all = jnp.concatenate([idx < cur[None, :, None, None], jnp.ones((b, t_len, NSA_G, 1), bool)], axis=-1)
    pos5 = idx_all[..., None] * NSA_BLK + jnp.arange(NSA_BLK, dtype=jnp.int32)
    ok5 = ok_all[..., None] & (pos5 <= qpos[None, :, None, None, None])
    pos = pos5.reshape(b, t_len, NSA_G, -1)
    ok_s = ok5.reshape(b, t_len, NSA_G, 1, -1)
    k_s, v_s = fetch(pos)
    tab3 = tab.astype(f32).reshape(REL_BUCKETS, NSA_G, NSA_HG)
    bias_s = jnp.moveaxis(tab3[_bucket(qpos[None, :, None, None] - pos), jnp.arange(NSA_G)[:, None]], -1, 3)
    lg_s = jnp.einsum('btghd,btgnd->btghn', q, k_s).astype(f32) + bias_s
    p_s = jax.nn.softmax(jnp.where(ok_s, lg_s, NEG), axis=-1)
    o_s = jnp.einsum('btghn,btgnd->btghd', p_s.astype(v_s.dtype), v_s)
    dist_w = qpos[:, None] - kwpos[None, :]
    ok_w = ((dist_w >= 0) & (dist_w < NSA_WIN) & (kwpos >= 0)[None, :])[:, None, None, :]
    lg_w = jnp.einsum('btghd,bngd->btghn', q, kw).astype(f32) + _tok_bias(dist_w, tab)
    p_w = jax.nn.softmax(jnp.where(ok_w, lg_w, NEG), axis=-1)
    o_w = jnp.einsum('btghn,bngd->btghd', p_w.astype(vw.dtype), vw)
    return jnp.stack([o_c, o_s, o_w], axis=-2)


def _over_query_blocks(fn, q, qpos):
    b, t_len = q.shape[:2]
    if t_len <= NSA_QB:
        return fn(q, qpos)
    nblk = -(-t_len // NSA_QB)
    pad = nblk * NSA_QB - t_len
    if pad:
        q = jnp.pad(q, [(0, 0), (0, pad)] + [(0, 0)] * (q.ndim - 2))
        qpos = jnp.concatenate([qpos, jnp.full((pad,), qpos[-1], qpos.dtype)])
    qb = jnp.swapaxes(q.reshape(b, nblk, NSA_QB, *q.shape[2:]), 0, 1)
    pb = qpos.reshape(nblk, NSA_QB)
    out = lax.map(lambda a: fn(a[0], a[1]), (qb, pb))
    out = jnp.swapaxes(out, 0, 1)
    return out.reshape(b, nblk * NSA_QB, *out.shape[3:])[:, :t_len]


def _nsa_combine(o3, ng):
    b, t_len = ng.shape[:2]
    gates = jax.nn.sigmoid(ng.astype(jnp.float32)).reshape(b, t_len, NSA_G, NSA_HG, 3, 1)
    return (o3.astype(jnp.float32) * gates).sum(axis=-2).reshape(b, t_len, NSA_H * NSA_D).astype(o3.dtype)


def _nsa_prompt(q, kc, vc, ks, vs, kw, vw, w_ck, w_cv, g_kc, tab):
    b, t_len = q.shape[:2]
    kc_b = _rms(_compress(kc, w_ck), g_kc)
    vc_b = _compress(vc, w_cv)
    padw = ((0, 0), (NSA_WIN, 0), (0, 0), (0, 0))
    kw_pad = jnp.pad(kw, padw)
    vw_pad = jnp.pad(vw, padw)
    bi = jnp.arange(b)[:, None, None, None]
    gi = jnp.arange(NSA_G)[None, None, :, None]

    def fetch(pos):
        return ks[bi, pos, gi], vs[bi, pos, gi]

    def block(qb, pb):
        start = pb[0]
        n = NSA_WIN + qb.shape[1]
        kwb = lax.dynamic_slice_in_dim(kw_pad, start, n, axis=1)
        vwb = lax.dynamic_slice_in_dim(vw_pad, start, n, axis=1)
        kwpos = start - NSA_WIN + jnp.arange(n, dtype=jnp.int32)
        return _nsa_core(qb, pb, kc_b, vc_b, fetch, kwb, vwb, kwpos, tab)

    return _over_query_blocks(block, q, jnp.arange(t_len, dtype=jnp.int32))


def _nsa_sample(q, kc, vc, ks, vs, kw, vw, pool_kc, pool_vc, pool_ks, pool_vs, buf_kw, buf_vw,
                page_table, w_ck, w_cv, g_kc, tab):
    db, ds = q.shape[:2]

    def full_rows(pool, new):
        past = pool[page_table].reshape(db, PAST_LEN, NSA_G, NSA_D)
        r = jnp.concatenate([past, new], axis=1)
        pad = (-r.shape[1]) % NSA_BLK
        return jnp.pad(r, ((0, 0), (0, pad), (0, 0), (0, 0)))

    kc_b = _rms(_compress(full_rows(pool_kc, kc), w_ck), g_kc)
    vc_b = _compress(full_rows(pool_vc, vc), w_cv)
    bi = jnp.arange(db)[:, None, None, None]
    gi = jnp.arange(NSA_G)[None, None, :, None]

    def fetch(pos):
        in_past = (pos < PAST_LEN)[..., None]
        pp = jnp.minimum(pos, PAST_LEN - 1)
        phys = page_table[bi, pp // PAGE_SIZE]
        off = pp % PAGE_SIZE
        pn = jnp.clip(pos - PAST_LEN, 0, ds - 1)
        k = jnp.where(in_past, pool_ks[phys, off, gi], ks[bi, pn, gi])
        v = jnp.where(in_past, pool_vs[phys, off, gi], vs[bi, pn, gi])
        return k, v

    wb = buf_kw.shape[1]
    kw_all = jnp.concatenate([buf_kw, kw], axis=1)
    vw_all = jnp.concatenate([buf_vw, vw], axis=1)
    kwpos = PAST_LEN - wb + jnp.arange(wb + ds, dtype=jnp.int32)
    qpos = PAST_LEN + jnp.arange(ds, dtype=jnp.int32)
    o3 = _over_query_blocks(lambda qb, pb: _nsa_core(qb, pb, kc_b, vc_b, fetch, kw_all, vw_all, kwpos, tab), q, qpos)
    return o3, kw_all[:, -wb:], vw_all[:, -wb:]


def _even_inputs(xn, w_in, qk_gain):
    b, t_len, _ = xn.shape
    (gq, gk, gv, gz, ga, gb, nq, kc, vc, ks, vs, kw, vw, ng) = _split(xn @ w_in, EVEN_WIDTHS)
    hd = lambda t: t.reshape(b, t_len, NSA_G, NSA_D)
    q = _rms(nq.reshape(b, t_len, NSA_G, NSA_HG, NSA_D), qk_gain[0]) * (NSA_D ** -0.5)
    ks = _rms(hd(ks), qk_gain[2])
    kw = _rms(hd(kw), qk_gain[3])
    return (gq, gk, gv, gz, ga, gb), (q, hd(kc), hd(vc), ks, hd(vs), kw, hd(vw), ng)


def _even_prompt(xn, w_in, w_out, conv_w, a_log, dt_bias, gnorm, qk_gain, w_ck, w_cv, tab):
    b, t_len, _ = xn.shape
    gdn_parts, (q, kc, vc, ks, vs, kw, vw, ng) = _even_inputs(xn, w_in, qk_gain)
    conv0 = jnp.zeros((b, GDN_CONV - 1, GDN_CONV_CH), xn.dtype)
    s0 = jnp.zeros((b, GDN_VH, GDN_DK, GDN_DV), xn.dtype)
    ya, conv_new, s = _gdn(gdn_parts, conv_w, a_log, dt_bias, gnorm, conv0, s0)
    yb = _nsa_combine(_nsa_prompt(q, kc, vc, ks, vs, kw, vw, w_ck, w_cv, qk_gain[1], tab), ng)
    y = jnp.concatenate([ya, yb], axis=-1) @ w_out
    nw = min(NSA_WIN, t_len)
    return y, (kc, vc, ks, vs, kw[:, -nw:], vw[:, -nw:], conv_new, s)


def _even_sample(xn, w_in, w_out, conv_w, a_log, dt_bias, gnorm, qk_gain, w_ck, w_cv, tab,
                 pool_kc, pool_vc, pool_ks, pool_vs, buf_kw, buf_vw, conv_st, s_st, page_table):
    gdn_parts, (q, kc, vc, ks, vs, kw, vw, ng) = _even_inputs(xn, w_in, qk_gain)
    ya, conv_new, s = _gdn(gdn_parts, conv_w, a_log, dt_bias, gnorm, conv_st, s_st)
    o3, kw_buf, vw_buf = _nsa_sample(q, kc, vc, ks, vs, kw, vw, pool_kc, pool_vc, pool_ks, pool_vs,
                                     buf_kw, buf_vw, page_table, w_ck, w_cv, qk_gain[1], tab)
    y = jnp.concatenate([ya, _nsa_combine(o3, ng)], axis=-1) @ w_out
    return y, (kc, vc, ks, vs, kw_buf, vw_buf, conv_new, s)


def _mlstm_chunked(q, k, v, ig, lf, c0, n0, m0):
    b, t_len, h, _ = q.shape
    c = min(ML_CHUNK, t_len)
    pad = (-t_len) % c
    incl = jnp.tril(jnp.ones((c, c), bool), 0)

    def body(carry, inp):
        cm, nv, m = carry
        qc, kc, vc, ic, fc = inp
        fcum = jnp.cumsum(fc, axis=-1)
        a = ic - fcum
        mt = fcum + jnp.maximum(m[..., None], lax.cummax(a, axis=2))
        dmat = jnp.exp(jnp.where(incl, a[..., None, :] + (fcum - mt)[..., :, None], -jnp.inf))
        dec0 = jnp.exp(fcum + m[..., None] - mt)
        s = jnp.einsum('bhid,bhjd->bhij', qc, kc) * dmat
        num = dec0[..., None] * jnp.einsum('bhid,bhde->bhie', qc, cm) + jnp.einsum('bhij,bhje->bhie', s, vc)
        den = dec0 * jnp.einsum('bhid,bhd->bhi', qc, nv) + s.sum(axis=-1)
        hc = num / jnp.maximum(jnp.abs(den), jnp.exp(-mt))[..., None]
        m_end = mt[..., -1]
        w = jnp.exp(a + (fcum[..., -1] - m_end)[..., None])
        dc = jnp.exp(fcum[..., -1] + m - m_end)
        cm = dc[..., None, None] * cm + jnp.einsum('bhj,bhjd,bhje->bhde', w, kc, vc)
        nv = dc[..., None] * nv + jnp.einsum('bhj,bhjd->bhd', w, kc)
        return (cm, nv, m_end), hc

    f32 = jnp.float32
    (cm, nv, m), hs = lax.scan(body, (c0.astype(f32), n0.astype(f32), m0.astype(f32)),
                               (_chunks(q, c, pad), _chunks(k, c, pad), _chunks(v, c, pad),
                                _chunks(ig, c, pad, NEG), _chunks(lf, c, pad)))
    return _unchunk(hs, t_len).astype(v.dtype), cm.astype(c0.dtype), nv.astype(n0.dtype), m.astype(m0.dtype)


def _odd(xn, w_in, w_out, b_if, mnorm, c0, n0, m0):
    b, t_len, _ = xn.shape
    q, k, v, o, ig, fg = _split(xn @ w_in, ODD_WIDTHS)
    q = q.reshape(b, t_len, ML_H, ML_DK) * (ML_DK ** -0.5)
    k = k.reshape(b, t_len, ML_H, ML_DK)
    v = v.reshape(b, t_len, ML_H, ML_DV)
    bf = b_if.astype(jnp.float32)
    ig = ig.astype(jnp.float32) + bf[:ML_H]
    lf = jax.nn.log_sigmoid(fg.astype(jnp.float32) + bf[ML_H:])
    hc, cm, nv, m = _mlstm_chunked(q, k, v, ig, lf, c0, n0, m0)
    hc = _rms(hc, mnorm.reshape(ML_H, ML_DV)).reshape(b, t_len, ODD_MIX)
    return (hc * jax.nn.sigmoid(o)) @ w_out, (cm, nv, m)


def setup_inputs(seed: int = 0) -> dict:
    key = jax.random.key(seed)
    keys = iter(jax.random.split(key, 64))
    f32 = jnp.float32

    def nrm(shape, scale):
        return scale * jax.random.normal(next(keys), shape, f32)

    n_pages = PAST_LEN // PAGE_SIZE
    n_phys = (DEC_BATCH * n_pages * 5) // 4
    wb = min(NSA_WIN, PAST_LEN)
    page_table = jax.random.permutation(next(keys), n_phys)[:DEC_BATCH * n_pages]
    page_table = page_table.reshape(DEC_BATCH, n_pages).astype(jnp.int32)
    pool = (N_EVEN, n_phys, PAGE_SIZE, NSA_G, NSA_D)
    buf = (N_EVEN, DEC_BATCH, wb, NSA_G, NSA_D)
    dt = jnp.exp(jax.random.uniform(next(keys), (N_EVEN, GDN_VH), f32, math.log(1e-3), math.log(1e-1)))
    return {
        'x_prompt': nrm((BATCH, SEQ, D_MODEL), 1.0),
        'x_sample': nrm((DEC_BATCH, DEC_SEQ, D_MODEL), 1.0),
        'cache_kc': nrm(pool, 1.0),
        'cache_vc': nrm(pool, 1.0),
        'cache_ks': nrm(pool, 1.0),
        'cache_vs': nrm(pool, 1.0),
        'cache_kw': nrm(buf, 1.0),
        'cache_vw': nrm(buf, 1.0),
        'state_gdn_conv': nrm((N_EVEN, DEC_BATCH, GDN_CONV - 1, GDN_CONV_CH), 1.0),
        'state_gdn_s': nrm((N_EVEN, DEC_BATCH, GDN_VH, GDN_DK, GDN_DV), 0.5),
        'state_ml_c': nrm((N_ODD, DEC_BATCH, ML_H, ML_DK, ML_DV), 0.5),
        'state_ml_n': nrm((N_ODD, DEC_BATCH, ML_H, ML_DK), 0.5),
        'state_ml_m': nrm((N_ODD, DEC_BATCH, ML_H), 1.0),
        'page_table': page_table,
        'rel_bias': nrm((REL_BUCKETS, NSA_H), 0.5),
        'norm_mix': 1.0 + nrm((DEPTH, D_MODEL), 0.01),
        'norm_mlp': 1.0 + nrm((DEPTH, D_MODEL), 0.01),
        'w_in_even': nrm((N_EVEN, D_MODEL, EVEN_IN), D_MODEL ** -0.5),
        'w_out_even': nrm((N_EVEN, EVEN_MIX, D_MODEL), EVEN_MIX ** -0.5),
        'gdn_conv_w': nrm((N_EVEN, GDN_CONV, GDN_CONV_CH), GDN_CONV ** -0.5),
        'gdn_a_log': jnp.log(jax.random.uniform(next(keys), (N_EVEN, GDN_VH), f32, 1.0, 16.0)),
        'gdn_dt_bias': dt + jnp.log(-jnp.expm1(-dt)),
        'gdn_norm_w': 1.0 + nrm((N_EVEN, GDN_DV), 0.01),
        'nsa_qk_gain': 1.0 + nrm((N_EVEN, 4, NSA_D), 0.01),
        'nsa_w_ck': (1.0 + nrm((N_EVEN, NSA_BLK, NSA_G, NSA_D), 0.1)) / NSA_BLK,
        'nsa_w_cv': (1.0 + nrm((N_EVEN, NSA_BLK, NSA_G, NSA_D), 0.1)) / NSA_BLK,
        'w_in_odd': nrm((N_ODD, D_MODEL, ODD_IN), D_MODEL ** -0.5),
        'w_out_odd': nrm((N_ODD, ODD_MIX, D_MODEL), ODD_MIX ** -0.5),
        'ml_b_if': jnp.concatenate([nrm((N_ODD, ML_H), 0.1), 3.0 + nrm((N_ODD, ML_H), 0.5)], axis=-1),
        'ml_norm_w': 1.0 + nrm((N_ODD, ML_H * ML_DV), 0.01),
        'w_ff1': nrm((DEPTH, D_MODEL, D_FF), D_MODEL ** -0.5),
        'w_ff2': nrm((DEPTH, D_FF, D_MODEL), D_FF ** -0.5),
    }


def reference(x_prompt, x_sample, cache_kc, cache_vc, cache_ks, cache_vs, cache_kw, cache_vw,
              state_gdn_conv, state_gdn_s, state_ml_c, state_ml_n, state_ml_m, page_table,
              rel_bias, norm_mix, norm_mlp, w_in_even, w_out_even, gdn_conv_w, gdn_a_log,
              gdn_dt_bias, gdn_norm_w, nsa_qk_gain, nsa_w_ck, nsa_w_cv, w_in_odd, w_out_odd,
              ml_b_if, ml_norm_w, w_ff1, w_ff2):
    hp, hs = x_prompt, x_sample
    ev_p, ev_s, od_p, od_s = [], [], [], []
    for l in range(DEPTH):
        j = l // 2
        xp = _rms(hp, norm_mix[l])
        xs = _rms(hs, norm_mix[l])
        if l % 2 == 0:
            yp, stp = _even_prompt(xp, w_in_even[j], w_out_even[j], gdn_conv_w[j], gdn_a_log[j],
                                   gdn_dt_bias[j], gdn_norm_w[j], nsa_qk_gain[j], nsa_w_ck[j],
                                   nsa_w_cv[j], rel_bias)
            ys, sts = _even_sample(xs, w_in_even[j], w_out_even[j], gdn_conv_w[j], gdn_a_log[j],
                                   gdn_dt_bias[j], gdn_norm_w[j], nsa_qk_gain[j], nsa_w_ck[j],
                                   nsa_w_cv[j], rel_bias, cache_kc[j], cache_vc[j], cache_ks[j],
                                   cache_vs[j], cache_kw[j], cache_vw[j], state_gdn_conv[j],
                                   state_gdn_s[j], page_table)
            ev_p.append(stp)
            ev_s.append(sts)
        else:
            bp = hp.shape[0]
            c0 = jnp.zeros((bp, ML_H, ML_DK, ML_DV), hp.dtype)
            n0 = jnp.zeros((bp, ML_H, ML_DK), hp.dtype)
            m0 = jnp.zeros((bp, ML_H), hp.dtype)
            yp, stp = _odd(xp, w_in_odd[j], w_out_odd[j], ml_b_if[j], ml_norm_w[j], c0, n0, m0)
            ys, sts = _odd(xs, w_in_odd[j], w_out_odd[j], ml_b_if[j], ml_norm_w[j],
                           state_ml_c[j], state_ml_n[j], state_ml_m[j])
            od_p.append(stp)
            od_s.append(sts)
        hp = hp + yp
        hs = hs + ys
        hp = hp + _mlp(_rms(hp, norm_mlp[l]), w_ff1[l], w_ff2[l])
        hs = hs + _mlp(_rms(hs, norm_mlp[l]), w_ff1[l], w_ff2[l])

    kc_p, vc_p, ks_p, vs_p, kw_p, vw_p, conv_p, gdn_p = [jnp.stack(a) for a in zip(*ev_p)]
    kc_s, vc_s, ks_s, vs_s, kw_s, vw_s, conv_s, gdn_s = [jnp.stack(a) for a in zip(*ev_s)]
    mlc_p, mln_p, mlm_p = [jnp.stack(a) for a in zip(*od_p)]
    mlc_s, mln_s, mlm_s = [jnp.stack(a) for a in zip(*od_s)]
    return (hp, hs, kc_p, kc_s, vc_p, vc_s, ks_p, ks_s, vs_p, vs_s, kw_p, kw_s, vw_p, vw_s,
            conv_p, conv_s, gdn_p, gdn_s, mlc_p, mlc_s, mln_p, mln_s, mlm_p, mlm_s)
```

```python
import functools
import math

import jax
import jax.numpy as jnp
import numpy as np
from jax import lax
from jax.experimental import pallas as pl
from jax.experimental.pallas import tpu as pltpu

D_MODEL = 2048
BATCH = 2
SEQ = 4096
DEPTH = 4
DEC_BATCH = 8
DEC_SEQ = 1
PAST_LEN = 16384
PAGE_SIZE = 128
RMS_EPS = 1e-6
NEG = -1e30
D_FF = 4 * D_MODEL

GDN_KH = 4
GDN_VH = 8
GDN_DK = 128
GDN_DV = 128
GDN_CONV = 4
GDN_CHUNK = 64
GDN_QK = GDN_KH * GDN_DK
GDN_V = GDN_VH * GDN_DV
GDN_CONV_CH = 2 * GDN_QK + GDN_V

NSA_H = 8
NSA_G = 2
NSA_HG = NSA_H // NSA_G
NSA_D = 128
NSA_BLK = 64
NSA_TOPK = 15
NSA_WIN = 512
NSA_QB = 128
NSA_KV = NSA_G * NSA_D

REL_BUCKETS = 32
REL_EXACT = 16
REL_MAX_DIST = 2048

ML_H = 8
ML_DK = 128
ML_DV = 256
ML_CHUNK = 64

EVEN_WIDTHS = (GDN_QK, GDN_QK, GDN_V, GDN_V, GDN_VH, GDN_VH, NSA_H * NSA_D,
               NSA_KV, NSA_KV, NSA_KV, NSA_KV, NSA_KV, NSA_KV, NSA_H * 3)
ODD_WIDTHS = (ML_H * ML_DK, ML_H * ML_DK, ML_H * ML_DV, ML_H * ML_DV, ML_H, ML_H)
ODD_MIX = ML_H * ML_DV

VMEM_LIMIT_BYTES = 56 * 1024 * 1024


def _mm_kernel(x_ref, w_ref, *rest, nk, act, has_res):
    if has_res:
        r_ref, o_ref, acc_ref = rest
    else:
        o_ref, acc_ref = rest
        r_ref = None
    k = pl.program_id(2)

    @pl.when(k == 0)
    def _():
        acc_ref[...] = jnp.zeros_like(acc_ref)

    acc_ref[...] += jnp.dot(x_ref[...], w_ref[...], preferred_element_type=jnp.float32)

    @pl.when(k == nk - 1)
    def _():
        y = acc_ref[...]
        if act == 'relu2':
            y = jnp.square(jnp.maximum(y, 0.0))
        if has_res:
            y = y + r_ref[...]
        o_ref[...] = y.astype(o_ref.dtype)


def _pick(n, prefs):
    for p in prefs:
        if n % p == 0:
            return p
    return n


def _matmul(x, w, *, act=None, res=None, out_dtype=jnp.float32):
    m, kdim = x.shape
    _, n = w.shape
    tm = _pick(m, (512, 256, 128))
    tn = _pick(n, (512, 896, 384, 256, 128))
    tk = _pick(kdim, (2048, 1024, 512))
    nk = kdim // tk
    in_specs = [pl.BlockSpec((tm, tk), lambda i, j, k: (i, k)),
                pl.BlockSpec((tk, tn), lambda i, j, k: (k, j))]
    args = [x, w]
    if res is not None:
        in_specs.append(pl.BlockSpec((tm, tn), lambda i, j, k: (i, j)))
        args.append(res)
    return pl.pallas_call(
        functools.partial(_mm_kernel, nk=nk, act=act, has_res=res is not None),
        grid=(m // tm, n // tn, nk),
        in_specs=in_specs,
        out_specs=pl.BlockSpec((tm, tn), lambda i, j, k: (i, j)),
        out_shape=jax.ShapeDtypeStruct((m, n), out_dtype),
        scratch_shapes=[pltpu.VMEM((tm, tn), jnp.float32)],
        compiler_params=pltpu.CompilerParams(
            dimension_semantics=("parallel", "parallel", "arbitrary"),
            vmem_limit_bytes=VMEM_LIMIT_BYTES),
    )(*args)


def _pmm(x, w):
    lead = x.shape[:-1]
    x2 = x.reshape(-1, x.shape[-1]).astype(jnp.bfloat16)
    n = w.shape[1]
    npad = (-n) % 128
    wb = w.astype(jnp.bfloat16)
    if npad:
        wb = jnp.pad(wb, ((0, 0), (0, npad)))
    y = _matmul(x2, wb)
    if npad:
        y = y[:, :n]
    return y.reshape(*lead, n)


def _split(x, widths):
    offs = [int(o) for o in np.cumsum(widths)[:-1]]
    return jnp.split(x, offs, axis=-1)


def _rms(x, g):
    xf = x.astype(jnp.float32)
    y = xf * lax.rsqrt(jnp.mean(xf * xf, axis=-1, keepdims=True) + RMS_EPS)
    return (y * g.astype(jnp.float32)).astype(x.dtype)


def _mlp(xn, w1, w2):
    h = jnp.square(jax.nn.relu(_pmm(xn, w1)))
    return _pmm(h, w2)


def _causal_conv(xin, w):
    c = xin.shape[-1]
    return lax.conv_general_dilated(xin, w[:, None, :].astype(xin.dtype), window_strides=(1,),
                                    padding='VALID', dimension_numbers=('NWC', 'WIO', 'NWC'),
                                    feature_group_count=c)


def _chunks(t, c, pad, pad_val=0.0):
    t = t.astype(jnp.float32)
    if pad:
        t = jnp.pad(t, [(0, 0), (0, pad)] + [(0, 0)] * (t.ndim - 2), constant_values=pad_val)
    b, tp = t.shape[:2]
    t = t.reshape(b, tp // c, c, *t.shape[2:])
    return jnp.swapaxes(jnp.moveaxis(t, 1, 0), 2, 3)


def _unchunk(o, t_len):
    n, b, h, c, d = o.shape
    return jnp.transpose(o, (1, 0, 3, 2, 4)).reshape(b, n * c, h, d)[:, :t_len]


def _gdn_chunked(q, k, v, g, beta, s0):
    b, t_len, h, dk = q.shape
    dv = v.shape[-1]
    c = min(GDN_CHUNK, t_len)
    pad = (-t_len) % c
    tri_s = jnp.tril(jnp.ones((c, c), bool), -1)
    tri_i = jnp.tril(jnp.ones((c, c), bool), 0)
    eye = jnp.eye(c, dtype=jnp.float32)

    def body(s, inp):
        qc, kc, vc, gc, bc = inp
        gcum = jnp.cumsum(gc, axis=-1)
        diff = gcum[..., :, None] - gcum[..., None, :]
        eg = jnp.exp(gcum)
        lmat = bc[..., :, None] * jnp.einsum('bhid,bhjd->bhij', kc, kc) * jnp.exp(jnp.where(tri_s, diff, -jnp.inf))
        rhs = jnp.concatenate([bc[..., None] * vc, (bc * eg)[..., None] * kc], axis=-1)
        sol = lax.linalg.triangular_solve(eye + lmat, rhs, left_side=True, lower=True)
        u = sol[..., :dv] - jnp.einsum('bhcd,bhde->bhce', sol[..., dv:], s)
        att = jnp.einsum('bhid,bhjd->bhij', qc, kc) * jnp.exp(jnp.where(tri_i, diff, -jnp.inf))
        o = eg[..., None] * jnp.einsum('bhcd,bhde->bhce', qc, s) + jnp.einsum('bhij,bhje->bhie', att, u)
        gl = gcum[..., -1]
        s = jnp.exp(gl)[..., None, None] * s + jnp.einsum('bhcd,bhce->bhde', kc * jnp.exp(gl[..., None] - gcum)[..., None], u)
        return s, o

    s, o = lax.scan(body, s0.astype(jnp.float32),
                    (_chunks(q, c, pad), _chunks(k, c, pad), _chunks(v, c, pad), _chunks(g, c, pad), _chunks(beta, c, pad)))
    return _unchunk(o, t_len).astype(v.dtype), s.astype(s0.dtype)


def _gdn(parts, conv_w, a_log, dt_bias, norm_w, conv_prev, s0):
    q, k, v, z, a, bgate = parts
    b, t_len = q.shape[:2]
    xin = jnp.concatenate([conv_prev.astype(q.dtype), jnp.concatenate([q, k, v], axis=-1)], axis=1)
    c = jax.nn.silu(_causal_conv(xin, conv_w))
    cq, ck, cv = _split(c, (GDN_QK, GDN_QK, GDN_V))

    def l2(t):
        t = t.reshape(b, t_len, GDN_KH, GDN_DK).astype(jnp.float32)
        return t * lax.rsqrt(jnp.sum(t * t, axis=-1, keepdims=True) + 1e-6)

    rep = GDN_VH // GDN_KH
    qh = jnp.repeat(l2(cq) * (GDN_DK ** -0.5), rep, axis=2)
    kh = jnp.repeat(l2(ck), rep, axis=2)
    vh = cv.reshape(b, t_len, GDN_VH, GDN_DV)
    beta = jax.nn.sigmoid(bgate.astype(jnp.float32))
    g = -jnp.exp(a_log.astype(jnp.float32)) * jax.nn.softplus(a.astype(jnp.float32) + dt_bias.astype(jnp.float32))
    o, s = _gdn_chunked(qh, kh, vh, g, beta, s0)
    o = _rms(o, norm_w) * jax.nn.silu(z.reshape(b, t_len, GDN_VH, GDN_DV))
    return o.reshape(b, t_len, GDN_V), xin[:, -(GDN_CONV - 1):], s


def _bucket(dist):
    n = jnp.maximum(dist, 0)
    nf = jnp.maximum(n, REL_EXACT).astype(jnp.float32)
    large = REL_EXACT + (jnp.log(nf / REL_EXACT) / math.log(REL_MAX_DIST / REL_EXACT)
                         * (REL_BUCKETS - REL_EXACT)).astype(jnp.int32)
    large = jnp.minimum(large, REL_BUCKETS - 1)
    return jnp.where(n < REL_EXACT, n, large)


def _tok_bias(dist, tab):
    t_len, n = dist.shape
    bias = tab.astype(jnp.float32)[_bucket(dist)]
    return bias.reshape(t_len, n, NSA_G, NSA_HG).transpose(0, 2, 3, 1)


def _compress(rows, w):
    b, t_len = rows.shape[:2]
    r = rows.reshape(b, t_len // NSA_BLK, NSA_BLK, NSA_G, NSA_D)
    return jnp.einsum('bnlgd,lgd->bngd', r, w)


def _nsa_core(q, qpos, kc_b, vc_b, fetch, kw, vw, kwpos, tab):
    b, t_len = q.shape[:2]
    f32 = jnp.float32
    nb = kc_b.shape[1]
    blk = jnp.arange(nb, dtype=jnp.int32)
    dist_c = qpos[:, None] - (blk * NSA_BLK + NSA_BLK - 1)[None, :]
    ok_c = (dist_c >= 0)[:, None, None, :]
    lg_c = jnp.einsum('btghd,bngd->btghn', q, kc_b).astype(f32) + _tok_bias(dist_c, tab)
    p_c = jax.nn.softmax(jnp.where(ok_c, lg_c, NEG), axis=-1) * ok_c
    o_c = jnp.einsum('btghn,bngd->btghd', p_c.astype(vc_b.dtype), vc_b)
    cur = qpos // NSA_BLK
    cand = (blk[None, :] < cur[:, None])[None, :, None, :]
    score = jnp.where(cand, p_c.sum(axis=3), -1.0)
    _, idx = lax.top_k(score, min(NSA_TOPK, nb))
    cur_b = jnp.broadcast_to(cur[None, :, None, None], (b, t_len, NSA_G, 1))
    idx_all = jnp.concatenate([idx, cur_b], axis=-1)
    ok_all = jnp.concatenate([idx < cur[None, :, None, None], jnp.ones((b, t_len, NSA_G, 1), bool)], axis=-1)
    pos5 = idx_all[..., None] * NSA_BLK + jnp.arange(NSA_BLK, dtype=jnp.int32)
    ok5 = ok_all[..., None] & (pos5 <= qpos[None, :, None, None, None])
    pos = pos5.reshape(b, t_len, NSA_G, -1)
    ok_s = ok5.reshape(b, t_len, NSA_G, 1, -1)
    k_s, v_s = fetch(pos)
    tab3 = tab.astype(f32).reshape(REL_BUCKETS, NSA_G, NSA_HG)
    bias_s = jnp.moveaxis(tab3[_bucket(qpos[None, :, None, None] - pos), jnp.arange(NSA_G)[:, None]], -1, 3)
    lg_s = jnp.einsum('btghd,btgnd->btghn', q, k_s).astype(f32) + bias_s
    p_s = jax.nn.softmax(jnp.where(ok_s, lg_s, NEG), axis=-1)
    o_s = jnp.einsum('btghn,btgnd->btghd', p_s.astype(v_s.dtype), v_s)
    dist_w = qpos[:, None] - kwpos[None, :]
    ok_w = ((dist_w >= 0) & (dist_w < NSA_WIN) & (kwpos >= 0)[None, :])[:, None, None, :]
    lg_w = jnp.einsum('btghd,bngd->btghn', q, kw).astype(f32) + _tok_bias(dist_w, tab)
    p_w = jax.nn.softmax(jnp.where(ok_w, lg_w, NEG), axis=-1)
    o_w = jnp.einsum('btghn,bngd->btghd', p_w.astype(vw.dtype), vw)
    return jnp.stack([o_c, o_s, o_w], axis=-2)


def _over_query_blocks(fn, q, qpos):
    b, t_len = q.shape[:2]
    if t_len <= NSA_QB:
        return fn(q, qpos)
    nblk = -(-t_len // NSA_QB)
    qb = jnp.swapaxes(q.reshape(b, nblk, NSA_QB, *q.shape[2:]), 0, 1)
    pb = qpos.reshape(nblk, NSA_QB)
    out = lax.map(lambda a: fn(a[0], a[1]), (qb, pb))
    out = jnp.swapaxes(out, 0, 1)
    return out.reshape(b, nblk * NSA_QB, *out.shape[3:])[:, :t_len]


def _nsa_combine(o3, ng):
    b, t_len = ng.shape[:2]
    gates = jax.nn.sigmoid(ng.astype(jnp.float32)).reshape(b, t_len, NSA_G, NSA_HG, 3, 1)
    return (o3.astype(jnp.float32) * gates).sum(axis=-2).reshape(b, t_len, NSA_H * NSA_D).astype(o3.dtype)


def _nsa_prompt(q, kc, vc, ks, vs, kw, vw, w_ck, w_cv, g_kc, tab):
    b, t_len = q.shape[:2]
    kc_b = _rms(_compress(kc, w_ck), g_kc)
    vc_b = _compress(vc, w_cv)
    padw = ((0, 0), (NSA_WIN, 0), (0, 0), (0, 0))
    kw_pad = jnp.pad(kw, padw)
    vw_pad = jnp.pad(vw, padw)
    bi = jnp.arange(b)[:, None, None, None]
    gi = jnp.arange(NSA_G)[None, None, :, None]

    def fetch(pos):
        return ks[bi, pos, gi], vs[bi, pos, gi]

    def block(qb, pb):
        start = pb[0]
        n = NSA_WIN + qb.shape[1]
        kwb = lax.dynamic_slice_in_dim(kw_pad, start, n, axis=1)
        vwb = lax.dynamic_slice_in_dim(vw_pad, start, n, axis=1)
        kwpos = start - NSA_WIN + jnp.arange(n, dtype=jnp.int32)
        return _nsa_core(qb, pb, kc_b, vc_b, fetch, kwb, vwb, kwpos, tab)

    return _over_query_blocks(block, q, jnp.arange(t_len, dtype=jnp.int32))


def _nsa_sample(q, kc, vc, ks, vs, kw, vw, pool_kc, pool_vc, pool_ks, pool_vs, buf_kw, buf_vw,
                page_table, w_ck, w_cv, g_kc, tab):
    db, ds = q.shape[:2]

    def full_rows(pool, new):
        past = pool[page_table].reshape(db, PAST_LEN, NSA_G, NSA_D)
        r = jnp.concatenate([past, new], axis=1)
        pad = (-r.shape[1]) % NSA_BLK
        return jnp.pad(r, ((0, 0), (0, pad), (0, 0), (0, 0)))

    kc_b = _rms(_compress(full_rows(pool_kc, kc), w_ck), g_kc)
    vc_b = _compress(full_rows(pool_vc, vc), w_cv)
    bi = jnp.arange(db)[:, None, None, None]
    gi = jnp.arange(NSA_G)[None, None, :, None]

    def fetch(pos):
        in_past = (pos < PAST_LEN)[..., None]
        pp = jnp.minimum(pos, PAST_LEN - 1)
        phys = page_table[bi, pp // PAGE_SIZE]
        off = pp % PAGE_SIZE
        pn = jnp.clip(pos - PAST_LEN, 0, ds - 1)
        k = jnp.where(in_past, pool_ks[phys, off, gi], ks[bi, pn, gi])
        v = jnp.where(in_past, pool_vs[phys, off, gi], vs[bi, pn, gi])
        return k, v

    wb = buf_kw.shape[1]
    kw_all = jnp.concatenate([buf_kw, kw], axis=1)
    vw_all = jnp.concatenate([buf_vw, vw], axis=1)
    kwpos = PAST_LEN - wb + jnp.arange(wb + ds, dtype=jnp.int32)
    qpos = PAST_LEN + jnp.arange(ds, dtype=jnp.int32)
    o3 = _over_query_blocks(lambda qb, pb: _nsa_core(qb, pb, kc_b, vc_b, fetch, kw_all, vw_all, kwpos, tab), q, qpos)
    return o3, kw_all[:, -wb:], vw_all[:, -wb:]


def _even_inputs(xn, w_in, qk_gain):
    b, t_len, _ = xn.shape
    (gq, gk, gv, gz, ga, gb, nq, kc, vc, ks, vs, kw, vw, ng) = _split(_pmm(xn, w_in), EVEN_WIDTHS)
    hd = lambda t: t.reshape(b, t_len, NSA_G, NSA_D)
    q = _rms(nq.reshape(b, t_len, NSA_G, NSA_HG, NSA_D), qk_gain[0]) * (NSA_D ** -0.5)
    ks = _rms(hd(ks), qk_gain[2])
    kw = _rms(hd(kw), qk_gain[3])
    return (gq, gk, gv, gz, ga, gb), (q, hd(kc), hd(vc), ks, hd(vs), kw, hd(vw), ng)


def _even_prompt(xn, w_in, w_out, conv_w, a_log, dt_bias, gnorm, qk_gain, w_ck, w_cv, tab):
    b, t_len, _ = xn.shape
    gdn_parts, (q, kc, vc, ks, vs, kw, vw, ng) = _even_inputs(xn, w_in, qk_gain)
    conv0 = jnp.zeros((b, GDN_CONV - 1, GDN_CONV_CH), xn.dtype)
    s0 = jnp.zeros((b, GDN_VH, GDN_DK, GDN_DV), xn.dtype)
    ya, conv_new, s = _gdn(gdn_parts, conv_w, a_log, dt_bias, gnorm, conv0, s0)
    yb = _nsa_combine(_nsa_prompt(q, kc, vc, ks, vs, kw, vw, w_ck, w_cv, qk_gain[1], tab), ng)
    y = _pmm(jnp.concatenate([ya, yb], axis=-1), w_out)
    nw = min(NSA_WIN, t_len)
    return y, (kc, vc, ks, vs, kw[:, -nw:], vw[:, -nw:], conv_new, s)


def _even_sample(xn, w_in, w_out, conv_w, a_log, dt_bias, gnorm, qk_gain, w_ck, w_cv, tab,
                 pool_kc, pool_vc, pool_ks, pool_vs, buf_kw, buf_vw, conv_st, s_st, page_table):
    gdn_parts, (q, kc, vc, ks, vs, kw, vw, ng) = _even_inputs(xn, w_in, qk_gain)
    ya, conv_new, s = _gdn(gdn_parts, conv_w, a_log, dt_bias, gnorm, conv_st, s_st)
    o3, kw_buf, vw_buf = _nsa_sample(q, kc, vc, ks, vs, kw, vw, pool_kc, pool_vc, pool_ks, pool_vs,
                                     buf_kw, buf_vw, page_table, w_ck, w_cv, qk_gain[1], tab)
    y = _pmm(jnp.concatenate([ya, _nsa_combine(o3, ng)], axis=-1), w_out)
    return y, (kc, vc, ks, vs, kw_buf, vw_buf, conv_new, s)


def _mlstm_chunked(q, k, v, ig, lf, c0, n0, m0):
    b, t_len, h, _ = q.shape
    c = min(ML_CHUNK, t_len)
    pad = (-t_len) % c
    incl = jnp.tril(jnp.ones((c, c), bool), 0)

    def body(carry, inp):
        cm, nv, m = carry
        qc, kc, vc, ic, fc = inp
        fcum = jnp.cumsum(fc, axis=-1)
        a = ic - fcum
        mt = fcum + jnp.maximum(m[..., None], lax.cummax(a, axis=2))
        dmat = jnp.exp(jnp.where(incl, a[..., None, :] + (fcum - mt)[..., :, None], -jnp.inf))
        dec0 = jnp.exp(fcum + m[..., None] - mt)
        s = jnp.einsum('bhid,bhjd->bhij', qc, kc) * dmat
        num = dec0[..., None] * jnp.einsum('bhid,bhde->bhie', qc, cm) + jnp.einsum('bhij,bhje->bhie', s, vc)
        den = dec0 * jnp.einsum('bhid,bhd->bhi', qc, nv) + s.sum(axis=-1)
        hc = num / jnp.maximum(jnp.abs(den), jnp.exp(-mt))[..., None]
        m_end = mt[..., -1]
        w = jnp.exp(a + (fcum[..., -1] - m_end)[..., None])
        dc = jnp.exp(fcum[..., -1] + m - m_end)
        cm = dc[..., None, None] * cm + jnp.einsum('bhj,bhjd,bhje->bhde', w, kc, vc)
        nv = dc[..., None] * nv + jnp.einsum('bhj,bhjd->bhd', w, kc)
        return (cm, nv, m_end), hc

    f32 = jnp.float32
    (cm, nv, m), hs = lax.scan(body, (c0.astype(f32), n0.astype(f32), m0.astype(f32)),
                               (_chunks(q, c, pad), _chunks(k, c, pad), _chunks(v, c, pad),
                                _chunks(ig, c, pad, NEG), _chunks(lf, c, pad)))
    return _unchunk(hs, t_len).astype(v.dtype), cm.astype(c0.dtype), nv.astype(n0.dtype), m.astype(m0.dtype)


def _odd(xn, w_in, w_out, b_if, mnorm, c0, n0, m0):
    b, t_len, _ = xn.shape
    q, k, v, o, ig, fg = _split(_pmm(xn, w_in), ODD_WIDTHS)
    q = q.reshape(b, t_len, ML_H, ML_DK) * (ML_DK ** -0.5)
    k = k.reshape(b, t_len, ML_H, ML_DK)
    v = v.reshape(b, t_len, ML_H, ML_DV)
    bf = b_if.astype(jnp.float32)
    ig = ig.astype(jnp.float32) + bf[:ML_H]
    lf = jax.nn.log_sigmoid(fg.astype(jnp.float32) + bf[ML_H:])
    hc, cm, nv, m = _mlstm_chunked(q, k, v, ig, lf, c0, n0, m0)
    hc = _rms(hc, mnorm.reshape(ML_H, ML_DV)).reshape(b, t_len, ODD_MIX)
    return _pmm(hc * jax.nn.sigmoid(o), w_out), (cm, nv, m)


def kernel(x_prompt, x_sample, cache_kc, cache_vc, cache_ks, cache_vs, cache_kw, cache_vw,
           state_gdn_conv, state_gdn_s, state_ml_c, state_ml_n, state_ml_m, page_table,
           rel_bias, norm_mix, norm_mlp, w_in_even, w_out_even, gdn_conv_w, gdn_a_log,
           gdn_dt_bias, gdn_norm_w, nsa_qk_gain, nsa_w_ck, nsa_w_cv, w_in_odd, w_out_odd,
           ml_b_if, ml_norm_w, w_ff1, w_ff2):
    hp, hs = x_prompt, x_sample
    ev_p, ev_s, od_p, od_s = [], [], [], []
    for l in range(DEPTH):
        j = l // 2
        xp = _rms(hp, norm_mix[l])
        xs = _rms(hs, norm_mix[l])
        if l % 2 == 0:
            yp, stp = _even_prompt(xp, w_in_even[j], w_out_even[j], gdn_conv_w[j], gdn_a_log[j],
                                   gdn_dt_bias[j], gdn_norm_w[j], nsa_qk_gain[j], nsa_w_ck[j],
                                   nsa_w_cv[j], rel_bias)
            ys, sts = _even_sample(xs, w_in_even[j], w_out_even[j], gdn_conv_w[j], gdn_a_log[j],
                                   gdn_dt_bias[j], gdn_norm_w[j], nsa_qk_gain[j], nsa_w_ck[j],
                                   nsa_w_cv[j], rel_bias, cache_kc[j], cache_vc[j], cache_ks[j],
                                   cache_vs[j], cache_kw[j], cache_vw[j], state_gdn_conv[j],
                                   state_gdn_s[j], page_table)
            ev_p.append(stp)
            ev_s.append(sts)
        else:
            bp = hp.shape[0]
            c0 = jnp.zeros((bp, ML_H, ML_DK, ML_DV), hp.dtype)
            n0 = jnp.zeros((bp, ML_H, ML_DK), hp.dtype)
            m0 = jnp.zeros((bp, ML_H), hp.dtype)
            yp, stp = _odd(xp, w_in_odd[j], w_out_odd[j], ml_b_if[j], ml_norm_w[j], c0, n0, m0)
            ys, sts = _odd(xs, w_in_odd[j], w_out_odd[j], ml_b_if[j], ml_norm_w[j],
                           state_ml_c[j], state_ml_n[j], state_ml_m[j])
            od_p.append(stp)
            od_s.append(sts)
        hp = hp + yp
        hs = hs + ys
        hp = hp + _mlp(_rms(hp, norm_mlp[l]), w_ff1[l], w_ff2[l])
        hs = hs + _mlp(_rms(hs, norm_mlp[l]), w_ff1[l], w_ff2[l])

    kc_p, vc_p, ks_p, vs_p, kw_p, vw_p, conv_p, gdn_p = [jnp.stack(a) for a in zip(*ev_p)]
    kc_s, vc_s, ks_s, vs_s, kw_s, vw_s, conv_s, gdn_s = [jnp.stack(a) for a in zip(*ev_s)]
    mlc_p, mln_p, mlm_p = [jnp.stack(a) for a in zip(*od_p)]
    mlc_s, mln_s, mlm_s = [jnp.stack(a) for a in zip(*od_s)]
    return (hp, hs, kc_p, kc_s, vc_p, vc_s, ks_p, ks_s, vs_p, vs_s, kw_p, kw_s, vw_p, vw_s,
            conv_p, conv_s, gdn_p, gdn_s, mlc_p, mlc_s, mln_p, mln_s, mlm_p, mlm_s)
```

```python
import functools
import math

import jax
import jax.numpy as jnp
import numpy as np
from jax import lax
from jax.experimental import pallas as pl
from jax.experimental.pallas import tpu as pltpu

D_MODEL = 2048
BATCH = 2
SEQ = 4096
DEPTH = 4
DEC_BATCH = 8
DEC_SEQ = 1
PAST_LEN = 16384
PAGE_SIZE = 128
RMS_EPS = 1e-6
NEG = -1e30
D_FF = 4 * D_MODEL

GDN_KH = 4
GDN_VH = 8
GDN_DK = 128
GDN_DV = 128
GDN_CONV = 4
GDN_CHUNK = 64
GDN_QK = GDN_KH * GDN_DK
GDN_V = GDN_VH * GDN_DV
GDN_CONV_CH = 2 * GDN_QK + GDN_V

NSA_H = 8
NSA_G = 2
NSA_HG = NSA_H // NSA_G
NSA_D = 128
NSA_BLK = 64
NSA_TOPK = 15
NSA_WIN = 512
NSA_QB = 128
NSA_KV = NSA_G * NSA_D

REL_BUCKETS = 32
REL_EXACT = 16
REL_MAX_DIST = 2048

ML_H = 8
ML_DK = 128
ML_DV = 256
ML_CHUNK = 64

EVEN_WIDTHS = (GDN_QK, GDN_QK, GDN_V, GDN_V, GDN_VH, GDN_VH, NSA_H * NSA_D,
               NSA_KV, NSA_KV, NSA_KV, NSA_KV, NSA_KV, NSA_KV, NSA_H * 3)
ODD_WIDTHS = (ML_H * ML_DK, ML_H * ML_DK, ML_H * ML_DV, ML_H * ML_DV, ML_H, ML_H)
ODD_MIX = ML_H * ML_DV

VMEM_LIMIT_BYTES = 56 * 1024 * 1024
MXU_DTYPE = jnp.bfloat16


def _mm_kernel(x_ref, w_ref, *rest, nk, act, has_res):
    if has_res:
        r_ref, o_ref, acc_ref = rest
    else:
        o_ref, acc_ref = rest
        r_ref = None
    k = pl.program_id(2)

    @pl.when(k == 0)
    def _():
        acc_ref[...] = jnp.zeros_like(acc_ref)

    acc_ref[...] += jnp.dot(x_ref[...], w_ref[...], preferred_element_type=jnp.float32)

    @pl.when(k == nk - 1)
    def _():
        y = acc_ref[...]
        if act == 'relu2':
            y = jnp.square(jnp.maximum(y, 0.0))
        if has_res:
            y = y + r_ref[...]
        o_ref[...] = y.astype(o_ref.dtype)


def _pick(n, prefs):
    for p in prefs:
        if n % p == 0:
            return p
    return n


def _matmul(x, w, *, act=None, res=None, out_dtype=jnp.float32):
    m, kdim = x.shape
    _, n = w.shape
    tm = _pick(m, (512, 256, 128))
    tn = _pick(n, (512, 896, 384, 256, 128))
    tk = _pick(kdim, (2048, 1024, 512))
    nk = kdim // tk
    in_specs = [pl.BlockSpec((tm, tk), lambda i, j, k: (i, k)),
                pl.BlockSpec((tk, tn), lambda i, j, k: (k, j))]
    args = [x, w]
    if res is not None:
        in_specs.append(pl.BlockSpec((tm, tn), lambda i, j, k: (i, j)))
        args.append(res)
    return pl.pallas_call(
        functools.partial(_mm_kernel, nk=nk, act=act, has_res=res is not None),
        grid=(m // tm, n // tn, nk),
        in_specs=in_specs,
        out_specs=pl.BlockSpec((tm, tn), lambda i, j, k: (i, j)),
        out_shape=jax.ShapeDtypeStruct((m, n), out_dtype),
        scratch_shapes=[pltpu.VMEM((tm, tn), jnp.float32)],
        compiler_params=pltpu.CompilerParams(
            dimension_semantics=("parallel", "parallel", "arbitrary"),
            vmem_limit_bytes=VMEM_LIMIT_BYTES),
    )(*args)


def _pmm(x, w):
    lead = x.shape[:-1]
    x2 = x.reshape(-1, x.shape[-1]).astype(MXU_DTYPE)
    n = w.shape[1]
    npad = (-n) % 128
    wb = w.astype(MXU_DTYPE)
    if npad:
        wb = jnp.pad(wb, ((0, 0), (0, npad)))
    y = _matmul(x2, wb)
    if npad:
        y = y[:, :n]
    return y.reshape(*lead, n)


EV_CONV = 0
EV_Z = GDN_CONV_CH
EV_NQ = EV_Z + GDN_V
EV_KV = EV_NQ + NSA_H * NSA_D
EV_SMALL = EV_KV + 6 * NSA_KV
EV_N = EV_SMALL + 128
EV_ORIG_SMALL = 2 * GDN_QK + 2 * GDN_V
EV_ORIG_N = EV_ORIG_SMALL + 2 * GDN_VH + NSA_H * NSA_D + 6 * NSA_KV + 3 * NSA_H


def _permute_even_w(w_in):
    ab = w_in[:, EV_ORIG_SMALL:EV_ORIG_SMALL + 2 * GDN_VH]
    big = w_in[:, EV_ORIG_SMALL + 2 * GDN_VH:EV_ORIG_N - 3 * NSA_H]
    ng = w_in[:, EV_ORIG_N - 3 * NSA_H:]
    pad = jnp.zeros((w_in.shape[0], EV_N - EV_ORIG_N), w_in.dtype)
    return jnp.concatenate([w_in[:, :EV_ORIG_SMALL], big, ab, ng, pad], axis=1).astype(MXU_DTYPE)


def _bucket_thresholds():
    n = np.arange(0, 4 * REL_MAX_DIST)
    nf = np.maximum(n, REL_EXACT).astype(np.float32)
    large = REL_EXACT + (np.log(nf / np.float32(REL_EXACT)) / np.float32(math.log(REL_MAX_DIST / REL_EXACT))
                         * np.float32(REL_BUCKETS - REL_EXACT)).astype(np.int32)
    b = np.where(n < REL_EXACT, n, np.minimum(large, REL_BUCKETS - 1))
    return tuple(int(np.argmax(b >= k)) for k in range(REL_BUCKETS))


BUCKET_THR = _bucket_thresholds()
BIAS_TILES = 14
assert BUCKET_THR[-1] <= (BIAS_TILES - 1) * 128 - 127


def _bias_of_dist(dist, tab_ref, head):
    val = jnp.full(dist.shape, tab_ref[0, head], jnp.float32)
    for k in range(1, REL_BUCKETS):
        val = jnp.where(dist >= BUCKET_THR[k], tab_ref[k, head], val)
    return val


def _bias_tiles_kernel(tab_ref, o_ref):
    h = pl.program_id(0)
    d = pl.program_id(1)
    i = lax.broadcasted_iota(jnp.int32, (128, 128), 0)
    j = lax.broadcasted_iota(jnp.int32, (128, 128), 1)
    dist = d * 128 + i - j
    val = jnp.full(dist.shape, tab_ref[0, h], jnp.float32)
    for k in range(1, REL_BUCKETS):
        val = jnp.where(dist >= BUCKET_THR[k], tab_ref[k, h], val)
    o_ref[0, 0] = val


def _bias_tiles(tab):
    return pl.pallas_call(
        _bias_tiles_kernel,
        grid=(NSA_H, BIAS_TILES),
        in_specs=[pl.BlockSpec(memory_space=pltpu.SMEM)],
        out_specs=pl.BlockSpec((1, 1, 128, 128), lambda h, d: (h, d, 0, 0)),
        out_shape=jax.ShapeDtypeStruct((NSA_H, BIAS_TILES, 128, 128), jnp.float32),
    )(tab.astype(jnp.float32))


def _group_rms(x, gain_row, scale=1.0):
    ms = jnp.mean(x * x, axis=-1, keepdims=True)
    y = x * lax.rsqrt(ms + RMS_EPS) * gain_row
    return y * scale if scale != 1.0 else y


def _nsa_prep_kernel(nq_ref, kc_ref, vc_ref, ks_ref, vs_ref, kw_ref, vw_ref, gain_ref, wck_ref, wcv_ref,
                     qn_ref, ksn_ref, kwn_ref, ksb_ref, vsb_ref, kwb_ref, vwb_ref, kcb_ref, vcb_ref):
    tm = nq_ref.shape[0]
    g0 = gain_ref[0:1, :]
    g1 = gain_ref[1:2, :]
    g2 = gain_ref[2:3, :]
    g3 = gain_ref[3:4, :]
    for h in range(NSA_H):
        sl = slice(h * NSA_D, (h + 1) * NSA_D)
        qn_ref[:, sl] = _group_rms(nq_ref[:, sl], g0, NSA_D ** -0.5).astype(qn_ref.dtype)
    for g in range(NSA_G):
        sl = slice(g * NSA_D, (g + 1) * NSA_D)
        ksn = _group_rms(ks_ref[:, sl], g2)
        kwn = _group_rms(kw_ref[:, sl], g3)
        ksn_ref[:, sl] = ksn
        kwn_ref[:, sl] = kwn
        ksb_ref[:, sl] = ksn.astype(ksb_ref.dtype)
        kwb_ref[:, sl] = kwn.astype(kwb_ref.dtype)
    vsb_ref[...] = vs_ref[...].astype(vsb_ref.dtype)
    vwb_ref[...] = vw_ref[...].astype(vwb_ref.dtype)
    nblk = tm // NSA_BLK
    kc3 = kc_ref[...].reshape(nblk, NSA_BLK, NSA_KV)
    vc3 = vc_ref[...].reshape(nblk, NSA_BLK, NSA_KV)
    kcb = jnp.sum(kc3 * wck_ref[...][None], axis=1)
    vcb = jnp.sum(vc3 * wcv_ref[...][None], axis=1)
    for g in range(NSA_G):
        sl = slice(g * NSA_D, (g + 1) * NSA_D)
        kcb_ref[:, sl] = _group_rms(kcb[:, sl], g1)
    vcb_ref[...] = vcb


def _nsa_prep(proj, gain, w_ck, w_cv):
    m = proj.shape[0]
    tm = 512
    kvb = EV_KV // NSA_KV
    f32, bf16 = jnp.float32, MXU_DTYPE

    def kv_spec(i):
        return pl.BlockSpec((tm, NSA_KV), lambda r, i=i: (r, kvb + i))

    row = lambda w: pl.BlockSpec((tm, w), lambda r: (r, 0))
    full = lambda a: pl.BlockSpec(a.shape, lambda r: (0,) * a.ndim)
    wck = w_ck.reshape(NSA_BLK, NSA_KV)
    wcv = w_cv.reshape(NSA_BLK, NSA_KV)
    outs = pl.pallas_call(
        _nsa_prep_kernel,
        grid=(m // tm,),
        in_specs=[pl.BlockSpec((tm, NSA_H * NSA_D), lambda r: (r, EV_NQ // (NSA_H * NSA_D)))]
                 + [kv_spec(i) for i in range(6)] + [full(gain), full(wck), full(wcv)],
        out_specs=[row(NSA_H * NSA_D)] + [row(NSA_KV)] * 6
                  + [pl.BlockSpec((tm // NSA_BLK, NSA_KV), lambda r: (r, 0))] * 2,
        out_shape=[jax.ShapeDtypeStruct((m, NSA_H * NSA_D), bf16),
                   jax.ShapeDtypeStruct((m, NSA_KV), f32), jax.ShapeDtypeStruct((m, NSA_KV), f32)]
                  + [jax.ShapeDtypeStruct((m, NSA_KV), bf16)] * 4
                  + [jax.ShapeDtypeStruct((m // NSA_BLK, NSA_KV), f32)] * 2,
        compiler_params=pltpu.CompilerParams(dimension_semantics=("parallel",),
                                             vmem_limit_bytes=VMEM_LIMIT_BYTES),
    )(proj, proj, proj, proj, proj, proj, proj, gain, wck, wcv)
    return outs


NSA_TK = 256


def _softmax_rows(lg):
    m = jnp.max(lg, axis=-1, keepdims=True)
    e = jnp.exp(lg - m)
    return e / jnp.sum(e, axis=-1, keepdims=True)


def _pad_rows(x, rows):
    if x.shape[0] == rows:
        return x
    return jnp.concatenate([x, jnp.zeros((rows - x.shape[0],) + x.shape[1:], x.dtype)], axis=0)


def _nsa_attn_kernel(tab_ref, q_ref, kcb_ref, vcb_ref, ks_ref, vs_ref, kw_ref, vw_ref, bias_ref, ng_ref,
                     o_ref, m_scr, l_scr, acc_scr):
    f32, bf16 = jnp.float32, MXU_DTYPE
    qb = pl.program_id(1)
    nb = kcb_ref.shape[0]
    qn = NSA_QB
    gates = jax.nn.sigmoid(ng_ref[...])
    tsub = NSA_TK // qn
    shift = NSA_BLK.bit_length() - 1

    for g in range(NSA_G):
        gl = slice(g * NSA_D, (g + 1) * NSA_D)
        q2 = jnp.concatenate([q_ref[:, (g * NSA_HG + h) * NSA_D:(g * NSA_HG + h + 1) * NSA_D]
                              for h in range(NSA_HG)], axis=0)

        kcb = kcb_ref[:, gl].astype(bf16)
        vcb = _pad_rows(vcb_ref[:, gl], qn).astype(bf16)
        lgt = lax.dot_general(kcb, q2, (((1,), (1,)), ((), ())), preferred_element_type=f32)
        blk = lax.broadcasted_iota(jnp.int32, (nb, qn), 0)
        blkf = blk.astype(f32)
        qpos = qb * qn + lax.broadcasted_iota(jnp.int32, (nb, qn), 1)
        dist_c = qpos - (blk * NSA_BLK + NSA_BLK - 1)
        ok_c = dist_c >= 0
        score = jnp.zeros((nb, qn), f32)
        o_c = []
        for h in range(NSA_HG):
            lg = lgt[:, h * qn:(h + 1) * qn] + _bias_of_dist(dist_c, tab_ref, g * NSA_HG + h)
            lg = jnp.where(ok_c, lg, NEG)
            mx = jnp.max(lg, axis=0, keepdims=True)
            e = jnp.exp(lg - mx)
            p = jnp.where(ok_c, e / jnp.sum(e, axis=0, keepdims=True), 0.0)
            score = score + p
            p_t = _pad_rows(p, qn).T.astype(bf16)
            o_c.append(jnp.dot(p_t, vcb, preferred_element_type=f32))
        cur = lax.shift_right_logical(qpos, shift)
        score = jnp.where(blk < cur, score, -1.0)

        sel = jnp.zeros((nb, qn), f32)
        for _ in range(min(NSA_TOPK, nb)):
            mx = jnp.max(score, axis=0, keepdims=True)
            first = jnp.min(jnp.where(score == mx, blkf, float(nb)), axis=0, keepdims=True)
            pick = blkf == first
            sel = jnp.where(pick, jnp.where(mx >= 0.0, 1.0, 0.0), sel)
            score = jnp.where(pick, -2.0, score)
        sel = jnp.where(blk == cur, 1.0, sel)
        sel_t = _pad_rows(sel, qn).T.astype(bf16)

        m_scr[...] = jnp.full(m_scr.shape, NEG, f32)
        l_scr[...] = jnp.zeros(l_scr.shape, f32)
        acc_scr[...] = jnp.zeros(acc_scr.shape, f32)
        n_tiles = qb // tsub + 1
        row_tok = qb * qn + lax.broadcasted_iota(jnp.int32, (qn, NSA_TK), 0)
        col = lax.broadcasted_iota(jnp.int32, (qn, NSA_TK), 1)
        eblk = lax.broadcasted_iota(jnp.int32, (qn, NSA_TK), 0)

        def body(i, carry, g=g, gl=gl, q2=q2, sel_t=sel_t):
            kt = n_tiles - 1 - i
            k0 = pl.multiple_of(kt * NSA_TK, NSA_TK)
            k = ks_ref[pl.ds(k0, NSA_TK), gl]
            v = vs_ref[pl.ds(k0, NSA_TK), gl]
            lg = lax.dot_general(q2, k, (((1,), (1,)), ((), ())), preferred_element_type=f32)
            expand = jnp.where(eblk == lax.shift_right_logical(k0 + col, shift), 1.0, 0.0).astype(bf16)
            member = jnp.dot(sel_t, expand, preferred_element_type=f32)
            member = jnp.where(k0 + col <= row_tok, member, 0.0)
            keep = member > 0.5
            parts = []
            for h in range(NSA_HG):
                tiles = [bias_ref[g * NSA_HG + h, jnp.clip(qb - (kt * tsub + j), 0, BIAS_TILES - 1)]
                         for j in range(tsub)]
                bias = jnp.concatenate(tiles, axis=1)
                parts.append(jnp.where(keep, lg[h * qn:(h + 1) * qn] + bias, NEG))
            lgm = jnp.concatenate(parts, axis=0)
            m_old = m_scr[...]
            m_new = jnp.maximum(m_old, jnp.max(lgm, axis=-1, keepdims=True))
            p = jnp.exp(lgm - m_new)
            alpha = jnp.exp(m_old - m_new)
            l_scr[...] = alpha * l_scr[...] + jnp.sum(p, axis=-1, keepdims=True)
            acc_scr[...] = alpha * acc_scr[...] + jnp.dot(p.astype(bf16), v, preferred_element_type=f32)
            m_scr[...] = m_new
            return carry

        lax.fori_loop(0, n_tiles, body, 0)
        o_s = acc_scr[...] / l_scr[...]

        nwin = NSA_WIN // qn + 1
        wi = lax.broadcasted_iota(jnp.int32, (qn, qn), 0)
        wj = lax.broadcasted_iota(jnp.int32, (qn, qn), 1)
        lgs, vws = [], []
        for d in range(nwin):
            sub = qb - d
            k0 = pl.multiple_of(jnp.maximum(sub, 0) * qn, qn)
            k = kw_ref[pl.ds(k0, qn), gl]
            vws.append(vw_ref[pl.ds(k0, qn), gl])
            lg = lax.dot_general(q2, k, (((1,), (1,)), ((), ())), preferred_element_type=f32)
            parts = []
            for h in range(NSA_HG):
                x = lg[h * qn:(h + 1) * qn] + bias_ref[g * NSA_HG + h, d]
                if d == 0:
                    x = jnp.where(wj <= wi, x, NEG)
                else:
                    if d == nwin - 1:
                        x = jnp.where(wj > wi, x, NEG)
                    x = jnp.where(sub >= 0, x, NEG)
                parts.append(x)
            lgs.append(jnp.concatenate(parts, axis=0))
        p_w = _softmax_rows(jnp.concatenate(lgs, axis=1))
        o_w = jnp.dot(p_w.astype(bf16), jnp.concatenate(vws, axis=0), preferred_element_type=f32)

        for h in range(NSA_HG):
            c = 2 * GDN_VH + (g * NSA_HG + h) * 3
            rs = slice(h * qn, (h + 1) * qn)
            y = (o_c[h] * gates[:, c:c + 1] + o_s[rs] * gates[:, c + 1:c + 2] + o_w[rs] * gates[:, c + 2:c + 3])
            o_ref[:, (g * NSA_HG + h) * NSA_D:(g * NSA_HG + h + 1) * NSA_D] = y.astype(o_ref.dtype)


def _nsa_prompt_attn(tab, bias_tiles, proj, qn, kcb, vcb, ksb, vsb, kwb, vwb, batch, t_len):
    nq = t_len // NSA_QB
    nb = t_len // NSA_BLK
    hd = NSA_H * NSA_D
    per_b = lambda w: pl.BlockSpec((t_len, w), lambda b, i: (b, 0))
    return pl.pallas_call(
        _nsa_attn_kernel,
        grid=(batch, nq),
        in_specs=[pl.BlockSpec(memory_space=pltpu.SMEM),
                  pl.BlockSpec((NSA_QB, hd), lambda b, i: (b * nq + i, 0)),
                  pl.BlockSpec((nb, NSA_KV), lambda b, i: (b, 0)),
                  pl.BlockSpec((nb, NSA_KV), lambda b, i: (b, 0)),
                  per_b(NSA_KV), per_b(NSA_KV), per_b(NSA_KV), per_b(NSA_KV),
                  pl.BlockSpec(bias_tiles.shape, lambda b, i: (0, 0, 0, 0)),
                  pl.BlockSpec((NSA_QB, 128), lambda b, i: (b * nq + i, EV_SMALL // 128 + 0))],
        out_specs=pl.BlockSpec((NSA_QB, hd), lambda b, i: (b * nq + i, 0)),
        out_shape=jax.ShapeDtypeStruct((batch * t_len, hd), MXU_DTYPE),
        scratch_shapes=[pltpu.VMEM((NSA_HG * NSA_QB, 1), jnp.float32),
                        pltpu.VMEM((NSA_HG * NSA_QB, 1), jnp.float32),
                        pltpu.VMEM((NSA_HG * NSA_QB, NSA_D), jnp.float32)],
        compiler_params=pltpu.CompilerParams(dimension_semantics=("parallel", "arbitrary"),
                                             vmem_limit_bytes=VMEM_LIMIT_BYTES),
    )(tab.astype(jnp.float32), qn, kcb, vcb, ksb, vsb, kwb, vwb, bias_tiles, proj)


def _split(x, widths):
    offs = [int(o) for o in np.cumsum(widths)[:-1]]
    return jnp.split(x, offs, axis=-1)


def _rms(x, g):
    xf = x.astype(jnp.float32)
    y = xf * lax.rsqrt(jnp.mean(xf * xf, axis=-1, keepdims=True) + RMS_EPS)
    return (y * g.astype(jnp.float32)).astype(x.dtype)


def _mlp(xn, w1, w2):
    h = jnp.square(jax.nn.relu(_pmm(xn, w1)))
    return _pmm(h, w2)


def _causal_conv(xin, w):
    c = xin.shape[-1]
    return lax.conv_general_dilated(xin, w[:, None, :].astype(xin.dtype), window_strides=(1,),
                                    padding='VALID', dimension_numbers=('NWC', 'WIO', 'NWC'),
                                    feature_group_count=c)


def _chunks(t, c, pad, pad_val=0.0):
    t = t.astype(jnp.float32)
    if pad:
        t = jnp.pad(t, [(0, 0), (0, pad)] + [(0, 0)] * (t.ndim - 2), constant_values=pad_val)
    b, tp = t.shape[:2]
    t = t.reshape(b, tp // c, c, *t.shape[2:])
    return jnp.swapaxes(jnp.moveaxis(t, 1, 0), 2, 3)


def _unchunk(o, t_len):
    n, b, h, c, d = o.shape
    return jnp.transpose(o, (1, 0, 3, 2, 4)).reshape(b, n * c, h, d)[:, :t_len]


def _gdn_chunked(q, k, v, g, beta, s0):
    b, t_len, h, dk = q.shape
    dv = v.shape[-1]
    c = min(GDN_CHUNK, t_len)
    pad = (-t_len) % c
    tri_s = jnp.tril(jnp.ones((c, c), bool), -1)
    tri_i = jnp.tril(jnp.ones((c, c), bool), 0)
    eye = jnp.eye(c, dtype=jnp.float32)

    def body(s, inp):
        qc, kc, vc, gc, bc = inp
        gcum = jnp.cumsum(gc, axis=-1)
        diff = gcum[..., :, None] - gcum[..., None, :]
        eg = jnp.exp(gcum)
        lmat = bc[..., :, None] * jnp.einsum('bhid,bhjd->bhij', kc, kc) * jnp.exp(jnp.where(tri_s, diff, -jnp.inf))
        rhs = jnp.concatenate([bc[..., None] * vc, (bc * eg)[..., None] * kc], axis=-1)
        sol = lax.linalg.triangular_solve(eye + lmat, rhs, left_side=True, lower=True)
        u = sol[..., :dv] - jnp.einsum('bhcd,bhde->bhce', sol[..., dv:], s)
        att = jnp.einsum('bhid,bhjd->bhij', qc, kc) * jnp.exp(jnp.where(tri_i, diff, -jnp.inf))
        o = eg[..., None] * jnp.einsum('bhcd,bhde->bhce', qc, s) + jnp.einsum('bhij,bhje->bhie', att, u)
        gl = gcum[..., -1]
        s = jnp.exp(gl)[..., None, None] * s + jnp.einsum('bhcd,bhce->bhde', kc * jnp.exp(gl[..., None] - gcum)[..., None], u)
        return s, o

    s, o = lax.scan(body, s0.astype(jnp.float32),
                    (_chunks(q, c, pad), _chunks(k, c, pad), _chunks(v, c, pad), _chunks(g, c, pad), _chunks(beta, c, pad)))
    return _unchunk(o, t_len).astype(v.dtype), s.astype(s0.dtype)


def _gdn(parts, conv_w, a_log, dt_bias, norm_w, conv_prev, s0):
    q, k, v, z, a, bgate = parts
    b, t_len = q.shape[:2]
    xin = jnp.concatenate([conv_prev.astype(q.dtype), jnp.concatenate([q, k, v], axis=-1)], axis=1)
    c = jax.nn.silu(_causal_conv(xin, conv_w))
    cq, ck, cv = _split(c, (GDN_QK, GDN_QK, GDN_V))

    def l2(t):
        t = t.reshape(b, t_len, GDN_KH, GDN_DK).astype(jnp.float32)
        return t * lax.rsqrt(jnp.sum(t * t, axis=-1, keepdims=True) + 1e-6)

    rep = GDN_VH // GDN_KH
    qh = jnp.repeat(l2(cq) * (GDN_DK ** -0.5), rep, axis=2)
    kh = jnp.repeat(l2(ck), rep, axis=2)
    vh = cv.reshape(b, t_len, GDN_VH, GDN_DV)
    beta = jax.nn.sigmoid(bgate.astype(jnp.float32))
    g = -jnp.exp(a_log.astype(jnp.float32)) * jax.nn.softplus(a.astype(jnp.float32) + dt_bias.astype(jnp.float32))
    o, s = _gdn_chunked(qh, kh, vh, g, beta, s0)
    o = _rms(o, norm_w) * jax.nn.silu(z.reshape(b, t_len, GDN_VH, GDN_DV))
    return o.reshape(b, t_len, GDN_V), xin[:, -(GDN_CONV - 1):], s


def _bucket(dist):
    n = jnp.maximum(dist, 0)
    nf = jnp.maximum(n, REL_EXACT).astype(jnp.float32)
    large = REL_EXACT + (jnp.log(nf / REL_EXACT) / math.log(REL_MAX_DIST / REL_EXACT)
                         * (REL_BUCKETS - REL_EXACT)).astype(jnp.int32)
    large = jnp.minimum(large, REL_BUCKETS - 1)
    return jnp.where(n < REL_EXACT, n, large)


def _tok_bias(dist, tab):
    t_len, n = dist.shape
    bias = tab.astype(jnp.float32)[_bucket(dist)]
    return bias.reshape(t_len, n, NSA_G, NSA_HG).transpose(0, 2, 3, 1)


def _compress(rows, w):
    b, t_len = rows.shape[:2]
    r = rows.reshape(b, t_len // NSA_BLK, NSA_BLK, NSA_G, NSA_D)
    return jnp.einsum('bnlgd,lgd->bngd', r, w)


def _nsa_core(q, qpos, kc_b, vc_b, fetch, kw, vw, kwpos, tab):
    b, t_len = q.shape[:2]
    f32 = jnp.float32
    nb = kc_b.shape[1]
    blk = jnp.arange(nb, dtype=jnp.int32)
    dist_c = qpos[:, None] - (blk * NSA_BLK + NSA_BLK - 1)[None, :]
    ok_c = (dist_c >= 0)[:, None, None, :]
    lg_c = jnp.einsum('btghd,bngd->btghn', q, kc_b).astype(f32) + _tok_bias(dist_c, tab)
    p_c = jax.nn.softmax(jnp.where(ok_c, lg_c, NEG), axis=-1) * ok_c
    o_c = jnp.einsum('btghn,bngd->btghd', p_c.astype(vc_b.dtype), vc_b)
    cur = qpos // NSA_BLK
    cand = (blk[None, :] < cur[:, None])[None, :, None, :]
    score = jnp.where(cand, p_c.sum(axis=3), -1.0)
    _, idx = lax.top_k(score, min(NSA_TOPK, nb))
    cur_b = jnp.broadcast_to(cur[None, :, None, None], (b, t_len, NSA_G, 1))
    idx_all = jnp.concatenate([idx, cur_b], axis=-1)
    ok_all = jnp.concatenate([idx < cur[None, :, None, None], jnp.ones((b, t_len, NSA_G, 1), bool)], axis=-1)
    pos5 = idx_all[..., None] * NSA_BLK + jnp.arange(NSA_BLK, dtype=jnp.int32)
    ok5 = ok_all[..., None] & (pos5 <= qpos[None, :, None, None, None])
    pos = pos5.reshape(b, t_len, NSA_G, -1)
    ok_s = ok5.reshape(b, t_len, NSA_G, 1, -1)
    k_s, v_s = fetch(pos)
    tab3 = tab.astype(f32).reshape(REL_BUCKETS, NSA_G, NSA_HG)
    bias_s = jnp.moveaxis(tab3[_bucket(qpos[None, :, None, None] - pos), jnp.arange(NSA_G)[:, None]], -1, 3)
    lg_s = jnp.einsum('btghd,btgnd->btghn', q, k_s).astype(f32) + bias_s
    p_s = jax.nn.softmax(jnp.where(ok_s, lg_s, NEG), axis=-1)
    o_s = jnp.einsum('btghn,btgnd->btghd', p_s.astype(v_s.dtype), v_s)
    dist_w = qpos[:, None] - kwpos[None, :]
    ok_w = ((dist_w >= 0) & (dist_w < NSA_WIN) & (kwpos >= 0)[None, :])[:, None, None, :]
    lg_w = jnp.einsum('btghd,bngd->btghn', q, kw).astype(f32) + _tok_bias(dist_w, tab)
    p_w = jax.nn.softmax(jnp.where(ok_w, lg_w, NEG), axis=-1)
    o_w = jnp.einsum('btghn,bngd->btghd', p_w.astype(vw.dtype), vw)
    return jnp.stack([o_c, o_s, o_w], axis=-2)


def _over_query_blocks(fn, q, qpos):
    b, t_len = q.shape[:2]
    if t_len <= NSA_QB:
        return fn(q, qpos)
    nblk = -(-t_len // NSA_QB)
    qb = jnp.swapaxes(q.reshape(b, nblk, NSA_QB, *q.shape[2:]), 0, 1)
    pb = qpos.reshape(nblk, NSA_QB)
    out = lax.map(lambda a: fn(a[0], a[1]), (qb, pb))
    out = jnp.swapaxes(out, 0, 1)
    return out.reshape(b, nblk * NSA_QB, *out.shape[3:])[:, :t_len]


def _nsa_combine(o3, ng):
    b, t_len = ng.shape[:2]
    gates = jax.nn.sigmoid(ng.astype(jnp.float32)).reshape(b, t_len, NSA_G, NSA_HG, 3, 1)
    return (o3.astype(jnp.float32) * gates).sum(axis=-2).reshape(b, t_len, NSA_H * NSA_D).astype(o3.dtype)


def _nsa_prompt(q, kc, vc, ks, vs, kw, vw, w_ck, w_cv, g_kc, tab):
    b, t_len = q.shape[:2]
    kc_b = _rms(_compress(kc, w_ck), g_kc)
    vc_b = _compress(vc, w_cv)
    padw = ((0, 0), (NSA_WIN, 0), (0, 0), (0, 0))
    kw_pad = jnp.pad(kw, padw)
    vw_pad = jnp.pad(vw, padw)
    bi = jnp.arange(b)[:, None, None, None]
    gi = jnp.arange(NSA_G)[None, None, :, None]

    def fetch(pos):
        return ks[bi, pos, gi], vs[bi, pos, gi]

    def block(qb, pb):
        start = pb[0]
        n = NSA_WIN + qb.shape[1]
        kwb = lax.dynamic_slice_in_dim(kw_pad, start, n, axis=1)
        vwb = lax.dynamic_slice_in_dim(vw_pad, start, n, axis=1)
        kwpos = start - NSA_WIN + jnp.arange(n, dtype=jnp.int32)
        return _nsa_core(qb, pb, kc_b, vc_b, fetch, kwb, vwb, kwpos, tab)

    return _over_query_blocks(block, q, jnp.arange(t_len, dtype=jnp.int32))


def _nsa_sample(q, kc, vc, ks, vs, kw, vw, pool_kc, pool_vc, pool_ks, pool_vs, buf_kw, buf_vw,
                page_table, w_ck, w_cv, g_kc, tab):
    db, ds = q.shape[:2]

    def full_rows(pool, new):
        past = pool[page_table].reshape(db, PAST_LEN, NSA_G, NSA_D)
        r = jnp.concatenate([past, new], axis=1)
        pad = (-r.shape[1]) % NSA_BLK
        return jnp.pad(r, ((0, 0), (0, pad), (0, 0), (0, 0)))

    kc_b = _rms(_compress(full_rows(pool_kc, kc), w_ck), g_kc)
    vc_b = _compress(full_rows(pool_vc, vc), w_cv)
    bi = jnp.arange(db)[:, None, None, None]
    gi = jnp.arange(NSA_G)[None, None, :, None]

    def fetch(pos):
        in_past = (pos < PAST_LEN)[..., None]
        pp = jnp.minimum(pos, PAST_LEN - 1)
        phys = page_table[bi, pp // PAGE_SIZE]
        off = pp % PAGE_SIZE
        pn = jnp.clip(pos - PAST_LEN, 0, ds - 1)
        k = jnp.where(in_past, pool_ks[phys, off, gi], ks[bi, pn, gi])
        v = jnp.where(in_past, pool_vs[phys, off, gi], vs[bi, pn, gi])
        return k, v

    wb = buf_kw.shape[1]
    kw_all = jnp.concatenate([buf_kw, kw], axis=1)
    vw_all = jnp.concatenate([buf_vw, vw], axis=1)
    kwpos = PAST_LEN - wb + jnp.arange(wb + ds, dtype=jnp.int32)
    qpos = PAST_LEN + jnp.arange(ds, dtype=jnp.int32)
    o3 = _over_query_blocks(lambda qb, pb: _nsa_core(qb, pb, kc_b, vc_b, fetch, kw_all, vw_all, kwpos, tab), q, qpos)
    return o3, kw_all[:, -wb:], vw_all[:, -wb:]


def _even_inputs(xn, w_in, qk_gain):
    b, t_len, _ = xn.shape
    (gq, gk, gv, gz, ga, gb, nq, kc, vc, ks, vs, kw, vw, ng) = _split(_pmm(xn, w_in), EVEN_WIDTHS)
    hd = lambda t: t.reshape(b, t_len, NSA_G, NSA_D)
    q = _rms(nq.reshape(b, t_len, NSA_G, NSA_HG, NSA_D), qk_gain[0]) * (NSA_D ** -0.5)
    ks = _rms(hd(ks), qk_gain[2])
    kw = _rms(hd(kw), qk_gain[3])
    return (gq, gk, gv, gz, ga, gb), (q, hd(kc), hd(vc), ks, hd(vs), kw, hd(vw), ng)


def _even_prompt(xn, w_in, w_out, conv_w, a_log, dt_bias, gnorm, qk_gain, w_ck, w_cv, tab, bias_tiles):
    b, t_len, d = xn.shape
    proj = _matmul(xn.reshape(b * t_len, d).astype(MXU_DTYPE), _permute_even_w(w_in))
    p3 = proj.reshape(b, t_len, EV_N)
    gq, gk, gv = _split(p3[..., :GDN_CONV_CH], (GDN_QK, GDN_QK, GDN_V))
    gdn_parts = (gq, gk, gv, p3[..., EV_Z:EV_Z + GDN_V], p3[..., EV_SMALL:EV_SMALL + GDN_VH],
                 p3[..., EV_SMALL + GDN_VH:EV_SMALL + 2 * GDN_VH])
    conv0 = jnp.zeros((b, GDN_CONV - 1, GDN_CONV_CH), xn.dtype)
    s0 = jnp.zeros((b, GDN_VH, GDN_DK, GDN_DV), xn.dtype)
    ya, conv_new, s = _gdn(gdn_parts, conv_w, a_log, dt_bias, gnorm, conv0, s0)
    qn, ksn, kwn, ksb, vsb, kwb, vwb, kcb, vcb = _nsa_prep(proj, qk_gain, w_ck, w_cv)
    yb = _nsa_prompt_attn(tab, bias_tiles, proj, qn, kcb, vcb, ksb, vsb, kwb, vwb, b, t_len)
    y = _pmm(jnp.concatenate([ya, yb.reshape(b, t_len, -1).astype(ya.dtype)], axis=-1), w_out)
    nw = min(NSA_WIN, t_len)
    hd = lambda t: t.reshape(b, t_len, NSA_G, NSA_D)
    kv = lambda i: hd(p3[..., EV_KV + i * NSA_KV:EV_KV + (i + 1) * NSA_KV])
    return y, (kv(0), kv(1), hd(ksn), kv(3), hd(kwn)[:, -nw:], kv(5)[:, -nw:], conv_new, s)


def _even_sample(xn, w_in, w_out, conv_w, a_log, dt_bias, gnorm, qk_gain, w_ck, w_cv, tab,
                 pool_kc, pool_vc, pool_ks, pool_vs, buf_kw, buf_vw, conv_st, s_st, page_table):
    gdn_parts, (q, kc, vc, ks, vs, kw, vw, ng) = _even_inputs(xn, w_in, qk_gain)
    ya, conv_new, s = _gdn(gdn_parts, conv_w, a_log, dt_bias, gnorm, conv_st, s_st)
    o3, kw_buf, vw_buf = _nsa_sample(q, kc, vc, ks, vs, kw, vw, pool_kc, pool_vc, pool_ks, pool_vs,
                                     buf_kw, buf_vw, page_table, w_ck, w_cv, qk_gain[1], tab)
    y = _pmm(jnp.concatenate([ya, _nsa_combine(o3, ng)], axis=-1), w_out)
    return y, (kc, vc, ks, vs, kw_buf, vw_buf, conv_new, s)


def _mlstm_chunked(q, k, v, ig, lf, c0, n0, m0):
    b, t_len, h, _ = q.shape
    c = min(ML_CHUNK, t_len)
    pad = (-t_len) % c
    incl = jnp.tril(jnp.ones((c, c), bool), 0)

    def body(carry, inp):
        cm, nv, m = carry
        qc, kc, vc, ic, fc = inp
        fcum = jnp.cumsum(fc, axis=-1)
        a = ic - fcum
        mt = fcum + jnp.maximum(m[..., None], lax.cummax(a, axis=2))
        dmat = jnp.exp(jnp.where(incl, a[..., None, :] + (fcum - mt)[..., :, None], -jnp.inf))
        dec0 = jnp.exp(fcum + m[..., None] - mt)
        s = jnp.einsum('bhid,bhjd->bhij', qc, kc) * dmat
        num = dec0[..., None] * jnp.einsum('bhid,bhde->bhie', qc, cm) + jnp.einsum('bhij,bhje->bhie', s, vc)
        den = dec0 * jnp.einsum('bhid,bhd->bhi', qc, nv) + s.sum(axis=-1)
        hc = num / jnp.maximum(jnp.abs(den), jnp.exp(-mt))[..., None]
        m_end = mt[..., -1]
        w = jnp.exp(a + (fcum[..., -1] - m_end)[..., None])
        dc = jnp.exp(fcum[..., -1] + m - m_end)
        cm = dc[..., None, None] * cm + jnp.einsum('bhj,bhjd,bhje->bhde', w, kc, vc)
        nv = dc[..., None] * nv + jnp.einsum('bhj,bhjd->bhd', w, kc)
        return (cm, nv, m_end), hc

    f32 = jnp.float32
    (cm, nv, m), hs = lax.scan(body, (c0.astype(f32), n0.astype(f32), m0.astype(f32)),
                               (_chunks(q, c, pad), _chunks(k, c, pad), _chunks(v, c, pad),
                                _chunks(ig, c, pad, NEG), _chunks(lf, c, pad)))
    return _unchunk(hs, t_len).astype(v.dtype), cm.astype(c0.dtype), nv.astype(n0.dtype), m.astype(m0.dtype)


def _odd(xn, w_in, w_out, b_if, mnorm, c0, n0, m0):
    b, t_len, _ = xn.shape
    q, k, v, o, ig, fg = _split(_pmm(xn, w_in), ODD_WIDTHS)
    q = q.reshape(b, t_len, ML_H, ML_DK) * (ML_DK ** -0.5)
    k = k.reshape(b, t_len, ML_H, ML_DK)
    v = v.reshape(b, t_len, ML_H, ML_DV)
    bf = b_if.astype(jnp.float32)
    ig = ig.astype(jnp.float32) + bf[:ML_H]
    lf = jax.nn.log_sigmoid(fg.astype(jnp.float32) + bf[ML_H:])
    hc, cm, nv, m = _mlstm_chunked(q, k, v, ig, lf, c0, n0, m0)
    hc = _rms(hc, mnorm.reshape(ML_H, ML_DV)).reshape(b, t_len, ODD_MIX)
    return _pmm(hc * jax.nn.sigmoid(o), w_out), (cm, nv, m)


def kernel(x_prompt, x_sample, cache_kc, cache_vc, cache_ks, cache_vs, cache_kw, cache_vw,
           state_gdn_conv, state_gdn_s, state_ml_c, state_ml_n, state_ml_m, page_table,
           rel_bias, norm_mix, norm_mlp, w_in_even, w_out_even, gdn_conv_w, gdn_a_log,
           gdn_dt_bias, gdn_norm_w, nsa_qk_gain, nsa_w_ck, nsa_w_cv, w_in_odd, w_out_odd,
           ml_b_if, ml_norm_w, w_ff1, w_ff2):
    hp, hs = x_prompt, x_sample
    bias_tiles = _bias_tiles(rel_bias)
    ev_p, ev_s, od_p, od_s = [], [], [], []
    for l in range(DEPTH):
        j = l // 2
        xp = _rms(hp, norm_mix[l])
        xs = _rms(hs, norm_mix[l])
        if l % 2 == 0:
            yp, stp = _even_prompt(xp, w_in_even[j], w_out_even[j], gdn_conv_w[j], gdn_a_log[j],
                                   gdn_dt_bias[j], gdn_norm_w[j], nsa_qk_gain[j], nsa_w_ck[j],
                                   nsa_w_cv[j], rel_bias, bias_tiles)
            ys, sts = _even_sample(xs, w_in_even[j], w_out_even[j], gdn_conv_w[j], gdn_a_log[j],
                                   gdn_dt_bias[j], gdn_norm_w[j], nsa_qk_gain[j], nsa_w_ck[j],
                                   nsa_w_cv[j], rel_bias, cache_kc[j], cache_vc[j], cache_ks[j],
                                   cache_vs[j], cache_kw[j], cache_vw[j], state_gdn_conv[j],
                                   state_gdn_s[j], page_table)
            ev_p.append(stp)
            ev_s.append(sts)
        else:
            bp = hp.shape[0]
            c0 = jnp.zeros((bp, ML_H, ML_DK, ML_DV), hp.dtype)
            n0 = jnp.zeros((bp, ML_H, ML_DK), hp.dtype)
            m0 = jnp.zeros((bp, ML_H), hp.dtype)
            yp, stp = _odd(xp, w_in_odd[j], w_out_odd[j], ml_b_if[j], ml_norm_w[j], c0, n0, m0)
            ys, sts = _odd(xs, w_in_odd[j], w_out_odd[j], ml_b_if[j], ml_norm_w[j],
                           state_ml_c[j], state_ml_n[j], state_ml_m[j])
            od_p.append(stp)
            od_s.append(sts)
        hp = hp + yp
        hs = hs + ys
        hp = hp + _mlp(_rms(hp, norm_mlp[l]), w_ff1[l], w_ff2[l])
        hs = hs + _mlp(_rms(hs, norm_mlp[l]), w_ff1[l], w_ff2[l])

    kc_p, vc_p, ks_p, vs_p, kw_p, vw_p, conv_p, gdn_p = [jnp.stack(a) for a in zip(*ev_p)]
    kc_s, vc_s, ks_s, vs_s, kw_s, vw_s, conv_s, gdn_s = [jnp.stack(a) for a in zip(*ev_s)]
    mlc_p, mln_p, mlm_p = [jnp.stack(a) for a in zip(*od_p)]
    mlc_s, mln_s, mlm_s = [jnp.stack(a) for a in zip(*od_s)]
    return (hp, hs, kc_p, kc_s, vc_p, vc_s, ks_p, ks_s, vs_p, vs_s, kw_p, kw_s, vw_p, vw_s,
            conv_p, conv_s, gdn_p, gdn_s, mlc_p, mlc_s, mln_p, mln_s, mlm_p, mlm_s)
```

```python
import functools
import math

import jax
import jax.numpy as jnp
import numpy as np
from jax import lax
from jax.experimental import pallas as pl
from jax.experimental.pallas import tpu as pltpu

D_MODEL = 2048
BATCH = 2
SEQ = 4096
DEPTH = 4
DEC_BATCH = 8
DEC_SEQ = 1
PAST_LEN = 16384
PAGE_SIZE = 128
RMS_EPS = 1e-6
NEG = -1e30
D_FF = 4 * D_MODEL

GDN_KH = 4
GDN_VH = 8
GDN_DK = 128
GDN_DV = 128
GDN_CONV = 4
GDN_CHUNK = 64
GDN_QK = GDN_KH * GDN_DK
GDN_V = GDN_VH * GDN_DV
GDN_CONV_CH = 2 * GDN_QK + GDN_V

NSA_H = 8
NSA_G = 2
NSA_HG = NSA_H // NSA_G
NSA_D = 128
NSA_BLK = 64
NSA_TOPK = 15
NSA_WIN = 512
NSA_QB = 128
NSA_KV = NSA_G * NSA_D

REL_BUCKETS = 32
REL_EXACT = 16
REL_MAX_DIST = 2048

ML_H = 8
ML_DK = 128
ML_DV = 256
ML_CHUNK = 64

EVEN_WIDTHS = (GDN_QK, GDN_QK, GDN_V, GDN_V, GDN_VH, GDN_VH, NSA_H * NSA_D,
               NSA_KV, NSA_KV, NSA_KV, NSA_KV, NSA_KV, NSA_KV, NSA_H * 3)
ODD_WIDTHS = (ML_H * ML_DK, ML_H * ML_DK, ML_H * ML_DV, ML_H * ML_DV, ML_H, ML_H)
ODD_MIX = ML_H * ML_DV

VMEM_LIMIT_BYTES = 56 * 1024 * 1024
MXU_DTYPE = jnp.bfloat16


def _mm_kernel(x_ref, w_ref, *rest, nk, act, has_res):
    if has_res:
        r_ref, o_ref, acc_ref = rest
    else:
        o_ref, acc_ref = rest
        r_ref = None
    k = pl.program_id(2)

    @pl.when(k == 0)
    def _():
        acc_ref[...] = jnp.zeros_like(acc_ref)

    acc_ref[...] += jnp.dot(x_ref[...], w_ref[...], preferred_element_type=jnp.float32)

    @pl.when(k == nk - 1)
    def _():
        y = acc_ref[...]
        if act == 'relu2':
            y = jnp.square(jnp.maximum(y, 0.0))
        if has_res:
            y = y + r_ref[...]
        o_ref[...] = y.astype(o_ref.dtype)


def _pick(n, prefs):
    for p in prefs:
        if n % p == 0:
            return p
    return n


def _matmul(x, w, *, act=None, res=None, out_dtype=jnp.float32):
    m, kdim = x.shape
    _, n = w.shape
    tm = _pick(m, (512, 256, 128))
    tn = _pick(n, (512, 896, 384, 256, 128))
    tk = _pick(kdim, (2048, 1024, 512))
    nk = kdim // tk
    in_specs = [pl.BlockSpec((tm, tk), lambda i, j, k: (i, k)),
                pl.BlockSpec((tk, tn), lambda i, j, k: (k, j))]
    args = [x, w]
    if res is not None:
        in_specs.append(pl.BlockSpec((tm, tn), lambda i, j, k: (i, j)))
        args.append(res)
    return pl.pallas_call(
        functools.partial(_mm_kernel, nk=nk, act=act, has_res=res is not None),
        grid=(m // tm, n // tn, nk),
        in_specs=in_specs,
        out_specs=pl.BlockSpec((tm, tn), lambda i, j, k: (i, j)),
        out_shape=jax.ShapeDtypeStruct((m, n), out_dtype),
        scratch_shapes=[pltpu.VMEM((tm, tn), jnp.float32)],
        compiler_params=pltpu.CompilerParams(
            dimension_semantics=("parallel", "parallel", "arbitrary"),
            vmem_limit_bytes=VMEM_LIMIT_BYTES),
    )(*args)


def _pmm(x, w):
    lead = x.shape[:-1]
    x2 = x.reshape(-1, x.shape[-1]).astype(MXU_DTYPE)
    n = w.shape[1]
    npad = (-n) % 128
    wb = w.astype(MXU_DTYPE)
    if npad:
        wb = jnp.pad(wb, ((0, 0), (0, npad)))
    y = _matmul(x2, wb)
    if npad:
        y = y[:, :n]
    return y.reshape(*lead, n)


EV_CONV = 0
EV_Z = GDN_CONV_CH
EV_NQ = EV_Z + GDN_V
EV_KV = EV_NQ + NSA_H * NSA_D
EV_SMALL = EV_KV + 6 * NSA_KV
EV_N = EV_SMALL + 128
EV_ORIG_SMALL = 2 * GDN_QK + 2 * GDN_V
EV_ORIG_N = EV_ORIG_SMALL + 2 * GDN_VH + NSA_H * NSA_D + 6 * NSA_KV + 3 * NSA_H


def _permute_even_w(w_in):
    ab = w_in[:, EV_ORIG_SMALL:EV_ORIG_SMALL + 2 * GDN_VH]
    big = w_in[:, EV_ORIG_SMALL + 2 * GDN_VH:EV_ORIG_N - 3 * NSA_H]
    ng = w_in[:, EV_ORIG_N - 3 * NSA_H:]
    pad = jnp.zeros((w_in.shape[0], EV_N - EV_ORIG_N), w_in.dtype)
    return jnp.concatenate([w_in[:, :EV_ORIG_SMALL], big, ab, ng, pad], axis=1).astype(MXU_DTYPE)


def _bucket_thresholds():
    n = np.arange(0, 4 * REL_MAX_DIST)
    nf = np.maximum(n, REL_EXACT).astype(np.float32)
    large = REL_EXACT + (np.log(nf / np.float32(REL_EXACT)) / np.float32(math.log(REL_MAX_DIST / REL_EXACT))
                         * np.float32(REL_BUCKETS - REL_EXACT)).astype(np.int32)
    b = np.where(n < REL_EXACT, n, np.minimum(large, REL_BUCKETS - 1))
    return tuple(int(np.argmax(b >= k)) for k in range(REL_BUCKETS))


BUCKET_THR = _bucket_thresholds()
BIAS_TILES = 14
assert BUCKET_THR[-1] <= (BIAS_TILES - 1) * 128 - 127


def _bias_of_dist(dist, tab_ref, head):
    val = jnp.full(dist.shape, tab_ref[0, head], jnp.float32)
    for k in range(1, REL_BUCKETS):
        val = jnp.where(dist >= BUCKET_THR[k], tab_ref[k, head], val)
    return val


def _bias_tiles_kernel(tab_ref, o_ref):
    h = pl.program_id(0)
    d = pl.program_id(1)
    i = lax.broadcasted_iota(jnp.int32, (128, 128), 0)
    j = lax.broadcasted_iota(jnp.int32, (128, 128), 1)
    dist = d * 128 + i - j
    val = jnp.full(dist.shape, tab_ref[0, h], jnp.float32)
    for k in range(1, REL_BUCKETS):
        val = jnp.where(dist >= BUCKET_THR[k], tab_ref[k, h], val)
    o_ref[0, 0] = val


def _bias_tiles(tab):
    return pl.pallas_call(
        _bias_tiles_kernel,
        grid=(NSA_H, BIAS_TILES),
        in_specs=[pl.BlockSpec(memory_space=pltpu.SMEM)],
        out_specs=pl.BlockSpec((1, 1, 128, 128), lambda h, d: (h, d, 0, 0)),
        out_shape=jax.ShapeDtypeStruct((NSA_H, BIAS_TILES, 128, 128), jnp.float32),
    )(tab.astype(jnp.float32))


def _group_rms(x, gain_row, scale=1.0):
    ms = jnp.mean(x * x, axis=-1, keepdims=True)
    y = x * lax.rsqrt(ms + RMS_EPS) * gain_row
    return y * scale if scale != 1.0 else y


def _nsa_prep_kernel(nq_ref, kc_ref, vc_ref, ks_ref, vs_ref, kw_ref, vw_ref, gain_ref, wck_ref, wcv_ref,
                     qn_ref, ksn_ref, kwn_ref, ksb_ref, vsb_ref, kwb_ref, vwb_ref, kcb_ref, vcb_ref):
    tm = nq_ref.shape[0]
    g0 = gain_ref[0:1, :]
    g1 = gain_ref[1:2, :]
    g2 = gain_ref[2:3, :]
    g3 = gain_ref[3:4, :]
    for h in range(NSA_H):
        sl = slice(h * NSA_D, (h + 1) * NSA_D)
        qn_ref[:, sl] = _group_rms(nq_ref[:, sl], g0, NSA_D ** -0.5).astype(qn_ref.dtype)
    for g in range(NSA_G):
        sl = slice(g * NSA_D, (g + 1) * NSA_D)
        ksn = _group_rms(ks_ref[:, sl], g2)
        kwn = _group_rms(kw_ref[:, sl], g3)
        ksn_ref[:, sl] = ksn
        kwn_ref[:, sl] = kwn
        ksb_ref[:, sl] = ksn.astype(ksb_ref.dtype)
        kwb_ref[:, sl] = kwn.astype(kwb_ref.dtype)
    vsb_ref[...] = vs_ref[...].astype(vsb_ref.dtype)
    vwb_ref[...] = vw_ref[...].astype(vwb_ref.dtype)
    nblk = tm // NSA_BLK
    kc3 = kc_ref[...].reshape(nblk, NSA_BLK, NSA_KV)
    vc3 = vc_ref[...].reshape(nblk, NSA_BLK, NSA_KV)
    kcb = jnp.sum(kc3 * wck_ref[...][None], axis=1)
    vcb = jnp.sum(vc3 * wcv_ref[...][None], axis=1)
    for g in range(NSA_G):
        sl = slice(g * NSA_D, (g + 1) * NSA_D)
        kcb_ref[:, sl] = _group_rms(kcb[:, sl], g1)
    vcb_ref[...] = vcb


def _nsa_prep(proj, gain, w_ck, w_cv):
    m = proj.shape[0]
    tm = 512
    kvb = EV_KV // NSA_KV
    f32, bf16 = jnp.float32, MXU_DTYPE

    def kv_spec(i):
        return pl.BlockSpec((tm, NSA_KV), lambda r, i=i: (r, kvb + i))

    row = lambda w: pl.BlockSpec((tm, w), lambda r: (r, 0))
    full = lambda a: pl.BlockSpec(a.shape, lambda r: (0,) * a.ndim)
    wck = w_ck.reshape(NSA_BLK, NSA_KV)
    wcv = w_cv.reshape(NSA_BLK, NSA_KV)
    outs = pl.pallas_call(
        _nsa_prep_kernel,
        grid=(m // tm,),
        in_specs=[pl.BlockSpec((tm, NSA_H * NSA_D), lambda r: (r, EV_NQ // (NSA_H * NSA_D)))]
                 + [kv_spec(i) for i in range(6)] + [full(gain), full(wck), full(wcv)],
        out_specs=[row(NSA_H * NSA_D)] + [row(NSA_KV)] * 6
                  + [pl.BlockSpec((tm // NSA_BLK, NSA_KV), lambda r: (r, 0))] * 2,
        out_shape=[jax.ShapeDtypeStruct((m, NSA_H * NSA_D), bf16),
                   jax.ShapeDtypeStruct((m, NSA_KV), f32), jax.ShapeDtypeStruct((m, NSA_KV), f32)]
                  + [jax.ShapeDtypeStruct((m, NSA_KV), bf16)] * 4
                  + [jax.ShapeDtypeStruct((m // NSA_BLK, NSA_KV), f32)] * 2,
        compiler_params=pltpu.CompilerParams(dimension_semantics=("parallel",),
                                             vmem_limit_bytes=VMEM_LIMIT_BYTES),
    )(proj, proj, proj, proj, proj, proj, proj, gain, wck, wcv)
    return outs


NSA_TK = 256


def _softmax_rows(lg):
    m = jnp.max(lg, axis=-1, keepdims=True)
    e = jnp.exp(lg - m)
    return e / jnp.sum(e, axis=-1, keepdims=True)


def _pad_rows(x, rows):
    if x.shape[0] == rows:
        return x
    return jnp.concatenate([x, jnp.zeros((rows - x.shape[0],) + x.shape[1:], x.dtype)], axis=0)


def _nsa_attn_kernel(tab_ref, q_ref, kcb_ref, vcb_ref, ks_ref, vs_ref, kw_ref, vw_ref, bias_ref, ng_ref,
                     o_ref, m_scr, l_scr, acc_scr):
    f32, bf16 = jnp.float32, MXU_DTYPE
    qb = pl.program_id(1)
    nb = kcb_ref.shape[0]
    qn = NSA_QB
    gates = jax.nn.sigmoid(ng_ref[...])
    tsub = NSA_TK // qn
    shift = NSA_BLK.bit_length() - 1

    for g in range(NSA_G):
        gl = slice(g * NSA_D, (g + 1) * NSA_D)
        q2 = jnp.concatenate([q_ref[:, (g * NSA_HG + h) * NSA_D:(g * NSA_HG + h + 1) * NSA_D]
                              for h in range(NSA_HG)], axis=0)

        kcb = kcb_ref[:, gl].astype(bf16)
        vcb = _pad_rows(vcb_ref[:, gl], qn).astype(bf16)
        lgt = lax.dot_general(kcb, q2, (((1,), (1,)), ((), ())), preferred_element_type=f32)
        blk = lax.broadcasted_iota(jnp.int32, (nb, qn), 0)
        blkf = blk.astype(f32)
        qpos = qb * qn + lax.broadcasted_iota(jnp.int32, (nb, qn), 1)
        dist_c = qpos - (blk * NSA_BLK + NSA_BLK - 1)
        ok_c = dist_c >= 0
        score = jnp.zeros((nb, qn), f32)
        o_c = []
        for h in range(NSA_HG):
            lg = lgt[:, h * qn:(h + 1) * qn] + _bias_of_dist(dist_c, tab_ref, g * NSA_HG + h)
            lg = jnp.where(ok_c, lg, NEG)
            mx = jnp.max(lg, axis=0, keepdims=True)
            e = jnp.exp(lg - mx)
            p = jnp.where(ok_c, e / jnp.sum(e, axis=0, keepdims=True), 0.0)
            score = score + p
            p_t = _pad_rows(p, qn).T.astype(bf16)
            o_c.append(jnp.dot(p_t, vcb, preferred_element_type=f32))
        cur = lax.shift_right_logical(qpos, shift)
        score = jnp.where(blk < cur, score, -1.0)

        sel = jnp.zeros((nb, qn), f32)
        for _ in range(min(NSA_TOPK, nb)):
            mx = jnp.max(score, axis=0, keepdims=True)
            first = jnp.min(jnp.where(score == mx, blkf, float(nb)), axis=0, keepdims=True)
            pick = blkf == first
            sel = jnp.where(pick, jnp.where(mx >= 0.0, 1.0, 0.0), sel)
            score = jnp.where(pick, -2.0, score)
        sel = jnp.where(blk == cur, 1.0, sel)
        sel_t = _pad_rows(sel, qn).T.astype(bf16)

        m_scr[...] = jnp.full(m_scr.shape, NEG, f32)
        l_scr[...] = jnp.zeros(l_scr.shape, f32)
        acc_scr[...] = jnp.zeros(acc_scr.shape, f32)
        n_tiles = qb // tsub + 1
        row_tok = qb * qn + lax.broadcasted_iota(jnp.int32, (qn, NSA_TK), 0)
        col = lax.broadcasted_iota(jnp.int32, (qn, NSA_TK), 1)
        eblk = lax.broadcasted_iota(jnp.int32, (qn, NSA_TK), 0)

        def body(i, carry, g=g, gl=gl, q2=q2, sel_t=sel_t):
            kt = n_tiles - 1 - i
            k0 = pl.multiple_of(kt * NSA_TK, NSA_TK)
            k = ks_ref[pl.ds(k0, NSA_TK), gl]
            v = vs_ref[pl.ds(k0, NSA_TK), gl]
            lg = lax.dot_general(q2, k, (((1,), (1,)), ((), ())), preferred_element_type=f32)
            expand = jnp.where(eblk == lax.shift_right_logical(k0 + col, shift), 1.0, 0.0).astype(bf16)
            member = jnp.dot(sel_t, expand, preferred_element_type=f32)
            member = jnp.where(k0 + col <= row_tok, member, 0.0)
            keep = member > 0.5
            parts = []
            for h in range(NSA_HG):
                tiles = [bias_ref[g * NSA_HG + h, jnp.clip(qb - (kt * tsub + j), 0, BIAS_TILES - 1)]
                         for j in range(tsub)]
                bias = jnp.concatenate(tiles, axis=1)
                parts.append(jnp.where(keep, lg[h * qn:(h + 1) * qn] + bias, NEG))
            lgm = jnp.concatenate(parts, axis=0)
            m_old = m_scr[...]
            m_new = jnp.maximum(m_old, jnp.max(lgm, axis=-1, keepdims=True))
            p = jnp.exp(lgm - m_new)
            alpha = jnp.exp(m_old - m_new)
            l_scr[...] = alpha * l_scr[...] + jnp.sum(p, axis=-1, keepdims=True)
            acc_scr[...] = alpha * acc_scr[...] + jnp.dot(p.astype(bf16), v, preferred_element_type=f32)
            m_scr[...] = m_new
            return carry

        lax.fori_loop(0, n_tiles, body, 0)
        o_s = acc_scr[...] / l_scr[...]

        nwin = NSA_WIN // qn + 1
        wi = lax.broadcasted_iota(jnp.int32, (qn, qn), 0)
        wj = lax.broadcasted_iota(jnp.int32, (qn, qn), 1)
        lgs, vws = [], []
        for d in range(nwin):
            sub = qb - d
            k0 = pl.multiple_of(jnp.maximum(sub, 0) * qn, qn)
            k = kw_ref[pl.ds(k0, qn), gl]
            vws.append(vw_ref[pl.ds(k0, qn), gl])
            lg = lax.dot_general(q2, k, (((1,), (1,)), ((), ())), preferred_element_type=f32)
            parts = []
            for h in range(NSA_HG):
                x = lg[h * qn:(h + 1) * qn] + bias_ref[g * NSA_HG + h, d]
                if d == 0:
                    x = jnp.where(wj <= wi, x, NEG)
                else:
                    if d == nwin - 1:
                        x = jnp.where(wj > wi, x, NEG)
                    x = jnp.where(sub >= 0, x, NEG)
                parts.append(x)
            lgs.append(jnp.concatenate(parts, axis=0))
        p_w = _softmax_rows(jnp.concatenate(lgs, axis=1))
        o_w = jnp.dot(p_w.astype(bf16), jnp.concatenate(vws, axis=0), preferred_element_type=f32)

        for h in range(NSA_HG):
            c = 2 * GDN_VH + (g * NSA_HG + h) * 3
            rs = slice(h * qn, (h + 1) * qn)
            y = (o_c[h] * gates[:, c:c + 1] + o_s[rs] * gates[:, c + 1:c + 2] + o_w[rs] * gates[:, c + 2:c + 3])
            o_ref[:, (g * NSA_HG + h) * NSA_D:(g * NSA_HG + h + 1) * NSA_D] = y.astype(o_ref.dtype)


def _nsa_prompt_attn(tab, bias_tiles, proj, qn, kcb, vcb, ksb, vsb, kwb, vwb, batch, t_len):
    nq = t_len // NSA_QB
    nb = t_len // NSA_BLK
    hd = NSA_H * NSA_D
    per_b = lambda w: pl.BlockSpec((t_len, w), lambda b, i: (b, 0))
    return pl.pallas_call(
        _nsa_attn_kernel,
        grid=(batch, nq),
        in_specs=[pl.BlockSpec(memory_space=pltpu.SMEM),
                  pl.BlockSpec((NSA_QB, hd), lambda b, i: (b * nq + i, 0)),
                  pl.BlockSpec((nb, NSA_KV), lambda b, i: (b, 0)),
                  pl.BlockSpec((nb, NSA_KV), lambda b, i: (b, 0)),
                  per_b(NSA_KV), per_b(NSA_KV), per_b(NSA_KV), per_b(NSA_KV),
                  pl.BlockSpec(bias_tiles.shape, lambda b, i: (0, 0, 0, 0)),
                  pl.BlockSpec((NSA_QB, 128), lambda b, i: (b * nq + i, EV_SMALL // 128 + 0))],
        out_specs=pl.BlockSpec((NSA_QB, hd), lambda b, i: (b * nq + i, 0)),
        out_shape=jax.ShapeDtypeStruct((batch * t_len, hd), MXU_DTYPE),
        scratch_shapes=[pltpu.VMEM((NSA_HG * NSA_QB, 1), jnp.float32),
                        pltpu.VMEM((NSA_HG * NSA_QB, 1), jnp.float32),
                        pltpu.VMEM((NSA_HG * NSA_QB, NSA_D), jnp.float32)],
        compiler_params=pltpu.CompilerParams(dimension_semantics=("parallel", "arbitrary"),
                                             vmem_limit_bytes=VMEM_LIMIT_BYTES),
    )(tab.astype(jnp.float32), qn, kcb, vcb, ksb, vsb, kwb, vwb, bias_tiles, proj)


def _hdot(a, b):
    return jnp.dot(a, b, precision=lax.Precision.HIGHEST, preferred_element_type=jnp.float32)


def _mdot(a, b):
    return jnp.dot(a.astype(MXU_DTYPE), b.astype(MXU_DTYPE), preferred_element_type=jnp.float32)


def _mdot_nt(a, b):
    return lax.dot_general(a.astype(MXU_DTYPE), b.astype(MXU_DTYPE), (((1,), (1,)), ((), ())),
                           preferred_element_type=jnp.float32)


def _mdot_tn(a, b):
    return lax.dot_general(a.astype(MXU_DTYPE), b.astype(MXU_DTYPE), (((0,), (0,)), ((), ())),
                           preferred_element_type=jnp.float32)


def _shift_rows(x, prev, s):
    xs = pltpu.roll(x, s, axis=0)
    ps = pltpu.roll(prev, s, axis=0)
    row8 = lax.broadcasted_iota(jnp.int32, prev.shape, 0)
    head = jnp.where(row8 < s, ps, xs[0:8])
    return jnp.concatenate([head, xs[8:]], axis=0)


def _gdn_prep_kernel(x_ref, prev_ref, first_ref, ab_ref, cw_ref, alog_ref, dtb_ref,
                     q_ref, k_ref, v_ref, gb_ref, *, tiles_per_batch, t_valid):
    r = pl.program_id(0)
    tm = x_ref.shape[0]
    tile = r % tiles_per_batch
    prev = jnp.where(tile == 0, first_ref[0], prev_ref[...])
    x = x_ref[...]
    acc = x * cw_ref[GDN_CONV - 1:GDN_CONV, :]
    for s in range(1, GDN_CONV):
        acc = acc + _shift_rows(x, prev, s) * cw_ref[GDN_CONV - 1 - s:GDN_CONV - s, :]
    c = acc * jax.nn.sigmoid(acc)

    def l2(t):
        return t * lax.rsqrt(jnp.sum(t * t, axis=-1, keepdims=True) + 1e-6)

    for h in range(GDN_KH):
        sl = slice(h * GDN_DK, (h + 1) * GDN_DK)
        q_ref[:, sl] = l2(c[:, sl]) * (GDN_DK ** -0.5)
        k_ref[:, sl] = l2(c[:, GDN_QK + h * GDN_DK:GDN_QK + (h + 1) * GDN_DK])
    v_ref[...] = c[:, 2 * GDN_QK:]
    ab = ab_ref[...]
    z = ab + dtb_ref[...]
    softplus = jnp.maximum(z, 0.0) + jnp.log(1.0 + jnp.exp(-jnp.abs(z)))
    gate = -jnp.exp(alog_ref[...]) * softplus
    lane = lax.broadcasted_iota(jnp.int32, ab.shape, 1)
    pos = tile * tm + lax.broadcasted_iota(jnp.int32, ab.shape, 0)
    gb = jnp.where(lane < GDN_VH, gate, jnp.where(lane < 2 * GDN_VH, jax.nn.sigmoid(ab), 0.0))
    gb_ref[...] = jnp.where(pos < t_valid, gb, 0.0)


def _gdn_prep(xin, first, ab, ab_col, conv_w, a_log, dt_bias, batch, t_pad, t_valid):
    m = batch * t_pad
    tm = _pick(t_pad, (256, 128, 64))
    tpb = t_pad // tm
    f32 = jnp.float32
    lanes = lambda v: jnp.zeros((1, 128), f32).at[0, :GDN_VH].set(v.astype(f32))
    return pl.pallas_call(
        functools.partial(_gdn_prep_kernel, tiles_per_batch=tpb, t_valid=t_valid),
        grid=(m // tm,),
        in_specs=[pl.BlockSpec((tm, GDN_CONV_CH), lambda r: (r, 0)),
                  pl.BlockSpec((8, GDN_CONV_CH), lambda r: (jnp.maximum(r * (tm // 8) - 1, 0), 0)),
                  pl.BlockSpec((1, 8, GDN_CONV_CH), lambda r: (r // tpb, 0, 0)),
                  pl.BlockSpec((tm, 128), lambda r: (r, ab_col)),
                  pl.BlockSpec((GDN_CONV, GDN_CONV_CH), lambda r: (0, 0)),
                  pl.BlockSpec((1, 128), lambda r: (0, 0)),
                  pl.BlockSpec((1, 128), lambda r: (0, 0))],
        out_specs=[pl.BlockSpec((tm, GDN_QK), lambda r: (r, 0)),
                   pl.BlockSpec((tm, GDN_QK), lambda r: (r, 0)),
                   pl.BlockSpec((tm, GDN_V), lambda r: (r, 0)),
                   pl.BlockSpec((tm, 128), lambda r: (r, 0))],
        out_shape=[jax.ShapeDtypeStruct((m, GDN_QK), f32), jax.ShapeDtypeStruct((m, GDN_QK), f32),
                   jax.ShapeDtypeStruct((m, GDN_V), f32), jax.ShapeDtypeStruct((m, 128), f32)],
        compiler_params=pltpu.CompilerParams(dimension_semantics=("arbitrary",),
                                             vmem_limit_bytes=VMEM_LIMIT_BYTES),
    )(xin, xin, first, ab, conv_w.astype(f32), lanes(a_log), lanes(dt_bias))


def _cumsum_rows(x):
    row = lax.broadcasted_iota(jnp.int32, x.shape, 0)
    s = 1
    while s < x.shape[0]:
        x = x + jnp.where(row >= s, pltpu.roll(x, s, axis=0), 0.0)
        s *= 2
    return x


def _unit_lower_inverse(lmat, row, col):
    eye = jnp.where(row == col, 1.0, 0.0)
    blk = lax.shift_right_logical(row, 4) == lax.shift_right_logical(col, 4)
    ld = jnp.where(blk, lmat, 0.0)
    x = eye - ld
    p = _hdot(ld, ld)
    x = x + _hdot(x, p)
    p = _hdot(p, p)
    x = x + _hdot(x, p)
    p = _hdot(p, p)
    x = x + _hdot(x, p)
    size = 16
    while size < lmat.shape[0]:
        sh = size.bit_length() - 1
        inner = lax.shift_right_logical(row, sh) == lax.shift_right_logical(col, sh)
        outer = lax.shift_right_logical(row, sh + 1) == lax.shift_right_logical(col, sh + 1)
        coff = jnp.where(outer, jnp.where(inner, 0.0, lmat), 0.0)
        x = x - _hdot(_hdot(x, coff), x)
        size *= 2
    return x


def _gdn_chunk_kernel(q_ref, k_ref, v_ref, gb_ref, z_ref, s0_ref, nw_ref, y_ref, s_ref):
    f32 = jnp.float32
    cn = q_ref.shape[0]

    @pl.when(pl.program_id(1) == 0)
    def _():
        s_ref[...] = s0_ref[...]

    gb = gb_ref[...]
    gcum = _cumsum_rows(gb)
    gcum_t = _pad_rows(gcum, 128).T
    row = lax.broadcasted_iota(jnp.int32, (cn, cn), 0)
    col = lax.broadcasted_iota(jnp.int32, (cn, cn), 1)
    rep = GDN_VH // GDN_KH
    kk, qk = {}, {}
    for h in range(GDN_VH):
        kh = h // rep
        q = q_ref[:, kh * GDN_DK:(kh + 1) * GDN_DK]
        k = k_ref[:, kh * GDN_DK:(kh + 1) * GDN_DK]
        v = v_ref[:, h * GDN_DV:(h + 1) * GDN_DV]
        if kh not in kk:
            kk[kh] = _mdot_nt(k, k)
            qk[kh] = _mdot_nt(q, k)
        gcol = gcum[:, h:h + 1]
        diff = gcol - gcum_t[h:h + 1, :cn]
        eg = jnp.exp(gcol)
        beta = gb[:, GDN_VH + h:GDN_VH + h + 1]
        lmat = beta * kk[kh] * jnp.exp(jnp.where(col < row, diff, -jnp.inf))
        ainv = _unit_lower_inverse(lmat, row, col)
        sol = _hdot(ainv, jnp.concatenate([beta * v, (beta * eg) * k], axis=1))
        s = s_ref[0, h]
        u = sol[:, :GDN_DV] - _mdot(sol[:, GDN_DV:], s)
        att = qk[kh] * jnp.exp(jnp.where(col <= row, diff, -jnp.inf))
        o = eg * _mdot(q, s) + _mdot(att, u)
        gl = gcum[cn - 1:cn, h:h + 1]
        s_ref[0, h] = jnp.exp(gl) * s + _mdot_tn(k * jnp.exp(gl - gcol), u)
        on = o * lax.rsqrt(jnp.mean(o * o, axis=-1, keepdims=True) + RMS_EPS) * nw_ref[...]
        z = z_ref[:, h * GDN_DV:(h + 1) * GDN_DV]
        y_ref[:, h * GDN_DV:(h + 1) * GDN_DV] = (on * (z * jax.nn.sigmoid(z))).astype(y_ref.dtype)


def _gdn_chunks(q, k, v, gb, zsrc, z_col, s0, norm_w, batch, t_pad):
    cn = GDN_CHUNK
    nc = t_pad // cn
    rowblk = lambda w, c=0: pl.BlockSpec((cn, w), lambda b, i, c=c: (b * nc + i, c))
    st = pl.BlockSpec((1, GDN_VH, GDN_DK, GDN_DV), lambda b, i: (b, 0, 0, 0))
    return pl.pallas_call(
        _gdn_chunk_kernel,
        grid=(batch, nc),
        in_specs=[rowblk(GDN_QK), rowblk(GDN_QK), rowblk(GDN_V), rowblk(128), rowblk(GDN_V, z_col), st,
                  pl.BlockSpec((1, GDN_DV), lambda b, i: (0, 0))],
        out_specs=[rowblk(GDN_V), st],
        out_shape=[jax.ShapeDtypeStruct((batch * t_pad, GDN_V), MXU_DTYPE),
                   jax.ShapeDtypeStruct((batch, GDN_VH, GDN_DK, GDN_DV), jnp.float32)],
        compiler_params=pltpu.CompilerParams(dimension_semantics=("parallel", "arbitrary"),
                                             vmem_limit_bytes=VMEM_LIMIT_BYTES),
    )(q, k, v, gb, zsrc, s0.astype(jnp.float32), norm_w.reshape(1, GDN_DV).astype(jnp.float32))


OD_Q = 0
OD_K = ML_H * ML_DK
OD_V = 2 * ML_H * ML_DK
OD_O = OD_V + ML_H * ML_DV
OD_SMALL = OD_O + ML_H * ML_DV
OD_N = OD_SMALL + 128


def _cummax_rows(x):
    row = lax.broadcasted_iota(jnp.int32, x.shape, 0)
    s = 1
    while s < x.shape[0]:
        x = jnp.maximum(x, jnp.where(row >= s, pltpu.roll(x, s, axis=0), -jnp.inf))
        s *= 2
    return x


def _round_mxu(x):
    return x.astype(MXU_DTYPE).astype(jnp.float32)


def _mlstm_chunk_kernel(q_ref, k_ref, v_ref, og_ref, if_ref, bif_ref, nw_ref, c0_ref, n0_ref, m0_ref,
                        y_ref, c_ref, n_ref, m_ref, *, t_valid):
    f32 = jnp.float32
    cn = q_ref.shape[0]
    ci = pl.program_id(1)

    @pl.when(ci == 0)
    def _():
        c_ref[...] = c0_ref[...]
        n_ref[...] = n0_ref[...]
        m_ref[...] = m0_ref[...]

    pre = if_ref[...] + bif_ref[...]
    pos = ci * cn + lax.broadcasted_iota(jnp.int32, pre.shape, 0)
    live = pos < t_valid
    lf = jnp.where(live, jnp.minimum(pre, 0.0) - jnp.log(1.0 + jnp.exp(-jnp.abs(pre))), 0.0)
    fcum = _cumsum_rows(lf)
    a_all = jnp.where(live, pre, NEG) - pltpu.roll(fcum, 128 - ML_H, axis=1)
    amax = _cummax_rows(a_all)
    a_t = _pad_rows(a_all, 128).T
    row = lax.broadcasted_iota(jnp.int32, (cn, cn), 0)
    col = lax.broadcasted_iota(jnp.int32, (cn, cn), 1)
    m_all = m_ref[0]
    m_new = []
    for h in range(ML_H):
        q = q_ref[:, h * ML_DK:(h + 1) * ML_DK] * (ML_DK ** -0.5)
        k = k_ref[:, h * ML_DK:(h + 1) * ML_DK]
        v = v_ref[:, h * ML_DV:(h + 1) * ML_DV]
        m_prev = m_all[:, h:h + 1]
        fc = fcum[:, ML_H + h:ML_H + h + 1]
        a_col = a_all[:, h:h + 1]
        mt = fc + jnp.maximum(m_prev, amax[:, h:h + 1])
        dmat = jnp.exp(jnp.where(col <= row, a_t[h:h + 1, :cn] + (fc - mt), -jnp.inf))
        dec0 = jnp.exp(fc + m_prev - mt)
        s = _mdot_nt(q, k) * dmat
        cm = c_ref[0, h]
        nv = n_ref[0, h:h + 1, :]
        num = dec0 * _mdot(q, cm) + _mdot(s, v)
        den = dec0 * jnp.sum(_round_mxu(q) * _round_mxu(nv), axis=-1, keepdims=True) \
            + jnp.sum(s, axis=-1, keepdims=True)
        hc = num / jnp.maximum(jnp.abs(den), jnp.exp(-mt))
        f_end = fc[cn - 1:cn]
        m_end = mt[cn - 1:cn]
        w = jnp.exp(a_col + (f_end - m_end))
        dc = jnp.exp(f_end + m_prev - m_end)
        c_ref[0, h] = dc * cm + _mdot_tn(w * k, v)
        n_ref[0, h:h + 1, :] = dc * nv + jnp.sum(_round_mxu(w) * _round_mxu(k), axis=0, keepdims=True)
        m_new.append(m_end)
        hn = hc * lax.rsqrt(jnp.mean(hc * hc, axis=-1, keepdims=True) + RMS_EPS) \
            * nw_ref[:, h * ML_DV:(h + 1) * ML_DV]
        og = og_ref[:, h * ML_DV:(h + 1) * ML_DV]
        y_ref[:, h * ML_DV:(h + 1) * ML_DV] = (hn * jax.nn.sigmoid(og)).astype(y_ref.dtype)
    lane = lax.broadcasted_iota(jnp.int32, m_all.shape, 1)
    out = m_all
    for h in range(ML_H):
        out = jnp.where(lane == h, m_new[h], out)
    m_ref[0] = out


def _mlstm_chunks(proj, b_if, mnorm, c0, n0, m0, batch, t_pad, t_valid):
    cn = ML_CHUNK
    nc = t_pad // cn
    f32 = jnp.float32
    blk = lambda w, c: pl.BlockSpec((cn, w), lambda b, i, c=c: (b * nc + i, c))
    const = lambda shape: pl.BlockSpec(shape, lambda b, i: (0,) * len(shape))
    st_c = pl.BlockSpec((1, ML_H, ML_DK, ML_DV), lambda b, i: (b, 0, 0, 0))
    st_n = pl.BlockSpec((1, ML_H, ML_DK), lambda b, i: (b, 0, 0))
    st_m = pl.BlockSpec((1, 1, 128), lambda b, i: (b, 0, 0))
    bif = jnp.zeros((1, 128), f32).at[0, :2 * ML_H].set(b_if.astype(f32))
    m0p = jnp.zeros((batch, 1, 128), f32).at[:, 0, :ML_H].set(m0.astype(f32))
    hq, hv = ML_H * ML_DK, ML_H * ML_DV
    y, c, n, m = pl.pallas_call(
        functools.partial(_mlstm_chunk_kernel, t_valid=t_valid),
        grid=(batch, nc),
        in_specs=[blk(hq, OD_Q // hq), blk(hq, OD_K // hq), blk(hv, OD_V // hv), blk(hv, OD_O // hv),
                  blk(128, OD_SMALL // 128), const((1, 128)), const((1, hv)), st_c, st_n, st_m],
        out_specs=[blk(hv, 0), st_c, st_n, st_m],
        out_shape=[jax.ShapeDtypeStruct((batch * t_pad, hv), MXU_DTYPE),
                   jax.ShapeDtypeStruct((batch, ML_H, ML_DK, ML_DV), f32),
                   jax.ShapeDtypeStruct((batch, ML_H, ML_DK), f32),
                   jax.ShapeDtypeStruct((batch, 1, 128), f32)],
        compiler_params=pltpu.CompilerParams(dimension_semantics=("parallel", "arbitrary"),
                                             vmem_limit_bytes=VMEM_LIMIT_BYTES),
    )(proj, proj, proj, proj, proj, bif, mnorm.reshape(1, hv).astype(f32), c0.astype(f32), n0.astype(f32), m0p)
    return y, c, n, m[:, 0, :ML_H]


def _permute_odd_w(w_in):
    pad = jnp.zeros((w_in.shape[0], OD_N - w_in.shape[1]), w_in.dtype)
    return jnp.concatenate([w_in, pad], axis=1).astype(MXU_DTYPE)


def _split(x, widths):
    offs = [int(o) for o in np.cumsum(widths)[:-1]]
    return jnp.split(x, offs, axis=-1)


def _rms(x, g):
    xf = x.astype(jnp.float32)
    y = xf * lax.rsqrt(jnp.mean(xf * xf, axis=-1, keepdims=True) + RMS_EPS)
    return (y * g.astype(jnp.float32)).astype(x.dtype)


def _mlp(xn, w1, w2):
    h = jnp.square(jax.nn.relu(_pmm(xn, w1)))
    return _pmm(h, w2)


def _causal_conv(xin, w):
    c = xin.shape[-1]
    return lax.conv_general_dilated(xin, w[:, None, :].astype(xin.dtype), window_strides=(1,),
                                    padding='VALID', dimension_numbers=('NWC', 'WIO', 'NWC'),
                                    feature_group_count=c)


def _chunks(t, c, pad, pad_val=0.0):
    t = t.astype(jnp.float32)
    if pad:
        t = jnp.pad(t, [(0, 0), (0, pad)] + [(0, 0)] * (t.ndim - 2), constant_values=pad_val)
    b, tp = t.shape[:2]
    t = t.reshape(b, tp // c, c, *t.shape[2:])
    return jnp.swapaxes(jnp.moveaxis(t, 1, 0), 2, 3)


def _unchunk(o, t_len):
    n, b, h, c, d = o.shape
    return jnp.transpose(o, (1, 0, 3, 2, 4)).reshape(b, n * c, h, d)[:, :t_len]


def _gdn_chunked(q, k, v, g, beta, s0):
    b, t_len, h, dk = q.shape
    dv = v.shape[-1]
    c = min(GDN_CHUNK, t_len)
    pad = (-t_len) % c
    tri_s = jnp.tril(jnp.ones((c, c), bool), -1)
    tri_i = jnp.tril(jnp.ones((c, c), bool), 0)
    eye = jnp.eye(c, dtype=jnp.float32)

    def body(s, inp):
        qc, kc, vc, gc, bc = inp
        gcum = jnp.cumsum(gc, axis=-1)
        diff = gcum[..., :, None] - gcum[..., None, :]
        eg = jnp.exp(gcum)
        lmat = bc[..., :, None] * jnp.einsum('bhid,bhjd->bhij', kc, kc) * jnp.exp(jnp.where(tri_s, diff, -jnp.inf))
        rhs = jnp.concatenate([bc[..., None] * vc, (bc * eg)[..., None] * kc], axis=-1)
        sol = lax.linalg.triangular_solve(eye + lmat, rhs, left_side=True, lower=True)
        u = sol[..., :dv] - jnp.einsum('bhcd,bhde->bhce', sol[..., dv:], s)
        att = jnp.einsum('bhid,bhjd->bhij', qc, kc) * jnp.exp(jnp.where(tri_i, diff, -jnp.inf))
        o = eg[..., None] * jnp.einsum('bhcd,bhde->bhce', qc, s) + jnp.einsum('bhij,bhje->bhie', att, u)
        gl = gcum[..., -1]
        s = jnp.exp(gl)[..., None, None] * s + jnp.einsum('bhcd,bhce->bhde', kc * jnp.exp(gl[..., None] - gcum)[..., None], u)
        return s, o

    s, o = lax.scan(body, s0.astype(jnp.float32),
                    (_chunks(q, c, pad), _chunks(k, c, pad), _chunks(v, c, pad), _chunks(g, c, pad), _chunks(beta, c, pad)))
    return _unchunk(o, t_len).astype(v.dtype), s.astype(s0.dtype)


def _gdn(parts, conv_w, a_log, dt_bias, norm_w, conv_prev, s0):
    q, k, v, z, a, bgate = parts
    b, t_len = q.shape[:2]
    xin = jnp.concatenate([conv_prev.astype(q.dtype), jnp.concatenate([q, k, v], axis=-1)], axis=1)
    c = jax.nn.silu(_causal_conv(xin, conv_w))
    cq, ck, cv = _split(c, (GDN_QK, GDN_QK, GDN_V))

    def l2(t):
        t = t.reshape(b, t_len, GDN_KH, GDN_DK).astype(jnp.float32)
        return t * lax.rsqrt(jnp.sum(t * t, axis=-1, keepdims=True) + 1e-6)

    rep = GDN_VH // GDN_KH
    qh = jnp.repeat(l2(cq) * (GDN_DK ** -0.5), rep, axis=2)
    kh = jnp.repeat(l2(ck), rep, axis=2)
    vh = cv.reshape(b, t_len, GDN_VH, GDN_DV)
    beta = jax.nn.sigmoid(bgate.astype(jnp.float32))
    g = -jnp.exp(a_log.astype(jnp.float32)) * jax.nn.softplus(a.astype(jnp.float32) + dt_bias.astype(jnp.float32))
    o, s = _gdn_chunked(qh, kh, vh, g, beta, s0)
    o = _rms(o, norm_w) * jax.nn.silu(z.reshape(b, t_len, GDN_VH, GDN_DV))
    return o.reshape(b, t_len, GDN_V), xin[:, -(GDN_CONV - 1):], s


def _bucket(dist):
    n = jnp.maximum(dist, 0)
    nf = jnp.maximum(n, REL_EXACT).astype(jnp.float32)
    large = REL_EXACT + (jnp.log(nf / REL_EXACT) / math.log(REL_MAX_DIST / REL_EXACT)
                         * (REL_BUCKETS - REL_EXACT)).astype(jnp.int32)
    large = jnp.minimum(large, REL_BUCKETS - 1)
    return jnp.where(n < REL_EXACT, n, large)


def _tok_bias(dist, tab):
    t_len, n = dist.shape
    bias = tab.astype(jnp.float32)[_bucket(dist)]
    return bias.reshape(t_len, n, NSA_G, NSA_HG).transpose(0, 2, 3, 1)


def _compress(rows, w):
    b, t_len = rows.shape[:2]
    r = rows.reshape(b, t_len // NSA_BLK, NSA_BLK, NSA_G, NSA_D)
    return jnp.einsum('bnlgd,lgd->bngd', r, w)


def _nsa_core(q, qpos, kc_b, vc_b, fetch, kw, vw, kwpos, tab):
    b, t_len = q.shape[:2]
    f32 = jnp.float32
    nb = kc_b.shape[1]
    blk = jnp.arange(nb, dtype=jnp.int32)
    dist_c = qpos[:, None] - (blk * NSA_BLK + NSA_BLK - 1)[None, :]
    ok_c = (dist_c >= 0)[:, None, None, :]
    lg_c = jnp.einsum('btghd,bngd->btghn', q, kc_b).astype(f32) + _tok_bias(dist_c, tab)
    p_c = jax.nn.softmax(jnp.where(ok_c, lg_c, NEG), axis=-1) * ok_c
    o_c = jnp.einsum('btghn,bngd->btghd', p_c.astype(vc_b.dtype), vc_b)
    cur = qpos // NSA_BLK
    cand = (blk[None, :] < cur[:, None])[None, :, None, :]
    score = jnp.where(cand, p_c.sum(axis=3), -1.0)
    _, idx = lax.top_k(score, min(NSA_TOPK, nb))
    cur_b = jnp.broadcast_to(cur[None, :, None, None], (b, t_len, NSA_G, 1))
    idx_all = jnp.concatenate([idx, cur_b], axis=-1)
    ok_all = jnp.concatenate([idx < cur[None, :, None, None], jnp.ones((b, t_len, NSA_G, 1), bool)], axis=-1)
    pos5 = idx_all[..., None] * NSA_BLK + jnp.arange(NSA_BLK, dtype=jnp.int32)
    ok5 = ok_all[..., None] & (pos5 <= qpos[None, :, None, None, None])
    pos = pos5.reshape(b, t_len, NSA_G, -1)
    ok_s = ok5.reshape(b, t_len, NSA_G, 1, -1)
    k_s, v_s = fetch(pos)
    tab3 = tab.astype(f32).reshape(REL_BUCKETS, NSA_G, NSA_HG)
    bias_s = jnp.moveaxis(tab3[_bucket(qpos[None, :, None, None] - pos), jnp.arange(NSA_G)[:, None]], -1, 3)
    lg_s = jnp.einsum('btghd,btgnd->btghn', q, k_s).astype(f32) + bias_s
    p_s = jax.nn.softmax(jnp.where(ok_s, lg_s, NEG), axis=-1)
    o_s = jnp.einsum('btghn,btgnd->btghd', p_s.astype(v_s.dtype), v_s)
    dist_w = qpos[:, None] - kwpos[None, :]
    ok_w = ((dist_w >= 0) & (dist_w < NSA_WIN) & (kwpos >= 0)[None, :])[:, None, None, :]
    lg_w = jnp.einsum('btghd,bngd->btghn', q, kw).astype(f32) + _tok_bias(dist_w, tab)
    p_w = jax.nn.softmax(jnp.where(ok_w, lg_w, NEG), axis=-1)
    o_w = jnp.einsum('btghn,bngd->btghd', p_w.astype(vw.dtype), vw)
    return jnp.stack([o_c, o_s, o_w], axis=-2)


def _over_query_blocks(fn, q, qpos):
    b, t_len = q.shape[:2]
    if t_len <= NSA_QB:
        return fn(q, qpos)
    nblk = -(-t_len // NSA_QB)
    qb = jnp.swapaxes(q.reshape(b, nblk, NSA_QB, *q.shape[2:]), 0, 1)
    pb = qpos.reshape(nblk, NSA_QB)
    out = lax.map(lambda a: fn(a[0], a[1]), (qb, pb))
    out = jnp.swapaxes(out, 0, 1)
    return out.reshape(b, nblk * NSA_QB, *out.shape[3:])[:, :t_len]


def _nsa_combine(o3, ng):
    b, t_len = ng.shape[:2]
    gates = jax.nn.sigmoid(ng.astype(jnp.float32)).reshape(b, t_len, NSA_G, NSA_HG, 3, 1)
    return (o3.astype(jnp.float32) * gates).sum(axis=-2).reshape(b, t_len, NSA_H * NSA_D).astype(o3.dtype)


def _nsa_prompt(q, kc, vc, ks, vs, kw, vw, w_ck, w_cv, g_kc, tab):
    b, t_len = q.shape[:2]
    kc_b = _rms(_compress(kc, w_ck), g_kc)
    vc_b = _compress(vc, w_cv)
    padw = ((0, 0), (NSA_WIN, 0), (0, 0), (0, 0))
    kw_pad = jnp.pad(kw, padw)
    vw_pad = jnp.pad(vw, padw)
    bi = jnp.arange(b)[:, None, None, None]
    gi = jnp.arange(NSA_G)[None, None, :, None]

    def fetch(pos):
        return ks[bi, pos, gi], vs[bi, pos, gi]

    def block(qb, pb):
        start = pb[0]
        n = NSA_WIN + qb.shape[1]
        kwb = lax.dynamic_slice_in_dim(kw_pad, start, n, axis=1)
        vwb = lax.dynamic_slice_in_dim(vw_pad, start, n, axis=1)
        kwpos = start - NSA_WIN + jnp.arange(n, dtype=jnp.int32)
        return _nsa_core(qb, pb, kc_b, vc_b, fetch, kwb, vwb, kwpos, tab)

    return _over_query_blocks(block, q, jnp.arange(t_len, dtype=jnp.int32))


def _nsa_sample(q, kc, vc, ks, vs, kw, vw, pool_kc, pool_vc, pool_ks, pool_vs, buf_kw, buf_vw,
                page_table, w_ck, w_cv, g_kc, tab):
    db, ds = q.shape[:2]

    def full_rows(pool, new):
        past = pool[page_table].reshape(db, PAST_LEN, NSA_G, NSA_D)
        r = jnp.concatenate([past, new], axis=1)
        pad = (-r.shape[1]) % NSA_BLK
        return jnp.pad(r, ((0, 0), (0, pad), (0, 0), (0, 0)))

    kc_b = _rms(_compress(full_rows(pool_kc, kc), w_ck), g_kc)
    vc_b = _compress(full_rows(pool_vc, vc), w_cv)
    bi = jnp.arange(db)[:, None, None, None]
    gi = jnp.arange(NSA_G)[None, None, :, None]

    def fetch(pos):
        in_past = (pos < PAST_LEN)[..., None]
        pp = jnp.minimum(pos, PAST_LEN - 1)
        phys = page_table[bi, pp // PAGE_SIZE]
        off = pp % PAGE_SIZE
        pn = jnp.clip(pos - PAST_LEN, 0, ds - 1)
        k = jnp.where(in_past, pool_ks[phys, off, gi], ks[bi, pn, gi])
        v = jnp.where(in_past, pool_vs[phys, off, gi], vs[bi, pn, gi])
        return k, v

    wb = buf_kw.shape[1]
    kw_all = jnp.concatenate([buf_kw, kw], axis=1)
    vw_all = jnp.concatenate([buf_vw, vw], axis=1)
    kwpos = PAST_LEN - wb + jnp.arange(wb + ds, dtype=jnp.int32)
    qpos = PAST_LEN + jnp.arange(ds, dtype=jnp.int32)
    o3 = _over_query_blocks(lambda qb, pb: _nsa_core(qb, pb, kc_b, vc_b, fetch, kw_all, vw_all, kwpos, tab), q, qpos)
    return o3, kw_all[:, -wb:], vw_all[:, -wb:]


def _even_inputs(xn, w_in, qk_gain):
    b, t_len, _ = xn.shape
    (gq, gk, gv, gz, ga, gb, nq, kc, vc, ks, vs, kw, vw, ng) = _split(_pmm(xn, w_in), EVEN_WIDTHS)
    hd = lambda t: t.reshape(b, t_len, NSA_G, NSA_D)
    q = _rms(nq.reshape(b, t_len, NSA_G, NSA_HG, NSA_D), qk_gain[0]) * (NSA_D ** -0.5)
    ks = _rms(hd(ks), qk_gain[2])
    kw = _rms(hd(kw), qk_gain[3])
    return (gq, gk, gv, gz, ga, gb), (q, hd(kc), hd(vc), ks, hd(vs), kw, hd(vw), ng)


def _even_prompt(xn, w_in, w_out, conv_w, a_log, dt_bias, gnorm, qk_gain, w_ck, w_cv, tab, bias_tiles):
    b, t_len, d = xn.shape
    proj = _matmul(xn.reshape(b * t_len, d).astype(MXU_DTYPE), _permute_even_w(w_in))
    p3 = proj.reshape(b, t_len, EV_N)
    gq, gk, gv = _split(p3[..., :GDN_CONV_CH], (GDN_QK, GDN_QK, GDN_V))
    gdn_parts = (gq, gk, gv, p3[..., EV_Z:EV_Z + GDN_V], p3[..., EV_SMALL:EV_SMALL + GDN_VH],
                 p3[..., EV_SMALL + GDN_VH:EV_SMALL + 2 * GDN_VH])
    conv0 = jnp.zeros((b, GDN_CONV - 1, GDN_CONV_CH), xn.dtype)
    s0 = jnp.zeros((b, GDN_VH, GDN_DK, GDN_DV), xn.dtype)
    ya, conv_new, s = _gdn(gdn_parts, conv_w, a_log, dt_bias, gnorm, conv0, s0)
    qn, ksn, kwn, ksb, vsb, kwb, vwb, kcb, vcb = _nsa_prep(proj, qk_gain, w_ck, w_cv)
    yb = _nsa_prompt_attn(tab, bias_tiles, proj, qn, kcb, vcb, ksb, vsb, kwb, vwb, b, t_len)
    y = _pmm(jnp.concatenate([ya, yb.reshape(b, t_len, -1).astype(ya.dtype)], axis=-1), w_out)
    nw = min(NSA_WIN, t_len)
    hd = lambda t: t.reshape(b, t_len, NSA_G, NSA_D)
    kv = lambda i: hd(p3[..., EV_KV + i * NSA_KV:EV_KV + (i + 1) * NSA_KV])
    return y, (kv(0), kv(1), hd(ksn), kv(3), hd(kwn)[:, -nw:], kv(5)[:, -nw:], conv_new, s)


def _even_sample(xn, w_in, w_out, conv_w, a_log, dt_bias, gnorm, qk_gain, w_ck, w_cv, tab,
                 pool_kc, pool_vc, pool_ks, pool_vs, buf_kw, buf_vw, conv_st, s_st, page_table):
    gdn_parts, (q, kc, vc, ks, vs, kw, vw, ng) = _even_inputs(xn, w_in, qk_gain)
    ya, conv_new, s = _gdn(gdn_parts, conv_w, a_log, dt_bias, gnorm, conv_st, s_st)
    o3, kw_buf, vw_buf = _nsa_sample(q, kc, vc, ks, vs, kw, vw, pool_kc, pool_vc, pool_ks, pool_vs,
                                     buf_kw, buf_vw, page_table, w_ck, w_cv, qk_gain[1], tab)
    y = _pmm(jnp.concatenate([ya, _nsa_combine(o3, ng)], axis=-1), w_out)
    return y, (kc, vc, ks, vs, kw_buf, vw_buf, conv_new, s)


def _mlstm_chunked(q, k, v, ig, lf, c0, n0, m0):
    b, t_len, h, _ = q.shape
    c = min(ML_CHUNK, t_len)
    pad = (-t_len) % c
    incl = jnp.tril(jnp.ones((c, c), bool), 0)

    def body(carry, inp):
        cm, nv, m = carry
        qc, kc, vc, ic, fc = inp
        fcum = jnp.cumsum(fc, axis=-1)
        a = ic - fcum
        mt = fcum + jnp.maximum(m[..., None], lax.cummax(a, axis=2))
        dmat = jnp.exp(jnp.where(incl, a[..., None, :] + (fcum - mt)[..., :, None], -jnp.inf))
        dec0 = jnp.exp(fcum + m[..., None] - mt)
        s = jnp.einsum('bhid,bhjd->bhij', qc, kc) * dmat
        num = dec0[..., None] * jnp.einsum('bhid,bhde->bhie', qc, cm) + jnp.einsum('bhij,bhje->bhie', s, vc)
        den = dec0 * jnp.einsum('bhid,bhd->bhi', qc, nv) + s.sum(axis=-1)
        hc = num / jnp.maximum(jnp.abs(den), jnp.exp(-mt))[..., None]
        m_end = mt[..., -1]
        w = jnp.exp(a + (fcum[..., -1] - m_end)[..., None])
        dc = jnp.exp(fcum[..., -1] + m - m_end)
        cm = dc[..., None, None] * cm + jnp.einsum('bhj,bhjd,bhje->bhde', w, kc, vc)
        nv = dc[..., None] * nv + jnp.einsum('bhj,bhjd->bhd', w, kc)
        return (cm, nv, m_end), hc

    f32 = jnp.float32
    (cm, nv, m), hs = lax.scan(body, (c0.astype(f32), n0.astype(f32), m0.astype(f32)),
                               (_chunks(q, c, pad), _chunks(k, c, pad), _chunks(v, c, pad),
                                _chunks(ig, c, pad, NEG), _chunks(lf, c, pad)))
    return _unchunk(hs, t_len).astype(v.dtype), cm.astype(c0.dtype), nv.astype(n0.dtype), m.astype(m0.dtype)


def _odd(xn, w_in, w_out, b_if, mnorm, c0, n0, m0):
    b, t_len, _ = xn.shape
    q, k, v, o, ig, fg = _split(_pmm(xn, w_in), ODD_WIDTHS)
    q = q.reshape(b, t_len, ML_H, ML_DK) * (ML_DK ** -0.5)
    k = k.reshape(b, t_len, ML_H, ML_DK)
    v = v.reshape(b, t_len, ML_H, ML_DV)
    bf = b_if.astype(jnp.float32)
    ig = ig.astype(jnp.float32) + bf[:ML_H]
    lf = jax.nn.log_sigmoid(fg.astype(jnp.float32) + bf[ML_H:])
    hc, cm, nv, m = _mlstm_chunked(q, k, v, ig, lf, c0, n0, m0)
    hc = _rms(hc, mnorm.reshape(ML_H, ML_DV)).reshape(b, t_len, ODD_MIX)
    return _pmm(hc * jax.nn.sigmoid(o), w_out), (cm, nv, m)


def _rms_cast_kernel(x_ref, g_ref, o_ref):
    x = x_ref[...]
    y = x * lax.rsqrt(jnp.mean(x * x, axis=-1, keepdims=True) + RMS_EPS) * g_ref[...]
    o_ref[...] = y.astype(o_ref.dtype)


def _rms_cast(h, g):
    m, d = h.shape
    tm = _pick(m, (256, 128))
    return pl.pallas_call(
        _rms_cast_kernel,
        grid=(m // tm,),
        in_specs=[pl.BlockSpec((tm, d), lambda i: (i, 0)), pl.BlockSpec((1, d), lambda i: (0, 0))],
        out_specs=pl.BlockSpec((tm, d), lambda i: (i, 0)),
        out_shape=jax.ShapeDtypeStruct((m, d), MXU_DTYPE),
        compiler_params=pltpu.CompilerParams(dimension_semantics=("parallel",),
                                             vmem_limit_bytes=VMEM_LIMIT_BYTES),
    )(h, g.reshape(1, d).astype(jnp.float32))


def _mm2_kernel(xa_ref, xb_ref, w_ref, r_ref, o_ref):
    ka = xa_ref.shape[1]
    y = jnp.dot(xa_ref[...], w_ref[:ka, :], preferred_element_type=jnp.float32)
    y = y + jnp.dot(xb_ref[...], w_ref[ka:, :], preferred_element_type=jnp.float32)
    o_ref[...] = y + r_ref[...]


def _matmul2_res(xa, xb, w, res):
    m, ka = xa.shape
    kb = xb.shape[1]
    n = w.shape[1]
    tm = _pick(m, (512, 256, 128))
    tn = _pick(n, (512, 256, 128))
    return pl.pallas_call(
        _mm2_kernel,
        grid=(m // tm, n // tn),
        in_specs=[pl.BlockSpec((tm, ka), lambda i, j: (i, 0)), pl.BlockSpec((tm, kb), lambda i, j: (i, 0)),
                  pl.BlockSpec((ka + kb, tn), lambda i, j: (0, j)), pl.BlockSpec((tm, tn), lambda i, j: (i, j))],
        out_specs=pl.BlockSpec((tm, tn), lambda i, j: (i, j)),
        out_shape=jax.ShapeDtypeStruct((m, n), jnp.float32),
        compiler_params=pltpu.CompilerParams(dimension_semantics=("parallel", "parallel"),
                                             vmem_limit_bytes=VMEM_LIMIT_BYTES),
    )(xa, xb, w, res)


def _mlp_block(h, g, w1, w2):
    mid = _matmul(_rms_cast(h, g), w1, act='relu2', out_dtype=MXU_DTYPE)
    return _matmul(mid, w2, res=h)


def _pad_time(x, t_pad):
    b, d = x.shape
    return jnp.zeros((b, t_pad, d), x.dtype).at[:, 0, :].set(x).reshape(b * t_pad, d)


def _even_layer(h, g_mix, w_in, w_out, conv_w, a_log, dt_bias, gnorm, qk_gain, w_ck, w_cv, tab, bias_tiles,
                batch, t_len, conv_st, s_st, decode):
    f32 = jnp.float32
    proj = _matmul(_rms_cast(h, g_mix), w_in)
    t_pad = t_len if decode is None else GDN_CHUNK
    pp = proj if decode is None else _pad_time(proj, t_pad)
    first = jnp.pad(conv_st.astype(f32), ((0, 0), (8 - (GDN_CONV - 1), 0), (0, 0)))
    q, k, v, gb = _gdn_prep(pp, first, pp, EV_SMALL // 128, conv_w, a_log, dt_bias, batch, t_pad, t_len)
    ya, s_new = _gdn_chunks(q, k, v, gb, pp, EV_Z // GDN_V, s_st, gnorm, batch, t_pad)
    p3 = proj.reshape(batch, t_len, EV_N)
    xin_tail = jnp.concatenate([conv_st.astype(f32), p3[..., :GDN_CONV_CH]], axis=1)[:, -(GDN_CONV - 1):]
    hd = lambda t: t.reshape(batch, t_len, NSA_G, NSA_D)
    kv = lambda i: hd(p3[..., EV_KV + i * NSA_KV:EV_KV + (i + 1) * NSA_KV])
    if decode is None:
        qn, ksn, kwn, ksb, vsb, kwb, vwb, kcb, vcb = _nsa_prep(proj, qk_gain, w_ck, w_cv)
        yb = _nsa_prompt_attn(tab, bias_tiles, proj, qn, kcb, vcb, ksb, vsb, kwb, vwb, batch, t_len)
        nw = min(NSA_WIN, t_len)
        kw_out, vw_out = hd(kwn)[:, -nw:], kv(5)[:, -nw:]
        ks_out = hd(ksn)
    else:
        ya = ya.reshape(batch, t_pad, GDN_V)[:, 0]
        nq = p3[..., EV_NQ:EV_NQ + NSA_H * NSA_D].reshape(batch, t_len, NSA_G, NSA_HG, NSA_D)
        qf = _rms(nq, qk_gain[0]) * (NSA_D ** -0.5)
        ks_out = _rms(kv(2), qk_gain[2])
        kwn = _rms(kv(4), qk_gain[3])
        ng = p3[..., EV_SMALL + 2 * GDN_VH:EV_SMALL + 2 * GDN_VH + 3 * NSA_H]
        o3, kw_out, vw_out = _nsa_sample(qf, kv(0), kv(1), ks_out, kv(3), kwn, kv(5), decode['kc'], decode['vc'],
                                         decode['ks'], decode['vs'], decode['kw'], decode['vw'],
                                         decode['page_table'], w_ck, w_cv, qk_gain[1], tab)
        yb = _nsa_combine(o3, ng).reshape(batch * t_len, NSA_H * NSA_D).astype(MXU_DTYPE)
    h = _matmul2_res(ya, yb, w_out, h)
    return h, (kv(0), kv(1), ks_out, kv(3), kw_out, vw_out, xin_tail, s_new)


def _odd_layer(h, g_mix, w_in, w_out, b_if, mnorm, batch, t_len, c0, n0, m0, decode):
    proj = _matmul(_rms_cast(h, g_mix), w_in)
    t_pad = t_len if not decode else ML_CHUNK
    pp = proj if not decode else _pad_time(proj, t_pad)
    y, c, n, m = _mlstm_chunks(pp, b_if, mnorm, c0, n0, m0, batch, t_pad, t_len)
    if decode:
        y = y.reshape(batch, t_pad, ODD_MIX)[:, 0]
    return _matmul(y, w_out, res=h), (c, n, m)


def kernel(x_prompt, x_sample, cache_kc, cache_vc, cache_ks, cache_vs, cache_kw, cache_vw,
           state_gdn_conv, state_gdn_s, state_ml_c, state_ml_n, state_ml_m, page_table,
           rel_bias, norm_mix, norm_mlp, w_in_even, w_out_even, gdn_conv_w, gdn_a_log,
           gdn_dt_bias, gdn_norm_w, nsa_qk_gain, nsa_w_ck, nsa_w_cv, w_in_odd, w_out_odd,
           ml_b_if, ml_norm_w, w_ff1, w_ff2):
    f32 = jnp.float32
    bp, tp, d = x_prompt.shape
    bs, ts, _ = x_sample.shape
    hp = x_prompt.reshape(bp * tp, d)
    hs = x_sample.reshape(bs * ts, d)
    bias_tiles = _bias_tiles(rel_bias)
    ev_p, ev_s, od_p, od_s = [], [], [], []
    for l in range(DEPTH):
        j = l // 2
        if l % 2 == 0:
            w_in = _permute_even_w(w_in_even[j])
            w_out = w_out_even[j].astype(MXU_DTYPE)
            args = (w_in, w_out, gdn_conv_w[j], gdn_a_log[j], gdn_dt_bias[j], gdn_norm_w[j], nsa_qk_gain[j],
                    nsa_w_ck[j], nsa_w_cv[j], rel_bias, bias_tiles)
            hp, stp = _even_layer(hp, norm_mix[l], *args, bp, tp,
                                  jnp.zeros((bp, GDN_CONV - 1, GDN_CONV_CH), f32),
                                  jnp.zeros((bp, GDN_VH, GDN_DK, GDN_DV), f32), None)
            caches = dict(kc=cache_kc[j], vc=cache_vc[j], ks=cache_ks[j], vs=cache_vs[j], kw=cache_kw[j],
                          vw=cache_vw[j], page_table=page_table)
            hs, sts = _even_layer(hs, norm_mix[l], *args, bs, ts, state_gdn_conv[j], state_gdn_s[j], caches)
            ev_p.append(stp)
            ev_s.append(sts)
        else:
            w_in = _permute_odd_w(w_in_odd[j])
            w_out = w_out_odd[j].astype(MXU_DTYPE)
            hp, stp = _odd_layer(hp, norm_mix[l], w_in, w_out, ml_b_if[j], ml_norm_w[j], bp, tp,
                                 jnp.zeros((bp, ML_H, ML_DK, ML_DV), f32), jnp.zeros((bp, ML_H, ML_DK), f32),
                                 jnp.zeros((bp, ML_H), f32), False)
            hs, sts = _odd_layer(hs, norm_mix[l], w_in, w_out, ml_b_if[j], ml_norm_w[j], bs, ts,
                                 state_ml_c[j], state_ml_n[j], state_ml_m[j], True)
            od_p.append(stp)
            od_s.append(sts)
        w1 = w_ff1[l].astype(MXU_DTYPE)
        w2 = w_ff2[l].astype(MXU_DTYPE)
        hp = _mlp_block(hp, norm_mlp[l], w1, w2)
        hs = _mlp_block(hs, norm_mlp[l], w1, w2)

    kc_p, vc_p, ks_p, vs_p, kw_p, vw_p, conv_p, gdn_p = [jnp.stack(a) for a in zip(*ev_p)]
    kc_s, vc_s, ks_s, vs_s, kw_s, vw_s, conv_s, gdn_s = [jnp.stack(a) for a in zip(*ev_s)]
    mlc_p, mln_p, mlm_p = [jnp.stack(a) for a in zip(*od_p)]
    mlc_s, mln_s, mlm_s = [jnp.stack(a) for a in zip(*od_s)]
    return (hp.reshape(bp, tp, d), hs.reshape(bs, ts, d), kc_p, kc_s, vc_p, vc_s, ks_p, ks_s, vs_p, vs_s,
            kw_p, kw_s, vw_p, vw_s, conv_p, conv_s, gdn_p, gdn_s, mlc_p, mlc_s, mln_p, mln_s, mlm_p, mlm_s)


def _kernel_old(x_prompt, x_sample, cache_kc, cache_vc, cache_ks, cache_vs, cache_kw, cache_vw,
           state_gdn_conv, state_gdn_s, state_ml_c, state_ml_n, state_ml_m, page_table,
           rel_bias, norm_mix, norm_mlp, w_in_even, w_out_even, gdn_conv_w, gdn_a_log,
           gdn_dt_bias, gdn_norm_w, nsa_qk_gain, nsa_w_ck, nsa_w_cv, w_in_odd, w_out_odd,
           ml_b_if, ml_norm_w, w_ff1, w_ff2):
    hp, hs = x_prompt, x_sample
    bias_tiles = _bias_tiles(rel_bias)
    ev_p, ev_s, od_p, od_s = [], [], [], []
    for l in range(DEPTH):
        j = l // 2
        xp = _rms(hp, norm_mix[l])
        xs = _rms(hs, norm_mix[l])
        if l % 2 == 0:
            yp, stp = _even_prompt(xp, w_in_even[j], w_out_even[j], gdn_conv_w[j], gdn_a_log[j],
                                   gdn_dt_bias[j], gdn_norm_w[j], nsa_qk_gain[j], nsa_w_ck[j],
                                   nsa_w_cv[j], rel_bias, bias_tiles)
            ys, sts = _even_sample(xs, w_in_even[j], w_out_even[j], gdn_conv_w[j], gdn_a_log[j],
                                   gdn_dt_bias[j], gdn_norm_w[j], nsa_qk_gain[j], nsa_w_ck[j],
                                   nsa_w_cv[j], rel_bias, cache_kc[j], cache_vc[j], cache_ks[j],
                                   cache_vs[j], cache_kw[j], cache_vw[j], state_gdn_conv[j],
                                   state_gdn_s[j], page_table)
            ev_p.append(stp)
            ev_s.append(sts)
        else:
            bp = hp.shape[0]
            c0 = jnp.zeros((bp, ML_H, ML_DK, ML_DV), hp.dtype)
            n0 = jnp.zeros((bp, ML_H, ML_DK), hp.dtype)
            m0 = jnp.zeros((bp, ML_H), hp.dtype)
            yp, stp = _odd(xp, w_in_odd[j], w_out_odd[j], ml_b_if[j], ml_norm_w[j], c0, n0, m0)
            ys, sts = _odd(xs, w_in_odd[j], w_out_odd[j], ml_b_if[j], ml_norm_w[j],
                           state_ml_c[j], state_ml_n[j], state_ml_m[j])
            od_p.append(stp)
            od_s.append(sts)
        hp = hp + yp
        hs = hs + ys
        hp = hp + _mlp(_rms(hp, norm_mlp[l]), w_ff1[l], w_ff2[l])
        hs = hs + _mlp(_rms(hs, norm_mlp[l]), w_ff1[l], w_ff2[l])

    kc_p, vc_p, ks_p, vs_p, kw_p, vw_p, conv_p, gdn_p = [jnp.stack(a) for a in zip(*ev_p)]
    kc_s, vc_s, ks_s, vs_s, kw_s, vw_s, conv_s, gdn_s = [jnp.stack(a) for a in zip(*ev_s)]
    mlc_p, mln_p, mlm_p = [jnp.stack(a) for a in zip(*od_p)]
    mlc_s, mln_s, mlm_s = [jnp.stack(a) for a in zip(*od_s)]
    return (hp, hs, kc_p, kc_s, vc_p, vc_s, ks_p, ks_s, vs_p, vs_s, kw_p, kw_s, vw_p, vw_s,
            conv_p, conv_s, gdn_p, gdn_s, mlc_p, mlc_s, mln_p, mln_s, mlm_p, mlm_s)
```

```python
import functools
import math

import jax
import jax.numpy as jnp
import numpy as np
from jax import lax
from jax.experimental import pallas as pl
from jax.experimental.pallas import tpu as pltpu

D_MODEL = 2048
BATCH = 2
SEQ = 4096
DEPTH = 4
DEC_BATCH = 8
DEC_SEQ = 1
PAST_LEN = 16384
PAGE_SIZE = 128
RMS_EPS = 1e-6
NEG = -1e30
D_FF = 4 * D_MODEL

GDN_KH = 4
GDN_VH = 8
GDN_DK = 128
GDN_DV = 128
GDN_CONV = 4
GDN_CHUNK = 64
GDN_QK = GDN_KH * GDN_DK
GDN_V = GDN_VH * GDN_DV
GDN_CONV_CH = 2 * GDN_QK + GDN_V

NSA_H = 8
NSA_G = 2
NSA_HG = NSA_H // NSA_G
NSA_D = 128
NSA_BLK = 64
NSA_TOPK = 15
NSA_WIN = 512
NSA_QB = 128
NSA_KV = NSA_G * NSA_D

REL_BUCKETS = 32
REL_EXACT = 16
REL_MAX_DIST = 2048

ML_H = 8
ML_DK = 128
ML_DV = 256
ML_CHUNK = 64

EVEN_WIDTHS = (GDN_QK, GDN_QK, GDN_V, GDN_V, GDN_VH, GDN_VH, NSA_H * NSA_D,
               NSA_KV, NSA_KV, NSA_KV, NSA_KV, NSA_KV, NSA_KV, NSA_H * 3)
ODD_WIDTHS = (ML_H * ML_DK, ML_H * ML_DK, ML_H * ML_DV, ML_H * ML_DV, ML_H, ML_H)
ODD_MIX = ML_H * ML_DV

VMEM_LIMIT_BYTES = 56 * 1024 * 1024
MXU_DTYPE = jnp.bfloat16


def _mm_kernel(x_ref, w_ref, *rest, nk, act, has_res):
    if has_res:
        r_ref, o_ref, acc_ref = rest
    else:
        o_ref, acc_ref = rest
        r_ref = None
    k = pl.program_id(2)

    @pl.when(k == 0)
    def _():
        acc_ref[...] = jnp.zeros_like(acc_ref)

    acc_ref[...] += jnp.dot(x_ref[...], w_ref[...], preferred_element_type=jnp.float32)

    @pl.when(k == nk - 1)
    def _():
        y = acc_ref[...]
        if act == 'relu2':
            y = jnp.square(jnp.maximum(y, 0.0))
        if has_res:
            y = y + r_ref[...]
        o_ref[...] = y.astype(o_ref.dtype)


def _pick(n, prefs):
    for p in prefs:
        if n % p == 0:
            return p
    return n


def _matmul(x, w, *, act=None, res=None, out_dtype=jnp.float32):
    m, kdim = x.shape
    _, n = w.shape
    tm = _pick(m, (512, 256, 128))
    tn = _pick(n, (512, 896, 384, 256, 128))
    tk = _pick(kdim, (2048, 1024, 512))
    nk = kdim // tk
    in_specs = [pl.BlockSpec((tm, tk), lambda i, j, k: (i, k)),
                pl.BlockSpec((tk, tn), lambda i, j, k: (k, j))]
    args = [x, w]
    if res is not None:
        in_specs.append(pl.BlockSpec((tm, tn), lambda i, j, k: (i, j)))
        args.append(res)
    return pl.pallas_call(
        functools.partial(_mm_kernel, nk=nk, act=act, has_res=res is not None),
        grid=(m // tm, n // tn, nk),
        in_specs=in_specs,
        out_specs=pl.BlockSpec((tm, tn), lambda i, j, k: (i, j)),
        out_shape=jax.ShapeDtypeStruct((m, n), out_dtype),
        scratch_shapes=[pltpu.VMEM((tm, tn), jnp.float32)],
        compiler_params=pltpu.CompilerParams(
            dimension_semantics=("parallel", "parallel", "arbitrary"),
            vmem_limit_bytes=VMEM_LIMIT_BYTES),
    )(*args)


def _pmm(x, w):
    lead = x.shape[:-1]
    x2 = x.reshape(-1, x.shape[-1]).astype(MXU_DTYPE)
    n = w.shape[1]
    npad = (-n) % 128
    wb = w.astype(MXU_DTYPE)
    if npad:
        wb = jnp.pad(wb, ((0, 0), (0, npad)))
    y = _matmul(x2, wb)
    if npad:
        y = y[:, :n]
    return y.reshape(*lead, n)


EV_CONV = 0
EV_Z = GDN_CONV_CH
EV_NQ = EV_Z + GDN_V
EV_KV = EV_NQ + NSA_H * NSA_D
EV_SMALL = EV_KV + 6 * NSA_KV
EV_N = EV_SMALL + 128
EV_ORIG_SMALL = 2 * GDN_QK + 2 * GDN_V
EV_ORIG_N = EV_ORIG_SMALL + 2 * GDN_VH + NSA_H * NSA_D + 6 * NSA_KV + 3 * NSA_H


def _permute_even_w(w_in):
    ab = w_in[:, EV_ORIG_SMALL:EV_ORIG_SMALL + 2 * GDN_VH]
    big = w_in[:, EV_ORIG_SMALL + 2 * GDN_VH:EV_ORIG_N - 3 * NSA_H]
    ng = w_in[:, EV_ORIG_N - 3 * NSA_H:]
    pad = jnp.zeros((w_in.shape[0], EV_N - EV_ORIG_N), w_in.dtype)
    return jnp.concatenate([w_in[:, :EV_ORIG_SMALL], big, ab, ng, pad], axis=1).astype(MXU_DTYPE)


def _bucket_thresholds():
    n = np.arange(0, 4 * REL_MAX_DIST)
    nf = np.maximum(n, REL_EXACT).astype(np.float32)
    large = REL_EXACT + (np.log(nf / np.float32(REL_EXACT)) / np.float32(math.log(REL_MAX_DIST / REL_EXACT))
                         * np.float32(REL_BUCKETS - REL_EXACT)).astype(np.int32)
    b = np.where(n < REL_EXACT, n, np.minimum(large, REL_BUCKETS - 1))
    return tuple(int(np.argmax(b >= k)) for k in range(REL_BUCKETS))


BUCKET_THR = _bucket_thresholds()
BIAS_TILES = 14
assert BUCKET_THR[-1] <= (BIAS_TILES - 1) * 128 - 127


def _bias_of_dist(dist, tab_ref, head):
    val = jnp.full(dist.shape, tab_ref[0, head], jnp.float32)
    for k in range(1, REL_BUCKETS):
        val = jnp.where(dist >= BUCKET_THR[k], tab_ref[k, head], val)
    return val


def _bias_tiles_kernel(tab_ref, o_ref):
    h = pl.program_id(0)
    d = pl.program_id(1)
    i = lax.broadcasted_iota(jnp.int32, (128, 128), 0)
    j = lax.broadcasted_iota(jnp.int32, (128, 128), 1)
    dist = d * 128 + i - j
    val = jnp.full(dist.shape, tab_ref[0, h], jnp.float32)
    for k in range(1, REL_BUCKETS):
        val = jnp.where(dist >= BUCKET_THR[k], tab_ref[k, h], val)
    o_ref[0, 0] = val


def _bias_tiles(tab):
    return pl.pallas_call(
        _bias_tiles_kernel,
        grid=(NSA_H, BIAS_TILES),
        in_specs=[pl.BlockSpec(memory_space=pltpu.SMEM)],
        out_specs=pl.BlockSpec((1, 1, 128, 128), lambda h, d: (h, d, 0, 0)),
        out_shape=jax.ShapeDtypeStruct((NSA_H, BIAS_TILES, 128, 128), jnp.float32),
    )(tab.astype(jnp.float32))


def _group_rms(x, gain_row, scale=1.0):
    ms = jnp.mean(x * x, axis=-1, keepdims=True)
    y = x * lax.rsqrt(ms + RMS_EPS) * gain_row
    return y * scale if scale != 1.0 else y


def _nsa_prep_kernel(nq_ref, kc_ref, vc_ref, ks_ref, vs_ref, kw_ref, vw_ref, gain_ref, wck_ref, wcv_ref,
                     qn_ref, ksn_ref, kwn_ref, ksb_ref, vsb_ref, kwb_ref, vwb_ref, kcb_ref, vcb_ref):
    tm = nq_ref.shape[0]
    g0 = gain_ref[0:1, :]
    g1 = gain_ref[1:2, :]
    g2 = gain_ref[2:3, :]
    g3 = gain_ref[3:4, :]
    for h in range(NSA_H):
        sl = slice(h * NSA_D, (h + 1) * NSA_D)
        qn_ref[:, sl] = _group_rms(nq_ref[:, sl], g0, NSA_D ** -0.5).astype(qn_ref.dtype)
    for g in range(NSA_G):
        sl = slice(g * NSA_D, (g + 1) * NSA_D)
        ksn = _group_rms(ks_ref[:, sl], g2)
        kwn = _group_rms(kw_ref[:, sl], g3)
        ksn_ref[:, sl] = ksn
        kwn_ref[:, sl] = kwn
        ksb_ref[:, sl] = ksn.astype(ksb_ref.dtype)
        kwb_ref[:, sl] = kwn.astype(kwb_ref.dtype)
    vsb_ref[...] = vs_ref[...].astype(vsb_ref.dtype)
    vwb_ref[...] = vw_ref[...].astype(vwb_ref.dtype)
    nblk = tm // NSA_BLK
    kc3 = kc_ref[...].reshape(nblk, NSA_BLK, NSA_KV)
    vc3 = vc_ref[...].reshape(nblk, NSA_BLK, NSA_KV)
    kcb = jnp.sum(kc3 * wck_ref[...][None], axis=1)
    vcb = jnp.sum(vc3 * wcv_ref[...][None], axis=1)
    for g in range(NSA_G):
        sl = slice(g * NSA_D, (g + 1) * NSA_D)
        kcb_ref[:, sl] = _group_rms(kcb[:, sl], g1)
    vcb_ref[...] = vcb


def _nsa_prep(proj, gain, w_ck, w_cv):
    m = proj.shape[0]
    tm = 512
    kvb = EV_KV // NSA_KV
    f32, bf16 = jnp.float32, MXU_DTYPE

    def kv_spec(i):
        return pl.BlockSpec((tm, NSA_KV), lambda r, i=i: (r, kvb + i))

    row = lambda w: pl.BlockSpec((tm, w), lambda r: (r, 0))
    full = lambda a: pl.BlockSpec(a.shape, lambda r: (0,) * a.ndim)
    wck = w_ck.reshape(NSA_BLK, NSA_KV)
    wcv = w_cv.reshape(NSA_BLK, NSA_KV)
    outs = pl.pallas_call(
        _nsa_prep_kernel,
        grid=(m // tm,),
        in_specs=[pl.BlockSpec((tm, NSA_H * NSA_D), lambda r: (r, EV_NQ // (NSA_H * NSA_D)))]
                 + [kv_spec(i) for i in range(6)] + [full(gain), full(wck), full(wcv)],
        out_specs=[row(NSA_H * NSA_D)] + [row(NSA_KV)] * 6
                  + [pl.BlockSpec((tm // NSA_BLK, NSA_KV), lambda r: (r, 0))] * 2,
        out_shape=[jax.ShapeDtypeStruct((m, NSA_H * NSA_D), bf16),
                   jax.ShapeDtypeStruct((m, NSA_KV), f32), jax.ShapeDtypeStruct((m, NSA_KV), f32)]
                  + [jax.ShapeDtypeStruct((m, NSA_KV), bf16)] * 4
                  + [jax.ShapeDtypeStruct((m // NSA_BLK, NSA_KV), f32)] * 2,
        compiler_params=pltpu.CompilerParams(dimension_semantics=("parallel",),
                                             vmem_limit_bytes=VMEM_LIMIT_BYTES),
    )(proj, proj, proj, proj, proj, proj, proj, gain, wck, wcv)
    return outs


NSA_TK = 256


def _softmax_rows(lg):
    m = jnp.max(lg, axis=-1, keepdims=True)
    e = jnp.exp(lg - m)
    return e / jnp.sum(e, axis=-1, keepdims=True)


def _pad_rows(x, rows):
    if x.shape[0] == rows:
        return x
    return jnp.concatenate([x, jnp.zeros((rows - x.shape[0],) + x.shape[1:], x.dtype)], axis=0)


def _nsa_attn_kernel(tab_ref, q_ref, kcb_ref, vcb_ref, ks_ref, vs_ref, kw_ref, vw_ref, bias_ref, ng_ref,
                     o_ref, m_scr, l_scr, acc_scr):
    f32, bf16 = jnp.float32, MXU_DTYPE
    qb = pl.program_id(1)
    nb = kcb_ref.shape[0]
    qn = NSA_QB
    gates = jax.nn.sigmoid(ng_ref[...])
    tsub = NSA_TK // qn
    shift = NSA_BLK.bit_length() - 1

    for g in range(NSA_G):
        gl = slice(g * NSA_D, (g + 1) * NSA_D)
        q2 = jnp.concatenate([q_ref[:, (g * NSA_HG + h) * NSA_D:(g * NSA_HG + h + 1) * NSA_D]
                              for h in range(NSA_HG)], axis=0)

        kcb = kcb_ref[:, gl].astype(bf16)
        vcb = _pad_rows(vcb_ref[:, gl], qn).astype(bf16)
        lgt = lax.dot_general(kcb, q2, (((1,), (1,)), ((), ())), preferred_element_type=f32)
        blk = lax.broadcasted_iota(jnp.int32, (nb, qn), 0)
        blkf = blk.astype(f32)
        qpos = qb * qn + lax.broadcasted_iota(jnp.int32, (nb, qn), 1)
        dist_c = qpos - (blk * NSA_BLK + NSA_BLK - 1)
        ok_c = dist_c >= 0
        score = jnp.zeros((nb, qn), f32)
        o_c = []
        for h in range(NSA_HG):
            lg = lgt[:, h * qn:(h + 1) * qn] + _bias_of_dist(dist_c, tab_ref, g * NSA_HG + h)
            lg = jnp.where(ok_c, lg, NEG)
            mx = jnp.max(lg, axis=0, keepdims=True)
            e = jnp.exp(lg - mx)
            p = jnp.where(ok_c, e / jnp.sum(e, axis=0, keepdims=True), 0.0)
            score = score + p
            p_t = _pad_rows(p, qn).T.astype(bf16)
            o_c.append(jnp.dot(p_t, vcb, preferred_element_type=f32))
        cur = lax.shift_right_logical(qpos, shift)
        score = jnp.where(blk < cur, score, -1.0)

        sel = jnp.zeros((nb, qn), f32)
        for _ in range(min(NSA_TOPK, nb)):
            mx = jnp.max(score, axis=0, keepdims=True)
            first = jnp.min(jnp.where(score == mx, blkf, float(nb)), axis=0, keepdims=True)
            pick = blkf == first
            sel = jnp.where(pick, jnp.where(mx >= 0.0, 1.0, 0.0), sel)
            score = jnp.where(pick, -2.0, score)
        sel = jnp.where(blk == cur, 1.0, sel)
        sel_t = _pad_rows(sel, qn).T.astype(bf16)

        m_scr[...] = jnp.full(m_scr.shape, NEG, f32)
        l_scr[...] = jnp.zeros(l_scr.shape, f32)
        acc_scr[...] = jnp.zeros(acc_scr.shape, f32)
        n_tiles = qb // tsub + 1
        row_tok = qb * qn + lax.broadcasted_iota(jnp.int32, (qn, NSA_TK), 0)
        col = lax.broadcasted_iota(jnp.int32, (qn, NSA_TK), 1)
        eblk = lax.broadcasted_iota(jnp.int32, (qn, NSA_TK), 0)

        def body(i, carry, g=g, gl=gl, q2=q2, sel_t=sel_t):
            kt = n_tiles - 1 - i
            k0 = pl.multiple_of(kt * NSA_TK, NSA_TK)
            k = ks_ref[pl.ds(k0, NSA_TK), gl]
            v = vs_ref[pl.ds(k0, NSA_TK), gl]
            lg = lax.dot_general(q2, k, (((1,), (1,)), ((), ())), preferred_element_type=f32)
            expand = jnp.where(eblk == lax.shift_right_logical(k0 + col, shift), 1.0, 0.0).astype(bf16)
            member = jnp.dot(sel_t, expand, preferred_element_type=f32)
            member = jnp.where(k0 + col <= row_tok, member, 0.0)
            keep = member > 0.5
            parts = []
            for h in range(NSA_HG):
                tiles = [bias_ref[g * NSA_HG + h, jnp.clip(qb - (kt * tsub + j), 0, BIAS_TILES - 1)]
                         for j in range(tsub)]
                bias = jnp.concatenate(tiles, axis=1)
                parts.append(jnp.where(keep, lg[h * qn:(h + 1) * qn] + bias, NEG))
            lgm = jnp.concatenate(parts, axis=0)
            m_old = m_scr[...]
            m_new = jnp.maximum(m_old, jnp.max(lgm, axis=-1, keepdims=True))
            p = jnp.exp(lgm - m_new)
            alpha = jnp.exp(m_old - m_new)
            l_scr[...] = alpha * l_scr[...] + jnp.sum(p, axis=-1, keepdims=True)
            acc_scr[...] = alpha * acc_scr[...] + jnp.dot(p.astype(bf16), v, preferred_element_type=f32)
            m_scr[...] = m_new
            return carry

        lax.fori_loop(0, n_tiles, body, 0)
        o_s = acc_scr[...] / l_scr[...]

        nwin = NSA_WIN // qn + 1
        wi = lax.broadcasted_iota(jnp.int32, (qn, qn), 0)
        wj = lax.broadcasted_iota(jnp.int32, (qn, qn), 1)
        lgs, vws = [], []
        for d in range(nwin):
            sub = qb - d
            k0 = pl.multiple_of(jnp.maximum(sub, 0) * qn, qn)
            k = kw_ref[pl.ds(k0, qn), gl]
            vws.append(vw_ref[pl.ds(k0, qn), gl])
            lg = lax.dot_general(q2, k, (((1,), (1,)), ((), ())), preferred_element_type=f32)
            parts = []
            for h in range(NSA_HG):
                x = lg[h * qn:(h + 1) * qn] + bias_ref[g * NSA_HG + h, d]
                if d == 0:
                    x = jnp.where(wj <= wi, x, NEG)
                else:
                    if d == nwin - 1:
                        x = jnp.where(wj > wi, x, NEG)
                    x = jnp.where(sub >= 0, x, NEG)
                parts.append(x)
            lgs.append(jnp.concatenate(parts, axis=0))
        p_w = _softmax_rows(jnp.concatenate(lgs, axis=1))
        o_w = jnp.dot(p_w.astype(bf16), jnp.concatenate(vws, axis=0), preferred_element_type=f32)

        for h in range(NSA_HG):
            c = 2 * GDN_VH + (g * NSA_HG + h) * 3
            rs = slice(h * qn, (h + 1) * qn)
            y = (o_c[h] * gates[:, c:c + 1] + o_s[rs] * gates[:, c + 1:c + 2] + o_w[rs] * gates[:, c + 2:c + 3])
            o_ref[:, (g * NSA_HG + h) * NSA_D:(g * NSA_HG + h + 1) * NSA_D] = y.astype(o_ref.dtype)


def _nsa_prompt_attn(tab, bias_tiles, proj, qn, kcb, vcb, ksb, vsb, kwb, vwb, batch, t_len):
    nq = t_len // NSA_QB
    nb = t_len // NSA_BLK
    hd = NSA_H * NSA_D
    per_b = lambda w: pl.BlockSpec((t_len, w), lambda b, i: (b, 0))
    return pl.pallas_call(
        _nsa_attn_kernel,
        grid=(batch, nq),
        in_specs=[pl.BlockSpec(memory_space=pltpu.SMEM),
                  pl.BlockSpec((NSA_QB, hd), lambda b, i: (b * nq + i, 0)),
                  pl.BlockSpec((nb, NSA_KV), lambda b, i: (b, 0)),
                  pl.BlockSpec((nb, NSA_KV), lambda b, i: (b, 0)),
                  per_b(NSA_KV), per_b(NSA_KV), per_b(NSA_KV), per_b(NSA_KV),
                  pl.BlockSpec(bias_tiles.shape, lambda b, i: (0, 0, 0, 0)),
                  pl.BlockSpec((NSA_QB, 128), lambda b, i: (b * nq + i, EV_SMALL // 128 + 0))],
        out_specs=pl.BlockSpec((NSA_QB, hd), lambda b, i: (b * nq + i, 0)),
        out_shape=jax.ShapeDtypeStruct((batch * t_len, hd), MXU_DTYPE),
        scratch_shapes=[pltpu.VMEM((NSA_HG * NSA_QB, 1), jnp.float32),
                        pltpu.VMEM((NSA_HG * NSA_QB, 1), jnp.float32),
                        pltpu.VMEM((NSA_HG * NSA_QB, NSA_D), jnp.float32)],
        compiler_params=pltpu.CompilerParams(dimension_semantics=("parallel", "arbitrary"),
                                             vmem_limit_bytes=VMEM_LIMIT_BYTES),
    )(tab.astype(jnp.float32), qn, kcb, vcb, ksb, vsb, kwb, vwb, bias_tiles, proj)


def _hdot(a, b):
    return jnp.dot(a, b, precision=lax.Precision.HIGHEST, preferred_element_type=jnp.float32)


def _mdot(a, b):
    return jnp.dot(a.astype(MXU_DTYPE), b.astype(MXU_DTYPE), preferred_element_type=jnp.float32)


def _mdot_nt(a, b):
    return lax.dot_general(a.astype(MXU_DTYPE), b.astype(MXU_DTYPE), (((1,), (1,)), ((), ())),
                           preferred_element_type=jnp.float32)


def _mdot_tn(a, b):
    return lax.dot_general(a.astype(MXU_DTYPE), b.astype(MXU_DTYPE), (((0,), (0,)), ((), ())),
                           preferred_element_type=jnp.float32)


def _shift_rows(x, prev, s):
    xs = pltpu.roll(x, s, axis=0)
    ps = pltpu.roll(prev, s, axis=0)
    row8 = lax.broadcasted_iota(jnp.int32, prev.shape, 0)
    head = jnp.where(row8 < s, ps, xs[0:8])
    return jnp.concatenate([head, xs[8:]], axis=0)


def _gdn_prep_kernel(x_ref, prev_ref, first_ref, ab_ref, cw_ref, alog_ref, dtb_ref,
                     q_ref, k_ref, v_ref, gb_ref, *, tiles_per_batch, t_valid):
    r = pl.program_id(0)
    tm = x_ref.shape[0]
    tile = r % tiles_per_batch
    prev = jnp.where(tile == 0, first_ref[0], prev_ref[...])
    x = x_ref[...]
    acc = x * cw_ref[GDN_CONV - 1:GDN_CONV, :]
    for s in range(1, GDN_CONV):
        acc = acc + _shift_rows(x, prev, s) * cw_ref[GDN_CONV - 1 - s:GDN_CONV - s, :]
    c = acc * jax.nn.sigmoid(acc)

    def l2(t):
        return t * lax.rsqrt(jnp.sum(t * t, axis=-1, keepdims=True) + 1e-6)

    for h in range(GDN_KH):
        sl = slice(h * GDN_DK, (h + 1) * GDN_DK)
        q_ref[:, sl] = l2(c[:, sl]) * (GDN_DK ** -0.5)
        k_ref[:, sl] = l2(c[:, GDN_QK + h * GDN_DK:GDN_QK + (h + 1) * GDN_DK])
    v_ref[...] = c[:, 2 * GDN_QK:]
    ab = ab_ref[...]
    z = ab + dtb_ref[...]
    softplus = jnp.maximum(z, 0.0) + jnp.log(1.0 + jnp.exp(-jnp.abs(z)))
    gate = -jnp.exp(alog_ref[...]) * softplus
    lane = lax.broadcasted_iota(jnp.int32, ab.shape, 1)
    pos = tile * tm + lax.broadcasted_iota(jnp.int32, ab.shape, 0)
    gb = jnp.where(lane < GDN_VH, gate, jnp.where(lane < 2 * GDN_VH, jax.nn.sigmoid(ab), 0.0))
    gb_ref[...] = jnp.where(pos < t_valid, gb, 0.0)


def _gdn_prep(xin, first, ab, ab_col, conv_w, a_log, dt_bias, batch, t_pad, t_valid):
    m = batch * t_pad
    tm = _pick(t_pad, (256, 128, 64))
    tpb = t_pad // tm
    f32 = jnp.float32
    lanes = lambda v: jnp.zeros((1, 128), f32).at[0, :GDN_VH].set(v.astype(f32))
    return pl.pallas_call(
        functools.partial(_gdn_prep_kernel, tiles_per_batch=tpb, t_valid=t_valid),
        grid=(m // tm,),
        in_specs=[pl.BlockSpec((tm, GDN_CONV_CH), lambda r: (r, 0)),
                  pl.BlockSpec((8, GDN_CONV_CH), lambda r: (jnp.maximum(r * (tm // 8) - 1, 0), 0)),
                  pl.BlockSpec((1, 8, GDN_CONV_CH), lambda r: (r // tpb, 0, 0)),
                  pl.BlockSpec((tm, 128), lambda r: (r, ab_col)),
                  pl.BlockSpec((GDN_CONV, GDN_CONV_CH), lambda r: (0, 0)),
                  pl.BlockSpec((1, 128), lambda r: (0, 0)),
                  pl.BlockSpec((1, 128), lambda r: (0, 0))],
        out_specs=[pl.BlockSpec((tm, GDN_QK), lambda r: (r, 0)),
                   pl.BlockSpec((tm, GDN_QK), lambda r: (r, 0)),
                   pl.BlockSpec((tm, GDN_V), lambda r: (r, 0)),
                   pl.BlockSpec((tm, 128), lambda r: (r, 0))],
        out_shape=[jax.ShapeDtypeStruct((m, GDN_QK), f32), jax.ShapeDtypeStruct((m, GDN_QK), f32),
                   jax.ShapeDtypeStruct((m, GDN_V), f32), jax.ShapeDtypeStruct((m, 128), f32)],
        compiler_params=pltpu.CompilerParams(dimension_semantics=("arbitrary",),
                                             vmem_limit_bytes=VMEM_LIMIT_BYTES),
    )(xin, xin, first, ab, conv_w.astype(f32), lanes(a_log), lanes(dt_bias))


def _cumsum_rows(x):
    row = lax.broadcasted_iota(jnp.int32, x.shape, 0)
    s = 1
    while s < x.shape[0]:
        x = x + jnp.where(row >= s, pltpu.roll(x, s, axis=0), 0.0)
        s *= 2
    return x


def _split3(a):
    hi = a.astype(jnp.bfloat16)
    lo = (a - hi.astype(jnp.float32)).astype(jnp.bfloat16)
    return hi, lo


def _dot3(a, b):
    if MXU_DTYPE != jnp.bfloat16:
        return jnp.dot(a, b, preferred_element_type=jnp.float32)
    ah, al = _split3(a)
    bh, bl = _split3(b)
    d = lambda x, y: jnp.dot(x, y, preferred_element_type=jnp.float32)
    return d(ah, bh) + (d(ah, bl) + d(al, bh))


def _unit_lower_inverses(lmats, row, col):
    eye = jnp.where(row == col, 1.0, 0.0)
    blk = lax.shift_right_logical(row, 4) == lax.shift_right_logical(col, 4)
    lds = [jnp.where(blk, l, 0.0) for l in lmats]
    xs = [eye - ld for ld in lds]
    ps = [_dot3(ld, ld) for ld in lds]
    for step in range(3):
        xs = [x + _dot3(x, p) for x, p in zip(xs, ps)]
        if step < 2:
            ps = [_dot3(p, p) for p in ps]
    size = 16
    while size < lmats[0].shape[0]:
        sh = size.bit_length() - 1
        inner = lax.shift_right_logical(row, sh) == lax.shift_right_logical(col, sh)
        outer = lax.shift_right_logical(row, sh + 1) == lax.shift_right_logical(col, sh + 1)
        coffs = [jnp.where(outer, jnp.where(inner, 0.0, l), 0.0) for l in lmats]
        ts = [_dot3(x, c) for x, c in zip(xs, coffs)]
        xs = [x - _dot3(t, x) for x, t in zip(xs, ts)]
        size *= 2
    return xs


def _gdn_chunk_kernel(q_ref, k_ref, v_ref, gb_ref, z_ref, s0_ref, nw_ref, y_ref, s_ref):
    f32 = jnp.float32
    cn = q_ref.shape[0]

    @pl.when(pl.program_id(1) == 0)
    def _():
        s_ref[...] = s0_ref[...]

    gb = gb_ref[...]
    gcum = _cumsum_rows(gb)
    gcum_t = _pad_rows(gcum, 128).T
    row = lax.broadcasted_iota(jnp.int32, (cn, cn), 0)
    col = lax.broadcasted_iota(jnp.int32, (cn, cn), 1)
    rep = GDN_VH // GDN_KH
    heads = range(GDN_VH)
    qs = [q_ref[:, kh * GDN_DK:(kh + 1) * GDN_DK] for kh in range(GDN_KH)]
    ks = [k_ref[:, kh * GDN_DK:(kh + 1) * GDN_DK] for kh in range(GDN_KH)]
    kk = [_mdot_nt(k, k) for k in ks]
    qk = [_mdot_nt(q, k) for q, k in zip(qs, ks)]
    gcols = [gcum[:, h:h + 1] for h in heads]
    diffs = [gcols[h] - gcum_t[h:h + 1, :cn] for h in heads]
    egs = [jnp.exp(g) for g in gcols]
    betas = [gb[:, GDN_VH + h:GDN_VH + h + 1] for h in heads]
    lmats = [betas[h] * kk[h // rep] * jnp.exp(jnp.where(col < row, diffs[h], -jnp.inf)) for h in heads]
    ainvs = _unit_lower_inverses(lmats, row, col)
    sols = [_dot3(ainvs[h], jnp.concatenate([betas[h] * v_ref[:, h * GDN_DV:(h + 1) * GDN_DV],
                                             (betas[h] * egs[h]) * ks[h // rep]], axis=1)) for h in heads]
    atts = [qk[h // rep] * jnp.exp(jnp.where(col <= row, diffs[h], -jnp.inf)) for h in heads]
    ss = [s_ref[0, h] for h in heads]
    us = [sols[h][:, :GDN_DV] - _mdot(sols[h][:, GDN_DV:], ss[h]) for h in heads]
    os_ = [egs[h] * _mdot(qs[h // rep], ss[h]) + _mdot(atts[h], us[h]) for h in heads]
    for h in heads:
        gl = gcum[cn - 1:cn, h:h + 1]
        s_ref[0, h] = jnp.exp(gl) * ss[h] + _mdot_tn(ks[h // rep] * jnp.exp(gl - gcols[h]), us[h])
    for h in heads:
        o = os_[h]
        on = o * lax.rsqrt(jnp.mean(o * o, axis=-1, keepdims=True) + RMS_EPS) * nw_ref[...]
        z = z_ref[:, h * GDN_DV:(h + 1) * GDN_DV]
        y_ref[:, h * GDN_DV:(h + 1) * GDN_DV] = (on * (z * jax.nn.sigmoid(z))).astype(y_ref.dtype)


def _gdn_chunks(q, k, v, gb, zsrc, z_col, s0, norm_w, batch, t_pad):
    cn = GDN_CHUNK
    nc = t_pad // cn
    rowblk = lambda w, c=0: pl.BlockSpec((cn, w), lambda b, i, c=c: (b * nc + i, c))
    st = pl.BlockSpec((1, GDN_VH, GDN_DK, GDN_DV), lambda b, i: (b, 0, 0, 0))
    return pl.pallas_call(
        _gdn_chunk_kernel,
        grid=(batch, nc),
        in_specs=[rowblk(GDN_QK), rowblk(GDN_QK), rowblk(GDN_V), rowblk(128), rowblk(GDN_V, z_col), st,
                  pl.BlockSpec((1, GDN_DV), lambda b, i: (0, 0))],
        out_specs=[rowblk(GDN_V), st],
        out_shape=[jax.ShapeDtypeStruct((batch * t_pad, GDN_V), MXU_DTYPE),
                   jax.ShapeDtypeStruct((batch, GDN_VH, GDN_DK, GDN_DV), jnp.float32)],
        compiler_params=pltpu.CompilerParams(dimension_semantics=("parallel", "arbitrary"),
                                             vmem_limit_bytes=VMEM_LIMIT_BYTES),
    )(q, k, v, gb, zsrc, s0.astype(jnp.float32), norm_w.reshape(1, GDN_DV).astype(jnp.float32))


OD_Q = 0
OD_K = ML_H * ML_DK
OD_V = 2 * ML_H * ML_DK
OD_O = OD_V + ML_H * ML_DV
OD_SMALL = OD_O + ML_H * ML_DV
OD_N = OD_SMALL + 128


def _cummax_rows(x):
    row = lax.broadcasted_iota(jnp.int32, x.shape, 0)
    s = 1
    while s < x.shape[0]:
        x = jnp.maximum(x, jnp.where(row >= s, pltpu.roll(x, s, axis=0), -jnp.inf))
        s *= 2
    return x


def _round_mxu(x):
    return x.astype(MXU_DTYPE).astype(jnp.float32)


def _mlstm_chunk_kernel(q_ref, k_ref, v_ref, og_ref, if_ref, bif_ref, nw_ref, c0_ref, n0_ref, m0_ref,
                        y_ref, c_ref, n_ref, m_ref, *, t_valid):
    f32 = jnp.float32
    cn = q_ref.shape[0]
    ci = pl.program_id(1)

    @pl.when(ci == 0)
    def _():
        c_ref[...] = c0_ref[...]
        n_ref[...] = n0_ref[...]
        m_ref[...] = m0_ref[...]

    pre = if_ref[...] + bif_ref[...]
    pos = ci * cn + lax.broadcasted_iota(jnp.int32, pre.shape, 0)
    live = pos < t_valid
    lf = jnp.where(live, jnp.minimum(pre, 0.0) - jnp.log(1.0 + jnp.exp(-jnp.abs(pre))), 0.0)
    fcum = _cumsum_rows(lf)
    a_all = jnp.where(live, pre, NEG) - pltpu.roll(fcum, 128 - ML_H, axis=1)
    amax = _cummax_rows(a_all)
    a_t = _pad_rows(a_all, 128).T
    row = lax.broadcasted_iota(jnp.int32, (cn, cn), 0)
    col = lax.broadcasted_iota(jnp.int32, (cn, cn), 1)
    m_all = m_ref[0]
    m_new = []
    for h in range(ML_H):
        q = q_ref[:, h * ML_DK:(h + 1) * ML_DK] * (ML_DK ** -0.5)
        k = k_ref[:, h * ML_DK:(h + 1) * ML_DK]
        v = v_ref[:, h * ML_DV:(h + 1) * ML_DV]
        m_prev = m_all[:, h:h + 1]
        fc = fcum[:, ML_H + h:ML_H + h + 1]
        a_col = a_all[:, h:h + 1]
        mt = fc + jnp.maximum(m_prev, amax[:, h:h + 1])
        dmat = jnp.exp(jnp.where(col <= row, a_t[h:h + 1, :cn] + (fc - mt), -jnp.inf))
        dec0 = jnp.exp(fc + m_prev - mt)
        s = _mdot_nt(q, k) * dmat
        cm = c_ref[0, h]
        nv = n_ref[0, h:h + 1, :]
        num = dec0 * _mdot(q, cm) + _mdot(s, v)
        den = dec0 * jnp.sum(_round_mxu(q) * _round_mxu(nv), axis=-1, keepdims=True) \
            + jnp.sum(s, axis=-1, keepdims=True)
        hc = num / jnp.maximum(jnp.abs(den), jnp.exp(-mt))
        f_end = fc[cn - 1:cn]
        m_end = mt[cn - 1:cn]
        w = jnp.exp(a_col + (f_end - m_end))
        dc = jnp.exp(f_end + m_prev - m_end)
        c_ref[0, h] = dc * cm + _mdot_tn(w * k, v)
        n_ref[0, h:h + 1, :] = dc * nv + jnp.sum(_round_mxu(w) * _round_mxu(k), axis=0, keepdims=True)
        m_new.append(m_end)
        hn = hc * lax.rsqrt(jnp.mean(hc * hc, axis=-1, keepdims=True) + RMS_EPS) \
            * nw_ref[:, h * ML_DV:(h + 1) * ML_DV]
        og = og_ref[:, h * ML_DV:(h + 1) * ML_DV]
        y_ref[:, h * ML_DV:(h + 1) * ML_DV] = (hn * jax.nn.sigmoid(og)).astype(y_ref.dtype)
    lane = lax.broadcasted_iota(jnp.int32, m_all.shape, 1)
    out = m_all
    for h in range(ML_H):
        out = jnp.where(lane == h, m_new[h], out)
    m_ref[0] = out


def _mlstm_chunks(proj, b_if, mnorm, c0, n0, m0, batch, t_pad, t_valid):
    cn = ML_CHUNK
    nc = t_pad // cn
    f32 = jnp.float32
    blk = lambda w, c: pl.BlockSpec((cn, w), lambda b, i, c=c: (b * nc + i, c))
    const = lambda shape: pl.BlockSpec(shape, lambda b, i: (0,) * len(shape))
    st_c = pl.BlockSpec((1, ML_H, ML_DK, ML_DV), lambda b, i: (b, 0, 0, 0))
    st_n = pl.BlockSpec((1, ML_H, ML_DK), lambda b, i: (b, 0, 0))
    st_m = pl.BlockSpec((1, 1, 128), lambda b, i: (b, 0, 0))
    bif = jnp.zeros((1, 128), f32).at[0, :2 * ML_H].set(b_if.astype(f32))
    m0p = jnp.zeros((batch, 1, 128), f32).at[:, 0, :ML_H].set(m0.astype(f32))
    hq, hv = ML_H * ML_DK, ML_H * ML_DV
    y, c, n, m = pl.pallas_call(
        functools.partial(_mlstm_chunk_kernel, t_valid=t_valid),
        grid=(batch, nc),
        in_specs=[blk(hq, OD_Q // hq), blk(hq, OD_K // hq), blk(hv, OD_V // hv), blk(hv, OD_O // hv),
                  blk(128, OD_SMALL // 128), const((1, 128)), const((1, hv)), st_c, st_n, st_m],
        out_specs=[blk(hv, 0), st_c, st_n, st_m],
        out_shape=[jax.ShapeDtypeStruct((batch * t_pad, hv), MXU_DTYPE),
                   jax.ShapeDtypeStruct((batch, ML_H, ML_DK, ML_DV), f32),
                   jax.ShapeDtypeStruct((batch, ML_H, ML_DK), f32),
                   jax.ShapeDtypeStruct((batch, 1, 128), f32)],
        compiler_params=pltpu.CompilerParams(dimension_semantics=("parallel", "arbitrary"),
                                             vmem_limit_bytes=VMEM_LIMIT_BYTES),
    )(proj, proj, proj, proj, proj, bif, mnorm.reshape(1, hv).astype(f32), c0.astype(f32), n0.astype(f32), m0p)
    return y, c, n, m[:, 0, :ML_H]


def _permute_odd_w(w_in):
    pad = jnp.zeros((w_in.shape[0], OD_N - w_in.shape[1]), w_in.dtype)
    return jnp.concatenate([w_in, pad], axis=1).astype(MXU_DTYPE)


DEC_PAGES_PER_STEP = 8


def _dec_compress_kernel(pt_ref, *refs, pp):
    kc_refs, vc_refs = refs[:pp], refs[pp:2 * pp]
    wck_ref, wcv_ref, g1_ref, kcb_ref, vcb_ref = refs[2 * pp:]
    per_page = PAGE_SIZE // NSA_BLK

    def compress(page_refs, w):
        rows = [jnp.sum(r[0].reshape(per_page, NSA_BLK, NSA_KV) * w[None], axis=1) for r in page_refs]
        return jnp.concatenate(rows, axis=0)

    kcb = compress(kc_refs, wck_ref[...])
    for g in range(NSA_G):
        sl = slice(g * NSA_D, (g + 1) * NSA_D)
        kcb_ref[0, :, sl] = _group_rms(kcb[:, sl], g1_ref[...])
    vcb_ref[0] = compress(vc_refs, wcv_ref[...])


def _dec_compress(page_table, pool_kc, pool_vc, w_ck, w_cv, g1):
    nseq, npages = page_table.shape
    pp = DEC_PAGES_PER_STEP
    per_page = PAGE_SIZE // NSA_BLK
    nphys = pool_kc.shape[0]
    kc = pool_kc.reshape(nphys, PAGE_SIZE, NSA_KV)
    vc = pool_vc.reshape(nphys, PAGE_SIZE, NSA_KV)
    page = lambda j: pl.BlockSpec((1, PAGE_SIZE, NSA_KV), lambda b, i, pt, j=j: (pt[b, i * pp + j], 0, 0))
    const = lambda shape: pl.BlockSpec(shape, lambda b, i, pt: (0,) * len(shape))
    out = pl.BlockSpec((1, pp * per_page, NSA_KV), lambda b, i, pt: (b, i, 0))
    nb = npages * per_page
    return pl.pallas_call(
        functools.partial(_dec_compress_kernel, pp=pp),
        grid_spec=pltpu.PrefetchScalarGridSpec(
            num_scalar_prefetch=1, grid=(nseq, npages // pp),
            in_specs=[page(j) for j in range(pp)] * 2 + [const((NSA_BLK, NSA_KV))] * 2 + [const((1, NSA_D))],
            out_specs=[out, out]),
        out_shape=[jax.ShapeDtypeStruct((nseq, nb, NSA_KV), jnp.float32)] * 2,
        compiler_params=pltpu.CompilerParams(dimension_semantics=("parallel", "arbitrary"),
                                             vmem_limit_bytes=VMEM_LIMIT_BYTES),
    )(page_table, *([kc] * pp), *([vc] * pp), w_ck.reshape(NSA_BLK, NSA_KV), w_cv.reshape(NSA_BLK, NSA_KV),
      g1.reshape(1, NSA_D))


def _rows_by_head(head_row, fn, nheads):
    out = fn(0)
    out = jnp.broadcast_to(out, (nheads, out.shape[1]))
    for h in range(1, nheads):
        out = jnp.where(head_row == h, fn(h), out)
    return out


def _dec_select_kernel(tab_ref, q_ref, kcb_ref, vcb_ref, oc_ref, idx_ref, *, qpos):
    f32 = jnp.float32
    nbp = kcb_ref.shape[1]
    q = q_ref[0]
    head_row = lax.broadcasted_iota(jnp.int32, (NSA_H, 1), 0)
    blk = lax.broadcasted_iota(jnp.int32, (1, nbp), 1)
    blkf = blk.astype(f32)
    dist = qpos - (blk * NSA_BLK + NSA_BLK - 1)
    ok = dist >= 0
    cur = qpos // NSA_BLK
    lg = jnp.zeros((NSA_H, nbp), f32)
    for g in range(NSA_G):
        lg_g = _mdot_nt(q, kcb_ref[0, :, g * NSA_D:(g + 1) * NSA_D])
        lg = jnp.where(head_row // NSA_HG == g, lg_g, lg)
    lg = lg + _rows_by_head(head_row, lambda h: _bias_of_dist(dist, tab_ref, h), NSA_H)
    lg = jnp.where(ok, lg, NEG)
    e = jnp.exp(lg - jnp.max(lg, axis=-1, keepdims=True))
    p = jnp.where(ok, e / jnp.sum(e, axis=-1, keepdims=True), 0.0)
    oc = jnp.zeros((NSA_H, NSA_D), f32)
    lane = lax.broadcasted_iota(jnp.int32, (1, 128), 1)
    for g in range(NSA_G):
        in_g = head_row // NSA_HG == g
        oc = jnp.where(in_g, _mdot(p, vcb_ref[0, :, g * NSA_D:(g + 1) * NSA_D]), oc)
        score = jnp.sum(jnp.where(in_g, p, 0.0), axis=0, keepdims=True)
        score = jnp.where(blk < cur, score, -1.0)
        idx = jnp.full((1, 128), -1.0, f32)
        for r in range(NSA_TOPK):
            mx = jnp.max(score, axis=-1, keepdims=True)
            first = jnp.min(jnp.where(score == mx, blkf, float(nbp)), axis=-1, keepdims=True)
            idx = jnp.where(lane == r, jnp.where(mx >= 0.0, first, -1.0), idx)
            score = jnp.where(blkf == first, -2.0, score)
        idx_ref[0, g:g + 1, :] = idx.astype(jnp.int32)
    oc_ref[0] = oc


def _dec_select(tab, qh, kcb, vcb, qpos):
    nseq = qh.shape[0]
    nbp = kcb.shape[1]
    seq = lambda shape: pl.BlockSpec((1,) + shape, lambda b: (b, 0, 0))
    return pl.pallas_call(
        functools.partial(_dec_select_kernel, qpos=qpos),
        grid=(nseq,),
        in_specs=[pl.BlockSpec(memory_space=pltpu.SMEM), seq((NSA_H, NSA_D)), seq((nbp, NSA_KV)),
                  seq((nbp, NSA_KV))],
        out_specs=[seq((NSA_H, NSA_D)), seq((NSA_G, 128))],
        out_shape=[jax.ShapeDtypeStruct((nseq, NSA_H, NSA_D), jnp.float32),
                   jax.ShapeDtypeStruct((nseq, NSA_G, 128), jnp.int32)],
        compiler_params=pltpu.CompilerParams(dimension_semantics=("parallel",),
                                             vmem_limit_bytes=VMEM_LIMIT_BYTES),
    )(tab.astype(jnp.float32), qh, kcb, vcb)


def _dec_attend_kernel(pt_ref, idx_ref, tab_ref, q_ref, *refs, qpos):
    f32 = jnp.float32
    ks_refs, vs_refs = refs[:NSA_TOPK], refs[NSA_TOPK:2 * NSA_TOPK]
    ksn_ref, vsn_ref, kwb_ref, vwb_ref, kwn_ref, vwn_ref, oc_ref, ng_ref, y_ref = refs[2 * NSA_TOPK:]
    b = pl.program_id(0)
    g = pl.program_id(1)
    cur = qpos // NSA_BLK
    head_row = lax.broadcasted_iota(jnp.int32, (NSA_HG, 1), 0)
    q4 = q_ref[0, pl.ds(g * NSA_HG, NSA_HG), :]
    q4r = _round_mxu(q4.astype(f32))
    new_row = b * NSA_G + g
    tab0 = _rows_by_head(head_row, lambda h: jnp.full((1, 1), tab_ref[0, g * NSA_HG + h], f32), NSA_HG)

    def attend(keys, vals, dist, ok, k_new, v_new):
        lg = _mdot_nt(q4, keys)
        lg = lg + _rows_by_head(head_row, lambda h: _bias_of_dist(dist, tab_ref, g * NSA_HG + h), NSA_HG)
        lg = jnp.where(ok, lg, NEG)
        lg_new = jnp.sum(q4r * _round_mxu(k_new), axis=-1, keepdims=True) + tab0
        m = jnp.maximum(jnp.max(lg, axis=-1, keepdims=True), lg_new)
        e = jnp.exp(lg - m)
        e_new = jnp.exp(lg_new - m)
        den = jnp.sum(e, axis=-1, keepdims=True) + e_new
        return _mdot(e / den, vals) + _round_mxu(e_new / den) * _round_mxu(v_new)

    pieces, oks = [], []
    jrow = lax.broadcasted_iota(jnp.int32, (1, NSA_BLK), 1)
    for s in range(NSA_TOPK):
        blk_id = idx_ref[b, g, s]
        valid = jnp.logical_and(blk_id >= 0, blk_id < cur)
        pos = jnp.maximum(blk_id, 0) * NSA_BLK + jrow
        pieces.append(qpos - pos)
        oks.append(jnp.where(valid, 1.0, 0.0) * jnp.where(pos <= qpos, 1.0, 0.0))
    dist_s = jnp.concatenate(pieces, axis=1)
    ok_s = jnp.concatenate(oks, axis=1) > 0.5
    k_sel = jnp.concatenate([r[0] for r in ks_refs], axis=0)
    v_sel = jnp.concatenate([r[0] for r in vs_refs], axis=0)
    o_s = attend(k_sel, v_sel, dist_s, ok_s, ksn_ref[pl.ds(new_row, 1), :], vsn_ref[pl.ds(new_row, 1), :])

    wb = kwb_ref.shape[1]
    dist_w = wb - lax.broadcasted_iota(jnp.int32, (1, wb), 1)
    ok_w = dist_w < NSA_WIN
    o_w = attend(kwb_ref[0], vwb_ref[0], dist_w, ok_w, kwn_ref[pl.ds(new_row, 1), :], vwn_ref[pl.ds(new_row, 1), :])

    gates = jax.nn.sigmoid(ng_ref[pl.ds(b, 1), :])
    lane = lax.broadcasted_iota(jnp.int32, (1, 128), 1)

    def gate(branch):
        def one(h):
            c = 2 * GDN_VH + (g * NSA_HG + h) * 3 + branch
            return jnp.sum(jnp.where(lane == c, gates, 0.0), axis=-1, keepdims=True)
        return _rows_by_head(head_row, one, NSA_HG)

    oc4 = oc_ref[0, pl.ds(g * NSA_HG, NSA_HG), :]
    y_ref[0, pl.ds(g * NSA_HG, NSA_HG), :] = oc4 * gate(0) + o_s * gate(1) + o_w * gate(2)


def _dec_attend(tab, page_table, idx, qh, pool_ks, pool_vs, ksn, vsn, buf_kw, buf_vw, kwn, vwn, oc, ng, qpos):
    nseq = qh.shape[0]
    nphys = pool_ks.shape[0]
    wb = buf_kw.shape[1]
    ks = pool_ks.reshape(nphys, PAGE_SIZE, NSA_KV)
    vs = pool_vs.reshape(nphys, PAGE_SIZE, NSA_KV)
    per_page = PAGE_SIZE // NSA_BLK

    def sel(s):
        def index(b, g, pt, ix, s=s):
            blk_id = jnp.maximum(ix[b, g, s], 0)
            return (pt[b, blk_id // per_page], blk_id % per_page, g)
        return pl.BlockSpec((1, NSA_BLK, NSA_D), index)

    whole = lambda a: pl.BlockSpec(a.shape, lambda b, g, pt, ix: (0,) * a.ndim)
    heads = pl.BlockSpec((1, NSA_H, NSA_D), lambda b, g, pt, ix: (b, 0, 0))
    win = pl.BlockSpec((1, wb, NSA_D), lambda b, g, pt, ix: (b, 0, g))
    rows = lambda a: a.reshape(nseq * NSA_G, NSA_D)
    new = [rows(ksn), rows(vsn), rows(kwn), rows(vwn)]
    return pl.pallas_call(
        functools.partial(_dec_attend_kernel, qpos=qpos),
        grid_spec=pltpu.PrefetchScalarGridSpec(
            num_scalar_prefetch=2, grid=(nseq, NSA_G),
            in_specs=[pl.BlockSpec(memory_space=pltpu.SMEM), heads] + [sel(s) for s in range(NSA_TOPK)] * 2
                     + [whole(new[0]), whole(new[1]), win, win, whole(new[2]), whole(new[3]), heads, whole(ng)],
            out_specs=heads),
        out_shape=jax.ShapeDtypeStruct((nseq, NSA_H, NSA_D), jnp.float32),
        compiler_params=pltpu.CompilerParams(dimension_semantics=("parallel", "arbitrary"),
                                             vmem_limit_bytes=VMEM_LIMIT_BYTES),
    )(page_table, idx, tab.astype(jnp.float32), qh, *([ks] * NSA_TOPK), *([vs] * NSA_TOPK), new[0], new[1],
      buf_kw.reshape(nseq, wb, NSA_KV), buf_vw.reshape(nseq, wb, NSA_KV), new[2], new[3], oc, ng)


def _nsa_decode(proj, tab, qk_gain, w_ck, w_cv, caches, nseq):
    assert PAST_LEN % PAGE_SIZE == 0 and DEC_SEQ == 1
    pad = jnp.pad(proj, ((0, 512 - nseq), (0, 0)))
    qn, ksn, kwn, _, _, _, _, _, _ = _nsa_prep(pad, qk_gain, w_ck, w_cv)
    qh = qn[:nseq].reshape(nseq, NSA_H, NSA_D).astype(jnp.float32)
    kv = lambda i: proj[:, EV_KV + i * NSA_KV:EV_KV + (i + 1) * NSA_KV]
    kcb, vcb = _dec_compress(caches['page_table'], caches['kc'], caches['vc'], w_ck, w_cv, qk_gain[1])
    oc, idx = _dec_select(tab, qh, kcb, vcb, PAST_LEN)
    ng = proj[:, EV_SMALL:EV_SMALL + 128]
    y = _dec_attend(tab, caches['page_table'], idx, qh, caches['ks'], caches['vs'], ksn[:nseq], kv(3),
                    caches['kw'], caches['vw'], kwn[:nseq], kv(5), oc, ng, PAST_LEN)
    return y.reshape(nseq, NSA_H * NSA_D), ksn[:nseq], kwn[:nseq]


def _split(x, widths):
    offs = [int(o) for o in np.cumsum(widths)[:-1]]
    return jnp.split(x, offs, axis=-1)


def _rms(x, g):
    xf = x.astype(jnp.float32)
    y = xf * lax.rsqrt(jnp.mean(xf * xf, axis=-1, keepdims=True) + RMS_EPS)
    return (y * g.astype(jnp.float32)).astype(x.dtype)


def _mlp(xn, w1, w2):
    h = jnp.square(jax.nn.relu(_pmm(xn, w1)))
    return _pmm(h, w2)


def _causal_conv(xin, w):
    c = xin.shape[-1]
    return lax.conv_general_dilated(xin, w[:, None, :].astype(xin.dtype), window_strides=(1,),
                                    padding='VALID', dimension_numbers=('NWC', 'WIO', 'NWC'),
                                    feature_group_count=c)


def _chunks(t, c, pad, pad_val=0.0):
    t = t.astype(jnp.float32)
    if pad:
        t = jnp.pad(t, [(0, 0), (0, pad)] + [(0, 0)] * (t.ndim - 2), constant_values=pad_val)
    b, tp = t.shape[:2]
    t = t.reshape(b, tp // c, c, *t.shape[2:])
    return jnp.swapaxes(jnp.moveaxis(t, 1, 0), 2, 3)


def _unchunk(o, t_len):
    n, b, h, c, d = o.shape
    return jnp.transpose(o, (1, 0, 3, 2, 4)).reshape(b, n * c, h, d)[:, :t_len]


def _gdn_chunked(q, k, v, g, beta, s0):
    b, t_len, h, dk = q.shape
    dv = v.shape[-1]
    c = min(GDN_CHUNK, t_len)
    pad = (-t_len) % c
    tri_s = jnp.tril(jnp.ones((c, c), bool), -1)
    tri_i = jnp.tril(jnp.ones((c, c), bool), 0)
    eye = jnp.eye(c, dtype=jnp.float32)

    def body(s, inp):
        qc, kc, vc, gc, bc = inp
        gcum = jnp.cumsum(gc, axis=-1)
        diff = gcum[..., :, None] - gcum[..., None, :]
        eg = jnp.exp(gcum)
        lmat = bc[..., :, None] * jnp.einsum('bhid,bhjd->bhij', kc, kc) * jnp.exp(jnp.where(tri_s, diff, -jnp.inf))
        rhs = jnp.concatenate([bc[..., None] * vc, (bc * eg)[..., None] * kc], axis=-1)
        sol = lax.linalg.triangular_solve(eye + lmat, rhs, left_side=True, lower=True)
        u = sol[..., :dv] - jnp.einsum('bhcd,bhde->bhce', sol[..., dv:], s)
        att = jnp.einsum('bhid,bhjd->bhij', qc, kc) * jnp.exp(jnp.where(tri_i, diff, -jnp.inf))
        o = eg[..., None] * jnp.einsum('bhcd,bhde->bhce', qc, s) + jnp.einsum('bhij,bhje->bhie', att, u)
        gl = gcum[..., -1]
        s = jnp.exp(gl)[..., None, None] * s + jnp.einsum('bhcd,bhce->bhde', kc * jnp.exp(gl[..., None] - gcum)[..., None], u)
        return s, o

    s, o = lax.scan(body, s0.astype(jnp.float32),
                    (_chunks(q, c, pad), _chunks(k, c, pad), _chunks(v, c, pad), _chunks(g, c, pad), _chunks(beta, c, pad)))
    return _unchunk(o, t_len).astype(v.dtype), s.astype(s0.dtype)


def _gdn(parts, conv_w, a_log, dt_bias, norm_w, conv_prev, s0):
    q, k, v, z, a, bgate = parts
    b, t_len = q.shape[:2]
    xin = jnp.concatenate([conv_prev.astype(q.dtype), jnp.concatenate([q, k, v], axis=-1)], axis=1)
    c = jax.nn.silu(_causal_conv(xin, conv_w))
    cq, ck, cv = _split(c, (GDN_QK, GDN_QK, GDN_V))

    def l2(t):
        t = t.reshape(b, t_len, GDN_KH, GDN_DK).astype(jnp.float32)
        return t * lax.rsqrt(jnp.sum(t * t, axis=-1, keepdims=True) + 1e-6)

    rep = GDN_VH // GDN_KH
    qh = jnp.repeat(l2(cq) * (GDN_DK ** -0.5), rep, axis=2)
    kh = jnp.repeat(l2(ck), rep, axis=2)
    vh = cv.reshape(b, t_len, GDN_VH, GDN_DV)
    beta = jax.nn.sigmoid(bgate.astype(jnp.float32))
    g = -jnp.exp(a_log.astype(jnp.float32)) * jax.nn.softplus(a.astype(jnp.float32) + dt_bias.astype(jnp.float32))
    o, s = _gdn_chunked(qh, kh, vh, g, beta, s0)
    o = _rms(o, norm_w) * jax.nn.silu(z.reshape(b, t_len, GDN_VH, GDN_DV))
    return o.reshape(b, t_len, GDN_V), xin[:, -(GDN_CONV - 1):], s


def _bucket(dist):
    n = jnp.maximum(dist, 0)
    nf = jnp.maximum(n, REL_EXACT).astype(jnp.float32)
    large = REL_EXACT + (jnp.log(nf / REL_EXACT) / math.log(REL_MAX_DIST / REL_EXACT)
                         * (REL_BUCKETS - REL_EXACT)).astype(jnp.int32)
    large = jnp.minimum(large, REL_BUCKETS - 1)
    return jnp.where(n < REL_EXACT, n, large)


def _tok_bias(dist, tab):
    t_len, n = dist.shape
    bias = tab.astype(jnp.float32)[_bucket(dist)]
    return bias.reshape(t_len, n, NSA_G, NSA_HG).transpose(0, 2, 3, 1)


def _compress(rows, w):
    b, t_len = rows.shape[:2]
    r = rows.reshape(b, t_len // NSA_BLK, NSA_BLK, NSA_G, NSA_D)
    return jnp.einsum('bnlgd,lgd->bngd', r, w)


def _nsa_core(q, qpos, kc_b, vc_b, fetch, kw, vw, kwpos, tab):
    b, t_len = q.shape[:2]
    f32 = jnp.float32
    nb = kc_b.shape[1]
    blk = jnp.arange(nb, dtype=jnp.int32)
    dist_c = qpos[:, None] - (blk * NSA_BLK + NSA_BLK - 1)[None, :]
    ok_c = (dist_c >= 0)[:, None, None, :]
    lg_c = jnp.einsum('btghd,bngd->btghn', q, kc_b).astype(f32) + _tok_bias(dist_c, tab)
    p_c = jax.nn.softmax(jnp.where(ok_c, lg_c, NEG), axis=-1) * ok_c
    o_c = jnp.einsum('btghn,bngd->btghd', p_c.astype(vc_b.dtype), vc_b)
    cur = qpos // NSA_BLK
    cand = (blk[None, :] < cur[:, None])[None, :, None, :]
    score = jnp.where(cand, p_c.sum(axis=3), -1.0)
    _, idx = lax.top_k(score, min(NSA_TOPK, nb))
    cur_b = jnp.broadcast_to(cur[None, :, None, None], (b, t_len, NSA_G, 1))
    idx_all = jnp.concatenate([idx, cur_b], axis=-1)
    ok_all = jnp.concatenate([idx < cur[None, :, None, None], jnp.ones((b, t_len, NSA_G, 1), bool)], axis=-1)
    pos5 = idx_all[..., None] * NSA_BLK + jnp.arange(NSA_BLK, dtype=jnp.int32)
    ok5 = ok_all[..., None] & (pos5 <= qpos[None, :, None, None, None])
    pos = pos5.reshape(b, t_len, NSA_G, -1)
    ok_s = ok5.reshape(b, t_len, NSA_G, 1, -1)
    k_s, v_s = fetch(pos)
    tab3 = tab.astype(f32).reshape(REL_BUCKETS, NSA_G, NSA_HG)
    bias_s = jnp.moveaxis(tab3[_bucket(qpos[None, :, None, None] - pos), jnp.arange(NSA_G)[:, None]], -1, 3)
    lg_s = jnp.einsum('btghd,btgnd->btghn', q, k_s).astype(f32) + bias_s
    p_s = jax.nn.softmax(jnp.where(ok_s, lg_s, NEG), axis=-1)
    o_s = jnp.einsum('btghn,btgnd->btghd', p_s.astype(v_s.dtype), v_s)
    dist_w = qpos[:, None] - kwpos[None, :]
    ok_w = ((dist_w >= 0) & (dist_w < NSA_WIN) & (kwpos >= 0)[None, :])[:, None, None, :]
    lg_w = jnp.einsum('btghd,bngd->btghn', q, kw).astype(f32) + _tok_bias(dist_w, tab)
    p_w = jax.nn.softmax(jnp.where(ok_w, lg_w, NEG), axis=-1)
    o_w = jnp.einsum('btghn,bngd->btghd', p_w.astype(vw.dtype), vw)
    return jnp.stack([o_c, o_s, o_w], axis=-2)


def _over_query_blocks(fn, q, qpos):
    b, t_len = q.shape[:2]
    if t_len <= NSA_QB:
        return fn(q, qpos)
    nblk = -(-t_len // NSA_QB)
    qb = jnp.swapaxes(q.reshape(b, nblk, NSA_QB, *q.shape[2:]), 0, 1)
    pb = qpos.reshape(nblk, NSA_QB)
    out = lax.map(lambda a: fn(a[0], a[1]), (qb, pb))
    out = jnp.swapaxes(out, 0, 1)
    return out.reshape(b, nblk * NSA_QB, *out.shape[3:])[:, :t_len]


def _nsa_combine(o3, ng):
    b, t_len = ng.shape[:2]
    gates = jax.nn.sigmoid(ng.astype(jnp.float32)).reshape(b, t_len, NSA_G, NSA_HG, 3, 1)
    return (o3.astype(jnp.float32) * gates).sum(axis=-2).reshape(b, t_len, NSA_H * NSA_D).astype(o3.dtype)


def _nsa_prompt(q, kc, vc, ks, vs, kw, vw, w_ck, w_cv, g_kc, tab):
    b, t_len = q.shape[:2]
    kc_b = _rms(_compress(kc, w_ck), g_kc)
    vc_b = _compress(vc, w_cv)
    padw = ((0, 0), (NSA_WIN, 0), (0, 0), (0, 0))
    kw_pad = jnp.pad(kw, padw)
    vw_pad = jnp.pad(vw, padw)
    bi = jnp.arange(b)[:, None, None, None]
    gi = jnp.arange(NSA_G)[None, None, :, None]

    def fetch(pos):
        return ks[bi, pos, gi], vs[bi, pos, gi]

    def block(qb, pb):
        start = pb[0]
        n = NSA_WIN + qb.shape[1]
        kwb = lax.dynamic_slice_in_dim(kw_pad, start, n, axis=1)
        vwb = lax.dynamic_slice_in_dim(vw_pad, start, n, axis=1)
        kwpos = start - NSA_WIN + jnp.arange(n, dtype=jnp.int32)
        return _nsa_core(qb, pb, kc_b, vc_b, fetch, kwb, vwb, kwpos, tab)

    return _over_query_blocks(block, q, jnp.arange(t_len, dtype=jnp.int32))


def _nsa_sample(q, kc, vc, ks, vs, kw, vw, pool_kc, pool_vc, pool_ks, pool_vs, buf_kw, buf_vw,
                page_table, w_ck, w_cv, g_kc, tab):
    db, ds = q.shape[:2]

    def full_rows(pool, new):
        past = pool[page_table].reshape(db, PAST_LEN, NSA_G, NSA_D)
        r = jnp.concatenate([past, new], axis=1)
        pad = (-r.shape[1]) % NSA_BLK
        return jnp.pad(r, ((0, 0), (0, pad), (0, 0), (0, 0)))

    kc_b = _rms(_compress(full_rows(pool_kc, kc), w_ck), g_kc)
    vc_b = _compress(full_rows(pool_vc, vc), w_cv)
    bi = jnp.arange(db)[:, None, None, None]
    gi = jnp.arange(NSA_G)[None, None, :, None]

    def fetch(pos):
        in_past = (pos < PAST_LEN)[..., None]
        pp = jnp.minimum(pos, PAST_LEN - 1)
        phys = page_table[bi, pp // PAGE_SIZE]
        off = pp % PAGE_SIZE
        pn = jnp.clip(pos - PAST_LEN, 0, ds - 1)
        k = jnp.where(in_past, pool_ks[phys, off, gi], ks[bi, pn, gi])
        v = jnp.where(in_past, pool_vs[phys, off, gi], vs[bi, pn, gi])
        return k, v

    wb = buf_kw.shape[1]
    kw_all = jnp.concatenate([buf_kw, kw], axis=1)
    vw_all = jnp.concatenate([buf_vw, vw], axis=1)
    kwpos = PAST_LEN - wb + jnp.arange(wb + ds, dtype=jnp.int32)
    qpos = PAST_LEN + jnp.arange(ds, dtype=jnp.int32)
    o3 = _over_query_blocks(lambda qb, pb: _nsa_core(qb, pb, kc_b, vc_b, fetch, kw_all, vw_all, kwpos, tab), q, qpos)
    return o3, kw_all[:, -wb:], vw_all[:, -wb:]


def _even_inputs(xn, w_in, qk_gain):
    b, t_len, _ = xn.shape
    (gq, gk, gv, gz, ga, gb, nq, kc, vc, ks, vs, kw, vw, ng) = _split(_pmm(xn, w_in), EVEN_WIDTHS)
    hd = lambda t: t.reshape(b, t_len, NSA_G, NSA_D)
    q = _rms(nq.reshape(b, t_len, NSA_G, NSA_HG, NSA_D), qk_gain[0]) * (NSA_D ** -0.5)
    ks = _rms(hd(ks), qk_gain[2])
    kw = _rms(hd(kw), qk_gain[3])
    return (gq, gk, gv, gz, ga, gb), (q, hd(kc), hd(vc), ks, hd(vs), kw, hd(vw), ng)


def _even_prompt(xn, w_in, w_out, conv_w, a_log, dt_bias, gnorm, qk_gain, w_ck, w_cv, tab, bias_tiles):
    b, t_len, d = xn.shape
    proj = _matmul(xn.reshape(b * t_len, d).astype(MXU_DTYPE), _permute_even_w(w_in))
    p3 = proj.reshape(b, t_len, EV_N)
    gq, gk, gv = _split(p3[..., :GDN_CONV_CH], (GDN_QK, GDN_QK, GDN_V))
    gdn_parts = (gq, gk, gv, p3[..., EV_Z:EV_Z + GDN_V], p3[..., EV_SMALL:EV_SMALL + GDN_VH],
                 p3[..., EV_SMALL + GDN_VH:EV_SMALL + 2 * GDN_VH])
    conv0 = jnp.zeros((b, GDN_CONV - 1, GDN_CONV_CH), xn.dtype)
    s0 = jnp.zeros((b, GDN_VH, GDN_DK, GDN_DV), xn.dtype)
    ya, conv_new, s = _gdn(gdn_parts, conv_w, a_log, dt_bias, gnorm, conv0, s0)
    qn, ksn, kwn, ksb, vsb, kwb, vwb, kcb, vcb = _nsa_prep(proj, qk_gain, w_ck, w_cv)
    yb = _nsa_prompt_attn(tab, bias_tiles, proj, qn, kcb, vcb, ksb, vsb, kwb, vwb, b, t_len)
    y = _pmm(jnp.concatenate([ya, yb.reshape(b, t_len, -1).astype(ya.dtype)], axis=-1), w_out)
    nw = min(NSA_WIN, t_len)
    hd = lambda t: t.reshape(b, t_len, NSA_G, NSA_D)
    kv = lambda i: hd(p3[..., EV_KV + i * NSA_KV:EV_KV + (i + 1) * NSA_KV])
    return y, (kv(0), kv(1), hd(ksn), kv(3), hd(kwn)[:, -nw:], kv(5)[:, -nw:], conv_new, s)


def _even_sample(xn, w_in, w_out, conv_w, a_log, dt_bias, gnorm, qk_gain, w_ck, w_cv, tab,
                 pool_kc, pool_vc, pool_ks, pool_vs, buf_kw, buf_vw, conv_st, s_st, page_table):
    gdn_parts, (q, kc, vc, ks, vs, kw, vw, ng) = _even_inputs(xn, w_in, qk_gain)
    ya, conv_new, s = _gdn(gdn_parts, conv_w, a_log, dt_bias, gnorm, conv_st, s_st)
    o3, kw_buf, vw_buf = _nsa_sample(q, kc, vc, ks, vs, kw, vw, pool_kc, pool_vc, pool_ks, pool_vs,
                                     buf_kw, buf_vw, page_table, w_ck, w_cv, qk_gain[1], tab)
    y = _pmm(jnp.concatenate([ya, _nsa_combine(o3, ng)], axis=-1), w_out)
    return y, (kc, vc, ks, vs, kw_buf, vw_buf, conv_new, s)


def _mlstm_chunked(q, k, v, ig, lf, c0, n0, m0):
    b, t_len, h, _ = q.shape
    c = min(ML_CHUNK, t_len)
    pad = (-t_len) % c
    incl = jnp.tril(jnp.ones((c, c), bool), 0)

    def body(carry, inp):
        cm, nv, m = carry
        qc, kc, vc, ic, fc = inp
        fcum = jnp.cumsum(fc, axis=-1)
        a = ic - fcum
        mt = fcum + jnp.maximum(m[..., None], lax.cummax(a, axis=2))
        dmat = jnp.exp(jnp.where(incl, a[..., None, :] + (fcum - mt)[..., :, None], -jnp.inf))
        dec0 = jnp.exp(fcum + m[..., None] - mt)
        s = jnp.einsum('bhid,bhjd->bhij', qc, kc) * dmat
        num = dec0[..., None] * jnp.einsum('bhid,bhde->bhie', qc, cm) + jnp.einsum('bhij,bhje->bhie', s, vc)
        den = dec0 * jnp.einsum('bhid,bhd->bhi', qc, nv) + s.sum(axis=-1)
        hc = num / jnp.maximum(jnp.abs(den), jnp.exp(-mt))[..., None]
        m_end = mt[..., -1]
        w = jnp.exp(a + (fcum[..., -1] - m_end)[..., None])
        dc = jnp.exp(fcum[..., -1] + m - m_end)
        cm = dc[..., None, None] * cm + jnp.einsum('bhj,bhjd,bhje->bhde', w, kc, vc)
        nv = dc[..., None] * nv + jnp.einsum('bhj,bhjd->bhd', w, kc)
        return (cm, nv, m_end), hc

    f32 = jnp.float32
    (cm, nv, m), hs = lax.scan(body, (c0.astype(f32), n0.astype(f32), m0.astype(f32)),
                               (_chunks(q, c, pad), _chunks(k, c, pad), _chunks(v, c, pad),
                                _chunks(ig, c, pad, NEG), _chunks(lf, c, pad)))
    return _unchunk(hs, t_len).astype(v.dtype), cm.astype(c0.dtype), nv.astype(n0.dtype), m.astype(m0.dtype)


def _odd(xn, w_in, w_out, b_if, mnorm, c0, n0, m0):
    b, t_len, _ = xn.shape
    q, k, v, o, ig, fg = _split(_pmm(xn, w_in), ODD_WIDTHS)
    q = q.reshape(b, t_len, ML_H, ML_DK) * (ML_DK ** -0.5)
    k = k.reshape(b, t_len, ML_H, ML_DK)
    v = v.reshape(b, t_len, ML_H, ML_DV)
    bf = b_if.astype(jnp.float32)
    ig = ig.astype(jnp.float32) + bf[:ML_H]
    lf = jax.nn.log_sigmoid(fg.astype(jnp.float32) + bf[ML_H:])
    hc, cm, nv, m = _mlstm_chunked(q, k, v, ig, lf, c0, n0, m0)
    hc = _rms(hc, mnorm.reshape(ML_H, ML_DV)).reshape(b, t_len, ODD_MIX)
    return _pmm(hc * jax.nn.sigmoid(o), w_out), (cm, nv, m)


def _rms_cast_kernel(x_ref, g_ref, o_ref):
    x = x_ref[...]
    y = x * lax.rsqrt(jnp.mean(x * x, axis=-1, keepdims=True) + RMS_EPS) * g_ref[...]
    o_ref[...] = y.astype(o_ref.dtype)


def _rms_cast(h, g):
    m, d = h.shape
    tm = _pick(m, (256, 128))
    return pl.pallas_call(
        _rms_cast_kernel,
        grid=(m // tm,),
        in_specs=[pl.BlockSpec((tm, d), lambda i: (i, 0)), pl.BlockSpec((1, d), lambda i: (0, 0))],
        out_specs=pl.BlockSpec((tm, d), lambda i: (i, 0)),
        out_shape=jax.ShapeDtypeStruct((m, d), MXU_DTYPE),
        compiler_params=pltpu.CompilerParams(dimension_semantics=("parallel",),
                                             vmem_limit_bytes=VMEM_LIMIT_BYTES),
    )(h, g.reshape(1, d).astype(jnp.float32))


def _mm2_kernel(xa_ref, xb_ref, w_ref, r_ref, o_ref):
    ka = xa_ref.shape[1]
    y = jnp.dot(xa_ref[...], w_ref[:ka, :], preferred_element_type=jnp.float32)
    y = y + jnp.dot(xb_ref[...], w_ref[ka:, :], preferred_element_type=jnp.float32)
    o_ref[...] = y + r_ref[...]


def _matmul2_res(xa, xb, w, res):
    m, ka = xa.shape
    kb = xb.shape[1]
    n = w.shape[1]
    tm = _pick(m, (512, 256, 128))
    tn = _pick(n, (512, 256, 128))
    return pl.pallas_call(
        _mm2_kernel,
        grid=(m // tm, n // tn),
        in_specs=[pl.BlockSpec((tm, ka), lambda i, j: (i, 0)), pl.BlockSpec((tm, kb), lambda i, j: (i, 0)),
                  pl.BlockSpec((ka + kb, tn), lambda i, j: (0, j)), pl.BlockSpec((tm, tn), lambda i, j: (i, j))],
        out_specs=pl.BlockSpec((tm, tn), lambda i, j: (i, j)),
        out_shape=jax.ShapeDtypeStruct((m, n), jnp.float32),
        compiler_params=pltpu.CompilerParams(dimension_semantics=("parallel", "parallel"),
                                             vmem_limit_bytes=VMEM_LIMIT_BYTES),
    )(xa, xb, w, res)


def _mlp_block(h, g, w1, w2):
    mid = _matmul(_rms_cast(h, g), w1, act='relu2', out_dtype=MXU_DTYPE)
    return _matmul(mid, w2, res=h)


def _pad_time(x, t_pad):
    b, d = x.shape
    return jnp.zeros((b, t_pad, d), x.dtype).at[:, 0, :].set(x).reshape(b * t_pad, d)


def _even_layer(h, g_mix, w_in, w_out, conv_w, a_log, dt_bias, gnorm, qk_gain, w_ck, w_cv, tab, bias_tiles,
                batch, t_len, conv_st, s_st, decode):
    f32 = jnp.float32
    proj = _matmul(_rms_cast(h, g_mix), w_in)
    t_pad = t_len if decode is None else GDN_CHUNK
    pp = proj if decode is None else _pad_time(proj, t_pad)
    first = jnp.pad(conv_st.astype(f32), ((0, 0), (8 - (GDN_CONV - 1), 0), (0, 0)))
    q, k, v, gb = _gdn_prep(pp, first, pp, EV_SMALL // 128, conv_w, a_log, dt_bias, batch, t_pad, t_len)
    ya, s_new = _gdn_chunks(q, k, v, gb, pp, EV_Z // GDN_V, s_st, gnorm, batch, t_pad)
    p3 = proj.reshape(batch, t_len, EV_N)
    xin_tail = jnp.concatenate([conv_st.astype(f32), p3[..., :GDN_CONV_CH]], axis=1)[:, -(GDN_CONV - 1):]
    hd = lambda t: t.reshape(batch, t_len, NSA_G, NSA_D)
    kv = lambda i: hd(p3[..., EV_KV + i * NSA_KV:EV_KV + (i + 1) * NSA_KV])
    if decode is None:
        qn, ksn, kwn, ksb, vsb, kwb, vwb, kcb, vcb = _nsa_prep(proj, qk_gain, w_ck, w_cv)
        yb = _nsa_prompt_attn(tab, bias_tiles, proj, qn, kcb, vcb, ksb, vsb, kwb, vwb, batch, t_len)
        nw = min(NSA_WIN, t_len)
        kw_out, vw_out = hd(kwn)[:, -nw:], kv(5)[:, -nw:]
        ks_out = hd(ksn)
    else:
        ya = ya.reshape(batch, t_pad, GDN_V)[:, 0]
        yb, ksn, kwn = _nsa_decode(proj, tab, qk_gain, w_ck, w_cv, decode, batch)
        yb = yb.astype(MXU_DTYPE)
        ks_out = hd(ksn)
        kw_out = jnp.concatenate([decode['kw'][:, 1:], hd(kwn)], axis=1)
        vw_out = jnp.concatenate([decode['vw'][:, 1:], kv(5)], axis=1)
    h = _matmul2_res(ya, yb, w_out, h)
    return h, (kv(0), kv(1), ks_out, kv(3), kw_out, vw_out, xin_tail, s_new)


def _odd_layer(h, g_mix, w_in, w_out, b_if, mnorm, batch, t_len, c0, n0, m0, decode):
    proj = _matmul(_rms_cast(h, g_mix), w_in)
    t_pad = t_len if not decode else ML_CHUNK
    pp = proj if not decode else _pad_time(proj, t_pad)
    y, c, n, m = _mlstm_chunks(pp, b_if, mnorm, c0, n0, m0, batch, t_pad, t_len)
    if decode:
        y = y.reshape(batch, t_pad, ODD_MIX)[:, 0]
    return _matmul(y, w_out, res=h), (c, n, m)


def kernel(x_prompt, x_sample, cache_kc, cache_vc, cache_ks, cache_vs, cache_kw, cache_vw,
           state_gdn_conv, state_gdn_s, state_ml_c, state_ml_n, state_ml_m, page_table,
           rel_bias, norm_mix, norm_mlp, w_in_even, w_out_even, gdn_conv_w, gdn_a_log,
           gdn_dt_bias, gdn_norm_w, nsa_qk_gain, nsa_w_ck, nsa_w_cv, w_in_odd, w_out_odd,
           ml_b_if, ml_norm_w, w_ff1, w_ff2):
    f32 = jnp.float32
    bp, tp, d = x_prompt.shape
    bs, ts, _ = x_sample.shape
    hp = x_prompt.reshape(bp * tp, d)
    hs = x_sample.reshape(bs * ts, d)
    bias_tiles = _bias_tiles(rel_bias)
    ev_p, ev_s, od_p, od_s = [], [], [], []
    for l in range(DEPTH):
        j = l // 2
        if l % 2 == 0:
            w_in = _permute_even_w(w_in_even[j])
            w_out = w_out_even[j].astype(MXU_DTYPE)
            args = (w_in, w_out, gdn_conv_w[j], gdn_a_log[j], gdn_dt_bias[j], gdn_norm_w[j], nsa_qk_gain[j],
                    nsa_w_ck[j], nsa_w_cv[j], rel_bias, bias_tiles)
            hp, stp = _even_layer(hp, norm_mix[l], *args, bp, tp,
                                  jnp.zeros((bp, GDN_CONV - 1, GDN_CONV_CH), f32),
                                  jnp.zeros((bp, GDN_VH, GDN_DK, GDN_DV), f32), None)
            caches = dict(kc=cache_kc[j], vc=cache_vc[j], ks=cache_ks[j], vs=cache_vs[j], kw=cache_kw[j],
                          vw=cache_vw[j], page_table=page_table)
            hs, sts = _even_layer(hs, norm_mix[l], *args, bs, ts, state_gdn_conv[j], state_gdn_s[j], caches)
            ev_p.append(stp)
            ev_s.append(sts)
        else:
            w_in = _permute_odd_w(w_in_odd[j])
            w_out = w_out_odd[j].astype(MXU_DTYPE)
            hp, stp = _odd_layer(hp, norm_mix[l], w_in, w_out, ml_b_if[j], ml_norm_w[j], bp, tp,
                                 jnp.zeros((bp, ML_H, ML_DK, ML_DV), f32), jnp.zeros((bp, ML_H, ML_DK), f32),
                                 jnp.zeros((bp, ML_H), f32), False)
            hs, sts = _odd_layer(hs, norm_mix[l], w_in, w_out, ml_b_if[j], ml_norm_w[j], bs, ts,
                                 state_ml_c[j], state_ml_n[j], state_ml_m[j], True)
            od_p.append(stp)
            od_s.append(sts)
        w1 = w_ff1[l].astype(MXU_DTYPE)
        w2 = w_ff2[l].astype(MXU_DTYPE)
        hp = _mlp_block(hp, norm_mlp[l], w1, w2)
        hs = _mlp_block(hs, norm_mlp[l], w1, w2)

    kc_p, vc_p, ks_p, vs_p, kw_p, vw_p, conv_p, gdn_p = [jnp.stack(a) for a in zip(*ev_p)]
    kc_s, vc_s, ks_s, vs_s, kw_s, vw_s, conv_s, gdn_s = [jnp.stack(a) for a in zip(*ev_s)]
    mlc_p, mln_p, mlm_p = [jnp.stack(a) for a in zip(*od_p)]
    mlc_s, mln_s, mlm_s = [jnp.stack(a) for a in zip(*od_s)]
    return (hp.reshape(bp, tp, d), hs.reshape(bs, ts, d), kc_p, kc_s, vc_p, vc_s, ks_p, ks_s, vs_p, vs_s,
            kw_p, kw_s, vw_p, vw_s, conv_p, conv_s, gdn_p, gdn_s, mlc_p, mlc_s, mln_p, mln_s, mlm_p, mlm_s)


def _kernel_old(x_prompt, x_sample, cache_kc, cache_vc, cache_ks, cache_vs, cache_kw, cache_vw,
           state_gdn_conv, state_gdn_s, state_ml_c, state_ml_n, state_ml_m, page_table,
           rel_bias, norm_mix, norm_mlp, w_in_even, w_out_even, gdn_conv_w, gdn_a_log,
           gdn_dt_bias, gdn_norm_w, nsa_qk_gain, nsa_w_ck, nsa_w_cv, w_in_odd, w_out_odd,
           ml_b_if, ml_norm_w, w_ff1, w_ff2):
    hp, hs = x_prompt, x_sample
    bias_tiles = _bias_tiles(rel_bias)
    ev_p, ev_s, od_p, od_s = [], [], [], []
    for l in range(DEPTH):
        j = l // 2
        xp = _rms(hp, norm_mix[l])
        xs = _rms(hs, norm_mix[l])
        if l % 2 == 0:
            yp, stp = _even_prompt(xp, w_in_even[j], w_out_even[j], gdn_conv_w[j], gdn_a_log[j],
                                   gdn_dt_bias[j], gdn_norm_w[j], nsa_qk_gain[j], nsa_w_ck[j],
                                   nsa_w_cv[j], rel_bias, bias_tiles)
            ys, sts = _even_sample(xs, w_in_even[j], w_out_even[j], gdn_conv_w[j], gdn_a_log[j],
                                   gdn_dt_bias[j], gdn_norm_w[j], nsa_qk_gain[j], nsa_w_ck[j],
                                   nsa_w_cv[j], rel_bias, cache_kc[j], cache_vc[j], cache_ks[j],
                                   cache_vs[j], cache_kw[j], cache_vw[j], state_gdn_conv[j],
                                   state_gdn_s[j], page_table)
            ev_p.append(stp)
            ev_s.append(sts)
        else:
            bp = hp.shape[0]
            c0 = jnp.zeros((bp, ML_H, ML_DK, ML_DV), hp.dtype)
            n0 = jnp.zeros((bp, ML_H, ML_DK), hp.dtype)
            m0 = jnp.zeros((bp, ML_H), hp.dtype)
            yp, stp = _odd(xp, w_in_odd[j], w_out_odd[j], ml_b_if[j], ml_norm_w[j], c0, n0, m0)
            ys, sts = _odd(xs, w_in_odd[j], w_out_odd[j], ml_b_if[j], ml_norm_w[j],
                           state_ml_c[j], state_ml_n[j], state_ml_m[j])
            od_p.append(stp)
            od_s.append(sts)
        hp = hp + yp
        hs = hs + ys
        hp = hp + _mlp(_rms(hp, norm_mlp[l]), w_ff1[l], w_ff2[l])
        hs = hs + _mlp(_rms(hs, norm_mlp[l]), w_ff1[l], w_ff2[l])

    kc_p, vc_p, ks_p, vs_p, kw_p, vw_p, conv_p, gdn_p = [jnp.stack(a) for a in zip(*ev_p)]
    kc_s, vc_s, ks_s, vs_s, kw_s, vw_s, conv_s, gdn_s = [jnp.stack(a) for a in zip(*ev_s)]
    mlc_p, mln_p, mlm_p = [jnp.stack(a) for a in zip(*od_p)]
    mlc_s, mln_s, mlm_s = [jnp.stack(a) for a in zip(*od_s)]
    return (hp, hs, kc_p, kc_s, vc_p, vc_s, ks_p, ks_s, vs_p, vs_s, kw_p, kw_s, vw_p, vw_s,
            conv_p, conv_s, gdn_p, gdn_s, mlc_p, mlc_s, mln_p, mln_s, mlm_p, mlm_s)
```

```python
import functools
import math

import jax
import jax.numpy as jnp
import numpy as np
from jax import lax
from jax.experimental import pallas as pl
from jax.experimental.pallas import tpu as pltpu

D_MODEL = 2048
BATCH = 2
SEQ = 4096
DEPTH = 4
DEC_BATCH = 8
DEC_SEQ = 1
PAST_LEN = 16384
PAGE_SIZE = 128
RMS_EPS = 1e-6
NEG = -1e30
D_FF = 4 * D_MODEL

GDN_KH = 4
GDN_VH = 8
GDN_DK = 128
GDN_DV = 128
GDN_CONV = 4
GDN_CHUNK = 64
GDN_QK = GDN_KH * GDN_DK
GDN_V = GDN_VH * GDN_DV
GDN_CONV_CH = 2 * GDN_QK + GDN_V

NSA_H = 8
NSA_G = 2
NSA_HG = NSA_H // NSA_G
NSA_D = 128
NSA_BLK = 64
NSA_TOPK = 15
NSA_WIN = 512
NSA_QB = 128
NSA_KV = NSA_G * NSA_D

REL_BUCKETS = 32
REL_EXACT = 16
REL_MAX_DIST = 2048

ML_H = 8
ML_DK = 128
ML_DV = 256
ML_CHUNK = 64

EVEN_WIDTHS = (GDN_QK, GDN_QK, GDN_V, GDN_V, GDN_VH, GDN_VH, NSA_H * NSA_D,
               NSA_KV, NSA_KV, NSA_KV, NSA_KV, NSA_KV, NSA_KV, NSA_H * 3)
ODD_WIDTHS = (ML_H * ML_DK, ML_H * ML_DK, ML_H * ML_DV, ML_H * ML_DV, ML_H, ML_H)
ODD_MIX = ML_H * ML_DV

VMEM_LIMIT_BYTES = 56 * 1024 * 1024
MXU_DTYPE = jnp.bfloat16


def _mm_kernel(x_ref, w_ref, *rest, nk, act, has_res):
    if has_res:
        r_ref, o_ref, acc_ref = rest
    else:
        o_ref, acc_ref = rest
        r_ref = None
    k = pl.program_id(2)

    @pl.when(k == 0)
    def _():
        acc_ref[...] = jnp.zeros_like(acc_ref)

    acc_ref[...] += jnp.dot(x_ref[...], w_ref[...], preferred_element_type=jnp.float32)

    @pl.when(k == nk - 1)
    def _():
        y = acc_ref[...]
        if act == 'relu2':
            y = jnp.square(jnp.maximum(y, 0.0))
        if has_res:
            y = y + r_ref[...]
        o_ref[...] = y.astype(o_ref.dtype)


def _pick(n, prefs):
    for p in prefs:
        if n % p == 0:
            return p
    return n


def _matmul(x, w, *, act=None, res=None, out_dtype=jnp.float32):
    m, kdim = x.shape
    _, n = w.shape
    tm = _pick(m, (512, 256, 128))
    tn = _pick(n, (512, 896, 384, 256, 128))
    tk = _pick(kdim, (2048, 1024, 512))
    nk = kdim // tk
    in_specs = [pl.BlockSpec((tm, tk), lambda i, j, k: (i, k)),
                pl.BlockSpec((tk, tn), lambda i, j, k: (k, j))]
    args = [x, w]
    if res is not None:
        in_specs.append(pl.BlockSpec((tm, tn), lambda i, j, k: (i, j)))
        args.append(res)
    return pl.pallas_call(
        functools.partial(_mm_kernel, nk=nk, act=act, has_res=res is not None),
        grid=(m // tm, n // tn, nk),
        in_specs=in_specs,
        out_specs=pl.BlockSpec((tm, tn), lambda i, j, k: (i, j)),
        out_shape=jax.ShapeDtypeStruct((m, n), out_dtype),
        scratch_shapes=[pltpu.VMEM((tm, tn), jnp.float32)],
        compiler_params=pltpu.CompilerParams(
            dimension_semantics=("parallel", "parallel", "arbitrary"),
            vmem_limit_bytes=VMEM_LIMIT_BYTES),
    )(*args)


def _pmm(x, w):
    lead = x.shape[:-1]
    x2 = x.reshape(-1, x.shape[-1]).astype(MXU_DTYPE)
    n = w.shape[1]
    npad = (-n) % 128
    wb = w.astype(MXU_DTYPE)
    if npad:
        wb = jnp.pad(wb, ((0, 0), (0, npad)))
    y = _matmul(x2, wb)
    if npad:
        y = y[:, :n]
    return y.reshape(*lead, n)


EV_CONV = 0
EV_Z = GDN_CONV_CH
EV_NQ = EV_Z + GDN_V
EV_KV = EV_NQ + NSA_H * NSA_D
EV_SMALL = EV_KV + 6 * NSA_KV
EV_N = EV_SMALL + 128
EV_ORIG_SMALL = 2 * GDN_QK + 2 * GDN_V
EV_ORIG_N = EV_ORIG_SMALL + 2 * GDN_VH + NSA_H * NSA_D + 6 * NSA_KV + 3 * NSA_H


def _permute_even_w(w_in):
    ab = w_in[:, EV_ORIG_SMALL:EV_ORIG_SMALL + 2 * GDN_VH]
    big = w_in[:, EV_ORIG_SMALL + 2 * GDN_VH:EV_ORIG_N - 3 * NSA_H]
    ng = w_in[:, EV_ORIG_N - 3 * NSA_H:]
    pad = jnp.zeros((w_in.shape[0], EV_N - EV_ORIG_N), w_in.dtype)
    return jnp.concatenate([w_in[:, :EV_ORIG_SMALL], big, ab, ng, pad], axis=1).astype(MXU_DTYPE)


def _bucket_thresholds():
    n = np.arange(0, 4 * REL_MAX_DIST)
    nf = np.maximum(n, REL_EXACT).astype(np.float32)
    large = REL_EXACT + (np.log(nf / np.float32(REL_EXACT)) / np.float32(math.log(REL_MAX_DIST / REL_EXACT))
                         * np.float32(REL_BUCKETS - REL_EXACT)).astype(np.int32)
    b = np.where(n < REL_EXACT, n, np.minimum(large, REL_BUCKETS - 1))
    return tuple(int(np.argmax(b >= k)) for k in range(REL_BUCKETS))


BUCKET_THR = _bucket_thresholds()
BIAS_TILES = 14
assert BUCKET_THR[-1] <= (BIAS_TILES - 1) * 128 - 127


def _bias_of_dist(dist, tab_ref, head):
    val = jnp.full(dist.shape, tab_ref[0, head], jnp.float32)
    for k in range(1, REL_BUCKETS):
        val = jnp.where(dist >= BUCKET_THR[k], tab_ref[k, head], val)
    return val


def _bias_tiles_kernel(tab_ref, o_ref):
    h = pl.program_id(0)
    d = pl.program_id(1)
    i = lax.broadcasted_iota(jnp.int32, (128, 128), 1)
    j = lax.broadcasted_iota(jnp.int32, (128, 128), 0)
    dist = d * 128 + i - j
    val = jnp.full(dist.shape, tab_ref[0, h], jnp.float32)
    for k in range(1, REL_BUCKETS):
        val = jnp.where(dist >= BUCKET_THR[k], tab_ref[k, h], val)
    o_ref[0, 0] = val


def _bias_tiles(tab):
    return pl.pallas_call(
        _bias_tiles_kernel,
        grid=(NSA_H, BIAS_TILES),
        in_specs=[pl.BlockSpec(memory_space=pltpu.SMEM)],
        out_specs=pl.BlockSpec((1, 1, 128, 128), lambda h, d: (h, d, 0, 0)),
        out_shape=jax.ShapeDtypeStruct((NSA_H, BIAS_TILES, 128, 128), jnp.float32),
    )(tab.astype(jnp.float32))


def _group_rms(x, gain_row, scale=1.0):
    ms = jnp.mean(x * x, axis=-1, keepdims=True)
    y = x * lax.rsqrt(ms + RMS_EPS) * gain_row
    return y * scale if scale != 1.0 else y


def _nsa_prep_kernel(nq_ref, kc_ref, vc_ref, ks_ref, vs_ref, kw_ref, vw_ref, gain_ref, wck_ref, wcv_ref,
                     qn_ref, ksn_ref, kwn_ref, ksb_ref, vst_ref, kwb_ref, vwt_ref, kcb_ref, vcb_ref):
    tm = nq_ref.shape[0]
    g0 = gain_ref[0:1, :]
    g1 = gain_ref[1:2, :]
    g2 = gain_ref[2:3, :]
    g3 = gain_ref[3:4, :]
    for h in range(NSA_H):
        sl = slice(h * NSA_D, (h + 1) * NSA_D)
        qn_ref[:, sl] = _group_rms(nq_ref[:, sl], g0, NSA_D ** -0.5).astype(qn_ref.dtype)
    for g in range(NSA_G):
        sl = slice(g * NSA_D, (g + 1) * NSA_D)
        ksn = _group_rms(ks_ref[:, sl], g2)
        kwn = _group_rms(kw_ref[:, sl], g3)
        ksn_ref[:, sl] = ksn
        kwn_ref[:, sl] = kwn
        ksb_ref[:, sl] = ksn.astype(ksb_ref.dtype)
        kwb_ref[:, sl] = kwn.astype(kwb_ref.dtype)
    for src, dst in ((vs_ref, vst_ref), (vw_ref, vwt_ref)):
        vt = src[...].T
        tk = dst.shape[2]
        for c in range(tm // tk):
            dst[c] = vt[:, c * tk:(c + 1) * tk].astype(dst.dtype)
    nblk = tm // NSA_BLK
    kc3 = kc_ref[...].reshape(nblk, NSA_BLK, NSA_KV)
    vc3 = vc_ref[...].reshape(nblk, NSA_BLK, NSA_KV)
    kcb = jnp.sum(kc3 * wck_ref[...][None], axis=1)
    vcb = jnp.sum(vc3 * wcv_ref[...][None], axis=1)
    for g in range(NSA_G):
        sl = slice(g * NSA_D, (g + 1) * NSA_D)
        kcb_ref[:, sl] = _group_rms(kcb[:, sl], g1)
    vcb_ref[...] = vcb


def _nsa_prep(proj, gain, w_ck, w_cv):
    m = proj.shape[0]
    tm = 512
    kvb = EV_KV // NSA_KV
    f32, bf16 = jnp.float32, MXU_DTYPE

    def kv_spec(i):
        return pl.BlockSpec((tm, NSA_KV), lambda r, i=i: (r, kvb + i))

    row = lambda w: pl.BlockSpec((tm, w), lambda r: (r, 0))
    slabs = lambda tk: pl.BlockSpec((tm // tk, NSA_KV, tk), lambda r: (r, 0, 0))
    full = lambda a: pl.BlockSpec(a.shape, lambda r: (0,) * a.ndim)
    wck = w_ck.reshape(NSA_BLK, NSA_KV)
    wcv = w_cv.reshape(NSA_BLK, NSA_KV)
    outs = pl.pallas_call(
        _nsa_prep_kernel,
        grid=(m // tm,),
        in_specs=[pl.BlockSpec((tm, NSA_H * NSA_D), lambda r: (r, EV_NQ // (NSA_H * NSA_D)))]
                 + [kv_spec(i) for i in range(6)] + [full(gain), full(wck), full(wcv)],
        out_specs=[row(NSA_H * NSA_D), row(NSA_KV), row(NSA_KV), row(NSA_KV), slabs(NSA_TK), row(NSA_KV),
                   slabs(NSA_QB)] + [pl.BlockSpec((tm // NSA_BLK, NSA_KV), lambda r: (r, 0))] * 2,
        out_shape=[jax.ShapeDtypeStruct((m, NSA_H * NSA_D), bf16),
                   jax.ShapeDtypeStruct((m, NSA_KV), f32), jax.ShapeDtypeStruct((m, NSA_KV), f32),
                   jax.ShapeDtypeStruct((m, NSA_KV), bf16), jax.ShapeDtypeStruct((m // NSA_TK, NSA_KV, NSA_TK), bf16),
                   jax.ShapeDtypeStruct((m, NSA_KV), bf16), jax.ShapeDtypeStruct((m // NSA_QB, NSA_KV, NSA_QB), bf16)]
                  + [jax.ShapeDtypeStruct((m // NSA_BLK, NSA_KV), f32)] * 2,
        compiler_params=pltpu.CompilerParams(dimension_semantics=("parallel",),
                                             vmem_limit_bytes=VMEM_LIMIT_BYTES),
    )(proj, proj, proj, proj, proj, proj, proj, gain, wck, wcv)
    return outs


NSA_TK = 256


def _softmax_rows(lg):
    m = jnp.max(lg, axis=-1, keepdims=True)
    e = jnp.exp(lg - m)
    return e / jnp.sum(e, axis=-1, keepdims=True)


def _pad_rows(x, rows):
    if x.shape[0] == rows:
        return x
    return jnp.concatenate([x, jnp.zeros((rows - x.shape[0],) + x.shape[1:], x.dtype)], axis=0)


def _nsa_attn_kernel(tab_ref, q_ref, kcb_ref, vcb_ref, ks_ref, vst_ref, kw_ref, vwt_ref, bias_ref, ng_ref,
                     o_ref, m_scr, l_scr, acc_scr):
    f32, bf16 = jnp.float32, MXU_DTYPE
    qb = pl.program_id(1)
    nb = kcb_ref.shape[0]
    qn = NSA_QB
    gates = jax.nn.sigmoid(ng_ref[...])
    tsub = NSA_TK // qn
    shift = NSA_BLK.bit_length() - 1

    for g in range(NSA_G):
        gl = slice(g * NSA_D, (g + 1) * NSA_D)
        qs = [q_ref[:, (g * NSA_HG + h) * NSA_D:(g * NSA_HG + h + 1) * NSA_D] for h in range(NSA_HG)]
        q2 = jnp.concatenate(qs, axis=0)
        qts = [q.astype(f32).T.astype(bf16) for q in qs]

        kcb = kcb_ref[:, gl].astype(bf16)
        vcb = _pad_rows(vcb_ref[:, gl], qn).astype(bf16)
        lgt = lax.dot_general(kcb, q2, (((1,), (1,)), ((), ())), preferred_element_type=f32)
        blk = lax.broadcasted_iota(jnp.int32, (nb, qn), 0)
        blkf = blk.astype(f32)
        qpos = qb * qn + lax.broadcasted_iota(jnp.int32, (nb, qn), 1)
        dist_c = qpos - (blk * NSA_BLK + NSA_BLK - 1)
        ok_c = dist_c >= 0
        score = jnp.zeros((nb, qn), f32)
        o_c = []
        for h in range(NSA_HG):
            lg = lgt[:, h * qn:(h + 1) * qn] + _bias_of_dist(dist_c, tab_ref, g * NSA_HG + h)
            lg = jnp.where(ok_c, lg, NEG)
            mx = jnp.max(lg, axis=0, keepdims=True)
            e = jnp.exp(lg - mx)
            p = jnp.where(ok_c, e / jnp.sum(e, axis=0, keepdims=True), 0.0)
            score = score + p
            p_t = _pad_rows(p, qn).T.astype(bf16)
            o_c.append(jnp.dot(p_t, vcb, preferred_element_type=f32))
        cur = lax.shift_right_logical(qpos, shift)
        score = jnp.where(blk < cur, score, -1.0)

        sel = jnp.zeros((nb, qn), f32)
        for _ in range(min(NSA_TOPK, nb)):
            mx = jnp.max(score, axis=0, keepdims=True)
            first = jnp.min(jnp.where(score == mx, blkf, float(nb)), axis=0, keepdims=True)
            pick = blkf == first
            sel = jnp.where(pick, jnp.where(mx >= 0.0, 1.0, 0.0), sel)
            score = jnp.where(pick, -2.0, score)
        sel = jnp.where(blk == cur, 1.0, sel)
        sel_p = _pad_rows(sel, qn).astype(bf16)

        def reset():
            m_scr[...] = jnp.full(m_scr.shape, NEG, f32)
            l_scr[...] = jnp.zeros(l_scr.shape, f32)
            acc_scr[...] = jnp.zeros(acc_scr.shape, f32)

        def update(h, lg, vt):
            m_old = m_scr[h]
            m_new = jnp.maximum(m_old, jnp.max(lg, axis=0, keepdims=True))
            p = jnp.exp(lg - m_new)
            alpha = jnp.exp(m_old - m_new)
            l_scr[h] = alpha * l_scr[h] + jnp.sum(p, axis=0, keepdims=True)
            acc_scr[h] = alpha * acc_scr[h] + jnp.dot(vt, p.astype(bf16), preferred_element_type=f32)
            m_scr[h] = m_new

        def result(h):
            return (acc_scr[h] / l_scr[h]).T

        reset()
        n_tiles = qb // tsub + 1
        tok = qb * qn + lax.broadcasted_iota(jnp.int32, (NSA_TK, qn), 1)
        key = lax.broadcasted_iota(jnp.int32, (NSA_TK, qn), 0)
        eblk = lax.broadcasted_iota(jnp.int32, (NSA_TK, qn), 1)

        def body(i, carry, g=g, gl=gl, qts=qts, sel_p=sel_p):
            kt = n_tiles - 1 - i
            k0 = pl.multiple_of(kt * NSA_TK, NSA_TK)
            k = ks_ref[pl.ds(k0, NSA_TK), gl]
            vt = vst_ref[kt, gl, :]
            expand = jnp.where(eblk == lax.shift_right_logical(k0 + key, shift), 1.0, 0.0).astype(bf16)
            member = jnp.dot(expand, sel_p, preferred_element_type=f32)
            keep = jnp.where(k0 + key <= tok, member, 0.0) > 0.5
            for h in range(NSA_HG):
                tiles = [bias_ref[g * NSA_HG + h, jnp.clip(qb - (kt * tsub + j), 0, BIAS_TILES - 1)]
                         for j in range(tsub)]
                lg = jnp.dot(k, qts[h], preferred_element_type=f32) + jnp.concatenate(tiles, axis=0)
                update(h, jnp.where(keep, lg, NEG), vt)
            return carry

        lax.fori_loop(0, n_tiles, body, 0)
        o_s = [result(h) for h in range(NSA_HG)]

        reset()
        nwin = NSA_WIN // qn + 1
        wi = lax.broadcasted_iota(jnp.int32, (qn, qn), 1)
        wj = lax.broadcasted_iota(jnp.int32, (qn, qn), 0)
        for d in range(nwin):
            sub = qb - d
            subc = jnp.maximum(sub, 0)
            k = kw_ref[pl.ds(pl.multiple_of(subc * qn, qn), qn), gl]
            vt = vwt_ref[subc, gl, :]
            for h in range(NSA_HG):
                x = jnp.dot(k, qts[h], preferred_element_type=f32) + bias_ref[g * NSA_HG + h, d]
                if d == 0:
                    x = jnp.where(wj <= wi, x, NEG)
                else:
                    if d == nwin - 1:
                        x = jnp.where(wj > wi, x, NEG)
                    x = jnp.where(sub >= 0, x, NEG)
                update(h, x, vt)
        o_w = [result(h) for h in range(NSA_HG)]

        for h in range(NSA_HG):
            c = 2 * GDN_VH + (g * NSA_HG + h) * 3
            y = o_c[h] * gates[:, c:c + 1] + o_s[h] * gates[:, c + 1:c + 2] + o_w[h] * gates[:, c + 2:c + 3]
            o_ref[:, (g * NSA_HG + h) * NSA_D:(g * NSA_HG + h + 1) * NSA_D] = y.astype(o_ref.dtype)


def _nsa_prompt_attn(tab, bias_tiles, proj, qn, kcb, vcb, ksb, vst, kwb, vwt, batch, t_len):
    nq = t_len // NSA_QB
    nb = t_len // NSA_BLK
    hd = NSA_H * NSA_D
    per_b = lambda w: pl.BlockSpec((t_len, w), lambda b, i: (b, 0))
    tiles_b = lambda tk: pl.BlockSpec((t_len // tk, NSA_KV, tk), lambda b, i: (b, 0, 0))
    return pl.pallas_call(
        _nsa_attn_kernel,
        grid=(batch, nq),
        in_specs=[pl.BlockSpec(memory_space=pltpu.SMEM),
                  pl.BlockSpec((NSA_QB, hd), lambda b, i: (b * nq + i, 0)),
                  pl.BlockSpec((nb, NSA_KV), lambda b, i: (b, 0)),
                  pl.BlockSpec((nb, NSA_KV), lambda b, i: (b, 0)),
                  per_b(NSA_KV), tiles_b(NSA_TK), per_b(NSA_KV), tiles_b(NSA_QB),
                  pl.BlockSpec(bias_tiles.shape, lambda b, i: (0, 0, 0, 0)),
                  pl.BlockSpec((NSA_QB, 128), lambda b, i: (b * nq + i, EV_SMALL // 128 + 0))],
        out_specs=pl.BlockSpec((NSA_QB, hd), lambda b, i: (b * nq + i, 0)),
        out_shape=jax.ShapeDtypeStruct((batch * t_len, hd), MXU_DTYPE),
        scratch_shapes=[pltpu.VMEM((NSA_HG, 1, NSA_QB), jnp.float32),
                        pltpu.VMEM((NSA_HG, 1, NSA_QB), jnp.float32),
                        pltpu.VMEM((NSA_HG, NSA_D, NSA_QB), jnp.float32)],
        compiler_params=pltpu.CompilerParams(dimension_semantics=("parallel", "arbitrary"),
                                             vmem_limit_bytes=VMEM_LIMIT_BYTES),
    )(tab.astype(jnp.float32), qn, kcb, vcb, ksb, vst, kwb, vwt, bias_tiles, proj)


def _hdot(a, b):
    return jnp.dot(a, b, precision=lax.Precision.HIGHEST, preferred_element_type=jnp.float32)


def _mdot(a, b):
    return jnp.dot(a.astype(MXU_DTYPE), b.astype(MXU_DTYPE), preferred_element_type=jnp.float32)


def _mdot_nt(a, b):
    return lax.dot_general(a.astype(MXU_DTYPE), b.astype(MXU_DTYPE), (((1,), (1,)), ((), ())),
                           preferred_element_type=jnp.float32)


def _mdot_tn(a, b):
    return lax.dot_general(a.astype(MXU_DTYPE), b.astype(MXU_DTYPE), (((0,), (0,)), ((), ())),
                           preferred_element_type=jnp.float32)


def _shift_rows(x, prev, s):
    xs = pltpu.roll(x, s, axis=0)
    ps = pltpu.roll(prev, s, axis=0)
    row8 = lax.broadcasted_iota(jnp.int32, prev.shape, 0)
    head = jnp.where(row8 < s, ps, xs[0:8])
    return jnp.concatenate([head, xs[8:]], axis=0)


def _gdn_prep_kernel(x_ref, prev_ref, first_ref, ab_ref, cw_ref, alog_ref, dtb_ref,
                     q_ref, k_ref, v_ref, gb_ref, *, tiles_per_batch, t_valid):
    r = pl.program_id(0)
    tm = x_ref.shape[0]
    tile = r % tiles_per_batch
    prev = jnp.where(tile == 0, first_ref[0], prev_ref[...])
    x = x_ref[...]
    acc = x * cw_ref[GDN_CONV - 1:GDN_CONV, :]
    for s in range(1, GDN_CONV):
        acc = acc + _shift_rows(x, prev, s) * cw_ref[GDN_CONV - 1 - s:GDN_CONV - s, :]
    c = acc * jax.nn.sigmoid(acc)

    def l2(t):
        return t * lax.rsqrt(jnp.sum(t * t, axis=-1, keepdims=True) + 1e-6)

    for h in range(GDN_KH):
        sl = slice(h * GDN_DK, (h + 1) * GDN_DK)
        q_ref[:, sl] = l2(c[:, sl]) * (GDN_DK ** -0.5)
        k_ref[:, sl] = l2(c[:, GDN_QK + h * GDN_DK:GDN_QK + (h + 1) * GDN_DK])
    v_ref[...] = c[:, 2 * GDN_QK:]
    ab = ab_ref[...]
    z = ab + dtb_ref[...]
    softplus = jnp.maximum(z, 0.0) + jnp.log(1.0 + jnp.exp(-jnp.abs(z)))
    gate = -jnp.exp(alog_ref[...]) * softplus
    lane = lax.broadcasted_iota(jnp.int32, ab.shape, 1)
    pos = tile * tm + lax.broadcasted_iota(jnp.int32, ab.shape, 0)
    gb = jnp.where(lane < GDN_VH, gate, jnp.where(lane < 2 * GDN_VH, jax.nn.sigmoid(ab), 0.0))
    gb_ref[...] = jnp.where(pos < t_valid, gb, 0.0)


def _gdn_prep(xin, first, ab, ab_col, conv_w, a_log, dt_bias, batch, t_pad, t_valid):
    m = batch * t_pad
    tm = _pick(t_pad, (256, 128, 64))
    tpb = t_pad // tm
    f32 = jnp.float32
    lanes = lambda v: jnp.zeros((1, 128), f32).at[0, :GDN_VH].set(v.astype(f32))
    return pl.pallas_call(
        functools.partial(_gdn_prep_kernel, tiles_per_batch=tpb, t_valid=t_valid),
        grid=(m // tm,),
        in_specs=[pl.BlockSpec((tm, GDN_CONV_CH), lambda r: (r, 0)),
                  pl.BlockSpec((8, GDN_CONV_CH), lambda r: (jnp.maximum(r * (tm // 8) - 1, 0), 0)),
                  pl.BlockSpec((1, 8, GDN_CONV_CH), lambda r: (r // tpb, 0, 0)),
                  pl.BlockSpec((tm, 128), lambda r: (r, ab_col)),
                  pl.BlockSpec((GDN_CONV, GDN_CONV_CH), lambda r: (0, 0)),
                  pl.BlockSpec((1, 128), lambda r: (0, 0)),
                  pl.BlockSpec((1, 128), lambda r: (0, 0))],
        out_specs=[pl.BlockSpec((tm, GDN_QK), lambda r: (r, 0)),
                   pl.BlockSpec((tm, GDN_QK), lambda r: (r, 0)),
                   pl.BlockSpec((tm, GDN_V), lambda r: (r, 0)),
                   pl.BlockSpec((tm, 128), lambda r: (r, 0))],
        out_shape=[jax.ShapeDtypeStruct((m, GDN_QK), f32), jax.ShapeDtypeStruct((m, GDN_QK), f32),
                   jax.ShapeDtypeStruct((m, GDN_V), f32), jax.ShapeDtypeStruct((m, 128), f32)],
        compiler_params=pltpu.CompilerParams(dimension_semantics=("arbitrary",),
                                             vmem_limit_bytes=VMEM_LIMIT_BYTES),
    )(xin, xin, first, ab, conv_w.astype(f32), lanes(a_log), lanes(dt_bias))


def _cumsum_rows(x):
    row = lax.broadcasted_iota(jnp.int32, x.shape, 0)
    s = 1
    while s < x.shape[0]:
        x = x + jnp.where(row >= s, pltpu.roll(x, s, axis=0), 0.0)
        s *= 2
    return x


def _split3(a):
    hi = a.astype(jnp.bfloat16)
    lo = (a - hi.astype(jnp.float32)).astype(jnp.bfloat16)
    return hi, lo


def _dot3(a, b):
    if MXU_DTYPE != jnp.bfloat16:
        return jnp.dot(a, b, preferred_element_type=jnp.float32)
    ah, al = _split3(a)
    bh, bl = _split3(b)
    d = lambda x, y: jnp.dot(x, y, preferred_element_type=jnp.float32)
    return d(ah, bh) + (d(ah, bl) + d(al, bh))


def _unit_lower_inverses(lmats, row, col):
    eye = jnp.where(row == col, 1.0, 0.0)
    blk = lax.shift_right_logical(row, 4) == lax.shift_right_logical(col, 4)
    lds = [jnp.where(blk, l, 0.0) for l in lmats]
    xs = [eye - ld for ld in lds]
    ps = [_dot3(ld, ld) for ld in lds]
    for step in range(3):
        xs = [x + _dot3(x, p) for x, p in zip(xs, ps)]
        if step < 2:
            ps = [_dot3(p, p) for p in ps]
    size = 16
    while size < lmats[0].shape[0]:
        sh = size.bit_length() - 1
        inner = lax.shift_right_logical(row, sh) == lax.shift_right_logical(col, sh)
        outer = lax.shift_right_logical(row, sh + 1) == lax.shift_right_logical(col, sh + 1)
        coffs = [jnp.where(outer, jnp.where(inner, 0.0, l), 0.0) for l in lmats]
        ts = [_dot3(x, c) for x, c in zip(xs, coffs)]
        xs = [x - _dot3(t, x) for x, t in zip(xs, ts)]
        size *= 2
    return xs


def _gdn_chunk_kernel(q_ref, k_ref, v_ref, gb_ref, z_ref, s0_ref, nw_ref, y_ref, s_ref):
    f32 = jnp.float32
    cn = q_ref.shape[0]

    @pl.when(pl.program_id(1) == 0)
    def _():
        s_ref[...] = s0_ref[...]

    gb = gb_ref[...]
    gcum = _cumsum_rows(gb)
    gcum_t = _pad_rows(gcum, 128).T
    row = lax.broadcasted_iota(jnp.int32, (cn, cn), 0)
    col = lax.broadcasted_iota(jnp.int32, (cn, cn), 1)
    rep = GDN_VH // GDN_KH
    heads = range(GDN_VH)
    qs = [q_ref[:, kh * GDN_DK:(kh + 1) * GDN_DK] for kh in range(GDN_KH)]
    ks = [k_ref[:, kh * GDN_DK:(kh + 1) * GDN_DK] for kh in range(GDN_KH)]
    kk = [_mdot_nt(k, k) for k in ks]
    qk = [_mdot_nt(q, k) for q, k in zip(qs, ks)]
    gcols = [gcum[:, h:h + 1] for h in heads]
    diffs = [gcols[h] - gcum_t[h:h + 1, :cn] for h in heads]
    egs = [jnp.exp(g) for g in gcols]
    betas = [gb[:, GDN_VH + h:GDN_VH + h + 1] for h in heads]
    lmats = [betas[h] * kk[h // rep] * jnp.exp(jnp.where(col < row, diffs[h], -jnp.inf)) for h in heads]
    ainvs = _unit_lower_inverses(lmats, row, col)
    sols = [_dot3(ainvs[h], jnp.concatenate([betas[h] * v_ref[:, h * GDN_DV:(h + 1) * GDN_DV],
                                             (betas[h] * egs[h]) * ks[h // rep]], axis=1)) for h in heads]
    atts = [qk[h // rep] * jnp.exp(jnp.where(col <= row, diffs[h], -jnp.inf)) for h in heads]
    ss = [s_ref[0, h] for h in heads]
    us = [sols[h][:, :GDN_DV] - _mdot(sols[h][:, GDN_DV:], ss[h]) for h in heads]
    os_ = [egs[h] * _mdot(qs[h // rep], ss[h]) + _mdot(atts[h], us[h]) for h in heads]
    for h in heads:
        gl = gcum[cn - 1:cn, h:h + 1]
        s_ref[0, h] = jnp.exp(gl) * ss[h] + _mdot_tn(ks[h // rep] * jnp.exp(gl - gcols[h]), us[h])
    for h in heads:
        o = os_[h]
        on = o * lax.rsqrt(jnp.mean(o * o, axis=-1, keepdims=True) + RMS_EPS) * nw_ref[...]
        z = z_ref[:, h * GDN_DV:(h + 1) * GDN_DV]
        y_ref[:, h * GDN_DV:(h + 1) * GDN_DV] = (on * (z * jax.nn.sigmoid(z))).astype(y_ref.dtype)


def _gdn_chunks(q, k, v, gb, zsrc, z_col, s0, norm_w, batch, t_pad):
    cn = GDN_CHUNK
    nc = t_pad // cn
    rowblk = lambda w, c=0: pl.BlockSpec((cn, w), lambda b, i, c=c: (b * nc + i, c))
    st = pl.BlockSpec((1, GDN_VH, GDN_DK, GDN_DV), lambda b, i: (b, 0, 0, 0))
    return pl.pallas_call(
        _gdn_chunk_kernel,
        grid=(batch, nc),
        in_specs=[rowblk(GDN_QK), rowblk(GDN_QK), rowblk(GDN_V), rowblk(128), rowblk(GDN_V, z_col), st,
                  pl.BlockSpec((1, GDN_DV), lambda b, i: (0, 0))],
        out_specs=[rowblk(GDN_V), st],
        out_shape=[jax.ShapeDtypeStruct((batch * t_pad, GDN_V), MXU_DTYPE),
                   jax.ShapeDtypeStruct((batch, GDN_VH, GDN_DK, GDN_DV), jnp.float32)],
        compiler_params=pltpu.CompilerParams(dimension_semantics=("parallel", "arbitrary"),
                                             vmem_limit_bytes=VMEM_LIMIT_BYTES),
    )(q, k, v, gb, zsrc, s0.astype(jnp.float32), norm_w.reshape(1, GDN_DV).astype(jnp.float32))


OD_Q = 0
OD_K = ML_H * ML_DK
OD_V = 2 * ML_H * ML_DK
OD_O = OD_V + ML_H * ML_DV
OD_SMALL = OD_O + ML_H * ML_DV
OD_N = OD_SMALL + 128


def _cummax_rows(x):
    row = lax.broadcasted_iota(jnp.int32, x.shape, 0)
    s = 1
    while s < x.shape[0]:
        x = jnp.maximum(x, jnp.where(row >= s, pltpu.roll(x, s, axis=0), -jnp.inf))
        s *= 2
    return x


def _round_mxu(x):
    return x.astype(MXU_DTYPE).astype(jnp.float32)


def _mlstm_chunk_kernel(q_ref, k_ref, v_ref, og_ref, if_ref, bif_ref, nw_ref, c0_ref, n0_ref, m0_ref,
                        y_ref, c_ref, n_ref, m_ref, *, t_valid):
    f32 = jnp.float32
    cn = q_ref.shape[0]
    ci = pl.program_id(1)

    @pl.when(ci == 0)
    def _():
        c_ref[...] = c0_ref[...]
        n_ref[...] = n0_ref[...]
        m_ref[...] = m0_ref[...]

    pre = if_ref[...] + bif_ref[...]
    pos = ci * cn + lax.broadcasted_iota(jnp.int32, pre.shape, 0)
    live = pos < t_valid
    lf = jnp.where(live, jnp.minimum(pre, 0.0) - jnp.log(1.0 + jnp.exp(-jnp.abs(pre))), 0.0)
    fcum = _cumsum_rows(lf)
    a_all = jnp.where(live, pre, NEG) - pltpu.roll(fcum, 128 - ML_H, axis=1)
    amax = _cummax_rows(a_all)
    a_t = _pad_rows(a_all, 128).T
    row = lax.broadcasted_iota(jnp.int32, (cn, cn), 0)
    col = lax.broadcasted_iota(jnp.int32, (cn, cn), 1)
    m_all = m_ref[0]
    m_new = []
    for h in range(ML_H):
        q = q_ref[:, h * ML_DK:(h + 1) * ML_DK] * (ML_DK ** -0.5)
        k = k_ref[:, h * ML_DK:(h + 1) * ML_DK]
        v = v_ref[:, h * ML_DV:(h + 1) * ML_DV]
        m_prev = m_all[:, h:h + 1]
        fc = fcum[:, ML_H + h:ML_H + h + 1]
        a_col = a_all[:, h:h + 1]
        mt = fc + jnp.maximum(m_prev, amax[:, h:h + 1])
        dmat = jnp.exp(jnp.where(col <= row, a_t[h:h + 1, :cn] + (fc - mt), -jnp.inf))
        dec0 = jnp.exp(fc + m_prev - mt)
        s = _mdot_nt(q, k) * dmat
        cm = c_ref[0, h]
        nv = n_ref[0, h:h + 1, :]
        num = dec0 * _mdot(q, cm) + _mdot(s, v)
        den = dec0 * jnp.sum(_round_mxu(q) * _round_mxu(nv), axis=-1, keepdims=True) \
            + jnp.sum(s, axis=-1, keepdims=True)
        hc = num / jnp.maximum(jnp.abs(den), jnp.exp(-mt))
        f_end = fc[cn - 1:cn]
        m_end = mt[cn - 1:cn]
        w = jnp.exp(a_col + (f_end - m_end))
        dc = jnp.exp(f_end + m_prev - m_end)
        c_ref[0, h] = dc * cm + _mdot_tn(w * k, v)
        n_ref[0, h:h + 1, :] = dc * nv + jnp.sum(_round_mxu(w) * _round_mxu(k), axis=0, keepdims=True)
        m_new.append(m_end)
        hn = hc * lax.rsqrt(jnp.mean(hc * hc, axis=-1, keepdims=True) + RMS_EPS) \
            * nw_ref[:, h * ML_DV:(h + 1) * ML_DV]
        og = og_ref[:, h * ML_DV:(h + 1) * ML_DV]
        y_ref[:, h * ML_DV:(h + 1) * ML_DV] = (hn * jax.nn.sigmoid(og)).astype(y_ref.dtype)
    lane = lax.broadcasted_iota(jnp.int32, m_all.shape, 1)
    out = m_all
    for h in range(ML_H):
        out = jnp.where(lane == h, m_new[h], out)
    m_ref[0] = out


def _mlstm_chunks(proj, b_if, mnorm, c0, n0, m0, batch, t_pad, t_valid):
    cn = ML_CHUNK
    nc = t_pad // cn
    f32 = jnp.float32
    blk = lambda w, c: pl.BlockSpec((cn, w), lambda b, i, c=c: (b * nc + i, c))
    const = lambda shape: pl.BlockSpec(shape, lambda b, i: (0,) * len(shape))
    st_c = pl.BlockSpec((1, ML_H, ML_DK, ML_DV), lambda b, i: (b, 0, 0, 0))
    st_n = pl.BlockSpec((1, ML_H, ML_DK), lambda b, i: (b, 0, 0))
    st_m = pl.BlockSpec((1, 1, 128), lambda b, i: (b, 0, 0))
    bif = jnp.zeros((1, 128), f32).at[0, :2 * ML_H].set(b_if.astype(f32))
    m0p = jnp.zeros((batch, 1, 128), f32).at[:, 0, :ML_H].set(m0.astype(f32))
    hq, hv = ML_H * ML_DK, ML_H * ML_DV
    y, c, n, m = pl.pallas_call(
        functools.partial(_mlstm_chunk_kernel, t_valid=t_valid),
        grid=(batch, nc),
        in_specs=[blk(hq, OD_Q // hq), blk(hq, OD_K // hq), blk(hv, OD_V // hv), blk(hv, OD_O // hv),
                  blk(128, OD_SMALL // 128), const((1, 128)), const((1, hv)), st_c, st_n, st_m],
        out_specs=[blk(hv, 0), st_c, st_n, st_m],
        out_shape=[jax.ShapeDtypeStruct((batch * t_pad, hv), MXU_DTYPE),
                   jax.ShapeDtypeStruct((batch, ML_H, ML_DK, ML_DV), f32),
                   jax.ShapeDtypeStruct((batch, ML_H, ML_DK), f32),
                   jax.ShapeDtypeStruct((batch, 1, 128), f32)],
        compiler_params=pltpu.CompilerParams(dimension_semantics=("parallel", "arbitrary"),
                                             vmem_limit_bytes=VMEM_LIMIT_BYTES),
    )(proj, proj, proj, proj, proj, bif, mnorm.reshape(1, hv).astype(f32), c0.astype(f32), n0.astype(f32), m0p)
    return y, c, n, m[:, 0, :ML_H]


def _permute_odd_w(w_in):
    pad = jnp.zeros((w_in.shape[0], OD_N - w_in.shape[1]), w_in.dtype)
    return jnp.concatenate([w_in, pad], axis=1).astype(MXU_DTYPE)


DEC_PAGES_PER_STEP = 8


def _dec_compress_kernel(pt_ref, *refs, pp):
    kc_refs, vc_refs = refs[:pp], refs[pp:2 * pp]
    wck_ref, wcv_ref, g1_ref, kcb_ref, vcb_ref = refs[2 * pp:]
    per_page = PAGE_SIZE // NSA_BLK

    def compress(page_refs, w):
        rows = [jnp.sum(r[0].reshape(per_page, NSA_BLK, NSA_KV) * w[None], axis=1) for r in page_refs]
        return jnp.concatenate(rows, axis=0)

    kcb = compress(kc_refs, wck_ref[...])
    for g in range(NSA_G):
        sl = slice(g * NSA_D, (g + 1) * NSA_D)
        kcb_ref[0, :, sl] = _group_rms(kcb[:, sl], g1_ref[...])
    vcb_ref[0] = compress(vc_refs, wcv_ref[...])


def _dec_compress(page_table, pool_kc, pool_vc, layer, w_ck, w_cv, g1):
    nseq, npages = page_table.shape
    pp = DEC_PAGES_PER_STEP
    per_page = PAGE_SIZE // NSA_BLK
    nphys = pool_kc.shape[1]
    base = layer * nphys
    kc = pool_kc.reshape(-1, PAGE_SIZE, NSA_KV)
    vc = pool_vc.reshape(-1, PAGE_SIZE, NSA_KV)
    page = lambda j: pl.BlockSpec((1, PAGE_SIZE, NSA_KV), lambda b, i, pt, j=j: (base + pt[b, i * pp + j], 0, 0))
    const = lambda shape: pl.BlockSpec(shape, lambda b, i, pt: (0,) * len(shape))
    out = pl.BlockSpec((1, pp * per_page, NSA_KV), lambda b, i, pt: (b, i, 0))
    nb = npages * per_page
    return pl.pallas_call(
        functools.partial(_dec_compress_kernel, pp=pp),
        grid_spec=pltpu.PrefetchScalarGridSpec(
            num_scalar_prefetch=1, grid=(nseq, npages // pp),
            in_specs=[page(j) for j in range(pp)] * 2 + [const((NSA_BLK, NSA_KV))] * 2 + [const((1, NSA_D))],
            out_specs=[out, out]),
        out_shape=[jax.ShapeDtypeStruct((nseq, nb, NSA_KV), jnp.float32)] * 2,
        compiler_params=pltpu.CompilerParams(dimension_semantics=("parallel", "arbitrary"),
                                             vmem_limit_bytes=VMEM_LIMIT_BYTES),
    )(page_table, *([kc] * pp), *([vc] * pp), w_ck.reshape(NSA_BLK, NSA_KV), w_cv.reshape(NSA_BLK, NSA_KV),
      g1.reshape(1, NSA_D))


def _rows_by_head(head_row, fn, nheads):
    out = fn(0)
    out = jnp.broadcast_to(out, (nheads, out.shape[1]))
    for h in range(1, nheads):
        out = jnp.where(head_row == h, fn(h), out)
    return out


def _dec_select_kernel(tab_ref, q_ref, kcb_ref, vcb_ref, oc_ref, idx_ref, *, qpos):
    f32 = jnp.float32
    nbp = kcb_ref.shape[1]
    q = q_ref[0]
    head_row = lax.broadcasted_iota(jnp.int32, (NSA_H, 1), 0)
    blk = lax.broadcasted_iota(jnp.int32, (1, nbp), 1)
    blkf = blk.astype(f32)
    dist = qpos - (blk * NSA_BLK + NSA_BLK - 1)
    ok = dist >= 0
    cur = qpos // NSA_BLK
    lg = jnp.zeros((NSA_H, nbp), f32)
    for g in range(NSA_G):
        lg_g = _mdot_nt(q, kcb_ref[0, :, g * NSA_D:(g + 1) * NSA_D])
        lg = jnp.where(head_row // NSA_HG == g, lg_g, lg)
    lg = lg + _rows_by_head(head_row, lambda h: _bias_of_dist(dist, tab_ref, h), NSA_H)
    lg = jnp.where(ok, lg, NEG)
    e = jnp.exp(lg - jnp.max(lg, axis=-1, keepdims=True))
    p = jnp.where(ok, e / jnp.sum(e, axis=-1, keepdims=True), 0.0)
    oc = jnp.zeros((NSA_H, NSA_D), f32)
    lane = lax.broadcasted_iota(jnp.int32, (1, 128), 1)
    for g in range(NSA_G):
        in_g = head_row // NSA_HG == g
        oc = jnp.where(in_g, _mdot(p, vcb_ref[0, :, g * NSA_D:(g + 1) * NSA_D]), oc)
        score = jnp.sum(jnp.where(in_g, p, 0.0), axis=0, keepdims=True)
        score = jnp.where(blk < cur, score, -1.0)
        idx = jnp.full((1, 128), -1.0, f32)
        for r in range(NSA_TOPK):
            mx = jnp.max(score, axis=-1, keepdims=True)
            first = jnp.min(jnp.where(score == mx, blkf, float(nbp)), axis=-1, keepdims=True)
            idx = jnp.where(lane == r, jnp.where(mx >= 0.0, first, -1.0), idx)
            score = jnp.where(blkf == first, -2.0, score)
        idx_ref[0, g:g + 1, :] = idx.astype(jnp.int32)
    oc_ref[0] = oc


def _dec_select(tab, qh, kcb, vcb, qpos):
    nseq = qh.shape[0]
    nbp = kcb.shape[1]
    seq = lambda shape: pl.BlockSpec((1,) + shape, lambda b: (b, 0, 0))
    return pl.pallas_call(
        functools.partial(_dec_select_kernel, qpos=qpos),
        grid=(nseq,),
        in_specs=[pl.BlockSpec(memory_space=pltpu.SMEM), seq((NSA_H, NSA_D)), seq((nbp, NSA_KV)),
                  seq((nbp, NSA_KV))],
        out_specs=[seq((NSA_H, NSA_D)), seq((NSA_G, 128))],
        out_shape=[jax.ShapeDtypeStruct((nseq, NSA_H, NSA_D), jnp.float32),
                   jax.ShapeDtypeStruct((nseq, NSA_G, 128), jnp.int32)],
        compiler_params=pltpu.CompilerParams(dimension_semantics=("parallel",),
                                             vmem_limit_bytes=VMEM_LIMIT_BYTES),
    )(tab.astype(jnp.float32), qh, kcb, vcb)


def _dec_attend_kernel(pt_ref, idx_ref, tab_ref, q_ref, *refs, qpos):
    f32 = jnp.float32
    ks_refs, vs_refs = refs[:NSA_TOPK], refs[NSA_TOPK:2 * NSA_TOPK]
    ksn_ref, vsn_ref, kwb_ref, vwb_ref, kwn_ref, vwn_ref, oc_ref, ng_ref, y_ref = refs[2 * NSA_TOPK:]
    b = pl.program_id(0)
    g = pl.program_id(1)
    cur = qpos // NSA_BLK
    head_row = lax.broadcasted_iota(jnp.int32, (NSA_HG, 1), 0)
    q4 = q_ref[0, pl.ds(g * NSA_HG, NSA_HG), :]
    q4r = _round_mxu(q4.astype(f32))
    new_row = b * NSA_G + g
    tab0 = _rows_by_head(head_row, lambda h: jnp.full((1, 1), tab_ref[0, g * NSA_HG + h], f32), NSA_HG)

    def attend(keys, vals, dist, ok, k_new, v_new):
        lg = _mdot_nt(q4, keys)
        lg = lg + _rows_by_head(head_row, lambda h: _bias_of_dist(dist, tab_ref, g * NSA_HG + h), NSA_HG)
        lg = jnp.where(ok, lg, NEG)
        lg_new = jnp.sum(q4r * _round_mxu(k_new), axis=-1, keepdims=True) + tab0
        m = jnp.maximum(jnp.max(lg, axis=-1, keepdims=True), lg_new)
        e = jnp.exp(lg - m)
        e_new = jnp.exp(lg_new - m)
        den = jnp.sum(e, axis=-1, keepdims=True) + e_new
        return _mdot(e / den, vals) + _round_mxu(e_new / den) * _round_mxu(v_new)

    pieces, oks = [], []
    jrow = lax.broadcasted_iota(jnp.int32, (1, NSA_BLK), 1)
    for s in range(NSA_TOPK):
        blk_id = idx_ref[b, g, s]
        valid = jnp.logical_and(blk_id >= 0, blk_id < cur)
        pos = jnp.maximum(blk_id, 0) * NSA_BLK + jrow
        pieces.append(qpos - pos)
        oks.append(jnp.where(valid, 1.0, 0.0) * jnp.where(pos <= qpos, 1.0, 0.0))
    dist_s = jnp.concatenate(pieces, axis=1)
    ok_s = jnp.concatenate(oks, axis=1) > 0.5
    k_sel = jnp.concatenate([r[0] for r in ks_refs], axis=0)
    v_sel = jnp.concatenate([r[0] for r in vs_refs], axis=0)
    o_s = attend(k_sel, v_sel, dist_s, ok_s, ksn_ref[pl.ds(new_row, 1), :], vsn_ref[pl.ds(new_row, 1), :])

    wb = kwb_ref.shape[1]
    dist_w = wb - lax.broadcasted_iota(jnp.int32, (1, wb), 1)
    ok_w = dist_w < NSA_WIN
    o_w = attend(kwb_ref[0], vwb_ref[0], dist_w, ok_w, kwn_ref[pl.ds(new_row, 1), :], vwn_ref[pl.ds(new_row, 1), :])

    gates = jax.nn.sigmoid(ng_ref[pl.ds(b, 1), :])
    lane = lax.broadcasted_iota(jnp.int32, (1, 128), 1)

    def gate(branch):
        def one(h):
            c = 2 * GDN_VH + (g * NSA_HG + h) * 3 + branch
            return jnp.sum(jnp.where(lane == c, gates, 0.0), axis=-1, keepdims=True)
        return _rows_by_head(head_row, one, NSA_HG)

    oc4 = oc_ref[0, pl.ds(g * NSA_HG, NSA_HG), :]
    y_ref[0, pl.ds(g * NSA_HG, NSA_HG), :] = oc4 * gate(0) + o_s * gate(1) + o_w * gate(2)


def _dec_attend(tab, page_table, idx, qh, pool_ks, pool_vs, layer, ksn, vsn, buf_kw, buf_vw, kwn, vwn, oc, ng,
                qpos):
    nseq = qh.shape[0]
    base = layer * pool_ks.shape[1]
    wb = buf_kw.shape[1]
    ks = pool_ks.reshape(-1, PAGE_SIZE, NSA_KV)
    vs = pool_vs.reshape(-1, PAGE_SIZE, NSA_KV)
    per_page = PAGE_SIZE // NSA_BLK

    def sel(s):
        def index(b, g, pt, ix, s=s):
            blk_id = jnp.maximum(ix[b, g, s], 0)
            return (base + pt[b, blk_id // per_page], blk_id % per_page, g)
        return pl.BlockSpec((1, NSA_BLK, NSA_D), index)

    whole = lambda a: pl.BlockSpec(a.shape, lambda b, g, pt, ix: (0,) * a.ndim)
    heads = pl.BlockSpec((1, NSA_H, NSA_D), lambda b, g, pt, ix: (b, 0, 0))
    win = pl.BlockSpec((1, wb, NSA_D), lambda b, g, pt, ix: (b, 0, g))
    rows = lambda a: a.reshape(nseq * NSA_G, NSA_D)
    new = [rows(ksn), rows(vsn), rows(kwn), rows(vwn)]
    return pl.pallas_call(
        functools.partial(_dec_attend_kernel, qpos=qpos),
        grid_spec=pltpu.PrefetchScalarGridSpec(
            num_scalar_prefetch=2, grid=(nseq, NSA_G),
            in_specs=[pl.BlockSpec(memory_space=pltpu.SMEM), heads] + [sel(s) for s in range(NSA_TOPK)] * 2
                     + [whole(new[0]), whole(new[1]), win, win, whole(new[2]), whole(new[3]), heads, whole(ng)],
            out_specs=heads),
        out_shape=jax.ShapeDtypeStruct((nseq, NSA_H, NSA_D), jnp.float32),
        compiler_params=pltpu.CompilerParams(dimension_semantics=("parallel", "arbitrary"),
                                             vmem_limit_bytes=VMEM_LIMIT_BYTES),
    )(page_table, idx, tab.astype(jnp.float32), qh, *([ks] * NSA_TOPK), *([vs] * NSA_TOPK), new[0], new[1],
      buf_kw.reshape(nseq, wb, NSA_KV), buf_vw.reshape(nseq, wb, NSA_KV), new[2], new[3], oc, ng)


def _nsa_decode(proj, tab, qk_gain, w_ck, w_cv, caches, nseq):
    assert PAST_LEN % PAGE_SIZE == 0 and DEC_SEQ == 1
    pad = jnp.pad(proj, ((0, 512 - nseq), (0, 0)))
    qn, ksn, kwn, _, _, _, _, _, _ = _nsa_prep(pad, qk_gain, w_ck, w_cv)
    qh = qn[:nseq].reshape(nseq, NSA_H, NSA_D).astype(jnp.float32)
    kv = lambda i: proj[:, EV_KV + i * NSA_KV:EV_KV + (i + 1) * NSA_KV]
    layer = caches['layer']
    kcb, vcb = _dec_compress(caches['page_table'], caches['kc'], caches['vc'], layer, w_ck, w_cv, qk_gain[1])
    oc, idx = _dec_select(tab, qh, kcb, vcb, PAST_LEN)
    ng = proj[:, EV_SMALL:EV_SMALL + 128]
    y = _dec_attend(tab, caches['page_table'], idx, qh, caches['ks'], caches['vs'], layer, ksn[:nseq], kv(3),
                    caches['kw'], caches['vw'], kwn[:nseq], kv(5), oc, ng, PAST_LEN)
    return y.reshape(nseq, NSA_H * NSA_D), ksn[:nseq], kwn[:nseq]


def _split(x, widths):
    offs = [int(o) for o in np.cumsum(widths)[:-1]]
    return jnp.split(x, offs, axis=-1)


def _rms(x, g):
    xf = x.astype(jnp.float32)
    y = xf * lax.rsqrt(jnp.mean(xf * xf, axis=-1, keepdims=True) + RMS_EPS)
    return (y * g.astype(jnp.float32)).astype(x.dtype)


def _mlp(xn, w1, w2):
    h = jnp.square(jax.nn.relu(_pmm(xn, w1)))
    return _pmm(h, w2)


def _causal_conv(xin, w):
    c = xin.shape[-1]
    return lax.conv_general_dilated(xin, w[:, None, :].astype(xin.dtype), window_strides=(1,),
                                    padding='VALID', dimension_numbers=('NWC', 'WIO', 'NWC'),
                                    feature_group_count=c)


def _chunks(t, c, pad, pad_val=0.0):
    t = t.astype(jnp.float32)
    if pad:
        t = jnp.pad(t, [(0, 0), (0, pad)] + [(0, 0)] * (t.ndim - 2), constant_values=pad_val)
    b, tp = t.shape[:2]
    t = t.reshape(b, tp // c, c, *t.shape[2:])
    return jnp.swapaxes(jnp.moveaxis(t, 1, 0), 2, 3)


def _unchunk(o, t_len):
    n, b, h, c, d = o.shape
    return jnp.transpose(o, (1, 0, 3, 2, 4)).reshape(b, n * c, h, d)[:, :t_len]


def _gdn_chunked(q, k, v, g, beta, s0):
    b, t_len, h, dk = q.shape
    dv = v.shape[-1]
    c = min(GDN_CHUNK, t_len)
    pad = (-t_len) % c
    tri_s = jnp.tril(jnp.ones((c, c), bool), -1)
    tri_i = jnp.tril(jnp.ones((c, c), bool), 0)
    eye = jnp.eye(c, dtype=jnp.float32)

    def body(s, inp):
        qc, kc, vc, gc, bc = inp
        gcum = jnp.cumsum(gc, axis=-1)
        diff = gcum[..., :, None] - gcum[..., None, :]
        eg = jnp.exp(gcum)
        lmat = bc[..., :, None] * jnp.einsum('bhid,bhjd->bhij', kc, kc) * jnp.exp(jnp.where(tri_s, diff, -jnp.inf))
        rhs = jnp.concatenate([bc[..., None] * vc, (bc * eg)[..., None] * kc], axis=-1)
        sol = lax.linalg.triangular_solve(eye + lmat, rhs, left_side=True, lower=True)
        u = sol[..., :dv] - jnp.einsum('bhcd,bhde->bhce', sol[..., dv:], s)
        att = jnp.einsum('bhid,bhjd->bhij', qc, kc) * jnp.exp(jnp.where(tri_i, diff, -jnp.inf))
        o = eg[..., None] * jnp.einsum('bhcd,bhde->bhce', qc, s) + jnp.einsum('bhij,bhje->bhie', att, u)
        gl = gcum[..., -1]
        s = jnp.exp(gl)[..., None, None] * s + jnp.einsum('bhcd,bhce->bhde', kc * jnp.exp(gl[..., None] - gcum)[..., None], u)
        return s, o

    s, o = lax.scan(body, s0.astype(jnp.float32),
                    (_chunks(q, c, pad), _chunks(k, c, pad), _chunks(v, c, pad), _chunks(g, c, pad), _chunks(beta, c, pad)))
    return _unchunk(o, t_len).astype(v.dtype), s.astype(s0.dtype)


def _gdn(parts, conv_w, a_log, dt_bias, norm_w, conv_prev, s0):
    q, k, v, z, a, bgate = parts
    b, t_len = q.shape[:2]
    xin = jnp.concatenate([conv_prev.astype(q.dtype), jnp.concatenate([q, k, v], axis=-1)], axis=1)
    c = jax.nn.silu(_causal_conv(xin, conv_w))
    cq, ck, cv = _split(c, (GDN_QK, GDN_QK, GDN_V))

    def l2(t):
        t = t.reshape(b, t_len, GDN_KH, GDN_DK).astype(jnp.float32)
        return t * lax.rsqrt(jnp.sum(t * t, axis=-1, keepdims=True) + 1e-6)

    rep = GDN_VH // GDN_KH
    qh = jnp.repeat(l2(cq) * (GDN_DK ** -0.5), rep, axis=2)
    kh = jnp.repeat(l2(ck), rep, axis=2)
    vh = cv.reshape(b, t_len, GDN_VH, GDN_DV)
    beta = jax.nn.sigmoid(bgate.astype(jnp.float32))
    g = -jnp.exp(a_log.astype(jnp.float32)) * jax.nn.softplus(a.astype(jnp.float32) + dt_bias.astype(jnp.float32))
    o, s = _gdn_chunked(qh, kh, vh, g, beta, s0)
    o = _rms(o, norm_w) * jax.nn.silu(z.reshape(b, t_len, GDN_VH, GDN_DV))
    return o.reshape(b, t_len, GDN_V), xin[:, -(GDN_CONV - 1):], s


def _bucket(dist):
    n = jnp.maximum(dist, 0)
    nf = jnp.maximum(n, REL_EXACT).astype(jnp.float32)
    large = REL_EXACT + (jnp.log(nf / REL_EXACT) / math.log(REL_MAX_DIST / REL_EXACT)
                         * (REL_BUCKETS - REL_EXACT)).astype(jnp.int32)
    large = jnp.minimum(large, REL_BUCKETS - 1)
    return jnp.where(n < REL_EXACT, n, large)


def _tok_bias(dist, tab):
    t_len, n = dist.shape
    bias = tab.astype(jnp.float32)[_bucket(dist)]
    return bias.reshape(t_len, n, NSA_G, NSA_HG).transpose(0, 2, 3, 1)


def _compress(rows, w):
    b, t_len = rows.shape[:2]
    r = rows.reshape(b, t_len // NSA_BLK, NSA_BLK, NSA_G, NSA_D)
    return jnp.einsum('bnlgd,lgd->bngd', r, w)


def _nsa_core(q, qpos, kc_b, vc_b, fetch, kw, vw, kwpos, tab):
    b, t_len = q.shape[:2]
    f32 = jnp.float32
    nb = kc_b.shape[1]
    blk = jnp.arange(nb, dtype=jnp.int32)
    dist_c = qpos[:, None] - (blk * NSA_BLK + NSA_BLK - 1)[None, :]
    ok_c = (dist_c >= 0)[:, None, None, :]
    lg_c = jnp.einsum('btghd,bngd->btghn', q, kc_b).astype(f32) + _tok_bias(dist_c, tab)
    p_c = jax.nn.softmax(jnp.where(ok_c, lg_c, NEG), axis=-1) * ok_c
    o_c = jnp.einsum('btghn,bngd->btghd', p_c.astype(vc_b.dtype), vc_b)
    cur = qpos // NSA_BLK
    cand = (blk[None, :] < cur[:, None])[None, :, None, :]
    score = jnp.where(cand, p_c.sum(axis=3), -1.0)
    _, idx = lax.top_k(score, min(NSA_TOPK, nb))
    cur_b = jnp.broadcast_to(cur[None, :, None, None], (b, t_len, NSA_G, 1))
    idx_all = jnp.concatenate([idx, cur_b], axis=-1)
    ok_all = jnp.concatenate([idx < cur[None, :, None, None], jnp.ones((b, t_len, NSA_G, 1), bool)], axis=-1)
    pos5 = idx_all[..., None] * NSA_BLK + jnp.arange(NSA_BLK, dtype=jnp.int32)
    ok5 = ok_all[..., None] & (pos5 <= qpos[None, :, None, None, None])
    pos = pos5.reshape(b, t_len, NSA_G, -1)
    ok_s = ok5.reshape(b, t_len, NSA_G, 1, -1)
    k_s, v_s = fetch(pos)
    tab3 = tab.astype(f32).reshape(REL_BUCKETS, NSA_G, NSA_HG)
    bias_s = jnp.moveaxis(tab3[_bucket(qpos[None, :, None, None] - pos), jnp.arange(NSA_G)[:, None]], -1, 3)
    lg_s = jnp.einsum('btghd,btgnd->btghn', q, k_s).astype(f32) + bias_s
    p_s = jax.nn.softmax(jnp.where(ok_s, lg_s, NEG), axis=-1)
    o_s = jnp.einsum('btghn,btgnd->btghd', p_s.astype(v_s.dtype), v_s)
    dist_w = qpos[:, None] - kwpos[None, :]
    ok_w = ((dist_w >= 0) & (dist_w < NSA_WIN) & (kwpos >= 0)[None, :])[:, None, None, :]
    lg_w = jnp.einsum('btghd,bngd->btghn', q, kw).astype(f32) + _tok_bias(dist_w, tab)
    p_w = jax.nn.softmax(jnp.where(ok_w, lg_w, NEG), axis=-1)
    o_w = jnp.einsum('btghn,bngd->btghd', p_w.astype(vw.dtype), vw)
    return jnp.stack([o_c, o_s, o_w], axis=-2)


def _over_query_blocks(fn, q, qpos):
    b, t_len = q.shape[:2]
    if t_len <= NSA_QB:
        return fn(q, qpos)
    nblk = -(-t_len // NSA_QB)
    qb = jnp.swapaxes(q.reshape(b, nblk, NSA_QB, *q.shape[2:]), 0, 1)
    pb = qpos.reshape(nblk, NSA_QB)
    out = lax.map(lambda a: fn(a[0], a[1]), (qb, pb))
    out = jnp.swapaxes(out, 0, 1)
    return out.reshape(b, nblk * NSA_QB, *out.shape[3:])[:, :t_len]


def _nsa_combine(o3, ng):
    b, t_len = ng.shape[:2]
    gates = jax.nn.sigmoid(ng.astype(jnp.float32)).reshape(b, t_len, NSA_G, NSA_HG, 3, 1)
    return (o3.astype(jnp.float32) * gates).sum(axis=-2).reshape(b, t_len, NSA_H * NSA_D).astype(o3.dtype)


def _nsa_prompt(q, kc, vc, ks, vs, kw, vw, w_ck, w_cv, g_kc, tab):
    b, t_len = q.shape[:2]
    kc_b = _rms(_compress(kc, w_ck), g_kc)
    vc_b = _compress(vc, w_cv)
    padw = ((0, 0), (NSA_WIN, 0), (0, 0), (0, 0))
    kw_pad = jnp.pad(kw, padw)
    vw_pad = jnp.pad(vw, padw)
    bi = jnp.arange(b)[:, None, None, None]
    gi = jnp.arange(NSA_G)[None, None, :, None]

    def fetch(pos):
        return ks[bi, pos, gi], vs[bi, pos, gi]

    def block(qb, pb):
        start = pb[0]
        n = NSA_WIN + qb.shape[1]
        kwb = lax.dynamic_slice_in_dim(kw_pad, start, n, axis=1)
        vwb = lax.dynamic_slice_in_dim(vw_pad, start, n, axis=1)
        kwpos = start - NSA_WIN + jnp.arange(n, dtype=jnp.int32)
        return _nsa_core(qb, pb, kc_b, vc_b, fetch, kwb, vwb, kwpos, tab)

    return _over_query_blocks(block, q, jnp.arange(t_len, dtype=jnp.int32))


def _nsa_sample(q, kc, vc, ks, vs, kw, vw, pool_kc, pool_vc, pool_ks, pool_vs, buf_kw, buf_vw,
                page_table, w_ck, w_cv, g_kc, tab):
    db, ds = q.shape[:2]

    def full_rows(pool, new):
        past = pool[page_table].reshape(db, PAST_LEN, NSA_G, NSA_D)
        r = jnp.concatenate([past, new], axis=1)
        pad = (-r.shape[1]) % NSA_BLK
        return jnp.pad(r, ((0, 0), (0, pad), (0, 0), (0, 0)))

    kc_b = _rms(_compress(full_rows(pool_kc, kc), w_ck), g_kc)
    vc_b = _compress(full_rows(pool_vc, vc), w_cv)
    bi = jnp.arange(db)[:, None, None, None]
    gi = jnp.arange(NSA_G)[None, None, :, None]

    def fetch(pos):
        in_past = (pos < PAST_LEN)[..., None]
        pp = jnp.minimum(pos, PAST_LEN - 1)
        phys = page_table[bi, pp // PAGE_SIZE]
        off = pp % PAGE_SIZE
        pn = jnp.clip(pos - PAST_LEN, 0, ds - 1)
        k = jnp.where(in_past, pool_ks[phys, off, gi], ks[bi, pn, gi])
        v = jnp.where(in_past, pool_vs[phys, off, gi], vs[bi, pn, gi])
        return k, v

    wb = buf_kw.shape[1]
    kw_all = jnp.concatenate([buf_kw, kw], axis=1)
    vw_all = jnp.concatenate([buf_vw, vw], axis=1)
    kwpos = PAST_LEN - wb + jnp.arange(wb + ds, dtype=jnp.int32)
    qpos = PAST_LEN + jnp.arange(ds, dtype=jnp.int32)
    o3 = _over_query_blocks(lambda qb, pb: _nsa_core(qb, pb, kc_b, vc_b, fetch, kw_all, vw_all, kwpos, tab), q, qpos)
    return o3, kw_all[:, -wb:], vw_all[:, -wb:]


def _even_inputs(xn, w_in, qk_gain):
    b, t_len, _ = xn.shape
    (gq, gk, gv, gz, ga, gb, nq, kc, vc, ks, vs, kw, vw, ng) = _split(_pmm(xn, w_in), EVEN_WIDTHS)
    hd = lambda t: t.reshape(b, t_len, NSA_G, NSA_D)
    q = _rms(nq.reshape(b, t_len, NSA_G, NSA_HG, NSA_D), qk_gain[0]) * (NSA_D ** -0.5)
    ks = _rms(hd(ks), qk_gain[2])
    kw = _rms(hd(kw), qk_gain[3])
    return (gq, gk, gv, gz, ga, gb), (q, hd(kc), hd(vc), ks, hd(vs), kw, hd(vw), ng)


def _even_prompt(xn, w_in, w_out, conv_w, a_log, dt_bias, gnorm, qk_gain, w_ck, w_cv, tab, bias_tiles):
    b, t_len, d = xn.shape
    proj = _matmul(xn.reshape(b * t_len, d).astype(MXU_DTYPE), _permute_even_w(w_in))
    p3 = proj.reshape(b, t_len, EV_N)
    gq, gk, gv = _split(p3[..., :GDN_CONV_CH], (GDN_QK, GDN_QK, GDN_V))
    gdn_parts = (gq, gk, gv, p3[..., EV_Z:EV_Z + GDN_V], p3[..., EV_SMALL:EV_SMALL + GDN_VH],
                 p3[..., EV_SMALL + GDN_VH:EV_SMALL + 2 * GDN_VH])
    conv0 = jnp.zeros((b, GDN_CONV - 1, GDN_CONV_CH), xn.dtype)
    s0 = jnp.zeros((b, GDN_VH, GDN_DK, GDN_DV), xn.dtype)
    ya, conv_new, s = _gdn(gdn_parts, conv_w, a_log, dt_bias, gnorm, conv0, s0)
    qn, ksn, kwn, ksb, vsb, kwb, vwb, kcb, vcb = _nsa_prep(proj, qk_gain, w_ck, w_cv)
    yb = _nsa_prompt_attn(tab, bias_tiles, proj, qn, kcb, vcb, ksb, vsb, kwb, vwb, b, t_len)
    y = _pmm(jnp.concatenate([ya, yb.reshape(b, t_len, -1).astype(ya.dtype)], axis=-1), w_out)
    nw = min(NSA_WIN, t_len)
    hd = lambda t: t.reshape(b, t_len, NSA_G, NSA_D)
    kv = lambda i: hd(p3[..., EV_KV + i * NSA_KV:EV_KV + (i + 1) * NSA_KV])
    return y, (kv(0), kv(1), hd(ksn), kv(3), hd(kwn)[:, -nw:], kv(5)[:, -nw:], conv_new, s)


def _even_sample(xn, w_in, w_out, conv_w, a_log, dt_bias, gnorm, qk_gain, w_ck, w_cv, tab,
                 pool_kc, pool_vc, pool_ks, pool_vs, buf_kw, buf_vw, conv_st, s_st, page_table):
    gdn_parts, (q, kc, vc, ks, vs, kw, vw, ng) = _even_inputs(xn, w_in, qk_gain)
    ya, conv_new, s = _gdn(gdn_parts, conv_w, a_log, dt_bias, gnorm, conv_st, s_st)
    o3, kw_buf, vw_buf = _nsa_sample(q, kc, vc, ks, vs, kw, vw, pool_kc, pool_vc, pool_ks, pool_vs,
                                     buf_kw, buf_vw, page_table, w_ck, w_cv, qk_gain[1], tab)
    y = _pmm(jnp.concatenate([ya, _nsa_combine(o3, ng)], axis=-1), w_out)
    return y, (kc, vc, ks, vs, kw_buf, vw_buf, conv_new, s)


def _mlstm_chunked(q, k, v, ig, lf, c0, n0, m0):
    b, t_len, h, _ = q.shape
    c = min(ML_CHUNK, t_len)
    pad = (-t_len) % c
    incl = jnp.tril(jnp.ones((c, c), bool), 0)

    def body(carry, inp):
        cm, nv, m = carry
        qc, kc, vc, ic, fc = inp
        fcum = jnp.cumsum(fc, axis=-1)
        a = ic - fcum
        mt = fcum + jnp.maximum(m[..., None], lax.cummax(a, axis=2))
        dmat = jnp.exp(jnp.where(incl, a[..., None, :] + (fcum - mt)[..., :, None], -jnp.inf))
        dec0 = jnp.exp(fcum + m[..., None] - mt)
        s = jnp.einsum('bhid,bhjd->bhij', qc, kc) * dmat
        num = dec0[..., None] * jnp.einsum('bhid,bhde->bhie', qc, cm) + jnp.einsum('bhij,bhje->bhie', s, vc)
        den = dec0 * jnp.einsum('bhid,bhd->bhi', qc, nv) + s.sum(axis=-1)
        hc = num / jnp.maximum(jnp.abs(den), jnp.exp(-mt))[..., None]
        m_end = mt[..., -1]
        w = jnp.exp(a + (fcum[..., -1] - m_end)[..., None])
        dc = jnp.exp(fcum[..., -1] + m - m_end)
        cm = dc[..., None, None] * cm + jnp.einsum('bhj,bhjd,bhje->bhde', w, kc, vc)
        nv = dc[..., None] * nv + jnp.einsum('bhj,bhjd->bhd', w, kc)
        return (cm, nv, m_end), hc

    f32 = jnp.float32
    (cm, nv, m), hs = lax.scan(body, (c0.astype(f32), n0.astype(f32), m0.astype(f32)),
                               (_chunks(q, c, pad), _chunks(k, c, pad), _chunks(v, c, pad),
                                _chunks(ig, c, pad, NEG), _chunks(lf, c, pad)))
    return _unchunk(hs, t_len).astype(v.dtype), cm.astype(c0.dtype), nv.astype(n0.dtype), m.astype(m0.dtype)


def _odd(xn, w_in, w_out, b_if, mnorm, c0, n0, m0):
    b, t_len, _ = xn.shape
    q, k, v, o, ig, fg = _split(_pmm(xn, w_in), ODD_WIDTHS)
    q = q.reshape(b, t_len, ML_H, ML_DK) * (ML_DK ** -0.5)
    k = k.reshape(b, t_len, ML_H, ML_DK)
    v = v.reshape(b, t_len, ML_H, ML_DV)
    bf = b_if.astype(jnp.float32)
    ig = ig.astype(jnp.float32) + bf[:ML_H]
    lf = jax.nn.log_sigmoid(fg.astype(jnp.float32) + bf[ML_H:])
    hc, cm, nv, m = _mlstm_chunked(q, k, v, ig, lf, c0, n0, m0)
    hc = _rms(hc, mnorm.reshape(ML_H, ML_DV)).reshape(b, t_len, ODD_MIX)
    return _pmm(hc * jax.nn.sigmoid(o), w_out), (cm, nv, m)


def _rms_cast_kernel(x_ref, g_ref, o_ref):
    x = x_ref[...]
    y = x * lax.rsqrt(jnp.mean(x * x, axis=-1, keepdims=True) + RMS_EPS) * g_ref[...]
    o_ref[...] = y.astype(o_ref.dtype)


def _rms_cast(h, g):
    m, d = h.shape
    tm = _pick(m, (256, 128))
    return pl.pallas_call(
        _rms_cast_kernel,
        grid=(m // tm,),
        in_specs=[pl.BlockSpec((tm, d), lambda i: (i, 0)), pl.BlockSpec((1, d), lambda i: (0, 0))],
        out_specs=pl.BlockSpec((tm, d), lambda i: (i, 0)),
        out_shape=jax.ShapeDtypeStruct((m, d), MXU_DTYPE),
        compiler_params=pltpu.CompilerParams(dimension_semantics=("parallel",),
                                             vmem_limit_bytes=VMEM_LIMIT_BYTES),
    )(h, g.reshape(1, d).astype(jnp.float32))


def _mm2_kernel(xa_ref, xb_ref, w_ref, r_ref, o_ref):
    ka = xa_ref.shape[1]
    y = jnp.dot(xa_ref[...], w_ref[:ka, :], preferred_element_type=jnp.float32)
    y = y + jnp.dot(xb_ref[...], w_ref[ka:, :], preferred_element_type=jnp.float32)
    o_ref[...] = y + r_ref[...]


def _matmul2_res(xa, xb, w, res):
    m, ka = xa.shape
    kb = xb.shape[1]
    n = w.shape[1]
    tm = _pick(m, (512, 256, 128))
    tn = _pick(n, (512, 256, 128))
    return pl.pallas_call(
        _mm2_kernel,
        grid=(m // tm, n // tn),
        in_specs=[pl.BlockSpec((tm, ka), lambda i, j: (i, 0)), pl.BlockSpec((tm, kb), lambda i, j: (i, 0)),
                  pl.BlockSpec((ka + kb, tn), lambda i, j: (0, j)), pl.BlockSpec((tm, tn), lambda i, j: (i, j))],
        out_specs=pl.BlockSpec((tm, tn), lambda i, j: (i, j)),
        out_shape=jax.ShapeDtypeStruct((m, n), jnp.float32),
        compiler_params=pltpu.CompilerParams(dimension_semantics=("parallel", "parallel"),
                                             vmem_limit_bytes=VMEM_LIMIT_BYTES),
    )(xa, xb, w, res)


def _mlp_block(h, g, w1, w2):
    mid = _matmul(_rms_cast(h, g), w1, act='relu2', out_dtype=MXU_DTYPE)
    return _matmul(mid, w2, res=h)


def _pad_time(x, t_pad):
    b, d = x.shape
    return jnp.zeros((b, t_pad, d), x.dtype).at[:, 0, :].set(x).reshape(b * t_pad, d)


def _even_layer(h, g_mix, w_in, w_out, conv_w, a_log, dt_bias, gnorm, qk_gain, w_ck, w_cv, tab, bias_tiles,
                batch, t_len, conv_st, s_st, decode):
    f32 = jnp.float32
    proj = _matmul(_rms_cast(h, g_mix), w_in)
    t_pad = t_len if decode is None else GDN_CHUNK
    pp = proj if decode is None else _pad_time(proj, t_pad)
    first = jnp.pad(conv_st.astype(f32), ((0, 0), (8 - (GDN_CONV - 1), 0), (0, 0)))
    q, k, v, gb = _gdn_prep(pp, first, pp, EV_SMALL // 128, conv_w, a_log, dt_bias, batch, t_pad, t_len)
    ya, s_new = _gdn_chunks(q, k, v, gb, pp, EV_Z // GDN_V, s_st, gnorm, batch, t_pad)
    p3 = proj.reshape(batch, t_len, EV_N)
    xin_tail = jnp.concatenate([conv_st.astype(f32), p3[..., :GDN_CONV_CH]], axis=1)[:, -(GDN_CONV - 1):]
    hd = lambda t: t.reshape(batch, t_len, NSA_G, NSA_D)
    kv = lambda i: hd(p3[..., EV_KV + i * NSA_KV:EV_KV + (i + 1) * NSA_KV])
    if decode is None:
        qn, ksn, kwn, ksb, vsb, kwb, vwb, kcb, vcb = _nsa_prep(proj, qk_gain, w_ck, w_cv)
        yb = _nsa_prompt_attn(tab, bias_tiles, proj, qn, kcb, vcb, ksb, vsb, kwb, vwb, batch, t_len)
        nw = min(NSA_WIN, t_len)
        kw_out, vw_out = hd(kwn)[:, -nw:], kv(5)[:, -nw:]
        ks_out = hd(ksn)
    else:
        ya = ya.reshape(batch, t_pad, GDN_V)[:, 0]
        yb, ksn, kwn = _nsa_decode(proj, tab, qk_gain, w_ck, w_cv, decode, batch)
        yb = yb.astype(MXU_DTYPE)
        ks_out = hd(ksn)
        kw_out = jnp.concatenate([decode['kw'][:, 1:], hd(kwn)], axis=1)
        vw_out = jnp.concatenate([decode['vw'][:, 1:], kv(5)], axis=1)
    h = _matmul2_res(ya, yb, w_out, h)
    return h, (kv(0), kv(1), ks_out, kv(3), kw_out, vw_out, xin_tail, s_new)


def _odd_layer(h, g_mix, w_in, w_out, b_if, mnorm, batch, t_len, c0, n0, m0, decode):
    proj = _matmul(_rms_cast(h, g_mix), w_in)
    t_pad = t_len if not decode else ML_CHUNK
    pp = proj if not decode else _pad_time(proj, t_pad)
    y, c, n, m = _mlstm_chunks(pp, b_if, mnorm, c0, n0, m0, batch, t_pad, t_len)
    if decode:
        y = y.reshape(batch, t_pad, ODD_MIX)[:, 0]
    return _matmul(y, w_out, res=h), (c, n, m)


def kernel(x_prompt, x_sample, cache_kc, cache_vc, cache_ks, cache_vs, cache_kw, cache_vw,
           state_gdn_conv, state_gdn_s, state_ml_c, state_ml_n, state_ml_m, page_table,
           rel_bias, norm_mix, norm_mlp, w_in_even, w_out_even, gdn_conv_w, gdn_a_log,
           gdn_dt_bias, gdn_norm_w, nsa_qk_gain, nsa_w_ck, nsa_w_cv, w_in_odd, w_out_odd,
           ml_b_if, ml_norm_w, w_ff1, w_ff2):
    f32 = jnp.float32
    bp, tp, d = x_prompt.shape
    bs, ts, _ = x_sample.shape
    hp = x_prompt.reshape(bp * tp, d)
    hs = x_sample.reshape(bs * ts, d)
    bias_tiles = _bias_tiles(rel_bias)
    ev_p, ev_s, od_p, od_s = [], [], [], []
    for l in range(DEPTH):
        j = l // 2
        if l % 2 == 0:
            w_in = _permute_even_w(w_in_even[j])
            w_out = w_out_even[j].astype(MXU_DTYPE)
            args = (w_in, w_out, gdn_conv_w[j], gdn_a_log[j], gdn_dt_bias[j], gdn_norm_w[j], nsa_qk_gain[j],
                    nsa_w_ck[j], nsa_w_cv[j], rel_bias, bias_tiles)
            hp, stp = _even_layer(hp, norm_mix[l], *args, bp, tp,
                                  jnp.zeros((bp, GDN_CONV - 1, GDN_CONV_CH), f32),
                                  jnp.zeros((bp, GDN_VH, GDN_DK, GDN_DV), f32), None)
            caches = dict(kc=cache_kc, vc=cache_vc, ks=cache_ks, vs=cache_vs, layer=j, kw=cache_kw[j],
                          vw=cache_vw[j], page_table=page_table)
            hs, sts = _even_layer(hs, norm_mix[l], *args, bs, ts, state_gdn_conv[j], state_gdn_s[j], caches)
            ev_p.append(stp)
            ev_s.append(sts)
        else:
            w_in = _permute_odd_w(w_in_odd[j])
            w_out = w_out_odd[j].astype(MXU_DTYPE)
            hp, stp = _odd_layer(hp, norm_mix[l], w_in, w_out, ml_b_if[j], ml_norm_w[j], bp, tp,
                                 jnp.zeros((bp, ML_H, ML_DK, ML_DV), f32), jnp.zeros((bp, ML_H, ML_DK), f32),
                                 jnp.zeros((bp, ML_H), f32), False)
            hs, sts = _odd_layer(hs, norm_mix[l], w_in, w_out, ml_b_if[j], ml_norm_w[j], bs, ts,
                                 state_ml_c[j], state_ml_n[j], state_ml_m[j], True)
            od_p.append(stp)
            od_s.append(sts)
        w1 = w_ff1[l].astype(MXU_DTYPE)
        w2 = w_ff2[l].astype(MXU_DTYPE)
        hp = _mlp_block(hp, norm_mlp[l], w1, w2)
        hs = _mlp_block(hs, norm_mlp[l], w1, w2)

    kc_p, vc_p, ks_p, vs_p, kw_p, vw_p, conv_p, gdn_p = [jnp.stack(a) for a in zip(*ev_p)]
    kc_s, vc_s, ks_s, vs_s, kw_s, vw_s, conv_s, gdn_s = [jnp.stack(a) for a in zip(*ev_s)]
    mlc_p, mln_p, mlm_p = [jnp.stack(a) for a in zip(*od_p)]
    mlc_s, mln_s, mlm_s = [jnp.stack(a) for a in zip(*od_s)]
    return (hp.reshape(bp, tp, d), hs.reshape(bs, ts, d), kc_p, kc_s, vc_p, vc_s, ks_p, ks_s, vs_p, vs_s,
            kw_p, kw_s, vw_p, vw_s, conv_p, conv_s, gdn_p, gdn_s, mlc_p, mlc_s, mln_p, mln_s, mlm_p, mlm_s)


def _kernel_old(x_prompt, x_sample, cache_kc, cache_vc, cache_ks, cache_vs, cache_kw, cache_vw,
           state_gdn_conv, state_gdn_s, state_ml_c, state_ml_n, state_ml_m, page_table,
           rel_bias, norm_mix, norm_mlp, w_in_even, w_out_even, gdn_conv_w, gdn_a_log,
           gdn_dt_bias, gdn_norm_w, nsa_qk_gain, nsa_w_ck, nsa_w_cv, w_in_odd, w_out_odd,
           ml_b_if, ml_norm_w, w_ff1, w_ff2):
    hp, hs = x_prompt, x_sample
    bias_tiles = _bias_tiles(rel_bias)
    ev_p, ev_s, od_p, od_s = [], [], [], []
    for l in range(DEPTH):
        j = l // 2
        xp = _rms(hp, norm_mix[l])
        xs = _rms(hs, norm_mix[l])
        if l % 2 == 0:
            yp, stp = _even_prompt(xp, w_in_even[j], w_out_even[j], gdn_conv_w[j], gdn_a_log[j],
                                   gdn_dt_bias[j], gdn_norm_w[j], nsa_qk_gain[j], nsa_w_ck[j],
                                   nsa_w_cv[j], rel_bias, bias_tiles)
            ys, sts = _even_sample(xs, w_in_even[j], w_out_even[j], gdn_conv_w[j], gdn_a_log[j],
                                   gdn_dt_bias[j], gdn_norm_w[j], nsa_qk_gain[j], nsa_w_ck[j],
                                   nsa_w_cv[j], rel_bias, cache_kc[j], cache_vc[j], cache_ks[j],
                                   cache_vs[j], cache_kw[j], cache_vw[j], state_gdn_conv[j],
                                   state_gdn_s[j], page_table)
            ev_p.append(stp)
            ev_s.append(sts)
        else:
            bp = hp.shape[0]
            c0 = jnp.zeros((bp, ML_H, ML_DK, ML_DV), hp.dtype)
            n0 = jnp.zeros((bp, ML_H, ML_DK), hp.dtype)
            m0 = jnp.zeros((bp, ML_H), hp.dtype)
            yp, stp = _odd(xp, w_in_odd[j], w_out_odd[j], ml_b_if[j], ml_norm_w[j], c0, n0, m0)
            ys, sts = _odd(xs, w_in_odd[j], w_out_odd[j], ml_b_if[j], ml_norm_w[j],
                           state_ml_c[j], state_ml_n[j], state_ml_m[j])
            od_p.append(stp)
            od_s.append(sts)
        hp = hp + yp
        hs = hs + ys
        hp = hp + _mlp(_rms(hp, norm_mlp[l]), w_ff1[l], w_ff2[l])
        hs = hs + _mlp(_rms(hs, norm_mlp[l]), w_ff1[l], w_ff2[l])

    kc_p, vc_p, ks_p, vs_p, kw_p, vw_p, conv_p, gdn_p = [jnp.stack(a) for a in zip(*ev_p)]
    kc_s, vc_s, ks_s, vs_s, kw_s, vw_s, conv_s, gdn_s = [jnp.stack(a) for a in zip(*ev_s)]
    mlc_p, mln_p, mlm_p = [jnp.stack(a) for a in zip(*od_p)]
    mlc_s, mln_s, mlm_s = [jnp.stack(a) for a in zip(*od_s)]
    return (hp, hs, kc_p, kc_s, vc_p, vc_s, ks_p, ks_s, vs_p, vs_s, kw_p, kw_s, vw_p, vw_s,
            conv_p, conv_s, gdn_p, gdn_s, mlc_p, mlc_s, mln_p, mln_s, mlm_p, mlm_s)
```

```python
import functools
import math

import jax
import jax.numpy as jnp
import numpy as np
from jax import lax
from jax.experimental import pallas as pl
from jax.experimental.pallas import tpu as pltpu

D_MODEL = 2048
BATCH = 2
SEQ = 4096
DEPTH = 4
DEC_BATCH = 8
DEC_SEQ = 1
PAST_LEN = 16384
PAGE_SIZE = 128
RMS_EPS = 1e-6
NEG = -1e30
D_FF = 4 * D_MODEL

GDN_KH = 4
GDN_VH = 8
GDN_DK = 128
GDN_DV = 128
GDN_CONV = 4
GDN_CHUNK = 64
GDN_QK = GDN_KH * GDN_DK
GDN_V = GDN_VH * GDN_DV
GDN_CONV_CH = 2 * GDN_QK + GDN_V

NSA_H = 8
NSA_G = 2
NSA_HG = NSA_H // NSA_G
NSA_D = 128
NSA_BLK = 64
NSA_TOPK = 15
NSA_WIN = 512
NSA_QB = 128
NSA_KV = NSA_G * NSA_D

REL_BUCKETS = 32
REL_EXACT = 16
REL_MAX_DIST = 2048

ML_H = 8
ML_DK = 128
ML_DV = 256
ML_CHUNK = 64

EVEN_WIDTHS = (GDN_QK, GDN_QK, GDN_V, GDN_V, GDN_VH, GDN_VH, NSA_H * NSA_D,
               NSA_KV, NSA_KV, NSA_KV, NSA_KV, NSA_KV, NSA_KV, NSA_H * 3)
ODD_WIDTHS = (ML_H * ML_DK, ML_H * ML_DK, ML_H * ML_DV, ML_H * ML_DV, ML_H, ML_H)
ODD_MIX = ML_H * ML_DV

VMEM_LIMIT_BYTES = 56 * 1024 * 1024
MXU_DTYPE = jnp.bfloat16


def _mm_kernel(x_ref, w_ref, *rest, nk, act, has_res):
    rest = list(rest)
    r_ref = rest.pop(0) if has_res else None
    o_ref = rest.pop(0)

    def finish(y):
        if act == 'relu2':
            y = jnp.square(jnp.maximum(y, 0.0))
        if has_res:
            y = y + r_ref[...]
        o_ref[...] = y.astype(o_ref.dtype)

    d = jnp.dot(x_ref[...], w_ref[...].astype(MXU_DTYPE), preferred_element_type=jnp.float32)
    if nk == 1:
        finish(d)
        return
    acc_ref = rest.pop(0)
    k = pl.program_id(2)

    @pl.when(k == 0)
    def _():
        acc_ref[...] = d

    @pl.when(k > 0)
    def _():
        acc_ref[...] += d

    @pl.when(k == nk - 1)
    def _():
        finish(acc_ref[...])


def _pick(n, prefs):
    for p in prefs:
        if n % p == 0:
            return p
    return n


def _matmul(x, w, *, act=None, res=None, out_dtype=jnp.float32, tiles=None):
    m, kdim = x.shape
    _, n = w.shape
    tm = _pick(m, (tiles[0],) if tiles else (512, 256, 128))
    tn = _pick(n, (tiles[1],) if tiles else (512, 896, 384, 256, 128))
    tk = _pick(kdim, (2048, 1024, 512))
    nk = kdim // tk
    in_specs = [pl.BlockSpec((tm, tk), lambda i, j, k: (i, k)),
                pl.BlockSpec((tk, tn), lambda i, j, k: (k, j))]
    args = [x, w]
    if res is not None:
        in_specs.append(pl.BlockSpec((tm, tn), lambda i, j, k: (i, j)))
        args.append(res)
    return pl.pallas_call(
        functools.partial(_mm_kernel, nk=nk, act=act, has_res=res is not None),
        grid=(m // tm, n // tn, nk),
        in_specs=in_specs,
        out_specs=pl.BlockSpec((tm, tn), lambda i, j, k: (i, j)),
        out_shape=jax.ShapeDtypeStruct((m, n), out_dtype),
        scratch_shapes=[pltpu.VMEM((tm, tn), jnp.float32)] if nk > 1 else [],
        compiler_params=pltpu.CompilerParams(
            dimension_semantics=("parallel", "parallel", "arbitrary"),
            vmem_limit_bytes=VMEM_LIMIT_BYTES),
    )(*args)


def _pmm(x, w):
    lead = x.shape[:-1]
    x2 = x.reshape(-1, x.shape[-1]).astype(MXU_DTYPE)
    n = w.shape[1]
    npad = (-n) % 128
    wb = w.astype(MXU_DTYPE)
    if npad:
        wb = jnp.pad(wb, ((0, 0), (0, npad)))
    y = _matmul(x2, wb)
    if npad:
        y = y[:, :n]
    return y.reshape(*lead, n)


EV_CONV = 0
EV_Z = GDN_CONV_CH
EV_NQ = EV_Z + GDN_V
EV_KV = EV_NQ + NSA_H * NSA_D
EV_SMALL = EV_KV + 6 * NSA_KV
EV_N = EV_SMALL + 128
EV_ORIG_SMALL = 2 * GDN_QK + 2 * GDN_V
EV_ORIG_N = EV_ORIG_SMALL + 2 * GDN_VH + NSA_H * NSA_D + 6 * NSA_KV + 3 * NSA_H


def _permute_even_w(w_in):
    ab = w_in[:, EV_ORIG_SMALL:EV_ORIG_SMALL + 2 * GDN_VH]
    big = w_in[:, EV_ORIG_SMALL + 2 * GDN_VH:EV_ORIG_N - 3 * NSA_H]
    ng = w_in[:, EV_ORIG_N - 3 * NSA_H:]
    pad = jnp.zeros((w_in.shape[0], EV_N - EV_ORIG_N), w_in.dtype)
    return jnp.concatenate([w_in[:, :EV_ORIG_SMALL], big, ab, ng, pad], axis=1).astype(MXU_DTYPE)


def _bucket_thresholds():
    n = np.arange(0, 4 * REL_MAX_DIST)
    nf = np.maximum(n, REL_EXACT).astype(np.float32)
    large = REL_EXACT + (np.log(nf / np.float32(REL_EXACT)) / np.float32(math.log(REL_MAX_DIST / REL_EXACT))
                         * np.float32(REL_BUCKETS - REL_EXACT)).astype(np.int32)
    b = np.where(n < REL_EXACT, n, np.minimum(large, REL_BUCKETS - 1))
    return tuple(int(np.argmax(b >= k)) for k in range(REL_BUCKETS))


BUCKET_THR = _bucket_thresholds()
BIAS_TILES = 14
assert BUCKET_THR[-1] <= (BIAS_TILES - 1) * 128 - 127


def _bias_of_dist(dist, tab_ref, head):
    val = jnp.full(dist.shape, tab_ref[0, head], jnp.float32)
    for k in range(1, REL_BUCKETS):
        val = jnp.where(dist >= BUCKET_THR[k], tab_ref[k, head], val)
    return val


def _bias_tiles_kernel(tab_ref, o_ref):
    h = pl.program_id(0)
    d = pl.program_id(1)
    i = lax.broadcasted_iota(jnp.int32, (128, 128), 1)
    j = lax.broadcasted_iota(jnp.int32, (128, 128), 0)
    dist = d * 128 + i - j
    val = jnp.full(dist.shape, tab_ref[0, h], jnp.float32)
    for k in range(1, REL_BUCKETS):
        val = jnp.where(dist >= BUCKET_THR[k], tab_ref[k, h], val)
    o_ref[0, 0] = val


def _bias_tiles(tab):
    return pl.pallas_call(
        _bias_tiles_kernel,
        grid=(NSA_H, BIAS_TILES),
        in_specs=[pl.BlockSpec(memory_space=pltpu.SMEM)],
        out_specs=pl.BlockSpec((1, 1, 128, 128), lambda h, d: (h, d, 0, 0)),
        out_shape=jax.ShapeDtypeStruct((NSA_H, BIAS_TILES, 128, 128), jnp.float32),
    )(tab.astype(jnp.float32))


def _group_rms(x, gain_row, scale=1.0):
    ms = jnp.mean(x * x, axis=-1, keepdims=True)
    y = x * lax.rsqrt(ms + RMS_EPS) * gain_row
    return y * scale if scale != 1.0 else y


def _nsa_prep_kernel(nq_ref, kc_ref, vc_ref, ks_ref, vs_ref, kw_ref, vw_ref, gain_ref, wck_ref, wcv_ref,
                     qn_ref, ksn_ref, kwn_ref, ksb_ref, vst_ref, kwb_ref, vwt_ref, kcb_ref, vcb_ref):
    tm = nq_ref.shape[0]
    g0 = gain_ref[0:1, :]
    g1 = gain_ref[1:2, :]
    g2 = gain_ref[2:3, :]
    g3 = gain_ref[3:4, :]
    for h in range(NSA_H):
        sl = slice(h * NSA_D, (h + 1) * NSA_D)
        qn_ref[:, sl] = _group_rms(nq_ref[:, sl], g0, NSA_D ** -0.5).astype(qn_ref.dtype)
    for g in range(NSA_G):
        sl = slice(g * NSA_D, (g + 1) * NSA_D)
        ksn = _group_rms(ks_ref[:, sl], g2)
        kwn = _group_rms(kw_ref[:, sl], g3)
        ksn_ref[:, sl] = ksn
        kwn_ref[:, sl] = kwn
        ksb_ref[:, sl] = ksn.astype(ksb_ref.dtype)
        kwb_ref[:, sl] = kwn.astype(kwb_ref.dtype)
    for src, dst in ((vs_ref, vst_ref), (vw_ref, vwt_ref)):
        vt = src[...].T
        tk = dst.shape[2]
        for c in range(tm // tk):
            dst[c] = vt[:, c * tk:(c + 1) * tk].astype(dst.dtype)
    nblk = tm // NSA_BLK
    kc3 = kc_ref[...].reshape(nblk, NSA_BLK, NSA_KV)
    vc3 = vc_ref[...].reshape(nblk, NSA_BLK, NSA_KV)
    kcb = jnp.sum(kc3 * wck_ref[...][None], axis=1)
    vcb = jnp.sum(vc3 * wcv_ref[...][None], axis=1)
    for g in range(NSA_G):
        sl = slice(g * NSA_D, (g + 1) * NSA_D)
        kcb_ref[:, sl] = _group_rms(kcb[:, sl], g1)
    vcb_ref[...] = vcb


def _nsa_prep(proj, gain, w_ck, w_cv):
    m = proj.shape[0]
    tm = 512
    kvb = EV_KV // NSA_KV
    f32, bf16 = jnp.float32, MXU_DTYPE

    def kv_spec(i):
        return pl.BlockSpec((tm, NSA_KV), lambda r, i=i: (r, kvb + i))

    row = lambda w: pl.BlockSpec((tm, w), lambda r: (r, 0))
    slabs = lambda tk: pl.BlockSpec((tm // tk, NSA_KV, tk), lambda r: (r, 0, 0))
    full = lambda a: pl.BlockSpec(a.shape, lambda r: (0,) * a.ndim)
    wck = w_ck.reshape(NSA_BLK, NSA_KV)
    wcv = w_cv.reshape(NSA_BLK, NSA_KV)
    outs = pl.pallas_call(
        _nsa_prep_kernel,
        grid=(m // tm,),
        in_specs=[pl.BlockSpec((tm, NSA_H * NSA_D), lambda r: (r, EV_NQ // (NSA_H * NSA_D)))]
                 + [kv_spec(i) for i in range(6)] + [full(gain), full(wck), full(wcv)],
        out_specs=[row(NSA_H * NSA_D), row(NSA_KV), row(NSA_KV), row(NSA_KV), slabs(NSA_TK), row(NSA_KV),
                   slabs(NSA_QB)] + [pl.BlockSpec((tm // NSA_BLK, NSA_KV), lambda r: (r, 0))] * 2,
        out_shape=[jax.ShapeDtypeStruct((m, NSA_H * NSA_D), bf16),
                   jax.ShapeDtypeStruct((m, NSA_KV), f32), jax.ShapeDtypeStruct((m, NSA_KV), f32),
                   jax.ShapeDtypeStruct((m, NSA_KV), bf16), jax.ShapeDtypeStruct((m // NSA_TK, NSA_KV, NSA_TK), bf16),
                   jax.ShapeDtypeStruct((m, NSA_KV), bf16), jax.ShapeDtypeStruct((m // NSA_QB, NSA_KV, NSA_QB), bf16)]
                  + [jax.ShapeDtypeStruct((m // NSA_BLK, NSA_KV), f32)] * 2,
        compiler_params=pltpu.CompilerParams(dimension_semantics=("parallel",),
                                             vmem_limit_bytes=VMEM_LIMIT_BYTES),
    )(proj, proj, proj, proj, proj, proj, proj, gain, wck, wcv)
    return outs


NSA_TK = 256


def _softmax_rows(lg):
    m = jnp.max(lg, axis=-1, keepdims=True)
    e = jnp.exp(lg - m)
    return e / jnp.sum(e, axis=-1, keepdims=True)


def _pad_rows(x, rows):
    if x.shape[0] == rows:
        return x
    return jnp.concatenate([x, jnp.zeros((rows - x.shape[0],) + x.shape[1:], x.dtype)], axis=0)


def _nsa_attn_kernel(tab_ref, q_ref, kcb_ref, vcb_ref, ks_ref, vst_ref, kw_ref, vwt_ref, bias_ref, ng_ref,
                     o_ref, m_scr, l_scr, acc_scr):
    f32, bf16 = jnp.float32, MXU_DTYPE
    qb = pl.program_id(1)
    nb = kcb_ref.shape[0]
    qn = NSA_QB
    gates = jax.nn.sigmoid(ng_ref[...])
    tsub = NSA_TK // qn
    shift = NSA_BLK.bit_length() - 1

    for g in range(NSA_G):
        gl = slice(g * NSA_D, (g + 1) * NSA_D)
        qs = [q_ref[:, (g * NSA_HG + h) * NSA_D:(g * NSA_HG + h + 1) * NSA_D] for h in range(NSA_HG)]
        q2 = jnp.concatenate(qs, axis=0)
        qts = [q.astype(f32).T.astype(bf16) for q in qs]

        kcb = kcb_ref[:, gl].astype(bf16)
        vcb = _pad_rows(vcb_ref[:, gl], qn).astype(bf16)
        lgt = lax.dot_general(kcb, q2, (((1,), (1,)), ((), ())), preferred_element_type=f32)
        blk = lax.broadcasted_iota(jnp.int32, (nb, qn), 0)
        blkf = blk.astype(f32)
        qpos = qb * qn + lax.broadcasted_iota(jnp.int32, (nb, qn), 1)
        dist_c = qpos - (blk * NSA_BLK + NSA_BLK - 1)
        ok_c = dist_c >= 0
        score = jnp.zeros((nb, qn), f32)
        o_c = []
        for h in range(NSA_HG):
            lg = lgt[:, h * qn:(h + 1) * qn] + _bias_of_dist(dist_c, tab_ref, g * NSA_HG + h)
            lg = jnp.where(ok_c, lg, NEG)
            mx = jnp.max(lg, axis=0, keepdims=True)
            e = jnp.exp(lg - mx)
            p = jnp.where(ok_c, e / jnp.sum(e, axis=0, keepdims=True), 0.0)
            score = score + p
            p_t = _pad_rows(p, qn).T.astype(bf16)
            o_c.append(jnp.dot(p_t, vcb, preferred_element_type=f32))
        cur = lax.shift_right_logical(qpos, shift)
        score = jnp.where(blk < cur, score, -1.0)

        sel = jnp.zeros((nb, qn), f32)
        for _ in range(min(NSA_TOPK, nb)):
            mx = jnp.max(score, axis=0, keepdims=True)
            first = jnp.min(jnp.where(score == mx, blkf, float(nb)), axis=0, keepdims=True)
            pick = blkf == first
            sel = jnp.where(pick, jnp.where(mx >= 0.0, 1.0, 0.0), sel)
            score = jnp.where(pick, -2.0, score)
        sel = jnp.where(blk == cur, 1.0, sel)
        sel_p = _pad_rows(sel, qn).astype(bf16)

        def reset():
            m_scr[...] = jnp.full(m_scr.shape, NEG, f32)
            l_scr[...] = jnp.zeros(l_scr.shape, f32)
            acc_scr[...] = jnp.zeros(acc_scr.shape, f32)

        def update(h, lg, vt):
            m_old = m_scr[h]
            m_new = jnp.maximum(m_old, jnp.max(lg, axis=0, keepdims=True))
            p = jnp.exp(lg - m_new)
            alpha = jnp.exp(m_old - m_new)
            l_scr[h] = alpha * l_scr[h] + jnp.sum(p, axis=0, keepdims=True)
            acc_scr[h] = alpha * acc_scr[h] + jnp.dot(vt, p.astype(bf16), preferred_element_type=f32)
            m_scr[h] = m_new

        def result(h):
            return (acc_scr[h] / l_scr[h]).T

        reset()
        n_tiles = qb // tsub + 1
        tok = qb * qn + lax.broadcasted_iota(jnp.int32, (NSA_TK, qn), 1)
        key = lax.broadcasted_iota(jnp.int32, (NSA_TK, qn), 0)
        eblk = lax.broadcasted_iota(jnp.int32, (NSA_TK, qn), 1)

        def body(i, carry, g=g, gl=gl, qts=qts, sel_p=sel_p):
            kt = n_tiles - 1 - i
            k0 = pl.multiple_of(kt * NSA_TK, NSA_TK)
            k = ks_ref[pl.ds(k0, NSA_TK), gl]
            vt = vst_ref[kt, gl, :]
            expand = jnp.where(eblk == lax.shift_right_logical(k0 + key, shift), 1.0, 0.0).astype(bf16)
            member = jnp.dot(expand, sel_p, preferred_element_type=f32)
            keep = jnp.where(k0 + key <= tok, member, 0.0) > 0.5
            for h in range(NSA_HG):
                tiles = [bias_ref[g * NSA_HG + h, jnp.clip(qb - (kt * tsub + j), 0, BIAS_TILES - 1)]
                         for j in range(tsub)]
                lg = jnp.dot(k, qts[h], preferred_element_type=f32) + jnp.concatenate(tiles, axis=0)
                update(h, jnp.where(keep, lg, NEG), vt)
            return carry

        lax.fori_loop(0, n_tiles, body, 0)
        o_s = [result(h) for h in range(NSA_HG)]

        reset()
        nwin = NSA_WIN // qn + 1
        wi = lax.broadcasted_iota(jnp.int32, (qn, qn), 1)
        wj = lax.broadcasted_iota(jnp.int32, (qn, qn), 0)
        for d in range(nwin):
            sub = qb - d
            subc = jnp.maximum(sub, 0)
            k = kw_ref[pl.ds(pl.multiple_of(subc * qn, qn), qn), gl]
            vt = vwt_ref[subc, gl, :]
            for h in range(NSA_HG):
                x = jnp.dot(k, qts[h], preferred_element_type=f32) + bias_ref[g * NSA_HG + h, d]
                if d == 0:
                    x = jnp.where(wj <= wi, x, NEG)
                else:
                    if d == nwin - 1:
                        x = jnp.where(wj > wi, x, NEG)
                    x = jnp.where(sub >= 0, x, NEG)
                update(h, x, vt)
        o_w = [result(h) for h in range(NSA_HG)]

        for h in range(NSA_HG):
            c = 2 * GDN_VH + (g * NSA_HG + h) * 3
            y = o_c[h] * gates[:, c:c + 1] + o_s[h] * gates[:, c + 1:c + 2] + o_w[h] * gates[:, c + 2:c + 3]
            o_ref[:, (g * NSA_HG + h) * NSA_D:(g * NSA_HG + h + 1) * NSA_D] = y.astype(o_ref.dtype)


def _nsa_prompt_attn(tab, bias_tiles, proj, qn, kcb, vcb, ksb, vst, kwb, vwt, batch, t_len):
    nq = t_len // NSA_QB
    nb = t_len // NSA_BLK
    hd = NSA_H * NSA_D
    per_b = lambda w: pl.BlockSpec((t_len, w), lambda b, i: (b, 0))
    tiles_b = lambda tk: pl.BlockSpec((t_len // tk, NSA_KV, tk), lambda b, i: (b, 0, 0))
    return pl.pallas_call(
        _nsa_attn_kernel,
        grid=(batch, nq),
        in_specs=[pl.BlockSpec(memory_space=pltpu.SMEM),
                  pl.BlockSpec((NSA_QB, hd), lambda b, i: (b * nq + i, 0)),
                  pl.BlockSpec((nb, NSA_KV), lambda b, i: (b, 0)),
                  pl.BlockSpec((nb, NSA_KV), lambda b, i: (b, 0)),
                  per_b(NSA_KV), tiles_b(NSA_TK), per_b(NSA_KV), tiles_b(NSA_QB),
                  pl.BlockSpec(bias_tiles.shape, lambda b, i: (0, 0, 0, 0)),
                  pl.BlockSpec((NSA_QB, 128), lambda b, i: (b * nq + i, EV_SMALL // 128 + 0))],
        out_specs=pl.BlockSpec((NSA_QB, hd), lambda b, i: (b * nq + i, 0)),
        out_shape=jax.ShapeDtypeStruct((batch * t_len, hd), MXU_DTYPE),
        scratch_shapes=[pltpu.VMEM((NSA_HG, 1, NSA_QB), jnp.float32),
                        pltpu.VMEM((NSA_HG, 1, NSA_QB), jnp.float32),
                        pltpu.VMEM((NSA_HG, NSA_D, NSA_QB), jnp.float32)],
        compiler_params=pltpu.CompilerParams(dimension_semantics=("parallel", "arbitrary"),
                                             vmem_limit_bytes=VMEM_LIMIT_BYTES),
    )(tab.astype(jnp.float32), qn, kcb, vcb, ksb, vst, kwb, vwt, bias_tiles, proj)


def _hdot(a, b):
    return jnp.dot(a, b, precision=lax.Precision.HIGHEST, preferred_element_type=jnp.float32)


def _mdot(a, b):
    return jnp.dot(a.astype(MXU_DTYPE), b.astype(MXU_DTYPE), preferred_element_type=jnp.float32)


def _mdot_nt(a, b):
    return lax.dot_general(a.astype(MXU_DTYPE), b.astype(MXU_DTYPE), (((1,), (1,)), ((), ())),
                           preferred_element_type=jnp.float32)


def _mdot_tn(a, b):
    return lax.dot_general(a.astype(MXU_DTYPE), b.astype(MXU_DTYPE), (((0,), (0,)), ((), ())),
                           preferred_element_type=jnp.float32)


def _shift_rows(x, prev, s):
    xs = pltpu.roll(x, s, axis=0)
    ps = pltpu.roll(prev, s, axis=0)
    row8 = lax.broadcasted_iota(jnp.int32, prev.shape, 0)
    head = jnp.where(row8 < s, ps, xs[0:8])
    return jnp.concatenate([head, xs[8:]], axis=0)


def _gdn_prep_kernel(x_ref, prev_ref, first_ref, ab_ref, cw_ref, alog_ref, dtb_ref,
                     q_ref, k_ref, v_ref, gb_ref, *, tiles_per_batch, t_valid):
    r = pl.program_id(0)
    tm = x_ref.shape[0]
    tile = r % tiles_per_batch
    prev = jnp.where(tile == 0, first_ref[0], prev_ref[...])
    x = x_ref[...]
    acc = x * cw_ref[GDN_CONV - 1:GDN_CONV, :]
    for s in range(1, GDN_CONV):
        acc = acc + _shift_rows(x, prev, s) * cw_ref[GDN_CONV - 1 - s:GDN_CONV - s, :]
    c = acc * jax.nn.sigmoid(acc)

    def l2(t):
        return t * lax.rsqrt(jnp.sum(t * t, axis=-1, keepdims=True) + 1e-6)

    for h in range(GDN_KH):
        sl = slice(h * GDN_DK, (h + 1) * GDN_DK)
        q_ref[:, sl] = l2(c[:, sl]) * (GDN_DK ** -0.5)
        k_ref[:, sl] = l2(c[:, GDN_QK + h * GDN_DK:GDN_QK + (h + 1) * GDN_DK])
    v_ref[...] = c[:, 2 * GDN_QK:]
    ab = ab_ref[...]
    z = ab + dtb_ref[...]
    softplus = jnp.maximum(z, 0.0) + jnp.log(1.0 + jnp.exp(-jnp.abs(z)))
    gate = -jnp.exp(alog_ref[...]) * softplus
    lane = lax.broadcasted_iota(jnp.int32, ab.shape, 1)
    pos = tile * tm + lax.broadcasted_iota(jnp.int32, ab.shape, 0)
    gb = jnp.where(lane < GDN_VH, gate, jnp.where(lane < 2 * GDN_VH, jax.nn.sigmoid(ab), 0.0))
    gb_ref[...] = jnp.where(pos < t_valid, gb, 0.0)


def _gdn_prep(xin, first, ab, ab_col, conv_w, a_log, dt_bias, batch, t_pad, t_valid):
    m = batch * t_pad
    tm = _pick(t_pad, (256, 128, 64))
    tpb = t_pad // tm
    f32 = jnp.float32
    lanes = lambda v: jnp.zeros((1, 128), f32).at[0, :GDN_VH].set(v.astype(f32))
    return pl.pallas_call(
        functools.partial(_gdn_prep_kernel, tiles_per_batch=tpb, t_valid=t_valid),
        grid=(m // tm,),
        in_specs=[pl.BlockSpec((tm, GDN_CONV_CH), lambda r: (r, 0)),
                  pl.BlockSpec((8, GDN_CONV_CH), lambda r: (jnp.maximum(r * (tm // 8) - 1, 0), 0)),
                  pl.BlockSpec((1, 8, GDN_CONV_CH), lambda r: (r // tpb, 0, 0)),
                  pl.BlockSpec((tm, 128), lambda r: (r, ab_col)),
                  pl.BlockSpec((GDN_CONV, GDN_CONV_CH), lambda r: (0, 0)),
                  pl.BlockSpec((1, 128), lambda r: (0, 0)),
                  pl.BlockSpec((1, 128), lambda r: (0, 0))],
        out_specs=[pl.BlockSpec((tm, GDN_QK), lambda r: (r, 0)),
                   pl.BlockSpec((tm, GDN_QK), lambda r: (r, 0)),
                   pl.BlockSpec((tm, GDN_V), lambda r: (r, 0)),
                   pl.BlockSpec((tm, 128), lambda r: (r, 0))],
        out_shape=[jax.ShapeDtypeStruct((m, GDN_QK), f32), jax.ShapeDtypeStruct((m, GDN_QK), f32),
                   jax.ShapeDtypeStruct((m, GDN_V), f32), jax.ShapeDtypeStruct((m, 128), f32)],
        compiler_params=pltpu.CompilerParams(dimension_semantics=("arbitrary",),
                                             vmem_limit_bytes=VMEM_LIMIT_BYTES),
    )(xin, xin, first, ab, conv_w.astype(f32), lanes(a_log), lanes(dt_bias))


def _cumsum_rows(x):
    row = lax.broadcasted_iota(jnp.int32, x.shape, 0)
    s = 1
    while s < x.shape[0]:
        x = x + jnp.where(row >= s, pltpu.roll(x, s, axis=0), 0.0)
        s *= 2
    return x


def _split3(a):
    hi = a.astype(jnp.bfloat16)
    lo = (a - hi.astype(jnp.float32)).astype(jnp.bfloat16)
    return hi, lo


def _dot3(a, b):
    if MXU_DTYPE != jnp.bfloat16:
        return jnp.dot(a, b, preferred_element_type=jnp.float32)
    ah, al = _split3(a)
    bh, bl = _split3(b)
    d = lambda x, y: jnp.dot(x, y, preferred_element_type=jnp.float32)
    return d(ah, bh) + (d(ah, bl) + d(al, bh))


def _unit_lower_inverses(lmats, row, col):
    eye = jnp.where(row == col, 1.0, 0.0)
    blk = lax.shift_right_logical(row, 4) == lax.shift_right_logical(col, 4)
    lds = [jnp.where(blk, l, 0.0) for l in lmats]
    xs = [eye - ld for ld in lds]
    ps = [_dot3(ld, ld) for ld in lds]
    for step in range(3):
        xs = [x + _dot3(x, p) for x, p in zip(xs, ps)]
        if step < 2:
            ps = [_dot3(p, p) for p in ps]
    size = 16
    while size < lmats[0].shape[0]:
        sh = size.bit_length() - 1
        inner = lax.shift_right_logical(row, sh) == lax.shift_right_logical(col, sh)
        outer = lax.shift_right_logical(row, sh + 1) == lax.shift_right_logical(col, sh + 1)
        coffs = [jnp.where(outer, jnp.where(inner, 0.0, l), 0.0) for l in lmats]
        ts = [_dot3(x, c) for x, c in zip(xs, coffs)]
        xs = [x - _dot3(t, x) for x, t in zip(xs, ts)]
        size *= 2
    return xs


def _gdn_chunk_kernel(q_ref, k_ref, v_ref, gb_ref, z_ref, s0_ref, nw_ref, y_ref, s_ref):
    f32 = jnp.float32
    cn = q_ref.shape[0]

    @pl.when(pl.program_id(1) == 0)
    def _():
        s_ref[...] = s0_ref[...]

    gb = gb_ref[...]
    gcum = _cumsum_rows(gb)
    gcum_t = _pad_rows(gcum, 128).T
    row = lax.broadcasted_iota(jnp.int32, (cn, cn), 0)
    col = lax.broadcasted_iota(jnp.int32, (cn, cn), 1)
    rep = GDN_VH // GDN_KH
    heads = range(GDN_VH)
    qs = [q_ref[:, kh * GDN_DK:(kh + 1) * GDN_DK] for kh in range(GDN_KH)]
    ks = [k_ref[:, kh * GDN_DK:(kh + 1) * GDN_DK] for kh in range(GDN_KH)]
    kk = [_mdot_nt(k, k) for k in ks]
    qk = [_mdot_nt(q, k) for q, k in zip(qs, ks)]
    gcols = [gcum[:, h:h + 1] for h in heads]
    diffs = [gcols[h] - gcum_t[h:h + 1, :cn] for h in heads]
    egs = [jnp.exp(g) for g in gcols]
    betas = [gb[:, GDN_VH + h:GDN_VH + h + 1] for h in heads]
    lmats = [betas[h] * kk[h // rep] * jnp.exp(jnp.where(col < row, diffs[h], -jnp.inf)) for h in heads]
    ainvs = _unit_lower_inverses(lmats, row, col)
    sols = [_dot3(ainvs[h], jnp.concatenate([betas[h] * v_ref[:, h * GDN_DV:(h + 1) * GDN_DV],
                                             (betas[h] * egs[h]) * ks[h // rep]], axis=1)) for h in heads]
    atts = [qk[h // rep] * jnp.exp(jnp.where(col <= row, diffs[h], -jnp.inf)) for h in heads]
    ss = [s_ref[0, h] for h in heads]
    us = [sols[h][:, :GDN_DV] - _mdot(sols[h][:, GDN_DV:], ss[h]) for h in heads]
    os_ = [egs[h] * _mdot(qs[h // rep], ss[h]) + _mdot(atts[h], us[h]) for h in heads]
    for h in heads:
        gl = gcum[cn - 1:cn, h:h + 1]
        s_ref[0, h] = jnp.exp(gl) * ss[h] + _mdot_tn(ks[h // rep] * jnp.exp(gl - gcols[h]), us[h])
    for h in heads:
        o = os_[h]
        on = o * lax.rsqrt(jnp.mean(o * o, axis=-1, keepdims=True) + RMS_EPS) * nw_ref[...]
        z = z_ref[:, h * GDN_DV:(h + 1) * GDN_DV]
        y_ref[:, h * GDN_DV:(h + 1) * GDN_DV] = (on * (z * jax.nn.sigmoid(z))).astype(y_ref.dtype)


def _gdn_chunks(q, k, v, gb, zsrc, z_col, s0, norm_w, batch, t_pad):
    cn = GDN_CHUNK
    nc = t_pad // cn
    rowblk = lambda w, c=0: pl.BlockSpec((cn, w), lambda b, i, c=c: (b * nc + i, c))
    st = pl.BlockSpec((1, GDN_VH, GDN_DK, GDN_DV), lambda b, i: (b, 0, 0, 0))
    return pl.pallas_call(
        _gdn_chunk_kernel,
        grid=(batch, nc),
        in_specs=[rowblk(GDN_QK), rowblk(GDN_QK), rowblk(GDN_V), rowblk(128), rowblk(GDN_V, z_col), st,
                  pl.BlockSpec((1, GDN_DV), lambda b, i: (0, 0))],
        out_specs=[rowblk(GDN_V), st],
        out_shape=[jax.ShapeDtypeStruct((batch * t_pad, GDN_V), MXU_DTYPE),
                   jax.ShapeDtypeStruct((batch, GDN_VH, GDN_DK, GDN_DV), jnp.float32)],
        compiler_params=pltpu.CompilerParams(dimension_semantics=("parallel", "arbitrary"),
                                             vmem_limit_bytes=VMEM_LIMIT_BYTES),
    )(q, k, v, gb, zsrc, s0.astype(jnp.float32), norm_w.reshape(1, GDN_DV).astype(jnp.float32))


OD_Q = 0
OD_K = ML_H * ML_DK
OD_V = 2 * ML_H * ML_DK
OD_O = OD_V + ML_H * ML_DV
OD_SMALL = OD_O + ML_H * ML_DV
OD_N = OD_SMALL + 128


def _cummax_rows(x):
    row = lax.broadcasted_iota(jnp.int32, x.shape, 0)
    s = 1
    while s < x.shape[0]:
        x = jnp.maximum(x, jnp.where(row >= s, pltpu.roll(x, s, axis=0), -jnp.inf))
        s *= 2
    return x


def _round_mxu(x):
    return x.astype(MXU_DTYPE).astype(jnp.float32)


def _mlstm_chunk_kernel(q_ref, k_ref, v_ref, og_ref, if_ref, bif_ref, nw_ref, c0_ref, n0_ref, m0_ref,
                        y_ref, c_ref, n_ref, m_ref, *, t_valid):
    f32 = jnp.float32
    cn = q_ref.shape[0]
    ci = pl.program_id(1)

    @pl.when(ci == 0)
    def _():
        c_ref[...] = c0_ref[...]
        n_ref[...] = n0_ref[...]
        m_ref[...] = m0_ref[...]

    pre = if_ref[...] + bif_ref[...]
    pos = ci * cn + lax.broadcasted_iota(jnp.int32, pre.shape, 0)
    live = pos < t_valid
    lf = jnp.where(live, jnp.minimum(pre, 0.0) - jnp.log(1.0 + jnp.exp(-jnp.abs(pre))), 0.0)
    fcum = _cumsum_rows(lf)
    a_all = jnp.where(live, pre, NEG) - pltpu.roll(fcum, 128 - ML_H, axis=1)
    amax = _cummax_rows(a_all)
    a_t = _pad_rows(a_all, 128).T
    row = lax.broadcasted_iota(jnp.int32, (cn, cn), 0)
    col = lax.broadcasted_iota(jnp.int32, (cn, cn), 1)
    m_all = m_ref[0]
    heads = range(ML_H)
    qs = [q_ref[:, h * ML_DK:(h + 1) * ML_DK] * (ML_DK ** -0.5) for h in heads]
    ks = [k_ref[:, h * ML_DK:(h + 1) * ML_DK] for h in heads]
    vs = [v_ref[:, h * ML_DV:(h + 1) * ML_DV] for h in heads]
    m_prevs = [m_all[:, h:h + 1] for h in heads]
    fcs = [fcum[:, ML_H + h:ML_H + h + 1] for h in heads]
    mts = [fcs[h] + jnp.maximum(m_prevs[h], amax[:, h:h + 1]) for h in heads]
    qks = [_mdot_nt(qs[h], ks[h]) for h in heads]
    ss = [qks[h] * jnp.exp(jnp.where(col <= row, a_t[h:h + 1, :cn] + (fcs[h] - mts[h]), -jnp.inf)) for h in heads]
    dec0s = [jnp.exp(fcs[h] + m_prevs[h] - mts[h]) for h in heads]
    cms = [c_ref[0, h] for h in heads]
    nvs = [n_ref[0, h:h + 1, :] for h in heads]
    nums = [dec0s[h] * _mdot(qs[h], cms[h]) + _mdot(ss[h], vs[h]) for h in heads]
    dens = [dec0s[h] * jnp.sum(_round_mxu(qs[h]) * _round_mxu(nvs[h]), axis=-1, keepdims=True)
            + jnp.sum(ss[h], axis=-1, keepdims=True) for h in heads]
    hcs = [nums[h] / jnp.maximum(jnp.abs(dens[h]), jnp.exp(-mts[h])) for h in heads]
    m_new = [mts[h][cn - 1:cn] for h in heads]
    for h in heads:
        f_end = fcs[h][cn - 1:cn]
        w = jnp.exp(a_all[:, h:h + 1] + (f_end - m_new[h]))
        dc = jnp.exp(f_end + m_prevs[h] - m_new[h])
        c_ref[0, h] = dc * cms[h] + _mdot_tn(w * ks[h], vs[h])
        n_ref[0, h:h + 1, :] = dc * nvs[h] + jnp.sum(_round_mxu(w) * _round_mxu(ks[h]), axis=0, keepdims=True)
    for h in heads:
        hc = hcs[h]
        hn = hc * lax.rsqrt(jnp.mean(hc * hc, axis=-1, keepdims=True) + RMS_EPS) \
            * nw_ref[:, h * ML_DV:(h + 1) * ML_DV]
        og = og_ref[:, h * ML_DV:(h + 1) * ML_DV]
        y_ref[:, h * ML_DV:(h + 1) * ML_DV] = (hn * jax.nn.sigmoid(og)).astype(y_ref.dtype)
    lane = lax.broadcasted_iota(jnp.int32, m_all.shape, 1)
    out = m_all
    for h in range(ML_H):
        out = jnp.where(lane == h, m_new[h], out)
    m_ref[0] = out


def _mlstm_chunks(proj, b_if, mnorm, c0, n0, m0, batch, t_pad, t_valid):
    cn = ML_CHUNK
    nc = t_pad // cn
    f32 = jnp.float32
    blk = lambda w, c: pl.BlockSpec((cn, w), lambda b, i, c=c: (b * nc + i, c))
    const = lambda shape: pl.BlockSpec(shape, lambda b, i: (0,) * len(shape))
    st_c = pl.BlockSpec((1, ML_H, ML_DK, ML_DV), lambda b, i: (b, 0, 0, 0))
    st_n = pl.BlockSpec((1, ML_H, ML_DK), lambda b, i: (b, 0, 0))
    st_m = pl.BlockSpec((1, 1, 128), lambda b, i: (b, 0, 0))
    bif = jnp.zeros((1, 128), f32).at[0, :2 * ML_H].set(b_if.astype(f32))
    m0p = jnp.zeros((batch, 1, 128), f32).at[:, 0, :ML_H].set(m0.astype(f32))
    hq, hv = ML_H * ML_DK, ML_H * ML_DV
    y, c, n, m = pl.pallas_call(
        functools.partial(_mlstm_chunk_kernel, t_valid=t_valid),
        grid=(batch, nc),
        in_specs=[blk(hq, OD_Q // hq), blk(hq, OD_K // hq), blk(hv, OD_V // hv), blk(hv, OD_O // hv),
                  blk(128, OD_SMALL // 128), const((1, 128)), const((1, hv)), st_c, st_n, st_m],
        out_specs=[blk(hv, 0), st_c, st_n, st_m],
        out_shape=[jax.ShapeDtypeStruct((batch * t_pad, hv), MXU_DTYPE),
                   jax.ShapeDtypeStruct((batch, ML_H, ML_DK, ML_DV), f32),
                   jax.ShapeDtypeStruct((batch, ML_H, ML_DK), f32),
                   jax.ShapeDtypeStruct((batch, 1, 128), f32)],
        compiler_params=pltpu.CompilerParams(dimension_semantics=("parallel", "arbitrary"),
                                             vmem_limit_bytes=VMEM_LIMIT_BYTES),
    )(proj, proj, proj, proj, proj, bif, mnorm.reshape(1, hv).astype(f32), c0.astype(f32), n0.astype(f32), m0p)
    return y, c, n, m[:, 0, :ML_H]


def _permute_odd_w(w_in):
    pad = jnp.zeros((w_in.shape[0], OD_N - w_in.shape[1]), w_in.dtype)
    return jnp.concatenate([w_in, pad], axis=1).astype(MXU_DTYPE)


DEC_PAGES_PER_STEP = 8


def _dec_compress_kernel(pt_ref, *refs, pp):
    kc_refs, vc_refs = refs[:pp], refs[pp:2 * pp]
    wck_ref, wcv_ref, g1_ref, kcb_ref, vcb_ref = refs[2 * pp:]
    per_page = PAGE_SIZE // NSA_BLK

    def compress(page_refs, w_ref, g):
        w = w_ref[:, g * NSA_D:(g + 1) * NSA_D]
        rows = [jnp.sum(r[pl.ds(g, PAGE_SIZE, stride=NSA_G), :].reshape(per_page, NSA_BLK, NSA_D) * w[None], axis=1)
                for r in page_refs]
        return jnp.concatenate(rows, axis=0)

    for g in range(NSA_G):
        sl = slice(g * NSA_D, (g + 1) * NSA_D)
        kcb_ref[0, :, sl] = _group_rms(compress(kc_refs, wck_ref, g), g1_ref[...])
        vcb_ref[0, :, sl] = compress(vc_refs, wcv_ref, g)


def _dec_compress(page_table, pool_kc, pool_vc, layer, w_ck, w_cv, g1):
    nseq, npages = page_table.shape
    pp = DEC_PAGES_PER_STEP
    per_page = PAGE_SIZE // NSA_BLK
    nphys = pool_kc.shape[1]
    base = layer * nphys
    kc = pool_kc.reshape(-1, NSA_D)
    vc = pool_vc.reshape(-1, NSA_D)
    page = lambda j: pl.BlockSpec((PAGE_SIZE * NSA_G, NSA_D), lambda b, i, pt, j=j: (base + pt[b, i * pp + j], 0))
    const = lambda shape: pl.BlockSpec(shape, lambda b, i, pt: (0,) * len(shape))
    out = pl.BlockSpec((1, pp * per_page, NSA_KV), lambda b, i, pt: (b, i, 0))
    nb = npages * per_page
    return pl.pallas_call(
        functools.partial(_dec_compress_kernel, pp=pp),
        grid_spec=pltpu.PrefetchScalarGridSpec(
            num_scalar_prefetch=1, grid=(nseq, npages // pp),
            in_specs=[page(j) for j in range(pp)] * 2 + [const((NSA_BLK, NSA_KV))] * 2 + [const((1, NSA_D))],
            out_specs=[out, out]),
        out_shape=[jax.ShapeDtypeStruct((nseq, nb, NSA_KV), jnp.float32)] * 2,
        compiler_params=pltpu.CompilerParams(dimension_semantics=("parallel", "arbitrary"),
                                             vmem_limit_bytes=VMEM_LIMIT_BYTES),
    )(page_table, *([kc] * pp), *([vc] * pp), w_ck.reshape(NSA_BLK, NSA_KV), w_cv.reshape(NSA_BLK, NSA_KV),
      g1.reshape(1, NSA_D))


def _rows_by_head(head_row, fn, nheads):
    out = fn(0)
    out = jnp.broadcast_to(out, (nheads, out.shape[1]))
    for h in range(1, nheads):
        out = jnp.where(head_row == h, fn(h), out)
    return out


def _dec_select_kernel(tab_ref, q_ref, kcb_ref, vcb_ref, oc_ref, idx_ref, *, qpos):
    f32 = jnp.float32
    nbp = kcb_ref.shape[1]
    q = q_ref[0]
    head_row = lax.broadcasted_iota(jnp.int32, (NSA_H, 1), 0)
    blk = lax.broadcasted_iota(jnp.int32, (1, nbp), 1)
    blkf = blk.astype(f32)
    dist = qpos - (blk * NSA_BLK + NSA_BLK - 1)
    ok = dist >= 0
    cur = qpos // NSA_BLK
    lg = jnp.zeros((NSA_H, nbp), f32)
    for g in range(NSA_G):
        lg_g = _mdot_nt(q, kcb_ref[0, :, g * NSA_D:(g + 1) * NSA_D])
        lg = jnp.where(head_row // NSA_HG == g, lg_g, lg)
    lg = lg + _rows_by_head(head_row, lambda h: _bias_of_dist(dist, tab_ref, h), NSA_H)
    lg = jnp.where(ok, lg, NEG)
    e = jnp.exp(lg - jnp.max(lg, axis=-1, keepdims=True))
    p = jnp.where(ok, e / jnp.sum(e, axis=-1, keepdims=True), 0.0)
    oc = jnp.zeros((NSA_H, NSA_D), f32)
    lane = lax.broadcasted_iota(jnp.int32, (1, 128), 1)
    for g in range(NSA_G):
        in_g = head_row // NSA_HG == g
        oc = jnp.where(in_g, _mdot(p, vcb_ref[0, :, g * NSA_D:(g + 1) * NSA_D]), oc)
        score = jnp.sum(jnp.where(in_g, p, 0.0), axis=0, keepdims=True)
        score = jnp.where(blk < cur, score, -1.0)
        idx = jnp.full((1, 128), -1.0, f32)
        for r in range(NSA_TOPK):
            mx = jnp.max(score, axis=-1, keepdims=True)
            first = jnp.min(jnp.where(score == mx, blkf, float(nbp)), axis=-1, keepdims=True)
            idx = jnp.where(lane == r, jnp.where(mx >= 0.0, first, -1.0), idx)
            score = jnp.where(blkf == first, -2.0, score)
        idx_ref[0, g:g + 1, :] = idx.astype(jnp.int32)
    oc_ref[0] = oc


def _dec_select(tab, qh, kcb, vcb, qpos):
    nseq = qh.shape[0]
    nbp = kcb.shape[1]
    seq = lambda shape: pl.BlockSpec((1,) + shape, lambda b: (b, 0, 0))
    return pl.pallas_call(
        functools.partial(_dec_select_kernel, qpos=qpos),
        grid=(nseq,),
        in_specs=[pl.BlockSpec(memory_space=pltpu.SMEM), seq((NSA_H, NSA_D)), seq((nbp, NSA_KV)),
                  seq((nbp, NSA_KV))],
        out_specs=[seq((NSA_H, NSA_D)), seq((NSA_G, 128))],
        out_shape=[jax.ShapeDtypeStruct((nseq, NSA_H, NSA_D), jnp.float32),
                   jax.ShapeDtypeStruct((nseq, NSA_G, 128), jnp.int32)],
        compiler_params=pltpu.CompilerParams(dimension_semantics=("parallel",),
                                             vmem_limit_bytes=VMEM_LIMIT_BYTES),
    )(tab.astype(jnp.float32), qh, kcb, vcb)


def _dec_attend_kernel(pt_ref, idx_ref, tab_ref, q_ref, *refs, qpos):
    f32 = jnp.float32
    ks_refs, vs_refs = refs[:NSA_TOPK], refs[NSA_TOPK:2 * NSA_TOPK]
    ksn_ref, vsn_ref, kwb_ref, vwb_ref, kwn_ref, vwn_ref, oc_ref, ng_ref, y_ref = refs[2 * NSA_TOPK:]
    b = pl.program_id(0)
    g = pl.program_id(1)
    cur = qpos // NSA_BLK
    head_row = lax.broadcasted_iota(jnp.int32, (NSA_HG, 1), 0)
    q4 = q_ref[0, pl.ds(g * NSA_HG, NSA_HG), :]
    q4r = _round_mxu(q4.astype(f32))
    new_row = b * NSA_G + g
    tab0 = _rows_by_head(head_row, lambda h: jnp.full((1, 1), tab_ref[0, g * NSA_HG + h], f32), NSA_HG)

    def attend(keys, vals, dist, ok, k_new, v_new):
        lg = _mdot_nt(q4, keys)
        lg = lg + _rows_by_head(head_row, lambda h: _bias_of_dist(dist, tab_ref, g * NSA_HG + h), NSA_HG)
        lg = jnp.where(ok, lg, NEG)
        lg_new = jnp.sum(q4r * _round_mxu(k_new), axis=-1, keepdims=True) + tab0
        m = jnp.maximum(jnp.max(lg, axis=-1, keepdims=True), lg_new)
        e = jnp.exp(lg - m)
        e_new = jnp.exp(lg_new - m)
        den = jnp.sum(e, axis=-1, keepdims=True) + e_new
        return _mdot(e / den, vals) + _round_mxu(e_new / den) * _round_mxu(v_new)

    pieces, oks = [], []
    jrow = lax.broadcasted_iota(jnp.int32, (1, NSA_BLK), 1)
    for s in range(NSA_TOPK):
        blk_id = idx_ref[b, g, s]
        valid = jnp.logical_and(blk_id >= 0, blk_id < cur)
        pos = jnp.maximum(blk_id, 0) * NSA_BLK + jrow
        pieces.append(qpos - pos)
        oks.append(jnp.where(valid, 1.0, 0.0) * jnp.where(pos <= qpos, 1.0, 0.0))
    dist_s = jnp.concatenate(pieces, axis=1)
    ok_s = jnp.concatenate(oks, axis=1) > 0.5
    mine = lambda r: r[pl.ds(g, NSA_BLK, stride=NSA_G), :]
    k_sel = jnp.concatenate([mine(r) for r in ks_refs], axis=0)
    v_sel = jnp.concatenate([mine(r) for r in vs_refs], axis=0)
    o_s = attend(k_sel, v_sel, dist_s, ok_s, ksn_ref[pl.ds(new_row, 1), :], vsn_ref[pl.ds(new_row, 1), :])

    wb = kwb_ref.shape[1]
    dist_w = wb - lax.broadcasted_iota(jnp.int32, (1, wb), 1)
    ok_w = dist_w < NSA_WIN
    o_w = attend(kwb_ref[0], vwb_ref[0], dist_w, ok_w, kwn_ref[pl.ds(new_row, 1), :], vwn_ref[pl.ds(new_row, 1), :])

    gates = jax.nn.sigmoid(ng_ref[pl.ds(b, 1), :])
    lane = lax.broadcasted_iota(jnp.int32, (1, 128), 1)

    def gate(branch):
        def one(h):
            c = 2 * GDN_VH + (g * NSA_HG + h) * 3 + branch
            return jnp.sum(jnp.where(lane == c, gates, 0.0), axis=-1, keepdims=True)
        return _rows_by_head(head_row, one, NSA_HG)

    oc4 = oc_ref[0, pl.ds(g * NSA_HG, NSA_HG), :]
    y_ref[0, pl.ds(g * NSA_HG, NSA_HG), :] = oc4 * gate(0) + o_s * gate(1) + o_w * gate(2)


def _dec_attend(tab, page_table, idx, qh, pool_ks, pool_vs, layer, ksn, vsn, buf_kw, buf_vw, kwn, vwn, oc, ng,
                qpos):
    nseq = qh.shape[0]
    base = layer * pool_ks.shape[1]
    wb = buf_kw.shape[1]
    ks = pool_ks.reshape(-1, NSA_D)
    vs = pool_vs.reshape(-1, NSA_D)
    per_page = PAGE_SIZE // NSA_BLK

    def sel(s):
        def index(b, g, pt, ix, s=s):
            blk_id = jnp.maximum(ix[b, g, s], 0)
            return ((base + pt[b, blk_id // per_page]) * per_page + blk_id % per_page, 0)
        return pl.BlockSpec((NSA_BLK * NSA_G, NSA_D), index)

    whole = lambda a: pl.BlockSpec(a.shape, lambda b, g, pt, ix: (0,) * a.ndim)
    heads = pl.BlockSpec((1, NSA_H, NSA_D), lambda b, g, pt, ix: (b, 0, 0))
    win = pl.BlockSpec((1, wb, NSA_D), lambda b, g, pt, ix: (b, 0, g))
    rows = lambda a: a.reshape(nseq * NSA_G, NSA_D)
    new = [rows(ksn), rows(vsn), rows(kwn), rows(vwn)]
    return pl.pallas_call(
        functools.partial(_dec_attend_kernel, qpos=qpos),
        grid_spec=pltpu.PrefetchScalarGridSpec(
            num_scalar_prefetch=2, grid=(nseq, NSA_G),
            in_specs=[pl.BlockSpec(memory_space=pltpu.SMEM), heads] + [sel(s) for s in range(NSA_TOPK)] * 2
                     + [whole(new[0]), whole(new[1]), win, win, whole(new[2]), whole(new[3]), heads, whole(ng)],
            out_specs=heads),
        out_shape=jax.ShapeDtypeStruct((nseq, NSA_H, NSA_D), jnp.float32),
        compiler_params=pltpu.CompilerParams(dimension_semantics=("parallel", "arbitrary"),
                                             vmem_limit_bytes=VMEM_LIMIT_BYTES),
    )(page_table, idx, tab.astype(jnp.float32), qh, *([ks] * NSA_TOPK), *([vs] * NSA_TOPK), new[0], new[1],
      buf_kw.reshape(nseq, wb, NSA_KV), buf_vw.reshape(nseq, wb, NSA_KV), new[2], new[3], oc, ng)


def _nsa_decode(proj, tab, qk_gain, w_ck, w_cv, caches, nseq):
    assert PAST_LEN % PAGE_SIZE == 0 and DEC_SEQ == 1
    pad = jnp.pad(proj, ((0, 512 - nseq), (0, 0)))
    qn, ksn, kwn, _, _, _, _, _, _ = _nsa_prep(pad, qk_gain, w_ck, w_cv)
    qh = qn[:nseq].reshape(nseq, NSA_H, NSA_D).astype(jnp.float32)
    kv = lambda i: proj[:, EV_KV + i * NSA_KV:EV_KV + (i + 1) * NSA_KV]
    layer = caches['layer']
    kcb, vcb = _dec_compress(caches['page_table'], caches['kc'], caches['vc'], layer, w_ck, w_cv, qk_gain[1])
    oc, idx = _dec_select(tab, qh, kcb, vcb, PAST_LEN)
    ng = proj[:, EV_SMALL:EV_SMALL + 128]
    y = _dec_attend(tab, caches['page_table'], idx, qh, caches['ks'], caches['vs'], layer, ksn[:nseq], kv(3),
                    caches['kw'], caches['vw'], kwn[:nseq], kv(5), oc, ng, PAST_LEN)
    return y.reshape(nseq, NSA_H * NSA_D), ksn[:nseq], kwn[:nseq]


def _split(x, widths):
    offs = [int(o) for o in np.cumsum(widths)[:-1]]
    return jnp.split(x, offs, axis=-1)


def _rms(x, g):
    xf = x.astype(jnp.float32)
    y = xf * lax.rsqrt(jnp.mean(xf * xf, axis=-1, keepdims=True) + RMS_EPS)
    return (y * g.astype(jnp.float32)).astype(x.dtype)


def _mlp(xn, w1, w2):
    h = jnp.square(jax.nn.relu(_pmm(xn, w1)))
    return _pmm(h, w2)


def _causal_conv(xin, w):
    c = xin.shape[-1]
    return lax.conv_general_dilated(xin, w[:, None, :].astype(xin.dtype), window_strides=(1,),
                                    padding='VALID', dimension_numbers=('NWC', 'WIO', 'NWC'),
                                    feature_group_count=c)


def _chunks(t, c, pad, pad_val=0.0):
    t = t.astype(jnp.float32)
    if pad:
        t = jnp.pad(t, [(0, 0), (0, pad)] + [(0, 0)] * (t.ndim - 2), constant_values=pad_val)
    b, tp = t.shape[:2]
    t = t.reshape(b, tp // c, c, *t.shape[2:])
    return jnp.swapaxes(jnp.moveaxis(t, 1, 0), 2, 3)


def _unchunk(o, t_len):
    n, b, h, c, d = o.shape
    return jnp.transpose(o, (1, 0, 3, 2, 4)).reshape(b, n * c, h, d)[:, :t_len]


def _gdn_chunked(q, k, v, g, beta, s0):
    b, t_len, h, dk = q.shape
    dv = v.shape[-1]
    c = min(GDN_CHUNK, t_len)
    pad = (-t_len) % c
    tri_s = jnp.tril(jnp.ones((c, c), bool), -1)
    tri_i = jnp.tril(jnp.ones((c, c), bool), 0)
    eye = jnp.eye(c, dtype=jnp.float32)

    def body(s, inp):
        qc, kc, vc, gc, bc = inp
        gcum = jnp.cumsum(gc, axis=-1)
        diff = gcum[..., :, None] - gcum[..., None, :]
        eg = jnp.exp(gcum)
        lmat = bc[..., :, None] * jnp.einsum('bhid,bhjd->bhij', kc, kc) * jnp.exp(jnp.where(tri_s, diff, -jnp.inf))
        rhs = jnp.concatenate([bc[..., None] * vc, (bc * eg)[..., None] * kc], axis=-1)
        sol = lax.linalg.triangular_solve(eye + lmat, rhs, left_side=True, lower=True)
        u = sol[..., :dv] - jnp.einsum('bhcd,bhde->bhce', sol[..., dv:], s)
        att = jnp.einsum('bhid,bhjd->bhij', qc, kc) * jnp.exp(jnp.where(tri_i, diff, -jnp.inf))
        o = eg[..., None] * jnp.einsum('bhcd,bhde->bhce', qc, s) + jnp.einsum('bhij,bhje->bhie', att, u)
        gl = gcum[..., -1]
        s = jnp.exp(gl)[..., None, None] * s + jnp.einsum('bhcd,bhce->bhde', kc * jnp.exp(gl[..., None] - gcum)[..., None], u)
        return s, o

    s, o = lax.scan(body, s0.astype(jnp.float32),
                    (_chunks(q, c, pad), _chunks(k, c, pad), _chunks(v, c, pad), _chunks(g, c, pad), _chunks(beta, c, pad)))
    return _unchunk(o, t_len).astype(v.dtype), s.astype(s0.dtype)


def _gdn(parts, conv_w, a_log, dt_bias, norm_w, conv_prev, s0):
    q, k, v, z, a, bgate = parts
    b, t_len = q.shape[:2]
    xin = jnp.concatenate([conv_prev.astype(q.dtype), jnp.concatenate([q, k, v], axis=-1)], axis=1)
    c = jax.nn.silu(_causal_conv(xin, conv_w))
    cq, ck, cv = _split(c, (GDN_QK, GDN_QK, GDN_V))

    def l2(t):
        t = t.reshape(b, t_len, GDN_KH, GDN_DK).astype(jnp.float32)
        return t * lax.rsqrt(jnp.sum(t * t, axis=-1, keepdims=True) + 1e-6)

    rep = GDN_VH // GDN_KH
    qh = jnp.repeat(l2(cq) * (GDN_DK ** -0.5), rep, axis=2)
    kh = jnp.repeat(l2(ck), rep, axis=2)
    vh = cv.reshape(b, t_len, GDN_VH, GDN_DV)
    beta = jax.nn.sigmoid(bgate.astype(jnp.float32))
    g = -jnp.exp(a_log.astype(jnp.float32)) * jax.nn.softplus(a.astype(jnp.float32) + dt_bias.astype(jnp.float32))
    o, s = _gdn_chunked(qh, kh, vh, g, beta, s0)
    o = _rms(o, norm_w) * jax.nn.silu(z.reshape(b, t_len, GDN_VH, GDN_DV))
    return o.reshape(b, t_len, GDN_V), xin[:, -(GDN_CONV - 1):], s


def _bucket(dist):
    n = jnp.maximum(dist, 0)
    nf = jnp.maximum(n, REL_EXACT).astype(jnp.float32)
    large = REL_EXACT + (jnp.log(nf / REL_EXACT) / math.log(REL_MAX_DIST / REL_EXACT)
                         * (REL_BUCKETS - REL_EXACT)).astype(jnp.int32)
    large = jnp.minimum(large, REL_BUCKETS - 1)
    return jnp.where(n < REL_EXACT, n, large)


def _tok_bias(dist, tab):
    t_len, n = dist.shape
    bias = tab.astype(jnp.float32)[_bucket(dist)]
    return bias.reshape(t_len, n, NSA_G, NSA_HG).transpose(0, 2, 3, 1)


def _compress(rows, w):
    b, t_len = rows.shape[:2]
    r = rows.reshape(b, t_len // NSA_BLK, NSA_BLK, NSA_G, NSA_D)
    return jnp.einsum('bnlgd,lgd->bngd', r, w)


def _nsa_core(q, qpos, kc_b, vc_b, fetch, kw, vw, kwpos, tab):
    b, t_len = q.shape[:2]
    f32 = jnp.float32
    nb = kc_b.shape[1]
    blk = jnp.arange(nb, dtype=jnp.int32)
    dist_c = qpos[:, None] - (blk * NSA_BLK + NSA_BLK - 1)[None, :]
    ok_c = (dist_c >= 0)[:, None, None, :]
    lg_c = jnp.einsum('btghd,bngd->btghn', q, kc_b).astype(f32) + _tok_bias(dist_c, tab)
    p_c = jax.nn.softmax(jnp.where(ok_c, lg_c, NEG), axis=-1) * ok_c
    o_c = jnp.einsum('btghn,bngd->btghd', p_c.astype(vc_b.dtype), vc_b)
    cur = qpos // NSA_BLK
    cand = (blk[None, :] < cur[:, None])[None, :, None, :]
    score = jnp.where(cand, p_c.sum(axis=3), -1.0)
    _, idx = lax.top_k(score, min(NSA_TOPK, nb))
    cur_b = jnp.broadcast_to(cur[None, :, None, None], (b, t_len, NSA_G, 1))
    idx_all = jnp.concatenate([idx, cur_b], axis=-1)
    ok_all = jnp.concatenate([idx < cur[None, :, None, None], jnp.ones((b, t_len, NSA_G, 1), bool)], axis=-1)
    pos5 = idx_all[..., None] * NSA_BLK + jnp.arange(NSA_BLK, dtype=jnp.int32)
    ok5 = ok_all[..., None] & (pos5 <= qpos[None, :, None, None, None])
    pos = pos5.reshape(b, t_len, NSA_G, -1)
    ok_s = ok5.reshape(b, t_len, NSA_G, 1, -1)
    k_s, v_s = fetch(pos)
    tab3 = tab.astype(f32).reshape(REL_BUCKETS, NSA_G, NSA_HG)
    bias_s = jnp.moveaxis(tab3[_bucket(qpos[None, :, None, None] - pos), jnp.arange(NSA_G)[:, None]], -1, 3)
    lg_s = jnp.einsum('btghd,btgnd->btghn', q, k_s).astype(f32) + bias_s
    p_s = jax.nn.softmax(jnp.where(ok_s, lg_s, NEG), axis=-1)
    o_s = jnp.einsum('btghn,btgnd->btghd', p_s.astype(v_s.dtype), v_s)
    dist_w = qpos[:, None] - kwpos[None, :]
    ok_w = ((dist_w >= 0) & (dist_w < NSA_WIN) & (kwpos >= 0)[None, :])[:, None, None, :]
    lg_w = jnp.einsum('btghd,bngd->btghn', q, kw).astype(f32) + _tok_bias(dist_w, tab)
    p_w = jax.nn.softmax(jnp.where(ok_w, lg_w, NEG), axis=-1)
    o_w = jnp.einsum('btghn,bngd->btghd', p_w.astype(vw.dtype), vw)
    return jnp.stack([o_c, o_s, o_w], axis=-2)


def _over_query_blocks(fn, q, qpos):
    b, t_len = q.shape[:2]
    if t_len <= NSA_QB:
        return fn(q, qpos)
    nblk = -(-t_len // NSA_QB)
    qb = jnp.swapaxes(q.reshape(b, nblk, NSA_QB, *q.shape[2:]), 0, 1)
    pb = qpos.reshape(nblk, NSA_QB)
    out = lax.map(lambda a: fn(a[0], a[1]), (qb, pb))
    out = jnp.swapaxes(out, 0, 1)
    return out.reshape(b, nblk * NSA_QB, *out.shape[3:])[:, :t_len]


def _nsa_combine(o3, ng):
    b, t_len = ng.shape[:2]
    gates = jax.nn.sigmoid(ng.astype(jnp.float32)).reshape(b, t_len, NSA_G, NSA_HG, 3, 1)
    return (o3.astype(jnp.float32) * gates).sum(axis=-2).reshape(b, t_len, NSA_H * NSA_D).astype(o3.dtype)


def _nsa_prompt(q, kc, vc, ks, vs, kw, vw, w_ck, w_cv, g_kc, tab):
    b, t_len = q.shape[:2]
    kc_b = _rms(_compress(kc, w_ck), g_kc)
    vc_b = _compress(vc, w_cv)
    padw = ((0, 0), (NSA_WIN, 0), (0, 0), (0, 0))
    kw_pad = jnp.pad(kw, padw)
    vw_pad = jnp.pad(vw, padw)
    bi = jnp.arange(b)[:, None, None, None]
    gi = jnp.arange(NSA_G)[None, None, :, None]

    def fetch(pos):
        return ks[bi, pos, gi], vs[bi, pos, gi]

    def block(qb, pb):
        start = pb[0]
        n = NSA_WIN + qb.shape[1]
        kwb = lax.dynamic_slice_in_dim(kw_pad, start, n, axis=1)
        vwb = lax.dynamic_slice_in_dim(vw_pad, start, n, axis=1)
        kwpos = start - NSA_WIN + jnp.arange(n, dtype=jnp.int32)
        return _nsa_core(qb, pb, kc_b, vc_b, fetch, kwb, vwb, kwpos, tab)

    return _over_query_blocks(block, q, jnp.arange(t_len, dtype=jnp.int32))


def _nsa_sample(q, kc, vc, ks, vs, kw, vw, pool_kc, pool_vc, pool_ks, pool_vs, buf_kw, buf_vw,
                page_table, w_ck, w_cv, g_kc, tab):
    db, ds = q.shape[:2]

    def full_rows(pool, new):
        past = pool[page_table].reshape(db, PAST_LEN, NSA_G, NSA_D)
        r = jnp.concatenate([past, new], axis=1)
        pad = (-r.shape[1]) % NSA_BLK
        return jnp.pad(r, ((0, 0), (0, pad), (0, 0), (0, 0)))

    kc_b = _rms(_compress(full_rows(pool_kc, kc), w_ck), g_kc)
    vc_b = _compress(full_rows(pool_vc, vc), w_cv)
    bi = jnp.arange(db)[:, None, None, None]
    gi = jnp.arange(NSA_G)[None, None, :, None]

    def fetch(pos):
        in_past = (pos < PAST_LEN)[..., None]
        pp = jnp.minimum(pos, PAST_LEN - 1)
        phys = page_table[bi, pp // PAGE_SIZE]
        off = pp % PAGE_SIZE
        pn = jnp.clip(pos - PAST_LEN, 0, ds - 1)
        k = jnp.where(in_past, pool_ks[phys, off, gi], ks[bi, pn, gi])
        v = jnp.where(in_past, pool_vs[phys, off, gi], vs[bi, pn, gi])
        return k, v

    wb = buf_kw.shape[1]
    kw_all = jnp.concatenate([buf_kw, kw], axis=1)
    vw_all = jnp.concatenate([buf_vw, vw], axis=1)
    kwpos = PAST_LEN - wb + jnp.arange(wb + ds, dtype=jnp.int32)
    qpos = PAST_LEN + jnp.arange(ds, dtype=jnp.int32)
    o3 = _over_query_blocks(lambda qb, pb: _nsa_core(qb, pb, kc_b, vc_b, fetch, kw_all, vw_all, kwpos, tab), q, qpos)
    return o3, kw_all[:, -wb:], vw_all[:, -wb:]


def _even_inputs(xn, w_in, qk_gain):
    b, t_len, _ = xn.shape
    (gq, gk, gv, gz, ga, gb, nq, kc, vc, ks, vs, kw, vw, ng) = _split(_pmm(xn, w_in), EVEN_WIDTHS)
    hd = lambda t: t.reshape(b, t_len, NSA_G, NSA_D)
    q = _rms(nq.reshape(b, t_len, NSA_G, NSA_HG, NSA_D), qk_gain[0]) * (NSA_D ** -0.5)
    ks = _rms(hd(ks), qk_gain[2])
    kw = _rms(hd(kw), qk_gain[3])
    return (gq, gk, gv, gz, ga, gb), (q, hd(kc), hd(vc), ks, hd(vs), kw, hd(vw), ng)


def _even_prompt(xn, w_in, w_out, conv_w, a_log, dt_bias, gnorm, qk_gain, w_ck, w_cv, tab, bias_tiles):
    b, t_len, d = xn.shape
    proj = _matmul(xn.reshape(b * t_len, d).astype(MXU_DTYPE), _permute_even_w(w_in))
    p3 = proj.reshape(b, t_len, EV_N)
    gq, gk, gv = _split(p3[..., :GDN_CONV_CH], (GDN_QK, GDN_QK, GDN_V))
    gdn_parts = (gq, gk, gv, p3[..., EV_Z:EV_Z + GDN_V], p3[..., EV_SMALL:EV_SMALL + GDN_VH],
                 p3[..., EV_SMALL + GDN_VH:EV_SMALL + 2 * GDN_VH])
    conv0 = jnp.zeros((b, GDN_CONV - 1, GDN_CONV_CH), xn.dtype)
    s0 = jnp.zeros((b, GDN_VH, GDN_DK, GDN_DV), xn.dtype)
    ya, conv_new, s = _gdn(gdn_parts, conv_w, a_log, dt_bias, gnorm, conv0, s0)
    qn, ksn, kwn, ksb, vsb, kwb, vwb, kcb, vcb = _nsa_prep(proj, qk_gain, w_ck, w_cv)
    yb = _nsa_prompt_attn(tab, bias_tiles, proj, qn, kcb, vcb, ksb, vsb, kwb, vwb, b, t_len)
    y = _pmm(jnp.concatenate([ya, yb.reshape(b, t_len, -1).astype(ya.dtype)], axis=-1), w_out)
    nw = min(NSA_WIN, t_len)
    hd = lambda t: t.reshape(b, t_len, NSA_G, NSA_D)
    kv = lambda i: hd(p3[..., EV_KV + i * NSA_KV:EV_KV + (i + 1) * NSA_KV])
    return y, (kv(0), kv(1), hd(ksn), kv(3), hd(kwn)[:, -nw:], kv(5)[:, -nw:], conv_new, s)


def _even_sample(xn, w_in, w_out, conv_w, a_log, dt_bias, gnorm, qk_gain, w_ck, w_cv, tab,
                 pool_kc, pool_vc, pool_ks, pool_vs, buf_kw, buf_vw, conv_st, s_st, page_table):
    gdn_parts, (q, kc, vc, ks, vs, kw, vw, ng) = _even_inputs(xn, w_in, qk_gain)
    ya, conv_new, s = _gdn(gdn_parts, conv_w, a_log, dt_bias, gnorm, conv_st, s_st)
    o3, kw_buf, vw_buf = _nsa_sample(q, kc, vc, ks, vs, kw, vw, pool_kc, pool_vc, pool_ks, pool_vs,
                                     buf_kw, buf_vw, page_table, w_ck, w_cv, qk_gain[1], tab)
    y = _pmm(jnp.concatenate([ya, _nsa_combine(o3, ng)], axis=-1), w_out)
    return y, (kc, vc, ks, vs, kw_buf, vw_buf, conv_new, s)


def _mlstm_chunked(q, k, v, ig, lf, c0, n0, m0):
    b, t_len, h, _ = q.shape
    c = min(ML_CHUNK, t_len)
    pad = (-t_len) % c
    incl = jnp.tril(jnp.ones((c, c), bool), 0)

    def body(carry, inp):
        cm, nv, m = carry
        qc, kc, vc, ic, fc = inp
        fcum = jnp.cumsum(fc, axis=-1)
        a = ic - fcum
        mt = fcum + jnp.maximum(m[..., None], lax.cummax(a, axis=2))
        dmat = jnp.exp(jnp.where(incl, a[..., None, :] + (fcum - mt)[..., :, None], -jnp.inf))
        dec0 = jnp.exp(fcum + m[..., None] - mt)
        s = jnp.einsum('bhid,bhjd->bhij', qc, kc) * dmat
        num = dec0[..., None] * jnp.einsum('bhid,bhde->bhie', qc, cm) + jnp.einsum('bhij,bhje->bhie', s, vc)
        den = dec0 * jnp.einsum('bhid,bhd->bhi', qc, nv) + s.sum(axis=-1)
        hc = num / jnp.maximum(jnp.abs(den), jnp.exp(-mt))[..., None]
        m_end = mt[..., -1]
        w = jnp.exp(a + (fcum[..., -1] - m_end)[..., None])
        dc = jnp.exp(fcum[..., -1] + m - m_end)
        cm = dc[..., None, None] * cm + jnp.einsum('bhj,bhjd,bhje->bhde', w, kc, vc)
        nv = dc[..., None] * nv + jnp.einsum('bhj,bhjd->bhd', w, kc)
        return (cm, nv, m_end), hc

    f32 = jnp.float32
    (cm, nv, m), hs = lax.scan(body, (c0.astype(f32), n0.astype(f32), m0.astype(f32)),
                               (_chunks(q, c, pad), _chunks(k, c, pad), _chunks(v, c, pad),
                                _chunks(ig, c, pad, NEG), _chunks(lf, c, pad)))
    return _unchunk(hs, t_len).astype(v.dtype), cm.astype(c0.dtype), nv.astype(n0.dtype), m.astype(m0.dtype)


def _odd(xn, w_in, w_out, b_if, mnorm, c0, n0, m0):
    b, t_len, _ = xn.shape
    q, k, v, o, ig, fg = _split(_pmm(xn, w_in), ODD_WIDTHS)
    q = q.reshape(b, t_len, ML_H, ML_DK) * (ML_DK ** -0.5)
    k = k.reshape(b, t_len, ML_H, ML_DK)
    v = v.reshape(b, t_len, ML_H, ML_DV)
    bf = b_if.astype(jnp.float32)
    ig = ig.astype(jnp.float32) + bf[:ML_H]
    lf = jax.nn.log_sigmoid(fg.astype(jnp.float32) + bf[ML_H:])
    hc, cm, nv, m = _mlstm_chunked(q, k, v, ig, lf, c0, n0, m0)
    hc = _rms(hc, mnorm.reshape(ML_H, ML_DV)).reshape(b, t_len, ODD_MIX)
    return _pmm(hc * jax.nn.sigmoid(o), w_out), (cm, nv, m)


def _rms_cast_kernel(x_ref, g_ref, o_ref):
    x = x_ref[...]
    y = x * lax.rsqrt(jnp.mean(x * x, axis=-1, keepdims=True) + RMS_EPS) * g_ref[...]
    o_ref[...] = y.astype(o_ref.dtype)


def _rms_cast(h, g):
    m, d = h.shape
    tm = _pick(m, (256, 128))
    return pl.pallas_call(
        _rms_cast_kernel,
        grid=(m // tm,),
        in_specs=[pl.BlockSpec((tm, d), lambda i: (i, 0)), pl.BlockSpec((1, d), lambda i: (0, 0))],
        out_specs=pl.BlockSpec((tm, d), lambda i: (i, 0)),
        out_shape=jax.ShapeDtypeStruct((m, d), MXU_DTYPE),
        compiler_params=pltpu.CompilerParams(dimension_semantics=("parallel",),
                                             vmem_limit_bytes=VMEM_LIMIT_BYTES),
    )(h, g.reshape(1, d).astype(jnp.float32))


def _mm2_kernel(xa_ref, xb_ref, w_ref, r_ref, o_ref):
    ka = xa_ref.shape[1]
    y = jnp.dot(xa_ref[...], w_ref[:ka, :].astype(MXU_DTYPE), preferred_element_type=jnp.float32)
    y = y + jnp.dot(xb_ref[...], w_ref[ka:, :].astype(MXU_DTYPE), preferred_element_type=jnp.float32)
    o_ref[...] = y + r_ref[...]


def _matmul2_res(xa, xb, w, res, tiles=None):
    m, ka = xa.shape
    kb = xb.shape[1]
    n = w.shape[1]
    tm = _pick(m, (tiles[0],) if tiles else (512, 256, 128))
    tn = _pick(n, (tiles[1],) if tiles else (512, 256, 128))
    return pl.pallas_call(
        _mm2_kernel,
        grid=(m // tm, n // tn),
        in_specs=[pl.BlockSpec((tm, ka), lambda i, j: (i, 0)), pl.BlockSpec((tm, kb), lambda i, j: (i, 0)),
                  pl.BlockSpec((ka + kb, tn), lambda i, j: (0, j)), pl.BlockSpec((tm, tn), lambda i, j: (i, j))],
        out_specs=pl.BlockSpec((tm, tn), lambda i, j: (i, j)),
        out_shape=jax.ShapeDtypeStruct((m, n), jnp.float32),
        compiler_params=pltpu.CompilerParams(dimension_semantics=("parallel", "parallel"),
                                             vmem_limit_bytes=VMEM_LIMIT_BYTES),
    )(xa, xb, w, res)


LAYER_TILES = (
    dict(inp=(1024, 384), out=(1024, 512), mlp=(1024, 512)),
    dict(inp=(1024, 896), out=(2048, 512), mlp=(2048, 512)),
    dict(inp=(2048, 640), out=(1024, 1024), mlp=(1024, 1024)),
    dict(inp=(2048, 896), out=(512, 1024), mlp=(2048, 256)),
)


def _mlp_block(h, g, w1, w2, tiles):
    mid = _matmul(_rms_cast(h, g), w1, act='relu2', out_dtype=MXU_DTYPE, tiles=tiles)
    return _matmul(mid, w2, res=h, tiles=tiles)


def _pad_time(x, t_pad):
    b, d = x.shape
    return jnp.zeros((b, t_pad, d), x.dtype).at[:, 0, :].set(x).reshape(b * t_pad, d)


def _even_layer(h, g_mix, w_in, w_out, conv_w, a_log, dt_bias, gnorm, qk_gain, w_ck, w_cv, tab, bias_tiles,
                batch, t_len, conv_st, s_st, decode, tiles):
    f32 = jnp.float32
    proj = _matmul(_rms_cast(h, g_mix), w_in, tiles=tiles['inp'])
    t_pad = t_len if decode is None else GDN_CHUNK
    pp = proj if decode is None else _pad_time(proj, t_pad)
    first = jnp.pad(conv_st.astype(f32), ((0, 0), (8 - (GDN_CONV - 1), 0), (0, 0)))
    q, k, v, gb = _gdn_prep(pp, first, pp, EV_SMALL // 128, conv_w, a_log, dt_bias, batch, t_pad, t_len)
    ya, s_new = _gdn_chunks(q, k, v, gb, pp, EV_Z // GDN_V, s_st, gnorm, batch, t_pad)
    p3 = proj.reshape(batch, t_len, EV_N)
    xin_tail = jnp.concatenate([conv_st.astype(f32), p3[..., :GDN_CONV_CH]], axis=1)[:, -(GDN_CONV - 1):]
    hd = lambda t: t.reshape(batch, t_len, NSA_G, NSA_D)
    kv = lambda i: hd(p3[..., EV_KV + i * NSA_KV:EV_KV + (i + 1) * NSA_KV])
    if decode is None:
        qn, ksn, kwn, ksb, vsb, kwb, vwb, kcb, vcb = _nsa_prep(proj, qk_gain, w_ck, w_cv)
        yb = _nsa_prompt_attn(tab, bias_tiles, proj, qn, kcb, vcb, ksb, vsb, kwb, vwb, batch, t_len)
        nw = min(NSA_WIN, t_len)
        kw_out, vw_out = hd(kwn)[:, -nw:], kv(5)[:, -nw:]
        ks_out = hd(ksn)
    else:
        ya = ya.reshape(batch, t_pad, GDN_V)[:, 0]
        yb, ksn, kwn = _nsa_decode(proj, tab, qk_gain, w_ck, w_cv, decode, batch)
        yb = yb.astype(MXU_DTYPE)
        ks_out = hd(ksn)
        kw_out = jnp.concatenate([decode['kw'][:, 1:], hd(kwn)], axis=1)
        vw_out = jnp.concatenate([decode['vw'][:, 1:], kv(5)], axis=1)
    h = _matmul2_res(ya, yb, w_out, h, tiles=tiles['out'])
    return h, (kv(0), kv(1), ks_out, kv(3), kw_out, vw_out, xin_tail, s_new)


def _odd_layer(h, g_mix, w_in, w_out, b_if, mnorm, batch, t_len, c0, n0, m0, decode, tiles):
    proj = _matmul(_rms_cast(h, g_mix), w_in, tiles=tiles['inp'])
    t_pad = t_len if not decode else ML_CHUNK
    pp = proj if not decode else _pad_time(proj, t_pad)
    y, c, n, m = _mlstm_chunks(pp, b_if, mnorm, c0, n0, m0, batch, t_pad, t_len)
    if decode:
        y = y.reshape(batch, t_pad, ODD_MIX)[:, 0]
    return _matmul(y, w_out, res=h, tiles=tiles['out']), (c, n, m)


def kernel(x_prompt, x_sample, cache_kc, cache_vc, cache_ks, cache_vs, cache_kw, cache_vw,
           state_gdn_conv, state_gdn_s, state_ml_c, state_ml_n, state_ml_m, page_table,
           rel_bias, norm_mix, norm_mlp, w_in_even, w_out_even, gdn_conv_w, gdn_a_log,
           gdn_dt_bias, gdn_norm_w, nsa_qk_gain, nsa_w_ck, nsa_w_cv, w_in_odd, w_out_odd,
           ml_b_if, ml_norm_w, w_ff1, w_ff2):
    f32 = jnp.float32
    bp, tp, d = x_prompt.shape
    bs, ts, _ = x_sample.shape
    hp = x_prompt.reshape(bp * tp, d)
    hs = x_sample.reshape(bs * ts, d)
    bias_tiles = _bias_tiles(rel_bias)
    ev_p, ev_s, od_p, od_s = [], [], [], []
    for l in range(DEPTH):
        j = l // 2
        tiles = LAYER_TILES[l]
        if l % 2 == 0:
            w_in = _permute_even_w(w_in_even[j])
            args = (w_in, w_out_even[j], gdn_conv_w[j], gdn_a_log[j], gdn_dt_bias[j], gdn_norm_w[j],
                    nsa_qk_gain[j], nsa_w_ck[j], nsa_w_cv[j], rel_bias, bias_tiles)
            hp, stp = _even_layer(hp, norm_mix[l], *args, bp, tp,
                                  jnp.zeros((bp, GDN_CONV - 1, GDN_CONV_CH), f32),
                                  jnp.zeros((bp, GDN_VH, GDN_DK, GDN_DV), f32), None, tiles)
            caches = dict(kc=cache_kc, vc=cache_vc, ks=cache_ks, vs=cache_vs, layer=j, kw=cache_kw[j],
                          vw=cache_vw[j], page_table=page_table)
            hs, sts = _even_layer(hs, norm_mix[l], *args, bs, ts, state_gdn_conv[j], state_gdn_s[j], caches,
                                  tiles)
            ev_p.append(stp)
            ev_s.append(sts)
        else:
            w_in = _permute_odd_w(w_in_odd[j])
            hp, stp = _odd_layer(hp, norm_mix[l], w_in, w_out_odd[j], ml_b_if[j], ml_norm_w[j], bp, tp,
                                 jnp.zeros((bp, ML_H, ML_DK, ML_DV), f32), jnp.zeros((bp, ML_H, ML_DK), f32),
                                 jnp.zeros((bp, ML_H), f32), False, tiles)
            hs, sts = _odd_layer(hs, norm_mix[l], w_in, w_out_odd[j], ml_b_if[j], ml_norm_w[j], bs, ts,
                                 state_ml_c[j], state_ml_n[j], state_ml_m[j], True, tiles)
            od_p.append(stp)
            od_s.append(sts)
        hp = _mlp_block(hp, norm_mlp[l], w_ff1[l], w_ff2[l], tiles['mlp'])
        hs = _mlp_block(hs, norm_mlp[l], w_ff1[l], w_ff2[l], tiles['mlp'])

    kc_p, vc_p, ks_p, vs_p, kw_p, vw_p, conv_p, gdn_p = [jnp.stack(a) for a in zip(*ev_p)]
    kc_s, vc_s, ks_s, vs_s, kw_s, vw_s, conv_s, gdn_s = [jnp.stack(a) for a in zip(*ev_s)]
    mlc_p, mln_p, mlm_p = [jnp.stack(a) for a in zip(*od_p)]
    mlc_s, mln_s, mlm_s = [jnp.stack(a) for a in zip(*od_s)]
    return (hp.reshape(bp, tp, d), hs.reshape(bs, ts, d), kc_p, kc_s, vc_p, vc_s, ks_p, ks_s, vs_p, vs_s,
            kw_p, kw_s, vw_p, vw_s, conv_p, conv_s, gdn_p, gdn_s, mlc_p, mlc_s, mln_p, mln_s, mlm_p, mlm_s)


def _kernel_old(x_prompt, x_sample, cache_kc, cache_vc, cache_ks, cache_vs, cache_kw, cache_vw,
           state_gdn_conv, state_gdn_s, state_ml_c, state_ml_n, state_ml_m, page_table,
           rel_bias, norm_mix, norm_mlp, w_in_even, w_out_even, gdn_conv_w, gdn_a_log,
           gdn_dt_bias, gdn_norm_w, nsa_qk_gain, nsa_w_ck, nsa_w_cv, w_in_odd, w_out_odd,
           ml_b_if, ml_norm_w, w_ff1, w_ff2):
    hp, hs = x_prompt, x_sample
    bias_tiles = _bias_tiles(rel_bias)
    ev_p, ev_s, od_p, od_s = [], [], [], []
    for l in range(DEPTH):
        j = l // 2
        xp = _rms(hp, norm_mix[l])
        xs = _rms(hs, norm_mix[l])
        if l % 2 == 0:
            yp, stp = _even_prompt(xp, w_in_even[j], w_out_even[j], gdn_conv_w[j], gdn_a_log[j],
                                   gdn_dt_bias[j], gdn_norm_w[j], nsa_qk_gain[j], nsa_w_ck[j],
                                   nsa_w_cv[j], rel_bias, bias_tiles)
            ys, sts = _even_sample(xs, w_in_even[j], w_out_even[j], gdn_conv_w[j], gdn_a_log[j],
                                   gdn_dt_bias[j], gdn_norm_w[j], nsa_qk_gain[j], nsa_w_ck[j],
                                   nsa_w_cv[j], rel_bias, cache_kc[j], cache_vc[j], cache_ks[j],
                                   cache_vs[j], cache_kw[j], cache_vw[j], state_gdn_conv[j],
                                   state_gdn_s[j], page_table)
            ev_p.append(stp)
            ev_s.append(sts)
        else:
            bp = hp.shape[0]
            c0 = jnp.zeros((bp, ML_H, ML_DK, ML_DV), hp.dtype)
            n0 = jnp.zeros((bp, ML_H, ML_DK), hp.dtype)
            m0 = jnp.zeros((bp, ML_H), hp.dtype)
            yp, stp = _odd(xp, w_in_odd[j], w_out_odd[j], ml_b_if[j], ml_norm_w[j], c0, n0, m0)
            ys, sts = _odd(xs, w_in_odd[j], w_out_odd[j], ml_b_if[j], ml_norm_w[j],
                           state_ml_c[j], state_ml_n[j], state_ml_m[j])
            od_p.append(stp)
            od_s.append(sts)
        hp = hp + yp
        hs = hs + ys
        hp = hp + _mlp(_rms(hp, norm_mlp[l]), w_ff1[l], w_ff2[l])
        hs = hs + _mlp(_rms(hs, norm_mlp[l]), w_ff1[l], w_ff2[l])

    kc_p, vc_p, ks_p, vs_p, kw_p, vw_p, conv_p, gdn_p = [jnp.stack(a) for a in zip(*ev_p)]
    kc_s, vc_s, ks_s, vs_s, kw_s, vw_s, conv_s, gdn_s = [jnp.stack(a) for a in zip(*ev_s)]
    mlc_p, mln_p, mlm_p = [jnp.stack(a) for a in zip(*od_p)]
    mlc_s, mln_s, mlm_s = [jnp.stack(a) for a in zip(*od_s)]
    return (hp, hs, kc_p, kc_s, vc_p, vc_s, ks_p, ks_s, vs_p, vs_s, kw_p, kw_s, vw_p, vw_s,
            conv_p, conv_s, gdn_p, gdn_s, mlc_p, mlc_s, mln_p, mln_s, mlm_p, mlm_s)
```

```python
import functools
import math

import jax
import jax.numpy as jnp
import numpy as np
from jax import lax
from jax.experimental import pallas as pl
from jax.experimental.pallas import tpu as pltpu

D_MODEL = 2048
BATCH = 2
SEQ = 4096
DEPTH = 4
DEC_BATCH = 8
DEC_SEQ = 1
PAST_LEN = 16384
PAGE_SIZE = 128
RMS_EPS = 1e-6
NEG = -1e30
D_FF = 4 * D_MODEL

GDN_KH = 4
GDN_VH = 8
GDN_DK = 128
GDN_DV = 128
GDN_CONV = 4
GDN_CHUNK = 64
GDN_QK = GDN_KH * GDN_DK
GDN_V = GDN_VH * GDN_DV
GDN_CONV_CH = 2 * GDN_QK + GDN_V

NSA_H = 8
NSA_G = 2
NSA_HG = NSA_H // NSA_G
NSA_D = 128
NSA_BLK = 64
NSA_TOPK = 15
NSA_WIN = 512
NSA_QB = 128
NSA_KV = NSA_G * NSA_D

REL_BUCKETS = 32
REL_EXACT = 16
REL_MAX_DIST = 2048

ML_H = 8
ML_DK = 128
ML_DV = 256
ML_CHUNK = 64

EVEN_WIDTHS = (GDN_QK, GDN_QK, GDN_V, GDN_V, GDN_VH, GDN_VH, NSA_H * NSA_D,
               NSA_KV, NSA_KV, NSA_KV, NSA_KV, NSA_KV, NSA_KV, NSA_H * 3)
ODD_WIDTHS = (ML_H * ML_DK, ML_H * ML_DK, ML_H * ML_DV, ML_H * ML_DV, ML_H, ML_H)
ODD_MIX = ML_H * ML_DV

VMEM_LIMIT_BYTES = 56 * 1024 * 1024
MXU_DTYPE = jnp.bfloat16


def _mm_kernel(x_ref, w_ref, *rest, nk, act, has_res):
    rest = list(rest)
    r_ref = rest.pop(0) if has_res else None
    o_ref = rest.pop(0)

    def finish(y):
        if act == 'relu2':
            y = jnp.square(jnp.maximum(y, 0.0))
        if has_res:
            y = y + r_ref[...]
        o_ref[...] = y.astype(o_ref.dtype)

    d = jnp.dot(x_ref[...], w_ref[...].astype(MXU_DTYPE), preferred_element_type=jnp.float32)
    if nk == 1:
        finish(d)
        return
    acc_ref = rest.pop(0)
    k = pl.program_id(2)

    @pl.when(k == 0)
    def _():
        acc_ref[...] = d

    @pl.when(k > 0)
    def _():
        acc_ref[...] += d

    @pl.when(k == nk - 1)
    def _():
        finish(acc_ref[...])


def _pick(n, prefs):
    for p in prefs:
        if n % p == 0:
            return p
    return n


def _matmul(x, w, *, act=None, res=None, out_dtype=jnp.float32, tiles=None, layer=None):
    m, kdim = x.shape
    n = w.shape[-1]
    tm = _pick(m, (tiles[0],) if tiles else (512, 256, 128))
    tn = _pick(n, (tiles[1],) if tiles else (512, 896, 384, 256, 128))
    tk = _pick(kdim, (2048, 1024, 512))
    nk = kdim // tk
    if layer is None:
        w_spec = pl.BlockSpec((tk, tn), lambda i, j, k: (k, j))
    else:
        w_spec = pl.BlockSpec((None, tk, tn), lambda i, j, k: (layer, k, j))
    in_specs = [pl.BlockSpec((tm, tk), lambda i, j, k: (i, k)), w_spec]
    args = [x, w]
    if res is not None:
        in_specs.append(pl.BlockSpec((tm, tn), lambda i, j, k: (i, j)))
        args.append(res)
    return pl.pallas_call(
        functools.partial(_mm_kernel, nk=nk, act=act, has_res=res is not None),
        grid=(m // tm, n // tn, nk),
        in_specs=in_specs,
        out_specs=pl.BlockSpec((tm, tn), lambda i, j, k: (i, j)),
        out_shape=jax.ShapeDtypeStruct((m, n), out_dtype),
        scratch_shapes=[pltpu.VMEM((tm, tn), jnp.float32)] if nk > 1 else [],
        compiler_params=pltpu.CompilerParams(
            dimension_semantics=("parallel", "parallel", "arbitrary"),
            vmem_limit_bytes=VMEM_LIMIT_BYTES),
    )(*args)


def _pmm(x, w):
    lead = x.shape[:-1]
    x2 = x.reshape(-1, x.shape[-1]).astype(MXU_DTYPE)
    n = w.shape[1]
    npad = (-n) % 128
    wb = w.astype(MXU_DTYPE)
    if npad:
        wb = jnp.pad(wb, ((0, 0), (0, npad)))
    y = _matmul(x2, wb)
    if npad:
        y = y[:, :n]
    return y.reshape(*lead, n)


EV_CONV = 0
EV_Z = GDN_CONV_CH
EV_NQ = EV_Z + GDN_V
EV_KV = EV_NQ + NSA_H * NSA_D
EV_SMALL = EV_KV + 6 * NSA_KV
EV_N = EV_SMALL + 128
EV_ORIG_SMALL = 2 * GDN_QK + 2 * GDN_V
EV_ORIG_N = EV_ORIG_SMALL + 2 * GDN_VH + NSA_H * NSA_D + 6 * NSA_KV + 3 * NSA_H


def _permute_even_w(w_in):
    ab = w_in[:, EV_ORIG_SMALL:EV_ORIG_SMALL + 2 * GDN_VH]
    big = w_in[:, EV_ORIG_SMALL + 2 * GDN_VH:EV_ORIG_N - 3 * NSA_H]
    ng = w_in[:, EV_ORIG_N - 3 * NSA_H:]
    pad = jnp.zeros((w_in.shape[0], EV_N - EV_ORIG_N), w_in.dtype)
    return jnp.concatenate([w_in[:, :EV_ORIG_SMALL], big, ab, ng, pad], axis=1).astype(MXU_DTYPE)


def _bucket_thresholds():
    n = np.arange(0, 4 * REL_MAX_DIST)
    nf = np.maximum(n, REL_EXACT).astype(np.float32)
    large = REL_EXACT + (np.log(nf / np.float32(REL_EXACT)) / np.float32(math.log(REL_MAX_DIST / REL_EXACT))
                         * np.float32(REL_BUCKETS - REL_EXACT)).astype(np.int32)
    b = np.where(n < REL_EXACT, n, np.minimum(large, REL_BUCKETS - 1))
    return tuple(int(np.argmax(b >= k)) for k in range(REL_BUCKETS))


BUCKET_THR = _bucket_thresholds()
BIAS_TILES = 14
assert BUCKET_THR[-1] <= (BIAS_TILES - 1) * 128 - 127


def _bias_of_dist(dist, tab_ref, head):
    val = jnp.full(dist.shape, tab_ref[0, head], jnp.float32)
    for k in range(1, REL_BUCKETS):
        val = jnp.where(dist >= BUCKET_THR[k], tab_ref[k, head], val)
    return val


def _bias_tiles_kernel(tab_ref, o_ref):
    h = pl.program_id(0)
    d = pl.program_id(1)
    i = lax.broadcasted_iota(jnp.int32, (128, 128), 1)
    j = lax.broadcasted_iota(jnp.int32, (128, 128), 0)
    dist = d * 128 + i - j
    val = jnp.full(dist.shape, tab_ref[0, h], jnp.float32)
    for k in range(1, REL_BUCKETS):
        val = jnp.where(dist >= BUCKET_THR[k], tab_ref[k, h], val)
    o_ref[0, 0] = val


def _bias_tiles(tab):
    return pl.pallas_call(
        _bias_tiles_kernel,
        grid=(NSA_H, BIAS_TILES),
        in_specs=[pl.BlockSpec(memory_space=pltpu.SMEM)],
        out_specs=pl.BlockSpec((1, 1, 128, 128), lambda h, d: (h, d, 0, 0)),
        out_shape=jax.ShapeDtypeStruct((NSA_H, BIAS_TILES, 128, 128), jnp.float32),
    )(tab.astype(jnp.float32))


def _group_rms(x, gain_row, scale=1.0):
    ms = jnp.mean(x * x, axis=-1, keepdims=True)
    y = x * lax.rsqrt(ms + RMS_EPS) * gain_row
    return y * scale if scale != 1.0 else y


def _nsa_prep_kernel(nq_ref, kc_ref, vc_ref, ks_ref, vs_ref, kw_ref, vw_ref, gain_ref, wck_ref, wcv_ref,
                     qn_ref, ksn_ref, kwn_ref, ksb_ref, vst_ref, kwb_ref, vwt_ref, kcb_ref, vcb_ref):
    tm = nq_ref.shape[0]
    g0 = gain_ref[0:1, :]
    g1 = gain_ref[1:2, :]
    g2 = gain_ref[2:3, :]
    g3 = gain_ref[3:4, :]
    for h in range(NSA_H):
        sl = slice(h * NSA_D, (h + 1) * NSA_D)
        qn_ref[:, sl] = _group_rms(nq_ref[:, sl], g0, NSA_D ** -0.5).astype(qn_ref.dtype)
    for g in range(NSA_G):
        sl = slice(g * NSA_D, (g + 1) * NSA_D)
        ksn = _group_rms(ks_ref[:, sl], g2)
        kwn = _group_rms(kw_ref[:, sl], g3)
        ksn_ref[:, sl] = ksn
        kwn_ref[:, sl] = kwn
        ksb_ref[:, sl] = ksn.astype(ksb_ref.dtype)
        kwb_ref[:, sl] = kwn.astype(kwb_ref.dtype)
    for src, dst in ((vs_ref, vst_ref), (vw_ref, vwt_ref)):
        vt = src[...].T
        tk = dst.shape[2]
        for c in range(tm // tk):
            dst[c] = vt[:, c * tk:(c + 1) * tk].astype(dst.dtype)
    nblk = tm // NSA_BLK
    kc3 = kc_ref[...].reshape(nblk, NSA_BLK, NSA_KV)
    vc3 = vc_ref[...].reshape(nblk, NSA_BLK, NSA_KV)
    kcb = jnp.sum(kc3 * wck_ref[...][None], axis=1)
    vcb = jnp.sum(vc3 * wcv_ref[...][None], axis=1)
    for g in range(NSA_G):
        sl = slice(g * NSA_D, (g + 1) * NSA_D)
        kcb_ref[:, sl] = _group_rms(kcb[:, sl], g1)
    vcb_ref[...] = vcb


def _nsa_prep(proj, gain, w_ck, w_cv):
    m = proj.shape[0]
    tm = 512
    kvb = EV_KV // NSA_KV
    f32, bf16 = jnp.float32, MXU_DTYPE

    def kv_spec(i):
        return pl.BlockSpec((tm, NSA_KV), lambda r, i=i: (r, kvb + i))

    row = lambda w: pl.BlockSpec((tm, w), lambda r: (r, 0))
    slabs = lambda tk: pl.BlockSpec((tm // tk, NSA_KV, tk), lambda r: (r, 0, 0))
    full = lambda a: pl.BlockSpec(a.shape, lambda r: (0,) * a.ndim)
    wck = w_ck.reshape(NSA_BLK, NSA_KV)
    wcv = w_cv.reshape(NSA_BLK, NSA_KV)
    outs = pl.pallas_call(
        _nsa_prep_kernel,
        grid=(m // tm,),
        in_specs=[pl.BlockSpec((tm, NSA_H * NSA_D), lambda r: (r, EV_NQ // (NSA_H * NSA_D)))]
                 + [kv_spec(i) for i in range(6)] + [full(gain), full(wck), full(wcv)],
        out_specs=[row(NSA_H * NSA_D), row(NSA_KV), row(NSA_KV), row(NSA_KV), slabs(NSA_TK), row(NSA_KV),
                   slabs(NSA_QB)] + [pl.BlockSpec((tm // NSA_BLK, NSA_KV), lambda r: (r, 0))] * 2,
        out_shape=[jax.ShapeDtypeStruct((m, NSA_H * NSA_D), bf16),
                   jax.ShapeDtypeStruct((m, NSA_KV), f32), jax.ShapeDtypeStruct((m, NSA_KV), f32),
                   jax.ShapeDtypeStruct((m, NSA_KV), bf16), jax.ShapeDtypeStruct((m // NSA_TK, NSA_KV, NSA_TK), bf16),
                   jax.ShapeDtypeStruct((m, NSA_KV), bf16), jax.ShapeDtypeStruct((m // NSA_QB, NSA_KV, NSA_QB), bf16)]
                  + [jax.ShapeDtypeStruct((m // NSA_BLK, NSA_KV), f32)] * 2,
        compiler_params=pltpu.CompilerParams(dimension_semantics=("parallel",),
                                             vmem_limit_bytes=VMEM_LIMIT_BYTES),
    )(proj, proj, proj, proj, proj, proj, proj, gain, wck, wcv)
    return outs


NSA_TK = 256


def _softmax_rows(lg):
    m = jnp.max(lg, axis=-1, keepdims=True)
    e = jnp.exp(lg - m)
    return e / jnp.sum(e, axis=-1, keepdims=True)


def _pad_rows(x, rows):
    if x.shape[0] == rows:
        return x
    return jnp.concatenate([x, jnp.zeros((rows - x.shape[0],) + x.shape[1:], x.dtype)], axis=0)


def _nsa_attn_kernel(tab_ref, q_ref, kcb_ref, vcb_ref, ks_ref, vst_ref, kw_ref, vwt_ref, bias_ref, ng_ref,
                     o_ref, m_scr, l_scr, acc_scr):
    f32, bf16 = jnp.float32, MXU_DTYPE
    qb = pl.program_id(1)
    nb = kcb_ref.shape[0]
    qn = NSA_QB
    gates = jax.nn.sigmoid(ng_ref[...])
    tsub = NSA_TK // qn
    shift = NSA_BLK.bit_length() - 1

    for g in range(NSA_G):
        gl = slice(g * NSA_D, (g + 1) * NSA_D)
        qs = [q_ref[:, (g * NSA_HG + h) * NSA_D:(g * NSA_HG + h + 1) * NSA_D] for h in range(NSA_HG)]
        q2 = jnp.concatenate(qs, axis=0)
        qts = [q.astype(f32).T.astype(bf16) for q in qs]

        kcb = kcb_ref[:, gl].astype(bf16)
        vcb = _pad_rows(vcb_ref[:, gl], qn).astype(bf16)
        lgt = lax.dot_general(kcb, q2, (((1,), (1,)), ((), ())), preferred_element_type=f32)
        blk = lax.broadcasted_iota(jnp.int32, (nb, qn), 0)
        blkf = blk.astype(f32)
        qpos = qb * qn + lax.broadcasted_iota(jnp.int32, (nb, qn), 1)
        dist_c = qpos - (blk * NSA_BLK + NSA_BLK - 1)
        ok_c = dist_c >= 0
        score = jnp.zeros((nb, qn), f32)
        o_c = []
        for h in range(NSA_HG):
            lg = lgt[:, h * qn:(h + 1) * qn] + _bias_of_dist(dist_c, tab_ref, g * NSA_HG + h)
            lg = jnp.where(ok_c, lg, NEG)
            mx = jnp.max(lg, axis=0, keepdims=True)
            e = jnp.exp(lg - mx)
            p = jnp.where(ok_c, e / jnp.sum(e, axis=0, keepdims=True), 0.0)
            score = score + p
            p_t = _pad_rows(p, qn).T.astype(bf16)
            o_c.append(jnp.dot(p_t, vcb, preferred_element_type=f32))
        cur = lax.shift_right_logical(qpos, shift)
        score = jnp.where(blk < cur, score, -1.0)

        sel = jnp.zeros((nb, qn), f32)
        for _ in range(min(NSA_TOPK, nb)):
            mx = jnp.max(score, axis=0, keepdims=True)
            first = jnp.min(jnp.where(score == mx, blkf, float(nb)), axis=0, keepdims=True)
            pick = blkf == first
            sel = jnp.where(pick, jnp.where(mx >= 0.0, 1.0, 0.0), sel)
            score = jnp.where(pick, -2.0, score)
        sel = jnp.where(blk == cur, 1.0, sel)
        sel_p = _pad_rows(sel, qn).astype(bf16)

        def reset():
            m_scr[...] = jnp.full(m_scr.shape, NEG, f32)
            l_scr[...] = jnp.zeros(l_scr.shape, f32)
            acc_scr[...] = jnp.zeros(acc_scr.shape, f32)

        def update(h, lg, vt):
            m_old = m_scr[h]
            m_new = jnp.maximum(m_old, jnp.max(lg, axis=0, keepdims=True))
            p = jnp.exp(lg - m_new)
            alpha = jnp.exp(m_old - m_new)
            l_scr[h] = alpha * l_scr[h] + jnp.sum(p, axis=0, keepdims=True)
            acc_scr[h] = alpha * acc_scr[h] + jnp.dot(vt, p.astype(bf16), preferred_element_type=f32)
            m_scr[h] = m_new

        def result(h):
            return (acc_scr[h] / l_scr[h]).T

        reset()
        n_tiles = qb // tsub + 1
        tok = qb * qn + lax.broadcasted_iota(jnp.int32, (NSA_TK, qn), 1)
        key = lax.broadcasted_iota(jnp.int32, (NSA_TK, qn), 0)
        eblk = lax.broadcasted_iota(jnp.int32, (NSA_TK, qn), 1)

        def body(i, carry, g=g, gl=gl, qts=qts, sel_p=sel_p):
            kt = n_tiles - 1 - i
            k0 = pl.multiple_of(kt * NSA_TK, NSA_TK)
            k = ks_ref[pl.ds(k0, NSA_TK), gl]
            vt = vst_ref[kt, gl, :]
            expand = jnp.where(eblk == lax.shift_right_logical(k0 + key, shift), 1.0, 0.0).astype(bf16)
            member = jnp.dot(expand, sel_p, preferred_element_type=f32)
            keep = jnp.where(k0 + key <= tok, member, 0.0) > 0.5
            for h in range(NSA_HG):
                tiles = [bias_ref[g * NSA_HG + h, jnp.clip(qb - (kt * tsub + j), 0, BIAS_TILES - 1)]
                         for j in range(tsub)]
                lg = jnp.dot(k, qts[h], preferred_element_type=f32) + jnp.concatenate(tiles, axis=0)
                update(h, jnp.where(keep, lg, NEG), vt)
            return carry

        lax.fori_loop(0, n_tiles, body, 0)
        o_s = [result(h) for h in range(NSA_HG)]

        reset()
        nwin = NSA_WIN // qn + 1
        wi = lax.broadcasted_iota(jnp.int32, (qn, qn), 1)
        wj = lax.broadcasted_iota(jnp.int32, (qn, qn), 0)
        for d in range(nwin):
            sub = qb - d
            subc = jnp.maximum(sub, 0)
            k = kw_ref[pl.ds(pl.multiple_of(subc * qn, qn), qn), gl]
            vt = vwt_ref[subc, gl, :]
            for h in range(NSA_HG):
                x = jnp.dot(k, qts[h], preferred_element_type=f32) + bias_ref[g * NSA_HG + h, d]
                if d == 0:
                    x = jnp.where(wj <= wi, x, NEG)
                else:
                    if d == nwin - 1:
                        x = jnp.where(wj > wi, x, NEG)
                    x = jnp.where(sub >= 0, x, NEG)
                update(h, x, vt)
        o_w = [result(h) for h in range(NSA_HG)]

        for h in range(NSA_HG):
            c = 2 * GDN_VH + (g * NSA_HG + h) * 3
            y = o_c[h] * gates[:, c:c + 1] + o_s[h] * gates[:, c + 1:c + 2] + o_w[h] * gates[:, c + 2:c + 3]
            o_ref[:, (g * NSA_HG + h) * NSA_D:(g * NSA_HG + h + 1) * NSA_D] = y.astype(o_ref.dtype)


def _nsa_prompt_attn(tab, bias_tiles, proj, qn, kcb, vcb, ksb, vst, kwb, vwt, batch, t_len):
    nq = t_len // NSA_QB
    nb = t_len // NSA_BLK
    hd = NSA_H * NSA_D
    per_b = lambda w: pl.BlockSpec((t_len, w), lambda b, i: (b, 0))
    tiles_b = lambda tk: pl.BlockSpec((t_len // tk, NSA_KV, tk), lambda b, i: (b, 0, 0))
    return pl.pallas_call(
        _nsa_attn_kernel,
        grid=(batch, nq),
        in_specs=[pl.BlockSpec(memory_space=pltpu.SMEM),
                  pl.BlockSpec((NSA_QB, hd), lambda b, i: (b * nq + i, 0)),
                  pl.BlockSpec((nb, NSA_KV), lambda b, i: (b, 0)),
                  pl.BlockSpec((nb, NSA_KV), lambda b, i: (b, 0)),
                  per_b(NSA_KV), tiles_b(NSA_TK), per_b(NSA_KV), tiles_b(NSA_QB),
                  pl.BlockSpec(bias_tiles.shape, lambda b, i: (0, 0, 0, 0)),
                  pl.BlockSpec((NSA_QB, 128), lambda b, i: (b * nq + i, EV_SMALL // 128 + 0))],
        out_specs=pl.BlockSpec((NSA_QB, hd), lambda b, i: (b * nq + i, 0)),
        out_shape=jax.ShapeDtypeStruct((batch * t_len, hd), MXU_DTYPE),
        scratch_shapes=[pltpu.VMEM((NSA_HG, 1, NSA_QB), jnp.float32),
                        pltpu.VMEM((NSA_HG, 1, NSA_QB), jnp.float32),
                        pltpu.VMEM((NSA_HG, NSA_D, NSA_QB), jnp.float32)],
        compiler_params=pltpu.CompilerParams(dimension_semantics=("parallel", "arbitrary"),
                                             vmem_limit_bytes=VMEM_LIMIT_BYTES),
    )(tab.astype(jnp.float32), qn, kcb, vcb, ksb, vst, kwb, vwt, bias_tiles, proj)


def _hdot(a, b):
    return jnp.dot(a, b, precision=lax.Precision.HIGHEST, preferred_element_type=jnp.float32)


def _mdot(a, b):
    return jnp.dot(a.astype(MXU_DTYPE), b.astype(MXU_DTYPE), preferred_element_type=jnp.float32)


def _mdot_nt(a, b):
    return lax.dot_general(a.astype(MXU_DTYPE), b.astype(MXU_DTYPE), (((1,), (1,)), ((), ())),
                           preferred_element_type=jnp.float32)


def _mdot_tn(a, b):
    return lax.dot_general(a.astype(MXU_DTYPE), b.astype(MXU_DTYPE), (((0,), (0,)), ((), ())),
                           preferred_element_type=jnp.float32)


def _shift_rows(x, prev, s):
    xs = pltpu.roll(x, s, axis=0)
    ps = pltpu.roll(prev, s, axis=0)
    row8 = lax.broadcasted_iota(jnp.int32, prev.shape, 0)
    head = jnp.where(row8 < s, ps, xs[0:8])
    return jnp.concatenate([head, xs[8:]], axis=0)


def _gdn_prep_kernel(x_ref, prev_ref, first_ref, ab_ref, cw_ref, alog_ref, dtb_ref,
                     q_ref, k_ref, v_ref, gb_ref, *, tiles_per_batch, t_valid):
    r = pl.program_id(0)
    tm = x_ref.shape[0]
    tile = r % tiles_per_batch
    prev = jnp.where(tile == 0, first_ref[0], prev_ref[...])
    x = x_ref[...]
    acc = x * cw_ref[GDN_CONV - 1:GDN_CONV, :]
    for s in range(1, GDN_CONV):
        acc = acc + _shift_rows(x, prev, s) * cw_ref[GDN_CONV - 1 - s:GDN_CONV - s, :]
    c = acc * jax.nn.sigmoid(acc)

    def l2(t):
        return t * lax.rsqrt(jnp.sum(t * t, axis=-1, keepdims=True) + 1e-6)

    for h in range(GDN_KH):
        sl = slice(h * GDN_DK, (h + 1) * GDN_DK)
        q_ref[:, sl] = l2(c[:, sl]) * (GDN_DK ** -0.5)
        k_ref[:, sl] = l2(c[:, GDN_QK + h * GDN_DK:GDN_QK + (h + 1) * GDN_DK])
    v_ref[...] = c[:, 2 * GDN_QK:]
    ab = ab_ref[...]
    z = ab + dtb_ref[...]
    softplus = jnp.maximum(z, 0.0) + jnp.log(1.0 + jnp.exp(-jnp.abs(z)))
    gate = -jnp.exp(alog_ref[...]) * softplus
    lane = lax.broadcasted_iota(jnp.int32, ab.shape, 1)
    pos = tile * tm + lax.broadcasted_iota(jnp.int32, ab.shape, 0)
    gb = jnp.where(lane < GDN_VH, gate, jnp.where(lane < 2 * GDN_VH, jax.nn.sigmoid(ab), 0.0))
    gb_ref[...] = jnp.where(pos < t_valid, gb, 0.0)


def _gdn_prep(xin, first, ab, ab_col, conv_w, a_log, dt_bias, batch, t_pad, t_valid):
    m = batch * t_pad
    tm = _pick(t_pad, (256, 128, 64))
    tpb = t_pad // tm
    f32 = jnp.float32
    lanes = lambda v: jnp.zeros((1, 128), f32).at[0, :GDN_VH].set(v.astype(f32))
    return pl.pallas_call(
        functools.partial(_gdn_prep_kernel, tiles_per_batch=tpb, t_valid=t_valid),
        grid=(m // tm,),
        in_specs=[pl.BlockSpec((tm, GDN_CONV_CH), lambda r: (r, 0)),
                  pl.BlockSpec((8, GDN_CONV_CH), lambda r: (jnp.maximum(r * (tm // 8) - 1, 0), 0)),
                  pl.BlockSpec((1, 8, GDN_CONV_CH), lambda r: (r // tpb, 0, 0)),
                  pl.BlockSpec((tm, 128), lambda r: (r, ab_col)),
                  pl.BlockSpec((GDN_CONV, GDN_CONV_CH), lambda r: (0, 0)),
                  pl.BlockSpec((1, 128), lambda r: (0, 0)),
                  pl.BlockSpec((1, 128), lambda r: (0, 0))],
        out_specs=[pl.BlockSpec((tm, GDN_QK), lambda r: (r, 0)),
                   pl.BlockSpec((tm, GDN_QK), lambda r: (r, 0)),
                   pl.BlockSpec((tm, GDN_V), lambda r: (r, 0)),
                   pl.BlockSpec((tm, 128), lambda r: (r, 0))],
        out_shape=[jax.ShapeDtypeStruct((m, GDN_QK), f32), jax.ShapeDtypeStruct((m, GDN_QK), f32),
                   jax.ShapeDtypeStruct((m, GDN_V), f32), jax.ShapeDtypeStruct((m, 128), f32)],
        compiler_params=pltpu.CompilerParams(dimension_semantics=("arbitrary",),
                                             vmem_limit_bytes=VMEM_LIMIT_BYTES),
    )(xin, xin, first, ab, conv_w.astype(f32), lanes(a_log), lanes(dt_bias))


def _cumsum_rows(x):
    row = lax.broadcasted_iota(jnp.int32, x.shape, 0)
    s = 1
    while s < x.shape[0]:
        x = x + jnp.where(row >= s, pltpu.roll(x, s, axis=0), 0.0)
        s *= 2
    return x


def _split3(a):
    hi = a.astype(jnp.bfloat16)
    lo = (a - hi.astype(jnp.float32)).astype(jnp.bfloat16)
    return hi, lo


def _dot3(a, b):
    if MXU_DTYPE != jnp.bfloat16:
        return jnp.dot(a, b, preferred_element_type=jnp.float32)
    ah, al = _split3(a)
    bh, bl = _split3(b)
    d = lambda x, y: jnp.dot(x, y, preferred_element_type=jnp.float32)
    return d(ah, bh) + (d(ah, bl) + d(al, bh))


def _unit_lower_inverses(lmats, row, col):
    eye = jnp.where(row == col, 1.0, 0.0)
    blk = lax.shift_right_logical(row, 4) == lax.shift_right_logical(col, 4)
    lds = [jnp.where(blk, l, 0.0) for l in lmats]
    xs = [eye - ld for ld in lds]
    ps = [_dot3(ld, ld) for ld in lds]
    for step in range(3):
        xs = [x + _dot3(x, p) for x, p in zip(xs, ps)]
        if step < 2:
            ps = [_dot3(p, p) for p in ps]
    size = 16
    while size < lmats[0].shape[0]:
        sh = size.bit_length() - 1
        inner = lax.shift_right_logical(row, sh) == lax.shift_right_logical(col, sh)
        outer = lax.shift_right_logical(row, sh + 1) == lax.shift_right_logical(col, sh + 1)
        coffs = [jnp.where(outer, jnp.where(inner, 0.0, l), 0.0) for l in lmats]
        ts = [_dot3(x, c) for x, c in zip(xs, coffs)]
        xs = [x - _dot3(t, x) for x, t in zip(xs, ts)]
        size *= 2
    return xs


def _gdn_chunk_kernel(q_ref, k_ref, v_ref, gb_ref, z_ref, s0_ref, nw_ref, y_ref, s_ref):
    f32 = jnp.float32
    cn = q_ref.shape[0]

    @pl.when(pl.program_id(1) == 0)
    def _():
        s_ref[...] = s0_ref[...]

    gb = gb_ref[...]
    gcum = _cumsum_rows(gb)
    gcum_t = _pad_rows(gcum, 128).T
    row = lax.broadcasted_iota(jnp.int32, (cn, cn), 0)
    col = lax.broadcasted_iota(jnp.int32, (cn, cn), 1)
    rep = GDN_VH // GDN_KH
    heads = range(GDN_VH)
    qs = [q_ref[:, kh * GDN_DK:(kh + 1) * GDN_DK] for kh in range(GDN_KH)]
    ks = [k_ref[:, kh * GDN_DK:(kh + 1) * GDN_DK] for kh in range(GDN_KH)]
    kk = [_mdot_nt(k, k) for k in ks]
    qk = [_mdot_nt(q, k) for q, k in zip(qs, ks)]
    gcols = [gcum[:, h:h + 1] for h in heads]
    diffs = [gcols[h] - gcum_t[h:h + 1, :cn] for h in heads]
    egs = [jnp.exp(g) for g in gcols]
    betas = [gb[:, GDN_VH + h:GDN_VH + h + 1] for h in heads]
    lmats = [betas[h] * kk[h // rep] * jnp.exp(jnp.where(col < row, diffs[h], -jnp.inf)) for h in heads]
    ainvs = _unit_lower_inverses(lmats, row, col)
    sols = [_dot3(ainvs[h], jnp.concatenate([betas[h] * v_ref[:, h * GDN_DV:(h + 1) * GDN_DV],
                                             (betas[h] * egs[h]) * ks[h // rep]], axis=1)) for h in heads]
    atts = [qk[h // rep] * jnp.exp(jnp.where(col <= row, diffs[h], -jnp.inf)) for h in heads]
    ss = [s_ref[0, h] for h in heads]
    us = [sols[h][:, :GDN_DV] - _mdot(sols[h][:, GDN_DV:], ss[h]) for h in heads]
    os_ = [egs[h] * _mdot(qs[h // rep], ss[h]) + _mdot(atts[h], us[h]) for h in heads]
    for h in heads:
        gl = gcum[cn - 1:cn, h:h + 1]
        s_ref[0, h] = jnp.exp(gl) * ss[h] + _mdot_tn(ks[h // rep] * jnp.exp(gl - gcols[h]), us[h])
    for h in heads:
        o = os_[h]
        on = o * lax.rsqrt(jnp.mean(o * o, axis=-1, keepdims=True) + RMS_EPS) * nw_ref[...]
        z = z_ref[:, h * GDN_DV:(h + 1) * GDN_DV]
        y_ref[:, h * GDN_DV:(h + 1) * GDN_DV] = (on * (z * jax.nn.sigmoid(z))).astype(y_ref.dtype)


def _gdn_chunks(q, k, v, gb, zsrc, z_col, s0, norm_w, batch, t_pad):
    cn = GDN_CHUNK
    nc = t_pad // cn
    rowblk = lambda w, c=0: pl.BlockSpec((cn, w), lambda b, i, c=c: (b * nc + i, c))
    st = pl.BlockSpec((1, GDN_VH, GDN_DK, GDN_DV), lambda b, i: (b, 0, 0, 0))
    return pl.pallas_call(
        _gdn_chunk_kernel,
        grid=(batch, nc),
        in_specs=[rowblk(GDN_QK), rowblk(GDN_QK), rowblk(GDN_V), rowblk(128), rowblk(GDN_V, z_col), st,
                  pl.BlockSpec((1, GDN_DV), lambda b, i: (0, 0))],
        out_specs=[rowblk(GDN_V), st],
        out_shape=[jax.ShapeDtypeStruct((batch * t_pad, GDN_V), MXU_DTYPE),
                   jax.ShapeDtypeStruct((batch, GDN_VH, GDN_DK, GDN_DV), jnp.float32)],
        compiler_params=pltpu.CompilerParams(dimension_semantics=("parallel", "arbitrary"),
                                             vmem_limit_bytes=VMEM_LIMIT_BYTES),
    )(q, k, v, gb, zsrc, s0.astype(jnp.float32), norm_w.reshape(1, GDN_DV).astype(jnp.float32))


OD_Q = 0
OD_K = ML_H * ML_DK
OD_V = 2 * ML_H * ML_DK
OD_O = OD_V + ML_H * ML_DV
OD_SMALL = OD_O + ML_H * ML_DV
OD_N = OD_SMALL + 128


def _cummax_rows(x):
    row = lax.broadcasted_iota(jnp.int32, x.shape, 0)
    s = 1
    while s < x.shape[0]:
        x = jnp.maximum(x, jnp.where(row >= s, pltpu.roll(x, s, axis=0), -jnp.inf))
        s *= 2
    return x


def _round_mxu(x):
    return x.astype(MXU_DTYPE).astype(jnp.float32)


def _mlstm_chunk_kernel(q_ref, k_ref, v_ref, og_ref, if_ref, bif_ref, nw_ref, c0_ref, n0_ref, m0_ref,
                        y_ref, c_ref, n_ref, m_ref, *, t_valid):
    f32 = jnp.float32
    cn = q_ref.shape[0]
    ci = pl.program_id(1)

    @pl.when(ci == 0)
    def _():
        c_ref[...] = c0_ref[...]
        n_ref[...] = n0_ref[...]
        m_ref[...] = m0_ref[...]

    pre = if_ref[...] + bif_ref[...]
    pos = ci * cn + lax.broadcasted_iota(jnp.int32, pre.shape, 0)
    live = pos < t_valid
    lf = jnp.where(live, jnp.minimum(pre, 0.0) - jnp.log(1.0 + jnp.exp(-jnp.abs(pre))), 0.0)
    fcum = _cumsum_rows(lf)
    a_all = jnp.where(live, pre, NEG) - pltpu.roll(fcum, 128 - ML_H, axis=1)
    amax = _cummax_rows(a_all)
    a_t = _pad_rows(a_all, 128).T
    row = lax.broadcasted_iota(jnp.int32, (cn, cn), 0)
    col = lax.broadcasted_iota(jnp.int32, (cn, cn), 1)
    m_all = m_ref[0]
    heads = range(ML_H)
    qs = [q_ref[:, h * ML_DK:(h + 1) * ML_DK] * (ML_DK ** -0.5) for h in heads]
    ks = [k_ref[:, h * ML_DK:(h + 1) * ML_DK] for h in heads]
    vs = [v_ref[:, h * ML_DV:(h + 1) * ML_DV] for h in heads]
    m_prevs = [m_all[:, h:h + 1] for h in heads]
    fcs = [fcum[:, ML_H + h:ML_H + h + 1] for h in heads]
    mts = [fcs[h] + jnp.maximum(m_prevs[h], amax[:, h:h + 1]) for h in heads]
    qks = [_mdot_nt(qs[h], ks[h]) for h in heads]
    ss = [qks[h] * jnp.exp(jnp.where(col <= row, a_t[h:h + 1, :cn] + (fcs[h] - mts[h]), -jnp.inf)) for h in heads]
    dec0s = [jnp.exp(fcs[h] + m_prevs[h] - mts[h]) for h in heads]
    cms = [c_ref[0, h] for h in heads]
    nvs = [n_ref[0, h:h + 1, :] for h in heads]
    nums = [dec0s[h] * _mdot(qs[h], cms[h]) + _mdot(ss[h], vs[h]) for h in heads]
    dens = [dec0s[h] * jnp.sum(_round_mxu(qs[h]) * _round_mxu(nvs[h]), axis=-1, keepdims=True)
            + jnp.sum(ss[h], axis=-1, keepdims=True) for h in heads]
    hcs = [nums[h] / jnp.maximum(jnp.abs(dens[h]), jnp.exp(-mts[h])) for h in heads]
    m_new = [mts[h][cn - 1:cn] for h in heads]
    for h in heads:
        f_end = fcs[h][cn - 1:cn]
        w = jnp.exp(a_all[:, h:h + 1] + (f_end - m_new[h]))
        dc = jnp.exp(f_end + m_prevs[h] - m_new[h])
        c_ref[0, h] = dc * cms[h] + _mdot_tn(w * ks[h], vs[h])
        n_ref[0, h:h + 1, :] = dc * nvs[h] + jnp.sum(_round_mxu(w) * _round_mxu(ks[h]), axis=0, keepdims=True)
    for h in heads:
        hc = hcs[h]
        hn = hc * lax.rsqrt(jnp.mean(hc * hc, axis=-1, keepdims=True) + RMS_EPS) \
            * nw_ref[:, h * ML_DV:(h + 1) * ML_DV]
        og = og_ref[:, h * ML_DV:(h + 1) * ML_DV]
        y_ref[:, h * ML_DV:(h + 1) * ML_DV] = (hn * jax.nn.sigmoid(og)).astype(y_ref.dtype)
    lane = lax.broadcasted_iota(jnp.int32, m_all.shape, 1)
    out = m_all
    for h in range(ML_H):
        out = jnp.where(lane == h, m_new[h], out)
    m_ref[0] = out


def _mlstm_chunks(proj, b_if, mnorm, c0, n0, m0, batch, t_pad, t_valid):
    cn = ML_CHUNK
    nc = t_pad // cn
    f32 = jnp.float32
    blk = lambda w, c: pl.BlockSpec((cn, w), lambda b, i, c=c: (b * nc + i, c))
    const = lambda shape: pl.BlockSpec(shape, lambda b, i: (0,) * len(shape))
    st_c = pl.BlockSpec((1, ML_H, ML_DK, ML_DV), lambda b, i: (b, 0, 0, 0))
    st_n = pl.BlockSpec((1, ML_H, ML_DK), lambda b, i: (b, 0, 0))
    st_m = pl.BlockSpec((1, 1, 128), lambda b, i: (b, 0, 0))
    bif = jnp.zeros((1, 128), f32).at[0, :2 * ML_H].set(b_if.astype(f32))
    m0p = jnp.zeros((batch, 1, 128), f32).at[:, 0, :ML_H].set(m0.astype(f32))
    hq, hv = ML_H * ML_DK, ML_H * ML_DV
    y, c, n, m = pl.pallas_call(
        functools.partial(_mlstm_chunk_kernel, t_valid=t_valid),
        grid=(batch, nc),
        in_specs=[blk(hq, OD_Q // hq), blk(hq, OD_K // hq), blk(hv, OD_V // hv), blk(hv, OD_O // hv),
                  blk(128, OD_SMALL // 128), const((1, 128)), const((1, hv)), st_c, st_n, st_m],
        out_specs=[blk(hv, 0), st_c, st_n, st_m],
        out_shape=[jax.ShapeDtypeStruct((batch * t_pad, hv), MXU_DTYPE),
                   jax.ShapeDtypeStruct((batch, ML_H, ML_DK, ML_DV), f32),
                   jax.ShapeDtypeStruct((batch, ML_H, ML_DK), f32),
                   jax.ShapeDtypeStruct((batch, 1, 128), f32)],
        compiler_params=pltpu.CompilerParams(dimension_semantics=("parallel", "arbitrary"),
                                             vmem_limit_bytes=VMEM_LIMIT_BYTES),
    )(proj, proj, proj, proj, proj, bif, mnorm.reshape(1, hv).astype(f32), c0.astype(f32), n0.astype(f32), m0p)
    return y, c, n, m[:, 0, :ML_H]


def _permute_odd_w(w_in):
    pad = jnp.zeros((w_in.shape[0], OD_N - w_in.shape[1]), w_in.dtype)
    return jnp.concatenate([w_in, pad], axis=1).astype(MXU_DTYPE)


DEC_PAGES_PER_STEP = 8


def _dec_compress_kernel(pt_ref, *refs, pp):
    kc_refs, vc_refs = refs[:pp], refs[pp:2 * pp]
    wck_ref, wcv_ref, g1_ref, kcb_ref, vcb_ref = refs[2 * pp:]
    per_page = PAGE_SIZE // NSA_BLK

    def compress(page_refs, w_ref, g):
        w = w_ref[:, g * NSA_D:(g + 1) * NSA_D]
        rows = [jnp.sum(r[pl.ds(g, PAGE_SIZE, stride=NSA_G), :].reshape(per_page, NSA_BLK, NSA_D) * w[None], axis=1)
                for r in page_refs]
        return jnp.concatenate(rows, axis=0)

    for g in range(NSA_G):
        sl = slice(g * NSA_D, (g + 1) * NSA_D)
        kcb_ref[0, :, sl] = _group_rms(compress(kc_refs, wck_ref, g), g1_ref[...])
        vcb_ref[0, :, sl] = compress(vc_refs, wcv_ref, g)


def _dec_compress(page_table, pool_kc, pool_vc, layer, w_ck, w_cv, g1):
    nseq, npages = page_table.shape
    pp = DEC_PAGES_PER_STEP
    per_page = PAGE_SIZE // NSA_BLK
    nphys = pool_kc.shape[1]
    base = layer * nphys
    kc = pool_kc.reshape(-1, NSA_D)
    vc = pool_vc.reshape(-1, NSA_D)
    page = lambda j: pl.BlockSpec((PAGE_SIZE * NSA_G, NSA_D), lambda b, i, pt, j=j: (base + pt[b, i * pp + j], 0))
    const = lambda shape: pl.BlockSpec(shape, lambda b, i, pt: (0,) * len(shape))
    out = pl.BlockSpec((1, pp * per_page, NSA_KV), lambda b, i, pt: (b, i, 0))
    nb = npages * per_page
    return pl.pallas_call(
        functools.partial(_dec_compress_kernel, pp=pp),
        grid_spec=pltpu.PrefetchScalarGridSpec(
            num_scalar_prefetch=1, grid=(nseq, npages // pp),
            in_specs=[page(j) for j in range(pp)] * 2 + [const((NSA_BLK, NSA_KV))] * 2 + [const((1, NSA_D))],
            out_specs=[out, out]),
        out_shape=[jax.ShapeDtypeStruct((nseq, nb, NSA_KV), jnp.float32)] * 2,
        compiler_params=pltpu.CompilerParams(dimension_semantics=("parallel", "arbitrary"),
                                             vmem_limit_bytes=VMEM_LIMIT_BYTES),
    )(page_table, *([kc] * pp), *([vc] * pp), w_ck.reshape(NSA_BLK, NSA_KV), w_cv.reshape(NSA_BLK, NSA_KV),
      g1.reshape(1, NSA_D))


def _rows_by_head(head_row, fn, nheads):
    out = fn(0)
    out = jnp.broadcast_to(out, (nheads, out.shape[1]))
    for h in range(1, nheads):
        out = jnp.where(head_row == h, fn(h), out)
    return out


def _dec_select_kernel(tab_ref, q_ref, kcb_ref, vcb_ref, oc_ref, idx_ref, *, qpos):
    f32 = jnp.float32
    nbp = kcb_ref.shape[1]
    q = q_ref[0]
    head_row = lax.broadcasted_iota(jnp.int32, (NSA_H, 1), 0)
    blk = lax.broadcasted_iota(jnp.int32, (1, nbp), 1)
    blkf = blk.astype(f32)
    dist = qpos - (blk * NSA_BLK + NSA_BLK - 1)
    ok = dist >= 0
    cur = qpos // NSA_BLK
    lg = jnp.zeros((NSA_H, nbp), f32)
    for g in range(NSA_G):
        lg_g = _mdot_nt(q, kcb_ref[0, :, g * NSA_D:(g + 1) * NSA_D])
        lg = jnp.where(head_row // NSA_HG == g, lg_g, lg)
    lg = lg + _rows_by_head(head_row, lambda h: _bias_of_dist(dist, tab_ref, h), NSA_H)
    lg = jnp.where(ok, lg, NEG)
    e = jnp.exp(lg - jnp.max(lg, axis=-1, keepdims=True))
    p = jnp.where(ok, e / jnp.sum(e, axis=-1, keepdims=True), 0.0)
    oc = jnp.zeros((NSA_H, NSA_D), f32)
    lane = lax.broadcasted_iota(jnp.int32, (1, 128), 1)
    for g in range(NSA_G):
        in_g = head_row // NSA_HG == g
        oc = jnp.where(in_g, _mdot(p, vcb_ref[0, :, g * NSA_D:(g + 1) * NSA_D]), oc)
        score = jnp.sum(jnp.where(in_g, p, 0.0), axis=0, keepdims=True)
        score = jnp.where(blk < cur, score, -1.0)
        idx = jnp.full((1, 128), -1.0, f32)
        for r in range(NSA_TOPK):
            mx = jnp.max(score, axis=-1, keepdims=True)
            first = jnp.min(jnp.where(score == mx, blkf, float(nbp)), axis=-1, keepdims=True)
            idx = jnp.where(lane == r, jnp.where(mx >= 0.0, first, -1.0), idx)
            score = jnp.where(blkf == first, -2.0, score)
        idx_ref[0, g:g + 1, :] = idx.astype(jnp.int32)
    oc_ref[0] = oc


def _dec_select(tab, qh, kcb, vcb, qpos):
    nseq = qh.shape[0]
    nbp = kcb.shape[1]
    seq = lambda shape: pl.BlockSpec((1,) + shape, lambda b: (b, 0, 0))
    return pl.pallas_call(
        functools.partial(_dec_select_kernel, qpos=qpos),
        grid=(nseq,),
        in_specs=[pl.BlockSpec(memory_space=pltpu.SMEM), seq((NSA_H, NSA_D)), seq((nbp, NSA_KV)),
                  seq((nbp, NSA_KV))],
        out_specs=[seq((NSA_H, NSA_D)), seq((NSA_G, 128))],
        out_shape=[jax.ShapeDtypeStruct((nseq, NSA_H, NSA_D), jnp.float32),
                   jax.ShapeDtypeStruct((nseq, NSA_G, 128), jnp.int32)],
        compiler_params=pltpu.CompilerParams(dimension_semantics=("parallel",),
                                             vmem_limit_bytes=VMEM_LIMIT_BYTES),
    )(tab.astype(jnp.float32), qh, kcb, vcb)


def _dec_attend_kernel(pt_ref, idx_ref, tab_ref, q_ref, *refs, qpos):
    f32 = jnp.float32
    ks_refs, vs_refs = refs[:NSA_TOPK], refs[NSA_TOPK:2 * NSA_TOPK]
    ksn_ref, vsn_ref, kwb_ref, vwb_ref, kwn_ref, vwn_ref, oc_ref, ng_ref, y_ref = refs[2 * NSA_TOPK:]
    b = pl.program_id(0)
    g = pl.program_id(1)
    cur = qpos // NSA_BLK
    head_row = lax.broadcasted_iota(jnp.int32, (NSA_HG, 1), 0)
    q4 = q_ref[0, pl.ds(g * NSA_HG, NSA_HG), :]
    q4r = _round_mxu(q4.astype(f32))
    new_row = b * NSA_G + g
    tab0 = _rows_by_head(head_row, lambda h: jnp.full((1, 1), tab_ref[0, g * NSA_HG + h], f32), NSA_HG)

    def attend(keys, vals, dist, ok, k_new, v_new):
        lg = _mdot_nt(q4, keys)
        lg = lg + _rows_by_head(head_row, lambda h: _bias_of_dist(dist, tab_ref, g * NSA_HG + h), NSA_HG)
        lg = jnp.where(ok, lg, NEG)
        lg_new = jnp.sum(q4r * _round_mxu(k_new), axis=-1, keepdims=True) + tab0
        m = jnp.maximum(jnp.max(lg, axis=-1, keepdims=True), lg_new)
        e = jnp.exp(lg - m)
        e_new = jnp.exp(lg_new - m)
        den = jnp.sum(e, axis=-1, keepdims=True) + e_new
        return _mdot(e / den, vals) + _round_mxu(e_new / den) * _round_mxu(v_new)

    pieces, oks = [], []
    jrow = lax.broadcasted_iota(jnp.int32, (1, NSA_BLK), 1)
    for s in range(NSA_TOPK):
        blk_id = idx_ref[b, g, s]
        valid = jnp.logical_and(blk_id >= 0, blk_id < cur)
        pos = jnp.maximum(blk_id, 0) * NSA_BLK + jrow
        pieces.append(qpos - pos)
        oks.append(jnp.where(valid, 1.0, 0.0) * jnp.where(pos <= qpos, 1.0, 0.0))
    dist_s = jnp.concatenate(pieces, axis=1)
    ok_s = jnp.concatenate(oks, axis=1) > 0.5
    mine = lambda r: r[pl.ds(g, NSA_BLK, stride=NSA_G), :]
    k_sel = jnp.concatenate([mine(r) for r in ks_refs], axis=0)
    v_sel = jnp.concatenate([mine(r) for r in vs_refs], axis=0)
    o_s = attend(k_sel, v_sel, dist_s, ok_s, ksn_ref[pl.ds(new_row, 1), :], vsn_ref[pl.ds(new_row, 1), :])

    wb = kwb_ref.shape[1]
    dist_w = wb - lax.broadcasted_iota(jnp.int32, (1, wb), 1)
    ok_w = dist_w < NSA_WIN
    o_w = attend(kwb_ref[0], vwb_ref[0], dist_w, ok_w, kwn_ref[pl.ds(new_row, 1), :], vwn_ref[pl.ds(new_row, 1), :])

    gates = jax.nn.sigmoid(ng_ref[pl.ds(b, 1), :])
    lane = lax.broadcasted_iota(jnp.int32, (1, 128), 1)

    def gate(branch):
        def one(h):
            c = 2 * GDN_VH + (g * NSA_HG + h) * 3 + branch
            return jnp.sum(jnp.where(lane == c, gates, 0.0), axis=-1, keepdims=True)
        return _rows_by_head(head_row, one, NSA_HG)

    oc4 = oc_ref[0, pl.ds(g * NSA_HG, NSA_HG), :]
    y_ref[0, pl.ds(g * NSA_HG, NSA_HG), :] = oc4 * gate(0) + o_s * gate(1) + o_w * gate(2)


def _dec_attend(tab, page_table, idx, qh, pool_ks, pool_vs, layer, ksn, vsn, buf_kw, buf_vw, kwn, vwn, oc, ng,
                qpos):
    nseq = qh.shape[0]
    base = layer * pool_ks.shape[1]
    wb = buf_kw.shape[1]
    ks = pool_ks.reshape(-1, NSA_D)
    vs = pool_vs.reshape(-1, NSA_D)
    per_page = PAGE_SIZE // NSA_BLK

    def sel(s):
        def index(b, g, pt, ix, s=s):
            blk_id = jnp.maximum(ix[b, g, s], 0)
            return ((base + pt[b, blk_id // per_page]) * per_page + blk_id % per_page, 0)
        return pl.BlockSpec((NSA_BLK * NSA_G, NSA_D), index)

    whole = lambda a: pl.BlockSpec(a.shape, lambda b, g, pt, ix: (0,) * a.ndim)
    heads = pl.BlockSpec((1, NSA_H, NSA_D), lambda b, g, pt, ix: (b, 0, 0))
    win = pl.BlockSpec((1, wb, NSA_D), lambda b, g, pt, ix: (b, 0, g))
    rows = lambda a: a.reshape(nseq * NSA_G, NSA_D)
    new = [rows(ksn), rows(vsn), rows(kwn), rows(vwn)]
    return pl.pallas_call(
        functools.partial(_dec_attend_kernel, qpos=qpos),
        grid_spec=pltpu.PrefetchScalarGridSpec(
            num_scalar_prefetch=2, grid=(nseq, NSA_G),
            in_specs=[pl.BlockSpec(memory_space=pltpu.SMEM), heads] + [sel(s) for s in range(NSA_TOPK)] * 2
                     + [whole(new[0]), whole(new[1]), win, win, whole(new[2]), whole(new[3]), heads, whole(ng)],
            out_specs=heads),
        out_shape=jax.ShapeDtypeStruct((nseq, NSA_H, NSA_D), jnp.float32),
        compiler_params=pltpu.CompilerParams(dimension_semantics=("parallel", "arbitrary"),
                                             vmem_limit_bytes=VMEM_LIMIT_BYTES),
    )(page_table, idx, tab.astype(jnp.float32), qh, *([ks] * NSA_TOPK), *([vs] * NSA_TOPK), new[0], new[1],
      buf_kw.reshape(nseq, wb, NSA_KV), buf_vw.reshape(nseq, wb, NSA_KV), new[2], new[3], oc, ng)


def _nsa_decode(proj, tab, qk_gain, w_ck, w_cv, caches, nseq):
    assert PAST_LEN % PAGE_SIZE == 0 and DEC_SEQ == 1
    pad = jnp.pad(proj, ((0, 512 - nseq), (0, 0)))
    qn, ksn, kwn, _, _, _, _, _, _ = _nsa_prep(pad, qk_gain, w_ck, w_cv)
    qh = qn[:nseq].reshape(nseq, NSA_H, NSA_D).astype(jnp.float32)
    kv = lambda i: proj[:, EV_KV + i * NSA_KV:EV_KV + (i + 1) * NSA_KV]
    layer = caches['layer']
    kcb, vcb = _dec_compress(caches['page_table'], caches['kc'], caches['vc'], layer, w_ck, w_cv, qk_gain[1])
    oc, idx = _dec_select(tab, qh, kcb, vcb, PAST_LEN)
    ng = proj[:, EV_SMALL:EV_SMALL + 128]
    y = _dec_attend(tab, caches['page_table'], idx, qh, caches['ks'], caches['vs'], layer, ksn[:nseq], kv(3),
                    caches['kw'], caches['vw'], kwn[:nseq], kv(5), oc, ng, PAST_LEN)
    return y.reshape(nseq, NSA_H * NSA_D), ksn[:nseq], kwn[:nseq]


def _split(x, widths):
    offs = [int(o) for o in np.cumsum(widths)[:-1]]
    return jnp.split(x, offs, axis=-1)


def _rms(x, g):
    xf = x.astype(jnp.float32)
    y = xf * lax.rsqrt(jnp.mean(xf * xf, axis=-1, keepdims=True) + RMS_EPS)
    return (y * g.astype(jnp.float32)).astype(x.dtype)


def _mlp(xn, w1, w2):
    h = jnp.square(jax.nn.relu(_pmm(xn, w1)))
    return _pmm(h, w2)


def _causal_conv(xin, w):
    c = xin.shape[-1]
    return lax.conv_general_dilated(xin, w[:, None, :].astype(xin.dtype), window_strides=(1,),
                                    padding='VALID', dimension_numbers=('NWC', 'WIO', 'NWC'),
                                    feature_group_count=c)


def _chunks(t, c, pad, pad_val=0.0):
    t = t.astype(jnp.float32)
    if pad:
        t = jnp.pad(t, [(0, 0), (0, pad)] + [(0, 0)] * (t.ndim - 2), constant_values=pad_val)
    b, tp = t.shape[:2]
    t = t.reshape(b, tp // c, c, *t.shape[2:])
    return jnp.swapaxes(jnp.moveaxis(t, 1, 0), 2, 3)


def _unchunk(o, t_len):
    n, b, h, c, d = o.shape
    return jnp.transpose(o, (1, 0, 3, 2, 4)).reshape(b, n * c, h, d)[:, :t_len]


def _gdn_chunked(q, k, v, g, beta, s0):
    b, t_len, h, dk = q.shape
    dv = v.shape[-1]
    c = min(GDN_CHUNK, t_len)
    pad = (-t_len) % c
    tri_s = jnp.tril(jnp.ones((c, c), bool), -1)
    tri_i = jnp.tril(jnp.ones((c, c), bool), 0)
    eye = jnp.eye(c, dtype=jnp.float32)

    def body(s, inp):
        qc, kc, vc, gc, bc = inp
        gcum = jnp.cumsum(gc, axis=-1)
        diff = gcum[..., :, None] - gcum[..., None, :]
        eg = jnp.exp(gcum)
        lmat = bc[..., :, None] * jnp.einsum('bhid,bhjd->bhij', kc, kc) * jnp.exp(jnp.where(tri_s, diff, -jnp.inf))
        rhs = jnp.concatenate([bc[..., None] * vc, (bc * eg)[..., None] * kc], axis=-1)
        sol = lax.linalg.triangular_solve(eye + lmat, rhs, left_side=True, lower=True)
        u = sol[..., :dv] - jnp.einsum('bhcd,bhde->bhce', sol[..., dv:], s)
        att = jnp.einsum('bhid,bhjd->bhij', qc, kc) * jnp.exp(jnp.where(tri_i, diff, -jnp.inf))
        o = eg[..., None] * jnp.einsum('bhcd,bhde->bhce', qc, s) + jnp.einsum('bhij,bhje->bhie', att, u)
        gl = gcum[..., -1]
        s = jnp.exp(gl)[..., None, None] * s + jnp.einsum('bhcd,bhce->bhde', kc * jnp.exp(gl[..., None] - gcum)[..., None], u)
        return s, o

    s, o = lax.scan(body, s0.astype(jnp.float32),
                    (_chunks(q, c, pad), _chunks(k, c, pad), _chunks(v, c, pad), _chunks(g, c, pad), _chunks(beta, c, pad)))
    return _unchunk(o, t_len).astype(v.dtype), s.astype(s0.dtype)


def _gdn(parts, conv_w, a_log, dt_bias, norm_w, conv_prev, s0):
    q, k, v, z, a, bgate = parts
    b, t_len = q.shape[:2]
    xin = jnp.concatenate([conv_prev.astype(q.dtype), jnp.concatenate([q, k, v], axis=-1)], axis=1)
    c = jax.nn.silu(_causal_conv(xin, conv_w))
    cq, ck, cv = _split(c, (GDN_QK, GDN_QK, GDN_V))

    def l2(t):
        t = t.reshape(b, t_len, GDN_KH, GDN_DK).astype(jnp.float32)
        return t * lax.rsqrt(jnp.sum(t * t, axis=-1, keepdims=True) + 1e-6)

    rep = GDN_VH // GDN_KH
    qh = jnp.repeat(l2(cq) * (GDN_DK ** -0.5), rep, axis=2)
    kh = jnp.repeat(l2(ck), rep, axis=2)
    vh = cv.reshape(b, t_len, GDN_VH, GDN_DV)
    beta = jax.nn.sigmoid(bgate.astype(jnp.float32))
    g = -jnp.exp(a_log.astype(jnp.float32)) * jax.nn.softplus(a.astype(jnp.float32) + dt_bias.astype(jnp.float32))
    o, s = _gdn_chunked(qh, kh, vh, g, beta, s0)
    o = _rms(o, norm_w) * jax.nn.silu(z.reshape(b, t_len, GDN_VH, GDN_DV))
    return o.reshape(b, t_len, GDN_V), xin[:, -(GDN_CONV - 1):], s


def _bucket(dist):
    n = jnp.maximum(dist, 0)
    nf = jnp.maximum(n, REL_EXACT).astype(jnp.float32)
    large = REL_EXACT + (jnp.log(nf / REL_EXACT) / math.log(REL_MAX_DIST / REL_EXACT)
                         * (REL_BUCKETS - REL_EXACT)).astype(jnp.int32)
    large = jnp.minimum(large, REL_BUCKETS - 1)
    return jnp.where(n < REL_EXACT, n, large)


def _tok_bias(dist, tab):
    t_len, n = dist.shape
    bias = tab.astype(jnp.float32)[_bucket(dist)]
    return bias.reshape(t_len, n, NSA_G, NSA_HG).transpose(0, 2, 3, 1)


def _compress(rows, w):
    b, t_len = rows.shape[:2]
    r = rows.reshape(b, t_len // NSA_BLK, NSA_BLK, NSA_G, NSA_D)
    return jnp.einsum('bnlgd,lgd->bngd', r, w)


def _nsa_core(q, qpos, kc_b, vc_b, fetch, kw, vw, kwpos, tab):
    b, t_len = q.shape[:2]
    f32 = jnp.float32
    nb = kc_b.shape[1]
    blk = jnp.arange(nb, dtype=jnp.int32)
    dist_c = qpos[:, None] - (blk * NSA_BLK + NSA_BLK - 1)[None, :]
    ok_c = (dist_c >= 0)[:, None, None, :]
    lg_c = jnp.einsum('btghd,bngd->btghn', q, kc_b).astype(f32) + _tok_bias(dist_c, tab)
    p_c = jax.nn.softmax(jnp.where(ok_c, lg_c, NEG), axis=-1) * ok_c
    o_c = jnp.einsum('btghn,bngd->btghd', p_c.astype(vc_b.dtype), vc_b)
    cur = qpos // NSA_BLK
    cand = (blk[None, :] < cur[:, None])[None, :, None, :]
    score = jnp.where(cand, p_c.sum(axis=3), -1.0)
    _, idx = lax.top_k(score, min(NSA_TOPK, nb))
    cur_b = jnp.broadcast_to(cur[None, :, None, None], (b, t_len, NSA_G, 1))
    idx_all = jnp.concatenate([idx, cur_b], axis=-1)
    ok_all = jnp.concatenate([idx < cur[None, :, None, None], jnp.ones((b, t_len, NSA_G, 1), bool)], axis=-1)
    pos5 = idx_all[..., None] * NSA_BLK + jnp.arange(NSA_BLK, dtype=jnp.int32)
    ok5 = ok_all[..., None] & (pos5 <= qpos[None, :, None, None, None])
    pos = pos5.reshape(b, t_len, NSA_G, -1)
    ok_s = ok5.reshape(b, t_len, NSA_G, 1, -1)
    k_s, v_s = fetch(pos)
    tab3 = tab.astype(f32).reshape(REL_BUCKETS, NSA_G, NSA_HG)
    bias_s = jnp.moveaxis(tab3[_bucket(qpos[None, :, None, None] - pos), jnp.arange(NSA_G)[:, None]], -1, 3)
    lg_s = jnp.einsum('btghd,btgnd->btghn', q, k_s).astype(f32) + bias_s
    p_s = jax.nn.softmax(jnp.where(ok_s, lg_s, NEG), axis=-1)
    o_s = jnp.einsum('btghn,btgnd->btghd', p_s.astype(v_s.dtype), v_s)
    dist_w = qpos[:, None] - kwpos[None, :]
    ok_w = ((dist_w >= 0) & (dist_w < NSA_WIN) & (kwpos >= 0)[None, :])[:, None, None, :]
    lg_w = jnp.einsum('btghd,bngd->btghn', q, kw).astype(f32) + _tok_bias(dist_w, tab)
    p_w = jax.nn.softmax(jnp.where(ok_w, lg_w, NEG), axis=-1)
    o_w = jnp.einsum('btghn,bngd->btghd', p_w.astype(vw.dtype), vw)
    return jnp.stack([o_c, o_s, o_w], axis=-2)


def _over_query_blocks(fn, q, qpos):
    b, t_len = q.shape[:2]
    if t_len <= NSA_QB:
        return fn(q, qpos)
    nblk = -(-t_len // NSA_QB)
    qb = jnp.swapaxes(q.reshape(b, nblk, NSA_QB, *q.shape[2:]), 0, 1)
    pb = qpos.reshape(nblk, NSA_QB)
    out = lax.map(lambda a: fn(a[0], a[1]), (qb, pb))
    out = jnp.swapaxes(out, 0, 1)
    return out.reshape(b, nblk * NSA_QB, *out.shape[3:])[:, :t_len]


def _nsa_combine(o3, ng):
    b, t_len = ng.shape[:2]
    gates = jax.nn.sigmoid(ng.astype(jnp.float32)).reshape(b, t_len, NSA_G, NSA_HG, 3, 1)
    return (o3.astype(jnp.float32) * gates).sum(axis=-2).reshape(b, t_len, NSA_H * NSA_D).astype(o3.dtype)


def _nsa_prompt(q, kc, vc, ks, vs, kw, vw, w_ck, w_cv, g_kc, tab):
    b, t_len = q.shape[:2]
    kc_b = _rms(_compress(kc, w_ck), g_kc)
    vc_b = _compress(vc, w_cv)
    padw = ((0, 0), (NSA_WIN, 0), (0, 0), (0, 0))
    kw_pad = jnp.pad(kw, padw)
    vw_pad = jnp.pad(vw, padw)
    bi = jnp.arange(b)[:, None, None, None]
    gi = jnp.arange(NSA_G)[None, None, :, None]

    def fetch(pos):
        return ks[bi, pos, gi], vs[bi, pos, gi]

    def block(qb, pb):
        start = pb[0]
        n = NSA_WIN + qb.shape[1]
        kwb = lax.dynamic_slice_in_dim(kw_pad, start, n, axis=1)
        vwb = lax.dynamic_slice_in_dim(vw_pad, start, n, axis=1)
        kwpos = start - NSA_WIN + jnp.arange(n, dtype=jnp.int32)
        return _nsa_core(qb, pb, kc_b, vc_b, fetch, kwb, vwb, kwpos, tab)

    return _over_query_blocks(block, q, jnp.arange(t_len, dtype=jnp.int32))


def _nsa_sample(q, kc, vc, ks, vs, kw, vw, pool_kc, pool_vc, pool_ks, pool_vs, buf_kw, buf_vw,
                page_table, w_ck, w_cv, g_kc, tab):
    db, ds = q.shape[:2]

    def full_rows(pool, new):
        past = pool[page_table].reshape(db, PAST_LEN, NSA_G, NSA_D)
        r = jnp.concatenate([past, new], axis=1)
        pad = (-r.shape[1]) % NSA_BLK
        return jnp.pad(r, ((0, 0), (0, pad), (0, 0), (0, 0)))

    kc_b = _rms(_compress(full_rows(pool_kc, kc), w_ck), g_kc)
    vc_b = _compress(full_rows(pool_vc, vc), w_cv)
    bi = jnp.arange(db)[:, None, None, None]
    gi = jnp.arange(NSA_G)[None, None, :, None]

    def fetch(pos):
        in_past = (pos < PAST_LEN)[..., None]
        pp = jnp.minimum(pos, PAST_LEN - 1)
        phys = page_table[bi, pp // PAGE_SIZE]
        off = pp % PAGE_SIZE
        pn = jnp.clip(pos - PAST_LEN, 0, ds - 1)
        k = jnp.where(in_past, pool_ks[phys, off, gi], ks[bi, pn, gi])
        v = jnp.where(in_past, pool_vs[phys, off, gi], vs[bi, pn, gi])
        return k, v

    wb = buf_kw.shape[1]
    kw_all = jnp.concatenate([buf_kw, kw], axis=1)
    vw_all = jnp.concatenate([buf_vw, vw], axis=1)
    kwpos = PAST_LEN - wb + jnp.arange(wb + ds, dtype=jnp.int32)
    qpos = PAST_LEN + jnp.arange(ds, dtype=jnp.int32)
    o3 = _over_query_blocks(lambda qb, pb: _nsa_core(qb, pb, kc_b, vc_b, fetch, kw_all, vw_all, kwpos, tab), q, qpos)
    return o3, kw_all[:, -wb:], vw_all[:, -wb:]


def _even_inputs(xn, w_in, qk_gain):
    b, t_len, _ = xn.shape
    (gq, gk, gv, gz, ga, gb, nq, kc, vc, ks, vs, kw, vw, ng) = _split(_pmm(xn, w_in), EVEN_WIDTHS)
    hd = lambda t: t.reshape(b, t_len, NSA_G, NSA_D)
    q = _rms(nq.reshape(b, t_len, NSA_G, NSA_HG, NSA_D), qk_gain[0]) * (NSA_D ** -0.5)
    ks = _rms(hd(ks), qk_gain[2])
    kw = _rms(hd(kw), qk_gain[3])
    return (gq, gk, gv, gz, ga, gb), (q, hd(kc), hd(vc), ks, hd(vs), kw, hd(vw), ng)


def _even_prompt(xn, w_in, w_out, conv_w, a_log, dt_bias, gnorm, qk_gain, w_ck, w_cv, tab, bias_tiles):
    b, t_len, d = xn.shape
    proj = _matmul(xn.reshape(b * t_len, d).astype(MXU_DTYPE), _permute_even_w(w_in))
    p3 = proj.reshape(b, t_len, EV_N)
    gq, gk, gv = _split(p3[..., :GDN_CONV_CH], (GDN_QK, GDN_QK, GDN_V))
    gdn_parts = (gq, gk, gv, p3[..., EV_Z:EV_Z + GDN_V], p3[..., EV_SMALL:EV_SMALL + GDN_VH],
                 p3[..., EV_SMALL + GDN_VH:EV_SMALL + 2 * GDN_VH])
    conv0 = jnp.zeros((b, GDN_CONV - 1, GDN_CONV_CH), xn.dtype)
    s0 = jnp.zeros((b, GDN_VH, GDN_DK, GDN_DV), xn.dtype)
    ya, conv_new, s = _gdn(gdn_parts, conv_w, a_log, dt_bias, gnorm, conv0, s0)
    qn, ksn, kwn, ksb, vsb, kwb, vwb, kcb, vcb = _nsa_prep(proj, qk_gain, w_ck, w_cv)
    yb = _nsa_prompt_attn(tab, bias_tiles, proj, qn, kcb, vcb, ksb, vsb, kwb, vwb, b, t_len)
    y = _pmm(jnp.concatenate([ya, yb.reshape(b, t_len, -1).astype(ya.dtype)], axis=-1), w_out)
    nw = min(NSA_WIN, t_len)
    hd = lambda t: t.reshape(b, t_len, NSA_G, NSA_D)
    kv = lambda i: hd(p3[..., EV_KV + i * NSA_KV:EV_KV + (i + 1) * NSA_KV])
    return y, (kv(0), kv(1), hd(ksn), kv(3), hd(kwn)[:, -nw:], kv(5)[:, -nw:], conv_new, s)


def _even_sample(xn, w_in, w_out, conv_w, a_log, dt_bias, gnorm, qk_gain, w_ck, w_cv, tab,
                 pool_kc, pool_vc, pool_ks, pool_vs, buf_kw, buf_vw, conv_st, s_st, page_table):
    gdn_parts, (q, kc, vc, ks, vs, kw, vw, ng) = _even_inputs(xn, w_in, qk_gain)
    ya, conv_new, s = _gdn(gdn_parts, conv_w, a_log, dt_bias, gnorm, conv_st, s_st)
    o3, kw_buf, vw_buf = _nsa_sample(q, kc, vc, ks, vs, kw, vw, pool_kc, pool_vc, pool_ks, pool_vs,
                                     buf_kw, buf_vw, page_table, w_ck, w_cv, qk_gain[1], tab)
    y = _pmm(jnp.concatenate([ya, _nsa_combine(o3, ng)], axis=-1), w_out)
    return y, (kc, vc, ks, vs, kw_buf, vw_buf, conv_new, s)


def _mlstm_chunked(q, k, v, ig, lf, c0, n0, m0):
    b, t_len, h, _ = q.shape
    c = min(ML_CHUNK, t_len)
    pad = (-t_len) % c
    incl = jnp.tril(jnp.ones((c, c), bool), 0)

    def body(carry, inp):
        cm, nv, m = carry
        qc, kc, vc, ic, fc = inp
        fcum = jnp.cumsum(fc, axis=-1)
        a = ic - fcum
        mt = fcum + jnp.maximum(m[..., None], lax.cummax(a, axis=2))
        dmat = jnp.exp(jnp.where(incl, a[..., None, :] + (fcum - mt)[..., :, None], -jnp.inf))
        dec0 = jnp.exp(fcum + m[..., None] - mt)
        s = jnp.einsum('bhid,bhjd->bhij', qc, kc) * dmat
        num = dec0[..., None] * jnp.einsum('bhid,bhde->bhie', qc, cm) + jnp.einsum('bhij,bhje->bhie', s, vc)
        den = dec0 * jnp.einsum('bhid,bhd->bhi', qc, nv) + s.sum(axis=-1)
        hc = num / jnp.maximum(jnp.abs(den), jnp.exp(-mt))[..., None]
        m_end = mt[..., -1]
        w = jnp.exp(a + (fcum[..., -1] - m_end)[..., None])
        dc = jnp.exp(fcum[..., -1] + m - m_end)
        cm = dc[..., None, None] * cm + jnp.einsum('bhj,bhjd,bhje->bhde', w, kc, vc)
        nv = dc[..., None] * nv + jnp.einsum('bhj,bhjd->bhd', w, kc)
        return (cm, nv, m_end), hc

    f32 = jnp.float32
    (cm, nv, m), hs = lax.scan(body, (c0.astype(f32), n0.astype(f32), m0.astype(f32)),
                               (_chunks(q, c, pad), _chunks(k, c, pad), _chunks(v, c, pad),
                                _chunks(ig, c, pad, NEG), _chunks(lf, c, pad)))
    return _unchunk(hs, t_len).astype(v.dtype), cm.astype(c0.dtype), nv.astype(n0.dtype), m.astype(m0.dtype)


def _odd(xn, w_in, w_out, b_if, mnorm, c0, n0, m0):
    b, t_len, _ = xn.shape
    q, k, v, o, ig, fg = _split(_pmm(xn, w_in), ODD_WIDTHS)
    q = q.reshape(b, t_len, ML_H, ML_DK) * (ML_DK ** -0.5)
    k = k.reshape(b, t_len, ML_H, ML_DK)
    v = v.reshape(b, t_len, ML_H, ML_DV)
    bf = b_if.astype(jnp.float32)
    ig = ig.astype(jnp.float32) + bf[:ML_H]
    lf = jax.nn.log_sigmoid(fg.astype(jnp.float32) + bf[ML_H:])
    hc, cm, nv, m = _mlstm_chunked(q, k, v, ig, lf, c0, n0, m0)
    hc = _rms(hc, mnorm.reshape(ML_H, ML_DV)).reshape(b, t_len, ODD_MIX)
    return _pmm(hc * jax.nn.sigmoid(o), w_out), (cm, nv, m)


def _rms_cast_kernel(x_ref, g_ref, o_ref):
    x = x_ref[...]
    y = x * lax.rsqrt(jnp.mean(x * x, axis=-1, keepdims=True) + RMS_EPS) * g_ref[...]
    o_ref[...] = y.astype(o_ref.dtype)


def _rms_cast(h, g):
    m, d = h.shape
    tm = _pick(m, (256, 128))
    return pl.pallas_call(
        _rms_cast_kernel,
        grid=(m // tm,),
        in_specs=[pl.BlockSpec((tm, d), lambda i: (i, 0)), pl.BlockSpec((1, d), lambda i: (0, 0))],
        out_specs=pl.BlockSpec((tm, d), lambda i: (i, 0)),
        out_shape=jax.ShapeDtypeStruct((m, d), MXU_DTYPE),
        compiler_params=pltpu.CompilerParams(dimension_semantics=("parallel",),
                                             vmem_limit_bytes=VMEM_LIMIT_BYTES),
    )(h, g.reshape(1, d).astype(jnp.float32))


def _mm2_kernel(xa_ref, xb_ref, w_ref, r_ref, o_ref):
    ka = xa_ref.shape[1]
    y = jnp.dot(xa_ref[...], w_ref[:ka, :].astype(MXU_DTYPE), preferred_element_type=jnp.float32)
    y = y + jnp.dot(xb_ref[...], w_ref[ka:, :].astype(MXU_DTYPE), preferred_element_type=jnp.float32)
    o_ref[...] = y + r_ref[...]


def _matmul2_res(xa, xb, w, res, layer, tiles=None):
    m, ka = xa.shape
    kb = xb.shape[1]
    n = w.shape[-1]
    tm = _pick(m, (tiles[0],) if tiles else (512, 256, 128))
    tn = _pick(n, (tiles[1],) if tiles else (512, 256, 128))
    return pl.pallas_call(
        _mm2_kernel,
        grid=(m // tm, n // tn),
        in_specs=[pl.BlockSpec((tm, ka), lambda i, j: (i, 0)), pl.BlockSpec((tm, kb), lambda i, j: (i, 0)),
                  pl.BlockSpec((None, ka + kb, tn), lambda i, j: (layer, 0, j)),
                  pl.BlockSpec((tm, tn), lambda i, j: (i, j))],
        out_specs=pl.BlockSpec((tm, tn), lambda i, j: (i, j)),
        out_shape=jax.ShapeDtypeStruct((m, n), jnp.float32),
        compiler_params=pltpu.CompilerParams(dimension_semantics=("parallel", "parallel"),
                                             vmem_limit_bytes=VMEM_LIMIT_BYTES),
    )(xa, xb, w, res)


LAYER_TILES = (
    dict(inp=(2048, 640), out=(2048, 512), mlp=(2048, 512)),
    dict(inp=(2048, 896), out=(2048, 512), mlp=(2048, 512)),
    dict(inp=(2048, 640), out=(2048, 512), mlp=(1024, 1024)),
    dict(inp=(2048, 896), out=(2048, 512), mlp=(2048, 512)),
)


def _mlp_block(h, g, w1, w2, layer, tiles):
    mid = _matmul(_rms_cast(h, g), w1, act='relu2', out_dtype=MXU_DTYPE, tiles=tiles, layer=layer)
    return _matmul(mid, w2, res=h, tiles=tiles, layer=layer)


def _pad_time(x, t_pad):
    b, d = x.shape
    return jnp.zeros((b, t_pad, d), x.dtype).at[:, 0, :].set(x).reshape(b * t_pad, d)


def _even_layer(h, g_mix, w_in, w_out, conv_w, a_log, dt_bias, gnorm, qk_gain, w_ck, w_cv, tab, bias_tiles,
                batch, t_len, conv_st, s_st, decode, tiles):
    f32 = jnp.float32
    proj = _matmul(_rms_cast(h, g_mix), w_in, tiles=tiles['inp'])
    t_pad = t_len if decode is None else GDN_CHUNK
    pp = proj if decode is None else _pad_time(proj, t_pad)
    first = jnp.pad(conv_st.astype(f32), ((0, 0), (8 - (GDN_CONV - 1), 0), (0, 0)))
    q, k, v, gb = _gdn_prep(pp, first, pp, EV_SMALL // 128, conv_w, a_log, dt_bias, batch, t_pad, t_len)
    ya, s_new = _gdn_chunks(q, k, v, gb, pp, EV_Z // GDN_V, s_st, gnorm, batch, t_pad)
    p3 = proj.reshape(batch, t_len, EV_N)
    xin_tail = jnp.concatenate([conv_st.astype(f32), p3[..., :GDN_CONV_CH]], axis=1)[:, -(GDN_CONV - 1):]
    hd = lambda t: t.reshape(batch, t_len, NSA_G, NSA_D)
    kv = lambda i: hd(p3[..., EV_KV + i * NSA_KV:EV_KV + (i + 1) * NSA_KV])
    if decode is None:
        qn, ksn, kwn, ksb, vsb, kwb, vwb, kcb, vcb = _nsa_prep(proj, qk_gain, w_ck, w_cv)
        yb = _nsa_prompt_attn(tab, bias_tiles, proj, qn, kcb, vcb, ksb, vsb, kwb, vwb, batch, t_len)
        nw = min(NSA_WIN, t_len)
        kw_out, vw_out = hd(kwn)[:, -nw:], kv(5)[:, -nw:]
        ks_out = hd(ksn)
    else:
        ya = ya.reshape(batch, t_pad, GDN_V)[:, 0]
        yb, ksn, kwn = _nsa_decode(proj, tab, qk_gain, w_ck, w_cv, decode, batch)
        yb = yb.astype(MXU_DTYPE)
        ks_out = hd(ksn)
        kw_out = jnp.concatenate([decode['kw'][:, 1:], hd(kwn)], axis=1)
        vw_out = jnp.concatenate([decode['vw'][:, 1:], kv(5)], axis=1)
    h = _matmul2_res(ya, yb, w_out[0], h, w_out[1], tiles=tiles['out'])
    return h, (kv(0), kv(1), ks_out, kv(3), kw_out, vw_out, xin_tail, s_new)


def _odd_layer(h, g_mix, w_in, w_out, b_if, mnorm, batch, t_len, c0, n0, m0, decode, tiles):
    proj = _matmul(_rms_cast(h, g_mix), w_in, tiles=tiles['inp'])
    t_pad = t_len if not decode else ML_CHUNK
    pp = proj if not decode else _pad_time(proj, t_pad)
    y, c, n, m = _mlstm_chunks(pp, b_if, mnorm, c0, n0, m0, batch, t_pad, t_len)
    if decode:
        y = y.reshape(batch, t_pad, ODD_MIX)[:, 0]
    return _matmul(y, w_out[0], res=h, tiles=tiles['out'], layer=w_out[1]), (c, n, m)


def kernel(x_prompt, x_sample, cache_kc, cache_vc, cache_ks, cache_vs, cache_kw, cache_vw,
           state_gdn_conv, state_gdn_s, state_ml_c, state_ml_n, state_ml_m, page_table,
           rel_bias, norm_mix, norm_mlp, w_in_even, w_out_even, gdn_conv_w, gdn_a_log,
           gdn_dt_bias, gdn_norm_w, nsa_qk_gain, nsa_w_ck, nsa_w_cv, w_in_odd, w_out_odd,
           ml_b_if, ml_norm_w, w_ff1, w_ff2):
    f32 = jnp.float32
    bp, tp, d = x_prompt.shape
    bs, ts, _ = x_sample.shape
    hp = x_prompt.reshape(bp * tp, d)
    hs = x_sample.reshape(bs * ts, d)
    bias_tiles = _bias_tiles(rel_bias)
    ev_p, ev_s, od_p, od_s = [], [], [], []
    for l in range(DEPTH):
        j = l // 2
        tiles = LAYER_TILES[l]
        if l % 2 == 0:
            w_in = _permute_even_w(w_in_even[j])
            args = (w_in, (w_out_even, j), gdn_conv_w[j], gdn_a_log[j], gdn_dt_bias[j], gdn_norm_w[j],
                    nsa_qk_gain[j], nsa_w_ck[j], nsa_w_cv[j], rel_bias, bias_tiles)
            hp, stp = _even_layer(hp, norm_mix[l], *args, bp, tp,
                                  jnp.zeros((bp, GDN_CONV - 1, GDN_CONV_CH), f32),
                                  jnp.zeros((bp, GDN_VH, GDN_DK, GDN_DV), f32), None, tiles)
            caches = dict(kc=cache_kc, vc=cache_vc, ks=cache_ks, vs=cache_vs, layer=j, kw=cache_kw[j],
                          vw=cache_vw[j], page_table=page_table)
            hs, sts = _even_layer(hs, norm_mix[l], *args, bs, ts, state_gdn_conv[j], state_gdn_s[j], caches,
                                  tiles)
            ev_p.append(stp)
            ev_s.append(sts)
        else:
            w_in = _permute_odd_w(w_in_odd[j])
            hp, stp = _odd_layer(hp, norm_mix[l], w_in, (w_out_odd, j), ml_b_if[j], ml_norm_w[j], bp, tp,
                                 jnp.zeros((bp, ML_H, ML_DK, ML_DV), f32), jnp.zeros((bp, ML_H, ML_DK), f32),
                                 jnp.zeros((bp, ML_H), f32), False, tiles)
            hs, sts = _odd_layer(hs, norm_mix[l], w_in, (w_out_odd, j), ml_b_if[j], ml_norm_w[j], bs, ts,
                                 state_ml_c[j], state_ml_n[j], state_ml_m[j], True, tiles)
            od_p.append(stp)
            od_s.append(sts)
        hp = _mlp_block(hp, norm_mlp[l], w_ff1, w_ff2, l, tiles['mlp'])
        hs = _mlp_block(hs, norm_mlp[l], w_ff1, w_ff2, l, tiles['mlp'])

    kc_p, vc_p, ks_p, vs_p, kw_p, vw_p, conv_p, gdn_p = [jnp.stack(a) for a in zip(*ev_p)]
    kc_s, vc_s, ks_s, vs_s, kw_s, vw_s, conv_s, gdn_s = [jnp.stack(a) for a in zip(*ev_s)]
    mlc_p, mln_p, mlm_p = [jnp.stack(a) for a in zip(*od_p)]
    mlc_s, mln_s, mlm_s = [jnp.stack(a) for a in zip(*od_s)]
    return (hp.reshape(bp, tp, d), hs.reshape(bs, ts, d), kc_p, kc_s, vc_p, vc_s, ks_p, ks_s, vs_p, vs_s,
            kw_p, kw_s, vw_p, vw_s, conv_p, conv_s, gdn_p, gdn_s, mlc_p, mlc_s, mln_p, mln_s, mlm_p, mlm_s)


def _kernel_old(x_prompt, x_sample, cache_kc, cache_vc, cache_ks, cache_vs, cache_kw, cache_vw,
           state_gdn_conv, state_gdn_s, state_ml_c, state_ml_n, state_ml_m, page_table,
           rel_bias, norm_mix, norm_mlp, w_in_even, w_out_even, gdn_conv_w, gdn_a_log,
           gdn_dt_bias, gdn_norm_w, nsa_qk_gain, nsa_w_ck, nsa_w_cv, w_in_odd, w_out_odd,
           ml_b_if, ml_norm_w, w_ff1, w_ff2):
    hp, hs = x_prompt, x_sample
    bias_tiles = _bias_tiles(rel_bias)
    ev_p, ev_s, od_p, od_s = [], [], [], []
    for l in range(DEPTH):
        j = l // 2
        xp = _rms(hp, norm_mix[l])
        xs = _rms(hs, norm_mix[l])
        if l % 2 == 0:
            yp, stp = _even_prompt(xp, w_in_even[j], w_out_even[j], gdn_conv_w[j], gdn_a_log[j],
                                   gdn_dt_bias[j], gdn_norm_w[j], nsa_qk_gain[j], nsa_w_ck[j],
                                   nsa_w_cv[j], rel_bias, bias_tiles)
            ys, sts = _even_sample(xs, w_in_even[j], w_out_even[j], gdn_conv_w[j], gdn_a_log[j],
                                   gdn_dt_bias[j], gdn_norm_w[j], nsa_qk_gain[j], nsa_w_ck[j],
                                   nsa_w_cv[j], rel_bias, cache_kc[j], cache_vc[j], cache_ks[j],
                                   cache_vs[j], cache_kw[j], cache_vw[j], state_gdn_conv[j],
                                   state_gdn_s[j], page_table)
            ev_p.append(stp)
            ev_s.append(sts)
        else:
            bp = hp.shape[0]
            c0 = jnp.zeros((bp, ML_H, ML_DK, ML_DV), hp.dtype)
            n0 = jnp.zeros((bp, ML_H, ML_DK), hp.dtype)
            m0 = jnp.zeros((bp, ML_H), hp.dtype)
            yp, stp = _odd(xp, w_in_odd[j], w_out_odd[j], ml_b_if[j], ml_norm_w[j], c0, n0, m0)
            ys, sts = _odd(xs, w_in_odd[j], w_out_odd[j], ml_b_if[j], ml_norm_w[j],
                           state_ml_c[j], state_ml_n[j], state_ml_m[j])
            od_p.append(stp)
            od_s.append(sts)
        hp = hp + yp
        hs = hs + ys
        hp = hp + _mlp(_rms(hp, norm_mlp[l]), w_ff1[l], w_ff2[l])
        hs = hs + _mlp(_rms(hs, norm_mlp[l]), w_ff1[l], w_ff2[l])

    kc_p, vc_p, ks_p, vs_p, kw_p, vw_p, conv_p, gdn_p = [jnp.stack(a) for a in zip(*ev_p)]
    kc_s, vc_s, ks_s, vs_s, kw_s, vw_s, conv_s, gdn_s = [jnp.stack(a) for a in zip(*ev_s)]
    mlc_p, mln_p, mlm_p = [jnp.stack(a) for a in zip(*od_p)]
    mlc_s, mln_s, mlm_s = [jnp.stack(a) for a in zip(*od_s)]
    return (hp, hs, kc_p, kc_s, vc_p, vc_s, ks_p, ks_s, vs_p, vs_s, kw_p, kw_s, vw_p, vw_s,
            conv_p, conv_s, gdn_p, gdn_s, mlc_p, mlc_s, mln_p, mln_s, mlm_p, mlm_s)
```

```python
import functools
import math

import jax
import jax.numpy as jnp
import numpy as np
from jax import lax
from jax.experimental import pallas as pl
from jax.experimental.pallas import tpu as pltpu

DEPTH = 4
DEC_SEQ = 1
PAST_LEN = 16384
PAGE_SIZE = 128
RMS_EPS = 1e-6
NEG = -1e30

GDN_KH = 4
GDN_VH = 8
GDN_DK = 128
GDN_DV = 128
GDN_CONV = 4
GDN_CHUNK = 64
GDN_QK = GDN_KH * GDN_DK
GDN_V = GDN_VH * GDN_DV
GDN_CONV_CH = 2 * GDN_QK + GDN_V

NSA_H = 8
NSA_G = 2
NSA_HG = NSA_H // NSA_G
NSA_D = 128
NSA_BLK = 64
NSA_TOPK = 15
NSA_WIN = 512
NSA_QB = 128
NSA_KV = NSA_G * NSA_D

REL_BUCKETS = 32
REL_EXACT = 16
REL_MAX_DIST = 2048

ML_H = 8
ML_DK = 128
ML_DV = 256
ML_CHUNK = 64

ODD_MIX = ML_H * ML_DV

VMEM_LIMIT_BYTES = 56 * 1024 * 1024
MXU_DTYPE = jnp.bfloat16


def _mm_kernel(x_ref, w_ref, *rest, nk, act, has_res):
    rest = list(rest)
    r_ref = rest.pop(0) if has_res else None
    o_ref = rest.pop(0)

    def finish(y):
        if act == 'relu2':
            y = jnp.square(jnp.maximum(y, 0.0))
        if has_res:
            y = y + r_ref[...]
        o_ref[...] = y.astype(o_ref.dtype)

    d = jnp.dot(x_ref[...], w_ref[...].astype(MXU_DTYPE), preferred_element_type=jnp.float32)
    if nk == 1:
        finish(d)
        return
    acc_ref = rest.pop(0)
    k = pl.program_id(2)

    @pl.when(k == 0)
    def _():
        acc_ref[...] = d

    @pl.when(k > 0)
    def _():
        acc_ref[...] += d

    @pl.when(k == nk - 1)
    def _():
        finish(acc_ref[...])


def _pick(n, prefs):
    for p in prefs:
        if n % p == 0:
            return p
    return n


def _matmul(x, w, *, act=None, res=None, out_dtype=jnp.float32, tiles=None, layer=None):
    m, kdim = x.shape
    n = w.shape[-1]
    tm = _pick(m, (tiles[0],) if tiles else (512, 256, 128))
    tn = _pick(n, (tiles[1],) if tiles else (512, 896, 384, 256, 128))
    tk = _pick(kdim, (2048, 1024, 512))
    nk = kdim // tk
    if layer is None:
        w_spec = pl.BlockSpec((tk, tn), lambda i, j, k: (k, j))
    else:
        w_spec = pl.BlockSpec((None, tk, tn), lambda i, j, k: (layer, k, j))
    in_specs = [pl.BlockSpec((tm, tk), lambda i, j, k: (i, k)), w_spec]
    args = [x, w]
    if res is not None:
        in_specs.append(pl.BlockSpec((tm, tn), lambda i, j, k: (i, j)))
        args.append(res)
    return pl.pallas_call(
        functools.partial(_mm_kernel, nk=nk, act=act, has_res=res is not None),
        grid=(m // tm, n // tn, nk),
        in_specs=in_specs,
        out_specs=pl.BlockSpec((tm, tn), lambda i, j, k: (i, j)),
        out_shape=jax.ShapeDtypeStruct((m, n), out_dtype),
        scratch_shapes=[pltpu.VMEM((tm, tn), jnp.float32)] if nk > 1 else [],
        compiler_params=pltpu.CompilerParams(
            dimension_semantics=("parallel", "parallel", "arbitrary"),
            vmem_limit_bytes=VMEM_LIMIT_BYTES),
    )(*args)


EV_CONV = 0
EV_Z = GDN_CONV_CH
EV_NQ = EV_Z + GDN_V
EV_KV = EV_NQ + NSA_H * NSA_D
EV_SMALL = EV_KV + 6 * NSA_KV
EV_N = EV_SMALL + 128
EV_ORIG_SMALL = 2 * GDN_QK + 2 * GDN_V
EV_ORIG_N = EV_ORIG_SMALL + 2 * GDN_VH + NSA_H * NSA_D + 6 * NSA_KV + 3 * NSA_H


def _permute_even_w(w_in):
    ab = w_in[:, EV_ORIG_SMALL:EV_ORIG_SMALL + 2 * GDN_VH]
    big = w_in[:, EV_ORIG_SMALL + 2 * GDN_VH:EV_ORIG_N - 3 * NSA_H]
    ng = w_in[:, EV_ORIG_N - 3 * NSA_H:]
    pad = jnp.zeros((w_in.shape[0], EV_N - EV_ORIG_N), w_in.dtype)
    return jnp.concatenate([w_in[:, :EV_ORIG_SMALL], big, ab, ng, pad], axis=1).astype(MXU_DTYPE)


def _bucket_thresholds():
    n = np.arange(0, 4 * REL_MAX_DIST)
    nf = np.maximum(n, REL_EXACT).astype(np.float32)
    large = REL_EXACT + (np.log(nf / np.float32(REL_EXACT)) / np.float32(math.log(REL_MAX_DIST / REL_EXACT))
                         * np.float32(REL_BUCKETS - REL_EXACT)).astype(np.int32)
    b = np.where(n < REL_EXACT, n, np.minimum(large, REL_BUCKETS - 1))
    return tuple(int(np.argmax(b >= k)) for k in range(REL_BUCKETS))


BUCKET_THR = _bucket_thresholds()
BIAS_TILES = 14
assert BUCKET_THR[-1] <= (BIAS_TILES - 1) * 128 - 127


def _bias_of_dist(dist, tab_ref, head):
    val = jnp.full(dist.shape, tab_ref[0, head], jnp.float32)
    for k in range(1, REL_BUCKETS):
        val = jnp.where(dist >= BUCKET_THR[k], tab_ref[k, head], val)
    return val


def _bias_tiles_kernel(tab_ref, o_ref):
    h = pl.program_id(0)
    d = pl.program_id(1)
    i = lax.broadcasted_iota(jnp.int32, (128, 128), 1)
    j = lax.broadcasted_iota(jnp.int32, (128, 128), 0)
    dist = d * 128 + i - j
    val = jnp.full(dist.shape, tab_ref[0, h], jnp.float32)
    for k in range(1, REL_BUCKETS):
        val = jnp.where(dist >= BUCKET_THR[k], tab_ref[k, h], val)
    o_ref[0, 0] = val


def _bias_tiles(tab):
    return pl.pallas_call(
        _bias_tiles_kernel,
        grid=(NSA_H, BIAS_TILES),
        in_specs=[pl.BlockSpec(memory_space=pltpu.SMEM)],
        out_specs=pl.BlockSpec((1, 1, 128, 128), lambda h, d: (h, d, 0, 0)),
        out_shape=jax.ShapeDtypeStruct((NSA_H, BIAS_TILES, 128, 128), jnp.float32),
    )(tab.astype(jnp.float32))


def _group_rms(x, gain_row, scale=1.0):
    ms = jnp.mean(x * x, axis=-1, keepdims=True)
    y = x * lax.rsqrt(ms + RMS_EPS) * gain_row
    return y * scale if scale != 1.0 else y


def _nsa_prep_kernel(nq_ref, kc_ref, vc_ref, ks_ref, vs_ref, kw_ref, vw_ref, gain_ref, wck_ref, wcv_ref,
                     qn_ref, ksn_ref, kwn_ref, ksb_ref, vst_ref, kwb_ref, vwt_ref, kcb_ref, vcb_ref):
    tm = nq_ref.shape[0]
    g0 = gain_ref[0:1, :]
    g1 = gain_ref[1:2, :]
    g2 = gain_ref[2:3, :]
    g3 = gain_ref[3:4, :]
    for h in range(NSA_H):
        sl = slice(h * NSA_D, (h + 1) * NSA_D)
        qn_ref[:, sl] = _group_rms(nq_ref[:, sl], g0, NSA_D ** -0.5).astype(qn_ref.dtype)
    for g in range(NSA_G):
        sl = slice(g * NSA_D, (g + 1) * NSA_D)
        ksn = _group_rms(ks_ref[:, sl], g2)
        kwn = _group_rms(kw_ref[:, sl], g3)
        ksn_ref[:, sl] = ksn
        kwn_ref[:, sl] = kwn
        ksb_ref[:, sl] = ksn.astype(ksb_ref.dtype)
        kwb_ref[:, sl] = kwn.astype(kwb_ref.dtype)
    for src, dst in ((vs_ref, vst_ref), (vw_ref, vwt_ref)):
        vt = src[...].T
        tk = dst.shape[2]
        for c in range(tm // tk):
            dst[c] = vt[:, c * tk:(c + 1) * tk].astype(dst.dtype)
    nblk = tm // NSA_BLK
    kc3 = kc_ref[...].reshape(nblk, NSA_BLK, NSA_KV)
    vc3 = vc_ref[...].reshape(nblk, NSA_BLK, NSA_KV)
    kcb = jnp.sum(kc3 * wck_ref[...][None], axis=1)
    vcb = jnp.sum(vc3 * wcv_ref[...][None], axis=1)
    for g in range(NSA_G):
        sl = slice(g * NSA_D, (g + 1) * NSA_D)
        kcb_ref[:, sl] = _group_rms(kcb[:, sl], g1)
    vcb_ref[...] = vcb


def _nsa_prep(proj, gain, w_ck, w_cv):
    m = proj.shape[0]
    tm = 512
    kvb = EV_KV // NSA_KV
    f32, bf16 = jnp.float32, MXU_DTYPE

    def kv_spec(i):
        return pl.BlockSpec((tm, NSA_KV), lambda r, i=i: (r, kvb + i))

    row = lambda w: pl.BlockSpec((tm, w), lambda r: (r, 0))
    slabs = lambda tk: pl.BlockSpec((tm // tk, NSA_KV, tk), lambda r: (r, 0, 0))
    full = lambda a: pl.BlockSpec(a.shape, lambda r: (0,) * a.ndim)
    wck = w_ck.reshape(NSA_BLK, NSA_KV)
    wcv = w_cv.reshape(NSA_BLK, NSA_KV)
    outs = pl.pallas_call(
        _nsa_prep_kernel,
        grid=(m // tm,),
        in_specs=[pl.BlockSpec((tm, NSA_H * NSA_D), lambda r: (r, EV_NQ // (NSA_H * NSA_D)))]
                 + [kv_spec(i) for i in range(6)] + [full(gain), full(wck), full(wcv)],
        out_specs=[row(NSA_H * NSA_D), row(NSA_KV), row(NSA_KV), row(NSA_KV), slabs(NSA_TK), row(NSA_KV),
                   slabs(NSA_QB)] + [pl.BlockSpec((tm // NSA_BLK, NSA_KV), lambda r: (r, 0))] * 2,
        out_shape=[jax.ShapeDtypeStruct((m, NSA_H * NSA_D), bf16),
                   jax.ShapeDtypeStruct((m, NSA_KV), f32), jax.ShapeDtypeStruct((m, NSA_KV), f32),
                   jax.ShapeDtypeStruct((m, NSA_KV), bf16), jax.ShapeDtypeStruct((m // NSA_TK, NSA_KV, NSA_TK), bf16),
                   jax.ShapeDtypeStruct((m, NSA_KV), bf16), jax.ShapeDtypeStruct((m // NSA_QB, NSA_KV, NSA_QB), bf16)]
                  + [jax.ShapeDtypeStruct((m // NSA_BLK, NSA_KV), f32)] * 2,
        compiler_params=pltpu.CompilerParams(dimension_semantics=("parallel",),
                                             vmem_limit_bytes=VMEM_LIMIT_BYTES),
    )(proj, proj, proj, proj, proj, proj, proj, gain, wck, wcv)
    return outs


NSA_TK = 256


def _pad_rows(x, rows):
    if x.shape[0] == rows:
        return x
    return jnp.concatenate([x, jnp.zeros((rows - x.shape[0],) + x.shape[1:], x.dtype)], axis=0)


def _nsa_attn_kernel(tab_ref, q_ref, kcb_ref, vcb_ref, ks_ref, vst_ref, kw_ref, vwt_ref, bias_ref, ng_ref,
                     o_ref, m_scr, l_scr, acc_scr):
    f32, bf16 = jnp.float32, MXU_DTYPE
    qb = pl.program_id(1)
    nb = kcb_ref.shape[0]
    qn = NSA_QB
    gates = jax.nn.sigmoid(ng_ref[...])
    tsub = NSA_TK // qn
    shift = NSA_BLK.bit_length() - 1

    heads = range(NSA_H)
    gls = [slice(g * NSA_D, (g + 1) * NSA_D) for g in range(NSA_G)]
    qs = [q_ref[:, hh * NSA_D:(hh + 1) * NSA_D] for hh in heads]
    qts = [q.astype(f32).T.astype(bf16) for q in qs]

    blk = lax.broadcasted_iota(jnp.int32, (nb, qn), 0)
    blkf = blk.astype(f32)
    qpos = qb * qn + lax.broadcasted_iota(jnp.int32, (nb, qn), 1)
    dist_c = qpos - (blk * NSA_BLK + NSA_BLK - 1)
    ok_c = dist_c >= 0
    cur = lax.shift_right_logical(qpos, shift)
    o_c, scores = [], []
    for g in range(NSA_G):
        q2 = jnp.concatenate(qs[g * NSA_HG:(g + 1) * NSA_HG], axis=0)
        kcb = kcb_ref[:, gls[g]].astype(bf16)
        vcb = _pad_rows(vcb_ref[:, gls[g]], qn).astype(bf16)
        lgt = lax.dot_general(kcb, q2, (((1,), (1,)), ((), ())), preferred_element_type=f32)
        score = jnp.zeros((nb, qn), f32)
        for h in range(NSA_HG):
            lg = lgt[:, h * qn:(h + 1) * qn] + _bias_of_dist(dist_c, tab_ref, g * NSA_HG + h)
            lg = jnp.where(ok_c, lg, NEG)
            mx = jnp.max(lg, axis=0, keepdims=True)
            e = jnp.exp(lg - mx)
            p = jnp.where(ok_c, e / jnp.sum(e, axis=0, keepdims=True), 0.0)
            score = score + p
            p_t = _pad_rows(p, qn).T.astype(bf16)
            o_c.append(jnp.dot(p_t, vcb, preferred_element_type=f32))
        scores.append(jnp.where(blk < cur, score, -1.0))

    sels = [jnp.zeros((nb, qn), f32) for _ in range(NSA_G)]
    for _ in range(min(NSA_TOPK, nb)):
        for g in range(NSA_G):
            mx = jnp.max(scores[g], axis=0, keepdims=True)
            first = jnp.min(jnp.where(scores[g] == mx, blkf, float(nb)), axis=0, keepdims=True)
            pick = blkf == first
            sels[g] = jnp.where(pick, jnp.where(mx >= 0.0, 1.0, 0.0), sels[g])
            scores[g] = jnp.where(pick, -2.0, scores[g])
    sel_ps = [_pad_rows(jnp.where(blk == cur, 1.0, s), qn).astype(bf16) for s in sels]

    def reset():
        m_scr[...] = jnp.full(m_scr.shape, NEG, f32)
        l_scr[...] = jnp.zeros(l_scr.shape, f32)
        acc_scr[...] = jnp.zeros(acc_scr.shape, f32)

    def update(hh, lg, vt):
        m_old = m_scr[hh]
        m_new = jnp.maximum(m_old, jnp.max(lg, axis=0, keepdims=True))
        p = jnp.exp(lg - m_new)
        alpha = jnp.exp(m_old - m_new)
        l_scr[hh] = alpha * l_scr[hh] + jnp.sum(p, axis=0, keepdims=True)
        acc_scr[hh] = alpha * acc_scr[hh] + jnp.dot(vt, p.astype(bf16), preferred_element_type=f32)
        m_scr[hh] = m_new

    def result(hh):
        return (acc_scr[hh] / l_scr[hh]).T

    reset()
    n_tiles = qb // tsub + 1
    tok = qb * qn + lax.broadcasted_iota(jnp.int32, (NSA_TK, qn), 1)
    key = lax.broadcasted_iota(jnp.int32, (NSA_TK, qn), 0)
    eblk = lax.broadcasted_iota(jnp.int32, (NSA_TK, qn), 1)

    def body(i, carry):
        kt = n_tiles - 1 - i
        k0 = pl.multiple_of(kt * NSA_TK, NSA_TK)
        expand = jnp.where(eblk == lax.shift_right_logical(k0 + key, shift), 1.0, 0.0).astype(bf16)
        causal = k0 + key <= tok
        for g in range(NSA_G):
            k = ks_ref[pl.ds(k0, NSA_TK), gls[g]]
            vt = vst_ref[kt, gls[g], :]
            member = jnp.dot(expand, sel_ps[g], preferred_element_type=f32)
            keep = jnp.where(causal, member, 0.0) > 0.5
            for h in range(NSA_HG):
                hh = g * NSA_HG + h
                tiles = [bias_ref[hh, jnp.clip(qb - (kt * tsub + j), 0, BIAS_TILES - 1)] for j in range(tsub)]
                lg = jnp.dot(k, qts[hh], preferred_element_type=f32) + jnp.concatenate(tiles, axis=0)
                update(hh, jnp.where(keep, lg, NEG), vt)
        return carry

    lax.fori_loop(0, n_tiles, body, 0)
    o_s = [result(hh) for hh in heads]

    reset()
    nwin = NSA_WIN // qn + 1
    wi = lax.broadcasted_iota(jnp.int32, (qn, qn), 1)
    wj = lax.broadcasted_iota(jnp.int32, (qn, qn), 0)
    for d in range(nwin):
        sub = qb - d
        subc = jnp.maximum(sub, 0)
        for g in range(NSA_G):
            k = kw_ref[pl.ds(pl.multiple_of(subc * qn, qn), qn), gls[g]]
            vt = vwt_ref[subc, gls[g], :]
            for h in range(NSA_HG):
                hh = g * NSA_HG + h
                x = jnp.dot(k, qts[hh], preferred_element_type=f32) + bias_ref[hh, d]
                if d == 0:
                    x = jnp.where(wj <= wi, x, NEG)
                else:
                    if d == nwin - 1:
                        x = jnp.where(wj > wi, x, NEG)
                    x = jnp.where(sub >= 0, x, NEG)
                update(hh, x, vt)
    o_w = [result(hh) for hh in heads]

    for hh in heads:
        c = 2 * GDN_VH + hh * 3
        y = o_c[hh] * gates[:, c:c + 1] + o_s[hh] * gates[:, c + 1:c + 2] + o_w[hh] * gates[:, c + 2:c + 3]
        o_ref[:, hh * NSA_D:(hh + 1) * NSA_D] = y.astype(o_ref.dtype)


def _nsa_prompt_attn(tab, bias_tiles, proj, qn, kcb, vcb, ksb, vst, kwb, vwt, batch, t_len):
    nq = t_len // NSA_QB
    nb = t_len // NSA_BLK
    hd = NSA_H * NSA_D
    per_b = lambda w: pl.BlockSpec((t_len, w), lambda b, i: (b, 0))
    tiles_b = lambda tk: pl.BlockSpec((t_len // tk, NSA_KV, tk), lambda b, i: (b, 0, 0))
    return pl.pallas_call(
        _nsa_attn_kernel,
        grid=(batch, nq),
        in_specs=[pl.BlockSpec(memory_space=pltpu.SMEM),
                  pl.BlockSpec((NSA_QB, hd), lambda b, i: (b * nq + i, 0)),
                  pl.BlockSpec((nb, NSA_KV), lambda b, i: (b, 0)),
                  pl.BlockSpec((nb, NSA_KV), lambda b, i: (b, 0)),
                  per_b(NSA_KV), tiles_b(NSA_TK), per_b(NSA_KV), tiles_b(NSA_QB),
                  pl.BlockSpec(bias_tiles.shape, lambda b, i: (0, 0, 0, 0)),
                  pl.BlockSpec((NSA_QB, 128), lambda b, i: (b * nq + i, EV_SMALL // 128 + 0))],
        out_specs=pl.BlockSpec((NSA_QB, hd), lambda b, i: (b * nq + i, 0)),
        out_shape=jax.ShapeDtypeStruct((batch * t_len, hd), MXU_DTYPE),
        scratch_shapes=[pltpu.VMEM((NSA_H, 1, NSA_QB), jnp.float32),
                        pltpu.VMEM((NSA_H, 1, NSA_QB), jnp.float32),
                        pltpu.VMEM((NSA_H, NSA_D, NSA_QB), jnp.float32)],
        compiler_params=pltpu.CompilerParams(dimension_semantics=("parallel", "arbitrary"),
                                             vmem_limit_bytes=VMEM_LIMIT_BYTES),
    )(tab.astype(jnp.float32), qn, kcb, vcb, ksb, vst, kwb, vwt, bias_tiles, proj)


def _mdot(a, b):
    return jnp.dot(a.astype(MXU_DTYPE), b.astype(MXU_DTYPE), preferred_element_type=jnp.float32)


def _mdot_nt(a, b):
    return lax.dot_general(a.astype(MXU_DTYPE), b.astype(MXU_DTYPE), (((1,), (1,)), ((), ())),
                           preferred_element_type=jnp.float32)


def _mdot_tn(a, b):
    return lax.dot_general(a.astype(MXU_DTYPE), b.astype(MXU_DTYPE), (((0,), (0,)), ((), ())),
                           preferred_element_type=jnp.float32)


def _shift_rows(x, prev, s):
    xs = pltpu.roll(x, s, axis=0)
    ps = pltpu.roll(prev, s, axis=0)
    row8 = lax.broadcasted_iota(jnp.int32, prev.shape, 0)
    head = jnp.where(row8 < s, ps, xs[0:8])
    return jnp.concatenate([head, xs[8:]], axis=0)


def _gdn_prep_kernel(x_ref, prev_ref, first_ref, ab_ref, cw_ref, alog_ref, dtb_ref,
                     q_ref, k_ref, v_ref, gb_ref, *, tiles_per_batch, t_valid):
    r = pl.program_id(0)
    tm = x_ref.shape[0]
    tile = r % tiles_per_batch
    prev = jnp.where(tile == 0, first_ref[0], prev_ref[...])
    x = x_ref[...]
    acc = x * cw_ref[GDN_CONV - 1:GDN_CONV, :]
    for s in range(1, GDN_CONV):
        acc = acc + _shift_rows(x, prev, s) * cw_ref[GDN_CONV - 1 - s:GDN_CONV - s, :]
    c = acc * jax.nn.sigmoid(acc)

    def l2(t):
        return t * lax.rsqrt(jnp.sum(t * t, axis=-1, keepdims=True) + 1e-6)

    for h in range(GDN_KH):
        sl = slice(h * GDN_DK, (h + 1) * GDN_DK)
        q_ref[:, sl] = l2(c[:, sl]) * (GDN_DK ** -0.5)
        k_ref[:, sl] = l2(c[:, GDN_QK + h * GDN_DK:GDN_QK + (h + 1) * GDN_DK])
    v_ref[...] = c[:, 2 * GDN_QK:]
    ab = ab_ref[...]
    z = ab + dtb_ref[...]
    softplus = jnp.maximum(z, 0.0) + jnp.log(1.0 + jnp.exp(-jnp.abs(z)))
    gate = -jnp.exp(alog_ref[...]) * softplus
    lane = lax.broadcasted_iota(jnp.int32, ab.shape, 1)
    pos = tile * tm + lax.broadcasted_iota(jnp.int32, ab.shape, 0)
    gb = jnp.where(lane < GDN_VH, gate, jnp.where(lane < 2 * GDN_VH, jax.nn.sigmoid(ab), 0.0))
    gb_ref[...] = jnp.where(pos < t_valid, gb, 0.0)


def _gdn_prep(xin, first, ab, ab_col, conv_w, a_log, dt_bias, batch, t_pad, t_valid):
    m = batch * t_pad
    tm = _pick(t_pad, (256, 128, 64))
    tpb = t_pad // tm
    f32 = jnp.float32
    lanes = lambda v: jnp.zeros((1, 128), f32).at[0, :GDN_VH].set(v.astype(f32))
    return pl.pallas_call(
        functools.partial(_gdn_prep_kernel, tiles_per_batch=tpb, t_valid=t_valid),
        grid=(m // tm,),
        in_specs=[pl.BlockSpec((tm, GDN_CONV_CH), lambda r: (r, 0)),
                  pl.BlockSpec((8, GDN_CONV_CH), lambda r: (jnp.maximum(r * (tm // 8) - 1, 0), 0)),
                  pl.BlockSpec((1, 8, GDN_CONV_CH), lambda r: (r // tpb, 0, 0)),
                  pl.BlockSpec((tm, 128), lambda r: (r, ab_col)),
                  pl.BlockSpec((GDN_CONV, GDN_CONV_CH), lambda r: (0, 0)),
                  pl.BlockSpec((1, 128), lambda r: (0, 0)),
                  pl.BlockSpec((1, 128), lambda r: (0, 0))],
        out_specs=[pl.BlockSpec((tm, GDN_QK), lambda r: (r, 0)),
                   pl.BlockSpec((tm, GDN_QK), lambda r: (r, 0)),
                   pl.BlockSpec((tm, GDN_V), lambda r: (r, 0)),
                   pl.BlockSpec((tm, 128), lambda r: (r, 0))],
        out_shape=[jax.ShapeDtypeStruct((m, GDN_QK), f32), jax.ShapeDtypeStruct((m, GDN_QK), f32),
                   jax.ShapeDtypeStruct((m, GDN_V), f32), jax.ShapeDtypeStruct((m, 128), f32)],
        compiler_params=pltpu.CompilerParams(dimension_semantics=("arbitrary",),
                                             vmem_limit_bytes=VMEM_LIMIT_BYTES),
    )(xin, xin, first, ab, conv_w.astype(f32), lanes(a_log), lanes(dt_bias))


def _cumsum_rows(x):
    row = lax.broadcasted_iota(jnp.int32, x.shape, 0)
    s = 1
    while s < x.shape[0]:
        x = x + jnp.where(row >= s, pltpu.roll(x, s, axis=0), 0.0)
        s *= 2
    return x


def _split3(a):
    hi = a.astype(jnp.bfloat16)
    lo = (a - hi.astype(jnp.float32)).astype(jnp.bfloat16)
    return hi, lo


def _dot3(a, b):
    if MXU_DTYPE != jnp.bfloat16:
        return jnp.dot(a, b, preferred_element_type=jnp.float32)
    ah, al = _split3(a)
    bh, bl = _split3(b)
    d = lambda x, y: jnp.dot(x, y, preferred_element_type=jnp.float32)
    return d(ah, bh) + (d(ah, bl) + d(al, bh))


def _unit_lower_inverses(lmats, row, col):
    eye = jnp.where(row == col, 1.0, 0.0)
    blk = lax.shift_right_logical(row, 4) == lax.shift_right_logical(col, 4)
    lds = [jnp.where(blk, l, 0.0) for l in lmats]
    xs = [eye - ld for ld in lds]
    ps = [_dot3(ld, ld) for ld in lds]
    for step in range(3):
        xs = [x + _dot3(x, p) for x, p in zip(xs, ps)]
        if step < 2:
            ps = [_dot3(p, p) for p in ps]
    size = 16
    while size < lmats[0].shape[0]:
        sh = size.bit_length() - 1
        inner = lax.shift_right_logical(row, sh) == lax.shift_right_logical(col, sh)
        outer = lax.shift_right_logical(row, sh + 1) == lax.shift_right_logical(col, sh + 1)
        coffs = [jnp.where(outer, jnp.where(inner, 0.0, l), 0.0) for l in lmats]
        ts = [_dot3(x, c) for x, c in zip(xs, coffs)]
        xs = [x - _dot3(t, x) for x, t in zip(xs, ts)]
        size *= 2
    return xs


def _gdn_chunk_kernel(q_ref, k_ref, v_ref, gb_ref, z_ref, s0_ref, nw_ref, y_ref, s_ref):
    f32 = jnp.float32
    cn = q_ref.shape[0]

    @pl.when(pl.program_id(1) == 0)
    def _():
        s_ref[...] = s0_ref[...]

    gb = gb_ref[...]
    gcum = _cumsum_rows(gb)
    gcum_t = _pad_rows(gcum, 128).T
    row = lax.broadcasted_iota(jnp.int32, (cn, cn), 0)
    col = lax.broadcasted_iota(jnp.int32, (cn, cn), 1)
    rep = GDN_VH // GDN_KH
    heads = range(GDN_VH)
    qs = [q_ref[:, kh * GDN_DK:(kh + 1) * GDN_DK] for kh in range(GDN_KH)]
    ks = [k_ref[:, kh * GDN_DK:(kh + 1) * GDN_DK] for kh in range(GDN_KH)]
    kk = [_mdot_nt(k, k) for k in ks]
    qk = [_mdot_nt(q, k) for q, k in zip(qs, ks)]
    gcols = [gcum[:, h:h + 1] for h in heads]
    diffs = [gcols[h] - gcum_t[h:h + 1, :cn] for h in heads]
    egs = [jnp.exp(g) for g in gcols]
    betas = [gb[:, GDN_VH + h:GDN_VH + h + 1] for h in heads]
    lmats = [betas[h] * kk[h // rep] * jnp.exp(jnp.where(col < row, diffs[h], -jnp.inf)) for h in heads]
    ainvs = _unit_lower_inverses(lmats, row, col)
    sols = [_dot3(ainvs[h], jnp.concatenate([betas[h] * v_ref[:, h * GDN_DV:(h + 1) * GDN_DV],
                                             (betas[h] * egs[h]) * ks[h // rep]], axis=1)) for h in heads]
    atts = [qk[h // rep] * jnp.exp(jnp.where(col <= row, diffs[h], -jnp.inf)) for h in heads]
    ss = [s_ref[0, h] for h in heads]
    us = [sols[h][:, :GDN_DV] - _mdot(sols[h][:, GDN_DV:], ss[h]) for h in heads]
    os_ = [egs[h] * _mdot(qs[h // rep], ss[h]) + _mdot(atts[h], us[h]) for h in heads]
    for h in heads:
        gl = gcum[cn - 1:cn, h:h + 1]
        s_ref[0, h] = jnp.exp(gl) * ss[h] + _mdot_tn(ks[h // rep] * jnp.exp(gl - gcols[h]), us[h])
    for h in heads:
        o = os_[h]
        on = o * lax.rsqrt(jnp.mean(o * o, axis=-1, keepdims=True) + RMS_EPS) * nw_ref[...]
        z = z_ref[:, h * GDN_DV:(h + 1) * GDN_DV]
        y_ref[:, h * GDN_DV:(h + 1) * GDN_DV] = (on * (z * jax.nn.sigmoid(z))).astype(y_ref.dtype)


def _gdn_chunks(q, k, v, gb, zsrc, z_col, s0, norm_w, batch, t_pad):
    cn = GDN_CHUNK
    nc = t_pad // cn
    rowblk = lambda w, c=0: pl.BlockSpec((cn, w), lambda b, i, c=c: (b * nc + i, c))
    st = pl.BlockSpec((1, GDN_VH, GDN_DK, GDN_DV), lambda b, i: (b, 0, 0, 0))
    return pl.pallas_call(
        _gdn_chunk_kernel,
        grid=(batch, nc),
        in_specs=[rowblk(GDN_QK), rowblk(GDN_QK), rowblk(GDN_V), rowblk(128), rowblk(GDN_V, z_col), st,
                  pl.BlockSpec((1, GDN_DV), lambda b, i: (0, 0))],
        out_specs=[rowblk(GDN_V), st],
        out_shape=[jax.ShapeDtypeStruct((batch * t_pad, GDN_V), MXU_DTYPE),
                   jax.ShapeDtypeStruct((batch, GDN_VH, GDN_DK, GDN_DV), jnp.float32)],
        compiler_params=pltpu.CompilerParams(dimension_semantics=("parallel", "arbitrary"),
                                             vmem_limit_bytes=VMEM_LIMIT_BYTES),
    )(q, k, v, gb, zsrc, s0.astype(jnp.float32), norm_w.reshape(1, GDN_DV).astype(jnp.float32))


OD_Q = 0
OD_K = ML_H * ML_DK
OD_V = 2 * ML_H * ML_DK
OD_O = OD_V + ML_H * ML_DV
OD_SMALL = OD_O + ML_H * ML_DV
OD_N = OD_SMALL + 128


def _cummax_rows(x):
    row = lax.broadcasted_iota(jnp.int32, x.shape, 0)
    s = 1
    while s < x.shape[0]:
        x = jnp.maximum(x, jnp.where(row >= s, pltpu.roll(x, s, axis=0), -jnp.inf))
        s *= 2
    return x


def _round_mxu(x):
    return x.astype(MXU_DTYPE).astype(jnp.float32)


def _mlstm_chunk_kernel(q_ref, k_ref, v_ref, og_ref, if_ref, bif_ref, nw_ref, c0_ref, n0_ref, m0_ref,
                        y_ref, c_ref, n_ref, m_ref, *, t_valid):
    f32 = jnp.float32
    cn = q_ref.shape[0]
    ci = pl.program_id(1)

    @pl.when(ci == 0)
    def _():
        c_ref[...] = c0_ref[...]
        n_ref[...] = n0_ref[...]
        m_ref[...] = m0_ref[...]

    pre = if_ref[...] + bif_ref[...]
    pos = ci * cn + lax.broadcasted_iota(jnp.int32, pre.shape, 0)
    live = pos < t_valid
    lf = jnp.where(live, jnp.minimum(pre, 0.0) - jnp.log(1.0 + jnp.exp(-jnp.abs(pre))), 0.0)
    fcum = _cumsum_rows(lf)
    a_all = jnp.where(live, pre, NEG) - pltpu.roll(fcum, 128 - ML_H, axis=1)
    amax = _cummax_rows(a_all)
    a_t = _pad_rows(a_all, 128).T
    row = lax.broadcasted_iota(jnp.int32, (cn, cn), 0)
    col = lax.broadcasted_iota(jnp.int32, (cn, cn), 1)
    m_all = m_ref[0]
    heads = range(ML_H)
    qs = [q_ref[:, h * ML_DK:(h + 1) * ML_DK] * (ML_DK ** -0.5) for h in heads]
    ks = [k_ref[:, h * ML_DK:(h + 1) * ML_DK] for h in heads]
    vs = [v_ref[:, h * ML_DV:(h + 1) * ML_DV] for h in heads]
    m_prevs = [m_all[:, h:h + 1] for h in heads]
    fcs = [fcum[:, ML_H + h:ML_H + h + 1] for h in heads]
    mts = [fcs[h] + jnp.maximum(m_prevs[h], amax[:, h:h + 1]) for h in heads]
    qks = [_mdot_nt(qs[h], ks[h]) for h in heads]
    ss = [qks[h] * jnp.exp(jnp.where(col <= row, a_t[h:h + 1, :cn] + (fcs[h] - mts[h]), -jnp.inf)) for h in heads]
    dec0s = [jnp.exp(fcs[h] + m_prevs[h] - mts[h]) for h in heads]
    cms = [c_ref[0, h] for h in heads]
    nvs = [n_ref[0, h:h + 1, :] for h in heads]
    nums = [dec0s[h] * _mdot(qs[h], cms[h]) + _mdot(ss[h], vs[h]) for h in heads]
    dens = [dec0s[h] * jnp.sum(_round_mxu(qs[h]) * _round_mxu(nvs[h]), axis=-1, keepdims=True)
            + jnp.sum(ss[h], axis=-1, keepdims=True) for h in heads]
    hcs = [nums[h] / jnp.maximum(jnp.abs(dens[h]), jnp.exp(-mts[h])) for h in heads]
    m_new = [mts[h][cn - 1:cn] for h in heads]
    for h in heads:
        f_end = fcs[h][cn - 1:cn]
        w = jnp.exp(a_all[:, h:h + 1] + (f_end - m_new[h]))
        dc = jnp.exp(f_end + m_prevs[h] - m_new[h])
        c_ref[0, h] = dc * cms[h] + _mdot_tn(w * ks[h], vs[h])
        n_ref[0, h:h + 1, :] = dc * nvs[h] + jnp.sum(_round_mxu(w) * _round_mxu(ks[h]), axis=0, keepdims=True)
    for h in heads:
        hc = hcs[h]
        hn = hc * lax.rsqrt(jnp.mean(hc * hc, axis=-1, keepdims=True) + RMS_EPS) \
            * nw_ref[:, h * ML_DV:(h + 1) * ML_DV]
        og = og_ref[:, h * ML_DV:(h + 1) * ML_DV]
        y_ref[:, h * ML_DV:(h + 1) * ML_DV] = (hn * jax.nn.sigmoid(og)).astype(y_ref.dtype)
    lane = lax.broadcasted_iota(jnp.int32, m_all.shape, 1)
    out = m_all
    for h in range(ML_H):
        out = jnp.where(lane == h, m_new[h], out)
    m_ref[0] = out


def _mlstm_chunks(proj, b_if, mnorm, c0, n0, m0, batch, t_pad, t_valid):
    cn = ML_CHUNK
    nc = t_pad // cn
    f32 = jnp.float32
    blk = lambda w, c: pl.BlockSpec((cn, w), lambda b, i, c=c: (b * nc + i, c))
    const = lambda shape: pl.BlockSpec(shape, lambda b, i: (0,) * len(shape))
    st_c = pl.BlockSpec((1, ML_H, ML_DK, ML_DV), lambda b, i: (b, 0, 0, 0))
    st_n = pl.BlockSpec((1, ML_H, ML_DK), lambda b, i: (b, 0, 0))
    st_m = pl.BlockSpec((1, 1, 128), lambda b, i: (b, 0, 0))
    bif = jnp.zeros((1, 128), f32).at[0, :2 * ML_H].set(b_if.astype(f32))
    m0p = jnp.zeros((batch, 1, 128), f32).at[:, 0, :ML_H].set(m0.astype(f32))
    hq, hv = ML_H * ML_DK, ML_H * ML_DV
    y, c, n, m = pl.pallas_call(
        functools.partial(_mlstm_chunk_kernel, t_valid=t_valid),
        grid=(batch, nc),
        in_specs=[blk(hq, OD_Q // hq), blk(hq, OD_K // hq), blk(hv, OD_V // hv), blk(hv, OD_O // hv),
                  blk(128, OD_SMALL // 128), const((1, 128)), const((1, hv)), st_c, st_n, st_m],
        out_specs=[blk(hv, 0), st_c, st_n, st_m],
        out_shape=[jax.ShapeDtypeStruct((batch * t_pad, hv), MXU_DTYPE),
                   jax.ShapeDtypeStruct((batch, ML_H, ML_DK, ML_DV), f32),
                   jax.ShapeDtypeStruct((batch, ML_H, ML_DK), f32),
                   jax.ShapeDtypeStruct((batch, 1, 128), f32)],
        compiler_params=pltpu.CompilerParams(dimension_semantics=("parallel", "arbitrary"),
                                             vmem_limit_bytes=VMEM_LIMIT_BYTES),
    )(proj, proj, proj, proj, proj, bif, mnorm.reshape(1, hv).astype(f32), c0.astype(f32), n0.astype(f32), m0p)
    return y, c, n, m[:, 0, :ML_H]


def _permute_odd_w(w_in):
    pad = jnp.zeros((w_in.shape[0], OD_N - w_in.shape[1]), w_in.dtype)
    return jnp.concatenate([w_in, pad], axis=1).astype(MXU_DTYPE)


DEC_PAGES_PER_STEP = 16


def _dec_compress_kernel(pt_ref, *refs, pp):
    kc_refs, vc_refs = refs[:pp], refs[pp:2 * pp]
    wck_ref, wcv_ref, g1_ref, kcb_ref, vcb_ref = refs[2 * pp:]
    per_page = PAGE_SIZE // NSA_BLK

    def compress(page_refs, w_ref, g):
        w = w_ref[:, g * NSA_D:(g + 1) * NSA_D]
        rows = [jnp.sum(r[pl.ds(g, PAGE_SIZE, stride=NSA_G), :].reshape(per_page, NSA_BLK, NSA_D) * w[None], axis=1)
                for r in page_refs]
        return jnp.concatenate(rows, axis=0)

    for g in range(NSA_G):
        sl = slice(g * NSA_D, (g + 1) * NSA_D)
        kcb_ref[0, :, sl] = _group_rms(compress(kc_refs, wck_ref, g), g1_ref[...])
        vcb_ref[0, :, sl] = compress(vc_refs, wcv_ref, g)


def _dec_compress(page_table, pool_kc, pool_vc, layer, w_ck, w_cv, g1):
    nseq, npages = page_table.shape
    pp = DEC_PAGES_PER_STEP
    per_page = PAGE_SIZE // NSA_BLK
    nphys = pool_kc.shape[1]
    base = layer * nphys
    kc = pool_kc.reshape(-1, NSA_D)
    vc = pool_vc.reshape(-1, NSA_D)
    page = lambda j: pl.BlockSpec((PAGE_SIZE * NSA_G, NSA_D), lambda b, i, pt, j=j: (base + pt[b, i * pp + j], 0))
    const = lambda shape: pl.BlockSpec(shape, lambda b, i, pt: (0,) * len(shape))
    out = pl.BlockSpec((1, pp * per_page, NSA_KV), lambda b, i, pt: (b, i, 0))
    nb = npages * per_page
    return pl.pallas_call(
        functools.partial(_dec_compress_kernel, pp=pp),
        grid_spec=pltpu.PrefetchScalarGridSpec(
            num_scalar_prefetch=1, grid=(nseq, npages // pp),
            in_specs=[page(j) for j in range(pp)] * 2 + [const((NSA_BLK, NSA_KV))] * 2 + [const((1, NSA_D))],
            out_specs=[out, out]),
        out_shape=[jax.ShapeDtypeStruct((nseq, nb, NSA_KV), jnp.float32)] * 2,
        compiler_params=pltpu.CompilerParams(dimension_semantics=("parallel", "arbitrary"),
                                             vmem_limit_bytes=VMEM_LIMIT_BYTES),
    )(page_table, *([kc] * pp), *([vc] * pp), w_ck.reshape(NSA_BLK, NSA_KV), w_cv.reshape(NSA_BLK, NSA_KV),
      g1.reshape(1, NSA_D))


def _rows_by_head(head_row, fn, nheads):
    out = fn(0)
    out = jnp.broadcast_to(out, (nheads, out.shape[1]))
    for h in range(1, nheads):
        out = jnp.where(head_row == h, fn(h), out)
    return out


def _dec_select_kernel(tab_ref, q_ref, kcb_ref, vcb_ref, oc_ref, idx_ref, *, qpos):
    f32 = jnp.float32
    nbp = kcb_ref.shape[1]
    q = q_ref[0]
    head_row = lax.broadcasted_iota(jnp.int32, (NSA_H, 1), 0)
    blk = lax.broadcasted_iota(jnp.int32, (1, nbp), 1)
    blkf = blk.astype(f32)
    dist = qpos - (blk * NSA_BLK + NSA_BLK - 1)
    ok = dist >= 0
    cur = qpos // NSA_BLK
    lg = jnp.zeros((NSA_H, nbp), f32)
    for g in range(NSA_G):
        lg_g = _mdot_nt(q, kcb_ref[0, :, g * NSA_D:(g + 1) * NSA_D])
        lg = jnp.where(head_row // NSA_HG == g, lg_g, lg)
    lg = lg + _rows_by_head(head_row, lambda h: _bias_of_dist(dist, tab_ref, h), NSA_H)
    lg = jnp.where(ok, lg, NEG)
    e = jnp.exp(lg - jnp.max(lg, axis=-1, keepdims=True))
    p = jnp.where(ok, e / jnp.sum(e, axis=-1, keepdims=True), 0.0)
    oc = jnp.zeros((NSA_H, NSA_D), f32)
    lane = lax.broadcasted_iota(jnp.int32, (1, 128), 1)
    for g in range(NSA_G):
        in_g = head_row // NSA_HG == g
        oc = jnp.where(in_g, _mdot(p, vcb_ref[0, :, g * NSA_D:(g + 1) * NSA_D]), oc)
        score = jnp.sum(jnp.where(in_g, p, 0.0), axis=0, keepdims=True)
        score = jnp.where(blk < cur, score, -1.0)
        idx = jnp.full((1, 128), -1.0, f32)
        for r in range(NSA_TOPK):
            mx = jnp.max(score, axis=-1, keepdims=True)
            first = jnp.min(jnp.where(score == mx, blkf, float(nbp)), axis=-1, keepdims=True)
            idx = jnp.where(lane == r, jnp.where(mx >= 0.0, first, -1.0), idx)
            score = jnp.where(blkf == first, -2.0, score)
        idx_ref[0, g:g + 1, :] = idx.astype(jnp.int32)
    oc_ref[0] = oc


def _dec_select(tab, qh, kcb, vcb, qpos):
    nseq = qh.shape[0]
    nbp = kcb.shape[1]
    seq = lambda shape: pl.BlockSpec((1,) + shape, lambda b: (b, 0, 0))
    return pl.pallas_call(
        functools.partial(_dec_select_kernel, qpos=qpos),
        grid=(nseq,),
        in_specs=[pl.BlockSpec(memory_space=pltpu.SMEM), seq((NSA_H, NSA_D)), seq((nbp, NSA_KV)),
                  seq((nbp, NSA_KV))],
        out_specs=[seq((NSA_H, NSA_D)), seq((NSA_G, 128))],
        out_shape=[jax.ShapeDtypeStruct((nseq, NSA_H, NSA_D), jnp.float32),
                   jax.ShapeDtypeStruct((nseq, NSA_G, 128), jnp.int32)],
        compiler_params=pltpu.CompilerParams(dimension_semantics=("parallel",),
                                             vmem_limit_bytes=VMEM_LIMIT_BYTES),
    )(tab.astype(jnp.float32), qh, kcb, vcb)


def _dec_attend_kernel(pt_ref, idx_ref, tab_ref, q_ref, *refs, qpos):
    f32 = jnp.float32
    ks_refs, vs_refs = refs[:NSA_TOPK], refs[NSA_TOPK:2 * NSA_TOPK]
    ksn_ref, vsn_ref, kwb_ref, vwb_ref, kwn_ref, vwn_ref, oc_ref, ng_ref, y_ref = refs[2 * NSA_TOPK:]
    b = pl.program_id(0)
    g = pl.program_id(1)
    cur = qpos // NSA_BLK
    head_row = lax.broadcasted_iota(jnp.int32, (NSA_HG, 1), 0)
    q4 = q_ref[0, pl.ds(g * NSA_HG, NSA_HG), :]
    q4r = _round_mxu(q4.astype(f32))
    new_row = b * NSA_G + g
    tab0 = _rows_by_head(head_row, lambda h: jnp.full((1, 1), tab_ref[0, g * NSA_HG + h], f32), NSA_HG)

    def attend(keys, vals, dist, ok, k_new, v_new):
        lg = _mdot_nt(q4, keys)
        lg = lg + _rows_by_head(head_row, lambda h: _bias_of_dist(dist, tab_ref, g * NSA_HG + h), NSA_HG)
        lg = jnp.where(ok, lg, NEG)
        lg_new = jnp.sum(q4r * _round_mxu(k_new), axis=-1, keepdims=True) + tab0
        m = jnp.maximum(jnp.max(lg, axis=-1, keepdims=True), lg_new)
        e = jnp.exp(lg - m)
        e_new = jnp.exp(lg_new - m)
        den = jnp.sum(e, axis=-1, keepdims=True) + e_new
        return _mdot(e / den, vals) + _round_mxu(e_new / den) * _round_mxu(v_new)

    pieces, oks = [], []
    jrow = lax.broadcasted_iota(jnp.int32, (1, NSA_BLK), 1)
    for s in range(NSA_TOPK):
        blk_id = idx_ref[b, g, s]
        valid = jnp.logical_and(blk_id >= 0, blk_id < cur)
        pos = jnp.maximum(blk_id, 0) * NSA_BLK + jrow
        pieces.append(qpos - pos)
        oks.append(jnp.where(valid, 1.0, 0.0) * jnp.where(pos <= qpos, 1.0, 0.0))
    dist_s = jnp.concatenate(pieces, axis=1)
    ok_s = jnp.concatenate(oks, axis=1) > 0.5
    mine = lambda r: r[pl.ds(g, NSA_BLK, stride=NSA_G), :]
    k_sel = jnp.concatenate([mine(r) for r in ks_refs], axis=0)
    v_sel = jnp.concatenate([mine(r) for r in vs_refs], axis=0)
    o_s = attend(k_sel, v_sel, dist_s, ok_s, ksn_ref[pl.ds(new_row, 1), :], vsn_ref[pl.ds(new_row, 1), :])

    wb = kwb_ref.shape[1]
    dist_w = wb - lax.broadcasted_iota(jnp.int32, (1, wb), 1)
    ok_w = dist_w < NSA_WIN
    o_w = attend(kwb_ref[0], vwb_ref[0], dist_w, ok_w, kwn_ref[pl.ds(new_row, 1), :], vwn_ref[pl.ds(new_row, 1), :])

    gates = jax.nn.sigmoid(ng_ref[pl.ds(b, 1), :])
    lane = lax.broadcasted_iota(jnp.int32, (1, 128), 1)

    def gate(branch):
        def one(h):
            c = 2 * GDN_VH + (g * NSA_HG + h) * 3 + branch
            return jnp.sum(jnp.where(lane == c, gates, 0.0), axis=-1, keepdims=True)
        return _rows_by_head(head_row, one, NSA_HG)

    oc4 = oc_ref[0, pl.ds(g * NSA_HG, NSA_HG), :]
    y_ref[0, pl.ds(g * NSA_HG, NSA_HG), :] = oc4 * gate(0) + o_s * gate(1) + o_w * gate(2)


def _dec_attend(tab, page_table, idx, qh, pool_ks, pool_vs, layer, ksn, vsn, buf_kw, buf_vw, kwn, vwn, oc, ng,
                qpos):
    nseq = qh.shape[0]
    base = layer * pool_ks.shape[1]
    wb = buf_kw.shape[1]
    ks = pool_ks.reshape(-1, NSA_D)
    vs = pool_vs.reshape(-1, NSA_D)
    per_page = PAGE_SIZE // NSA_BLK

    def sel(s):
        def index(b, g, pt, ix, s=s):
            blk_id = jnp.maximum(ix[b, g, s], 0)
            return ((base + pt[b, blk_id // per_page]) * per_page + blk_id % per_page, 0)
        return pl.BlockSpec((NSA_BLK * NSA_G, NSA_D), index)

    whole = lambda a: pl.BlockSpec(a.shape, lambda b, g, pt, ix: (0,) * a.ndim)
    heads = pl.BlockSpec((1, NSA_H, NSA_D), lambda b, g, pt, ix: (b, 0, 0))
    win = pl.BlockSpec((1, wb, NSA_D), lambda b, g, pt, ix: (b, 0, g))
    rows = lambda a: a.reshape(nseq * NSA_G, NSA_D)
    new = [rows(ksn), rows(vsn), rows(kwn), rows(vwn)]
    return pl.pallas_call(
        functools.partial(_dec_attend_kernel, qpos=qpos),
        grid_spec=pltpu.PrefetchScalarGridSpec(
            num_scalar_prefetch=2, grid=(nseq, NSA_G),
            in_specs=[pl.BlockSpec(memory_space=pltpu.SMEM), heads] + [sel(s) for s in range(NSA_TOPK)] * 2
                     + [whole(new[0]), whole(new[1]), win, win, whole(new[2]), whole(new[3]), heads, whole(ng)],
            out_specs=heads),
        out_shape=jax.ShapeDtypeStruct((nseq, NSA_H, NSA_D), jnp.float32),
        compiler_params=pltpu.CompilerParams(dimension_semantics=("parallel", "arbitrary"),
                                             vmem_limit_bytes=VMEM_LIMIT_BYTES),
    )(page_table, idx, tab.astype(jnp.float32), qh, *([ks] * NSA_TOPK), *([vs] * NSA_TOPK), new[0], new[1],
      buf_kw.reshape(nseq, wb, NSA_KV), buf_vw.reshape(nseq, wb, NSA_KV), new[2], new[3], oc, ng)


def _nsa_decode(proj, tab, qk_gain, w_ck, w_cv, caches, nseq):
    assert PAST_LEN % PAGE_SIZE == 0 and DEC_SEQ == 1
    pad = jnp.pad(proj, ((0, 512 - nseq), (0, 0)))
    qn, ksn, kwn, _, _, _, _, _, _ = _nsa_prep(pad, qk_gain, w_ck, w_cv)
    qh = qn[:nseq].reshape(nseq, NSA_H, NSA_D).astype(jnp.float32)
    kv = lambda i: proj[:, EV_KV + i * NSA_KV:EV_KV + (i + 1) * NSA_KV]
    layer = caches['layer']
    kcb, vcb = _dec_compress(caches['page_table'], caches['kc'], caches['vc'], layer, w_ck, w_cv, qk_gain[1])
    oc, idx = _dec_select(tab, qh, kcb, vcb, PAST_LEN)
    ng = proj[:, EV_SMALL:EV_SMALL + 128]
    y = _dec_attend(tab, caches['page_table'], idx, qh, caches['ks'], caches['vs'], layer, ksn[:nseq], kv(3),
                    caches['kw'], caches['vw'], kwn[:nseq], kv(5), oc, ng, PAST_LEN)
    return y.reshape(nseq, NSA_H * NSA_D), ksn[:nseq], kwn[:nseq]


def _rms_cast_kernel(x_ref, g_ref, o_ref):
    x = x_ref[...]
    y = x * lax.rsqrt(jnp.mean(x * x, axis=-1, keepdims=True) + RMS_EPS) * g_ref[...]
    o_ref[...] = y.astype(o_ref.dtype)


def _rms_cast(h, g):
    m, d = h.shape
    tm = _pick(m, (256, 128))
    return pl.pallas_call(
        _rms_cast_kernel,
        grid=(m // tm,),
        in_specs=[pl.BlockSpec((tm, d), lambda i: (i, 0)), pl.BlockSpec((1, d), lambda i: (0, 0))],
        out_specs=pl.BlockSpec((tm, d), lambda i: (i, 0)),
        out_shape=jax.ShapeDtypeStruct((m, d), MXU_DTYPE),
        compiler_params=pltpu.CompilerParams(dimension_semantics=("parallel",),
                                             vmem_limit_bytes=VMEM_LIMIT_BYTES),
    )(h, g.reshape(1, d).astype(jnp.float32))


def _mm2_kernel(xa_ref, xb_ref, w_ref, r_ref, o_ref):
    ka = xa_ref.shape[1]
    y = jnp.dot(xa_ref[...], w_ref[:ka, :].astype(MXU_DTYPE), preferred_element_type=jnp.float32)
    y = y + jnp.dot(xb_ref[...], w_ref[ka:, :].astype(MXU_DTYPE), preferred_element_type=jnp.float32)
    o_ref[...] = y + r_ref[...]


def _matmul2_res(xa, xb, w, res, layer, tiles=None):
    m, ka = xa.shape
    kb = xb.shape[1]
    n = w.shape[-1]
    tm = _pick(m, (tiles[0],) if tiles else (512, 256, 128))
    tn = _pick(n, (tiles[1],) if tiles else (512, 256, 128))
    return pl.pallas_call(
        _mm2_kernel,
        grid=(m // tm, n // tn),
        in_specs=[pl.BlockSpec((tm, ka), lambda i, j: (i, 0)), pl.BlockSpec((tm, kb), lambda i, j: (i, 0)),
                  pl.BlockSpec((None, ka + kb, tn), lambda i, j: (layer, 0, j)),
                  pl.BlockSpec((tm, tn), lambda i, j: (i, j))],
        out_specs=pl.BlockSpec((tm, tn), lambda i, j: (i, j)),
        out_shape=jax.ShapeDtypeStruct((m, n), jnp.float32),
        compiler_params=pltpu.CompilerParams(dimension_semantics=("parallel", "parallel"),
                                             vmem_limit_bytes=VMEM_LIMIT_BYTES),
    )(xa, xb, w, res)


LAYER_TILES = (
    dict(inp=(2048, 640), out=(2048, 512), mlp=(2048, 512)),
    dict(inp=(2048, 896), out=(2048, 512), mlp=(2048, 512)),
    dict(inp=(2048, 640), out=(2048, 512), mlp=(1024, 1024)),
    dict(inp=(2048, 896), out=(2048, 512), mlp=(2048, 512)),
)


def _mlp_block(h, g, w1, w2, layer, tiles):
    mid = _matmul(_rms_cast(h, g), w1, act='relu2', out_dtype=MXU_DTYPE, tiles=tiles, layer=layer)
    return _matmul(mid, w2, res=h, tiles=tiles, layer=layer)


def _pad_time(x, t_pad):
    b, d = x.shape
    return jnp.zeros((b, t_pad, d), x.dtype).at[:, 0, :].set(x).reshape(b * t_pad, d)


def _even_layer(h, g_mix, w_in, w_out, conv_w, a_log, dt_bias, gnorm, qk_gain, w_ck, w_cv, tab, bias_tiles,
                batch, t_len, conv_st, s_st, decode, tiles):
    f32 = jnp.float32
    proj = _matmul(_rms_cast(h, g_mix), w_in, tiles=tiles['inp'])
    t_pad = t_len if decode is None else GDN_CHUNK
    pp = proj if decode is None else _pad_time(proj, t_pad)
    first = jnp.pad(conv_st.astype(f32), ((0, 0), (8 - (GDN_CONV - 1), 0), (0, 0)))
    q, k, v, gb = _gdn_prep(pp, first, pp, EV_SMALL // 128, conv_w, a_log, dt_bias, batch, t_pad, t_len)
    ya, s_new = _gdn_chunks(q, k, v, gb, pp, EV_Z // GDN_V, s_st, gnorm, batch, t_pad)
    p3 = proj.reshape(batch, t_len, EV_N)
    xin_tail = jnp.concatenate([conv_st.astype(f32), p3[..., :GDN_CONV_CH]], axis=1)[:, -(GDN_CONV - 1):]
    hd = lambda t: t.reshape(batch, t_len, NSA_G, NSA_D)
    kv = lambda i: hd(p3[..., EV_KV + i * NSA_KV:EV_KV + (i + 1) * NSA_KV])
    if decode is None:
        qn, ksn, kwn, ksb, vsb, kwb, vwb, kcb, vcb = _nsa_prep(proj, qk_gain, w_ck, w_cv)
        yb = _nsa_prompt_attn(tab, bias_tiles, proj, qn, kcb, vcb, ksb, vsb, kwb, vwb, batch, t_len)
        nw = min(NSA_WIN, t_len)
        kw_out, vw_out = hd(kwn)[:, -nw:], kv(5)[:, -nw:]
        ks_out = hd(ksn)
    else:
        ya = ya.reshape(batch, t_pad, GDN_V)[:, 0]
        yb, ksn, kwn = _nsa_decode(proj, tab, qk_gain, w_ck, w_cv, decode, batch)
        yb = yb.astype(MXU_DTYPE)
        ks_out = hd(ksn)
        kw_out = jnp.concatenate([decode['kw'][:, 1:], hd(kwn)], axis=1)
        vw_out = jnp.concatenate([decode['vw'][:, 1:], kv(5)], axis=1)
    h = _matmul2_res(ya, yb, w_out[0], h, w_out[1], tiles=tiles['out'])
    return h, (kv(0), kv(1), ks_out, kv(3), kw_out, vw_out, xin_tail, s_new)


def _odd_layer(h, g_mix, w_in, w_out, b_if, mnorm, batch, t_len, c0, n0, m0, decode, tiles):
    proj = _matmul(_rms_cast(h, g_mix), w_in, tiles=tiles['inp'])
    t_pad = t_len if not decode else ML_CHUNK
    pp = proj if not decode else _pad_time(proj, t_pad)
    y, c, n, m = _mlstm_chunks(pp, b_if, mnorm, c0, n0, m0, batch, t_pad, t_len)
    if decode:
        y = y.reshape(batch, t_pad, ODD_MIX)[:, 0]
    return _matmul(y, w_out[0], res=h, tiles=tiles['out'], layer=w_out[1]), (c, n, m)


def kernel(x_prompt, x_sample, cache_kc, cache_vc, cache_ks, cache_vs, cache_kw, cache_vw,
           state_gdn_conv, state_gdn_s, state_ml_c, state_ml_n, state_ml_m, page_table,
           rel_bias, norm_mix, norm_mlp, w_in_even, w_out_even, gdn_conv_w, gdn_a_log,
           gdn_dt_bias, gdn_norm_w, nsa_qk_gain, nsa_w_ck, nsa_w_cv, w_in_odd, w_out_odd,
           ml_b_if, ml_norm_w, w_ff1, w_ff2):
    f32 = jnp.float32
    bp, tp, d = x_prompt.shape
    bs, ts, _ = x_sample.shape
    hp = x_prompt.reshape(bp * tp, d)
    hs = x_sample.reshape(bs * ts, d)
    bias_tiles = _bias_tiles(rel_bias)
    ev_p, ev_s, od_p, od_s = [], [], [], []
    for l in range(DEPTH):
        j = l // 2
        tiles = LAYER_TILES[l]
        if l % 2 == 0:
            w_in = _permute_even_w(w_in_even[j])
            args = (w_in, (w_out_even, j), gdn_conv_w[j], gdn_a_log[j], gdn_dt_bias[j], gdn_norm_w[j],
                    nsa_qk_gain[j], nsa_w_ck[j], nsa_w_cv[j], rel_bias, bias_tiles)
            hp, stp = _even_layer(hp, norm_mix[l], *args, bp, tp,
                                  jnp.zeros((bp, GDN_CONV - 1, GDN_CONV_CH), f32),
                                  jnp.zeros((bp, GDN_VH, GDN_DK, GDN_DV), f32), None, tiles)
            caches = dict(kc=cache_kc, vc=cache_vc, ks=cache_ks, vs=cache_vs, layer=j, kw=cache_kw[j],
                          vw=cache_vw[j], page_table=page_table)
            hs, sts = _even_layer(hs, norm_mix[l], *args, bs, ts, state_gdn_conv[j], state_gdn_s[j], caches,
                                  tiles)
            ev_p.append(stp)
            ev_s.append(sts)
        else:
            w_in = _permute_odd_w(w_in_odd[j])
            hp, stp = _odd_layer(hp, norm_mix[l], w_in, (w_out_odd, j), ml_b_if[j], ml_norm_w[j], bp, tp,
                                 jnp.zeros((bp, ML_H, ML_DK, ML_DV), f32), jnp.zeros((bp, ML_H, ML_DK), f32),
                                 jnp.zeros((bp, ML_H), f32), False, tiles)
            hs, sts = _odd_layer(hs, norm_mix[l], w_in, (w_out_odd, j), ml_b_if[j], ml_norm_w[j], bs, ts,
                                 state_ml_c[j], state_ml_n[j], state_ml_m[j], True, tiles)
            od_p.append(stp)
            od_s.append(sts)
        hp = _mlp_block(hp, norm_mlp[l], w_ff1, w_ff2, l, tiles['mlp'])
        hs = _mlp_block(hs, norm_mlp[l], w_ff1, w_ff2, l, tiles['mlp'])

    kc_p, vc_p, ks_p, vs_p, kw_p, vw_p, conv_p, gdn_p = [jnp.stack(a) for a in zip(*ev_p)]
    kc_s, vc_s, ks_s, vs_s, kw_s, vw_s, conv_s, gdn_s = [jnp.stack(a) for a in zip(*ev_s)]
    mlc_p, mln_p, mlm_p = [jnp.stack(a) for a in zip(*od_p)]
    mlc_s, mln_s, mlm_s = [jnp.stack(a) for a in zip(*od_s)]
    return (hp.reshape(bp, tp, d), hs.reshape(bs, ts, d), kc_p, kc_s, vc_p, vc_s, ks_p, ks_s, vs_p, vs_s,
            kw_p, kw_s, vw_p, vw_s, conv_p, conv_s, gdn_p, gdn_s, mlc_p, mlc_s, mln_p, mln_s, mlm_p, mlm_s)
```

```python
import functools
import math

import jax
import jax.numpy as jnp
import numpy as np
from jax import lax
from jax.experimental import pallas as pl
from jax.experimental.pallas import tpu as pltpu

DEPTH = 4
DEC_SEQ = 1
PAST_LEN = 16384
PAGE_SIZE = 128
RMS_EPS = 1e-6
NEG = -1e30

GDN_KH = 4
GDN_VH = 8
GDN_DK = 128
GDN_DV = 128
GDN_CONV = 4
GDN_CHUNK = 64
GDN_QK = GDN_KH * GDN_DK
GDN_V = GDN_VH * GDN_DV
GDN_CONV_CH = 2 * GDN_QK + GDN_V

NSA_H = 8
NSA_G = 2
NSA_HG = NSA_H // NSA_G
NSA_D = 128
NSA_BLK = 64
NSA_TOPK = 15
NSA_WIN = 512
NSA_QB = 128
NSA_KV = NSA_G * NSA_D

REL_BUCKETS = 32
REL_EXACT = 16
REL_MAX_DIST = 2048

ML_H = 8
ML_DK = 128
ML_DV = 256
ML_CHUNK = 64

ODD_MIX = ML_H * ML_DV

VMEM_LIMIT_BYTES = 56 * 1024 * 1024
MXU_DTYPE = jnp.bfloat16


def _mm_kernel(x_ref, w_ref, *rest, nk, act, has_res):
    rest = list(rest)
    r_ref = rest.pop(0) if has_res else None
    o_ref = rest.pop(0)

    def finish(y):
        if act == 'relu2':
            y = jnp.square(jnp.maximum(y, 0.0))
        if has_res:
            y = y + r_ref[...]
        o_ref[...] = y.astype(o_ref.dtype)

    d = jnp.dot(x_ref[...], w_ref[...].astype(MXU_DTYPE), preferred_element_type=jnp.float32)
    if nk == 1:
        finish(d)
        return
    acc_ref = rest.pop(0)
    k = pl.program_id(2)

    @pl.when(k == 0)
    def _():
        acc_ref[...] = d

    @pl.when(k > 0)
    def _():
        acc_ref[...] += d

    @pl.when(k == nk - 1)
    def _():
        finish(acc_ref[...])


def _pick(n, prefs):
    for p in prefs:
        if n % p == 0:
            return p
    return n


def _matmul(x, w, *, act=None, res=None, out_dtype=jnp.float32, tiles=None, layer=None):
    m, kdim = x.shape
    n = w.shape[-1]
    tm = _pick(m, (tiles[0],) if tiles else (512, 256, 128))
    tn = _pick(n, (tiles[1],) if tiles else (512, 896, 384, 256, 128))
    tk = _pick(kdim, (tiles[2],) if tiles and len(tiles) > 2 else (2048, 1024, 512))
    nk = kdim // tk
    if layer is None:
        w_spec = pl.BlockSpec((tk, tn), lambda i, j, k: (k, j))
    else:
        w_spec = pl.BlockSpec((None, tk, tn), lambda i, j, k: (layer, k, j))
    in_specs = [pl.BlockSpec((tm, tk), lambda i, j, k: (i, k)), w_spec]
    args = [x, w]
    if res is not None:
        in_specs.append(pl.BlockSpec((tm, tn), lambda i, j, k: (i, j)))
        args.append(res)
    return pl.pallas_call(
        functools.partial(_mm_kernel, nk=nk, act=act, has_res=res is not None),
        grid=(m // tm, n // tn, nk),
        in_specs=in_specs,
        out_specs=pl.BlockSpec((tm, tn), lambda i, j, k: (i, j)),
        out_shape=jax.ShapeDtypeStruct((m, n), out_dtype),
        scratch_shapes=[pltpu.VMEM((tm, tn), jnp.float32)] if nk > 1 else [],
        compiler_params=pltpu.CompilerParams(
            dimension_semantics=("parallel", "parallel", "arbitrary"),
            vmem_limit_bytes=VMEM_LIMIT_BYTES),
    )(*args)


EV_CONV = 0
EV_Z = GDN_CONV_CH
EV_NQ = EV_Z + GDN_V
EV_KV = EV_NQ + NSA_H * NSA_D
EV_SMALL = EV_KV + 6 * NSA_KV
EV_N = EV_SMALL + 128
EV_ORIG_SMALL = 2 * GDN_QK + 2 * GDN_V
EV_ORIG_N = EV_ORIG_SMALL + 2 * GDN_VH + NSA_H * NSA_D + 6 * NSA_KV + 3 * NSA_H


def _permute_even_w(w_in):
    ab = w_in[:, EV_ORIG_SMALL:EV_ORIG_SMALL + 2 * GDN_VH]
    big = w_in[:, EV_ORIG_SMALL + 2 * GDN_VH:EV_ORIG_N - 3 * NSA_H]
    ng = w_in[:, EV_ORIG_N - 3 * NSA_H:]
    pad = jnp.zeros((w_in.shape[0], EV_N - EV_ORIG_N), w_in.dtype)
    return jnp.concatenate([w_in[:, :EV_ORIG_SMALL], big, ab, ng, pad], axis=1).astype(MXU_DTYPE)


def _bucket_thresholds():
    n = np.arange(0, 4 * REL_MAX_DIST)
    nf = np.maximum(n, REL_EXACT).astype(np.float32)
    large = REL_EXACT + (np.log(nf / np.float32(REL_EXACT)) / np.float32(math.log(REL_MAX_DIST / REL_EXACT))
                         * np.float32(REL_BUCKETS - REL_EXACT)).astype(np.int32)
    b = np.where(n < REL_EXACT, n, np.minimum(large, REL_BUCKETS - 1))
    return tuple(int(np.argmax(b >= k)) for k in range(REL_BUCKETS))


BUCKET_THR = _bucket_thresholds()
BIAS_TILES = 14
assert BUCKET_THR[-1] <= (BIAS_TILES - 1) * 128 - 127


def _bias_of_dist(dist, tab_ref, head):
    val = jnp.full(dist.shape, tab_ref[0, head], jnp.float32)
    for k in range(1, REL_BUCKETS):
        val = jnp.where(dist >= BUCKET_THR[k], tab_ref[k, head], val)
    return val


def _bias_tiles_kernel(tab_ref, o_ref):
    h = pl.program_id(0)
    d = pl.program_id(1)
    i = lax.broadcasted_iota(jnp.int32, (128, 128), 1)
    j = lax.broadcasted_iota(jnp.int32, (128, 128), 0)
    dist = d * 128 + i - j
    val = jnp.full(dist.shape, tab_ref[0, h], jnp.float32)
    for k in range(1, REL_BUCKETS):
        val = jnp.where(dist >= BUCKET_THR[k], tab_ref[k, h], val)
    o_ref[0, 0] = val


def _bias_tiles(tab):
    return pl.pallas_call(
        _bias_tiles_kernel,
        grid=(NSA_H, BIAS_TILES),
        in_specs=[pl.BlockSpec(memory_space=pltpu.SMEM)],
        out_specs=pl.BlockSpec((1, 1, 128, 128), lambda h, d: (h, d, 0, 0)),
        out_shape=jax.ShapeDtypeStruct((NSA_H, BIAS_TILES, 128, 128), jnp.float32),
    )(tab.astype(jnp.float32))


def _group_rms(x, gain_row, scale=1.0):
    ms = jnp.mean(x * x, axis=-1, keepdims=True)
    y = x * lax.rsqrt(ms + RMS_EPS) * gain_row
    return y * scale if scale != 1.0 else y


def _nsa_prep_kernel(nq_ref, kc_ref, vc_ref, ks_ref, vs_ref, kw_ref, vw_ref, gain_ref, wck_ref, wcv_ref,
                     qn_ref, ksn_ref, kwn_ref, ksb_ref, vst_ref, kwb_ref, vwt_ref, kcb_ref, vcb_ref):
    tm = nq_ref.shape[0]
    g0 = gain_ref[0:1, :]
    g1 = gain_ref[1:2, :]
    g2 = gain_ref[2:3, :]
    g3 = gain_ref[3:4, :]
    for h in range(NSA_H):
        sl = slice(h * NSA_D, (h + 1) * NSA_D)
        qn_ref[:, sl] = _group_rms(nq_ref[:, sl], g0, NSA_D ** -0.5).astype(qn_ref.dtype)
    for g in range(NSA_G):
        sl = slice(g * NSA_D, (g + 1) * NSA_D)
        ksn = _group_rms(ks_ref[:, sl], g2)
        kwn = _group_rms(kw_ref[:, sl], g3)
        ksn_ref[:, sl] = ksn
        kwn_ref[:, sl] = kwn
        ksb_ref[:, sl] = ksn.astype(ksb_ref.dtype)
        kwb_ref[:, sl] = kwn.astype(kwb_ref.dtype)
    for src, dst in ((vs_ref, vst_ref), (vw_ref, vwt_ref)):
        vt = src[...].T
        tk = dst.shape[2]
        for c in range(tm // tk):
            dst[c] = vt[:, c * tk:(c + 1) * tk].astype(dst.dtype)
    nblk = tm // NSA_BLK
    kc3 = kc_ref[...].reshape(nblk, NSA_BLK, NSA_KV)
    vc3 = vc_ref[...].reshape(nblk, NSA_BLK, NSA_KV)
    kcb = jnp.sum(kc3 * wck_ref[...][None], axis=1)
    vcb = jnp.sum(vc3 * wcv_ref[...][None], axis=1)
    for g in range(NSA_G):
        sl = slice(g * NSA_D, (g + 1) * NSA_D)
        kcb_ref[:, sl] = _group_rms(kcb[:, sl], g1)
    vcb_ref[...] = vcb


def _nsa_prep(proj, gain, w_ck, w_cv):
    m = proj.shape[0]
    tm = 512
    kvb = EV_KV // NSA_KV
    f32, bf16 = jnp.float32, MXU_DTYPE

    def kv_spec(i):
        return pl.BlockSpec((tm, NSA_KV), lambda r, i=i: (r, kvb + i))

    row = lambda w: pl.BlockSpec((tm, w), lambda r: (r, 0))
    slabs = lambda tk: pl.BlockSpec((tm // tk, NSA_KV, tk), lambda r: (r, 0, 0))
    full = lambda a: pl.BlockSpec(a.shape, lambda r: (0,) * a.ndim)
    wck = w_ck.reshape(NSA_BLK, NSA_KV)
    wcv = w_cv.reshape(NSA_BLK, NSA_KV)
    outs = pl.pallas_call(
        _nsa_prep_kernel,
        grid=(m // tm,),
        in_specs=[pl.BlockSpec((tm, NSA_H * NSA_D), lambda r: (r, EV_NQ // (NSA_H * NSA_D)))]
                 + [kv_spec(i) for i in range(6)] + [full(gain), full(wck), full(wcv)],
        out_specs=[row(NSA_H * NSA_D), row(NSA_KV), row(NSA_KV), row(NSA_KV), slabs(NSA_TK), row(NSA_KV),
                   slabs(NSA_QB)] + [pl.BlockSpec((tm // NSA_BLK, NSA_KV), lambda r: (r, 0))] * 2,
        out_shape=[jax.ShapeDtypeStruct((m, NSA_H * NSA_D), bf16),
                   jax.ShapeDtypeStruct((m, NSA_KV), f32), jax.ShapeDtypeStruct((m, NSA_KV), f32),
                   jax.ShapeDtypeStruct((m, NSA_KV), bf16), jax.ShapeDtypeStruct((m // NSA_TK, NSA_KV, NSA_TK), bf16),
                   jax.ShapeDtypeStruct((m, NSA_KV), bf16), jax.ShapeDtypeStruct((m // NSA_QB, NSA_KV, NSA_QB), bf16)]
                  + [jax.ShapeDtypeStruct((m // NSA_BLK, NSA_KV), f32)] * 2,
        compiler_params=pltpu.CompilerParams(dimension_semantics=("parallel",),
                                             vmem_limit_bytes=VMEM_LIMIT_BYTES),
    )(proj, proj, proj, proj, proj, proj, proj, gain, wck, wcv)
    return outs


NSA_TK = 256


def _pad_rows(x, rows):
    if x.shape[0] == rows:
        return x
    return jnp.concatenate([x, jnp.zeros((rows - x.shape[0],) + x.shape[1:], x.dtype)], axis=0)


def _nsa_attn_kernel(tab_ref, q_ref, kcb_ref, vcb_ref, ks_ref, vst_ref, kw_ref, vwt_ref, bias_ref, ng_ref,
                     o_ref, m_scr, l_scr, acc_scr):
    f32, bf16 = jnp.float32, MXU_DTYPE
    qb = pl.program_id(1)
    nb = kcb_ref.shape[0]
    qn = NSA_QB
    gates = jax.nn.sigmoid(ng_ref[...])
    tsub = NSA_TK // qn
    shift = NSA_BLK.bit_length() - 1

    heads = range(NSA_H)
    gls = [slice(g * NSA_D, (g + 1) * NSA_D) for g in range(NSA_G)]
    qs = [q_ref[:, hh * NSA_D:(hh + 1) * NSA_D] for hh in heads]
    qts = [q.astype(f32).T.astype(bf16) for q in qs]

    blk = lax.broadcasted_iota(jnp.int32, (nb, qn), 0)
    blkf = blk.astype(f32)
    qpos = qb * qn + lax.broadcasted_iota(jnp.int32, (nb, qn), 1)
    dist_c = qpos - (blk * NSA_BLK + NSA_BLK - 1)
    ok_c = dist_c >= 0
    cur = lax.shift_right_logical(qpos, shift)
    o_c, scores = [], []
    for g in range(NSA_G):
        q2 = jnp.concatenate(qs[g * NSA_HG:(g + 1) * NSA_HG], axis=0)
        kcb = kcb_ref[:, gls[g]].astype(bf16)
        vcb = _pad_rows(vcb_ref[:, gls[g]], qn).astype(bf16)
        lgt = lax.dot_general(kcb, q2, (((1,), (1,)), ((), ())), preferred_element_type=f32)
        score = jnp.zeros((nb, qn), f32)
        for h in range(NSA_HG):
            lg = lgt[:, h * qn:(h + 1) * qn] + _bias_of_dist(dist_c, tab_ref, g * NSA_HG + h)
            lg = jnp.where(ok_c, lg, NEG)
            mx = jnp.max(lg, axis=0, keepdims=True)
            e = jnp.exp(lg - mx)
            p = jnp.where(ok_c, e / jnp.sum(e, axis=0, keepdims=True), 0.0)
            score = score + p
            p_t = _pad_rows(p, qn).T.astype(bf16)
            o_c.append(jnp.dot(p_t, vcb, preferred_element_type=f32))
        scores.append(jnp.where(blk < cur, score, -1.0))

    sels = [jnp.zeros((nb, qn), f32) for _ in range(NSA_G)]
    for _ in range(min(NSA_TOPK, nb)):
        for g in range(NSA_G):
            mx = jnp.max(scores[g], axis=0, keepdims=True)
            first = jnp.min(jnp.where(scores[g] == mx, blkf, float(nb)), axis=0, keepdims=True)
            pick = blkf == first
            sels[g] = jnp.where(pick, jnp.where(mx >= 0.0, 1.0, 0.0), sels[g])
            scores[g] = jnp.where(pick, -2.0, scores[g])
    sel_ps = [_pad_rows(jnp.where(blk == cur, 1.0, s), qn).astype(bf16) for s in sels]

    def reset():
        m_scr[...] = jnp.full(m_scr.shape, NEG, f32)
        l_scr[...] = jnp.zeros(l_scr.shape, f32)
        acc_scr[...] = jnp.zeros(acc_scr.shape, f32)

    def update(hh, lg, vt):
        m_old = m_scr[hh]
        m_new = jnp.maximum(m_old, jnp.max(lg, axis=0, keepdims=True))
        p = jnp.exp(lg - m_new)
        alpha = jnp.exp(m_old - m_new)
        l_scr[hh] = alpha * l_scr[hh] + jnp.sum(p, axis=0, keepdims=True)
        acc_scr[hh] = alpha * acc_scr[hh] + jnp.dot(vt, p.astype(bf16), preferred_element_type=f32)
        m_scr[hh] = m_new

    def result(hh):
        return (acc_scr[hh] / l_scr[hh]).T

    reset()
    n_tiles = qb // tsub + 1
    tok = qb * qn + lax.broadcasted_iota(jnp.int32, (NSA_TK, qn), 1)
    key = lax.broadcasted_iota(jnp.int32, (NSA_TK, qn), 0)
    eblk = lax.broadcasted_iota(jnp.int32, (NSA_TK, qn), 1)

    def body(i, carry):
        kt = n_tiles - 1 - i
        k0 = pl.multiple_of(kt * NSA_TK, NSA_TK)
        expand = jnp.where(eblk == lax.shift_right_logical(k0 + key, shift), 1.0, 0.0).astype(bf16)
        causal = k0 + key <= tok
        for g in range(NSA_G):
            k = ks_ref[pl.ds(k0, NSA_TK), gls[g]]
            vt = vst_ref[kt, gls[g], :]
            member = jnp.dot(expand, sel_ps[g], preferred_element_type=f32)
            keep = jnp.where(causal, member, 0.0) > 0.5
            for h in range(NSA_HG):
                hh = g * NSA_HG + h
                tiles = [bias_ref[hh, jnp.clip(qb - (kt * tsub + j), 0, BIAS_TILES - 1)] for j in range(tsub)]
                lg = jnp.dot(k, qts[hh], preferred_element_type=f32) + jnp.concatenate(tiles, axis=0)
                update(hh, jnp.where(keep, lg, NEG), vt)
        return carry

    lax.fori_loop(0, n_tiles, body, 0)
    o_s = [result(hh) for hh in heads]

    reset()
    nwin = NSA_WIN // qn + 1
    wi = lax.broadcasted_iota(jnp.int32, (qn, qn), 1)
    wj = lax.broadcasted_iota(jnp.int32, (qn, qn), 0)
    for d in range(nwin):
        sub = qb - d
        subc = jnp.maximum(sub, 0)
        for g in range(NSA_G):
            k = kw_ref[pl.ds(pl.multiple_of(subc * qn, qn), qn), gls[g]]
            vt = vwt_ref[subc, gls[g], :]
            for h in range(NSA_HG):
                hh = g * NSA_HG + h
                x = jnp.dot(k, qts[hh], preferred_element_type=f32) + bias_ref[hh, d]
                if d == 0:
                    x = jnp.where(wj <= wi, x, NEG)
                else:
                    if d == nwin - 1:
                        x = jnp.where(wj > wi, x, NEG)
                    x = jnp.where(sub >= 0, x, NEG)
                update(hh, x, vt)
    o_w = [result(hh) for hh in heads]

    for hh in heads:
        c = 2 * GDN_VH + hh * 3
        y = o_c[hh] * gates[:, c:c + 1] + o_s[hh] * gates[:, c + 1:c + 2] + o_w[hh] * gates[:, c + 2:c + 3]
        o_ref[:, hh * NSA_D:(hh + 1) * NSA_D] = y.astype(o_ref.dtype)


def _nsa_prompt_attn(tab, bias_tiles, proj, qn, kcb, vcb, ksb, vst, kwb, vwt, batch, t_len):
    nq = t_len // NSA_QB
    nb = t_len // NSA_BLK
    hd = NSA_H * NSA_D
    per_b = lambda w: pl.BlockSpec((t_len, w), lambda b, i: (b, 0))
    tiles_b = lambda tk: pl.BlockSpec((t_len // tk, NSA_KV, tk), lambda b, i: (b, 0, 0))
    return pl.pallas_call(
        _nsa_attn_kernel,
        grid=(batch, nq),
        in_specs=[pl.BlockSpec(memory_space=pltpu.SMEM),
                  pl.BlockSpec((NSA_QB, hd), lambda b, i: (b * nq + i, 0)),
                  pl.BlockSpec((nb, NSA_KV), lambda b, i: (b, 0)),
                  pl.BlockSpec((nb, NSA_KV), lambda b, i: (b, 0)),
                  per_b(NSA_KV), tiles_b(NSA_TK), per_b(NSA_KV), tiles_b(NSA_QB),
                  pl.BlockSpec(bias_tiles.shape, lambda b, i: (0, 0, 0, 0)),
                  pl.BlockSpec((NSA_QB, 128), lambda b, i: (b * nq + i, EV_SMALL // 128 + 0))],
        out_specs=pl.BlockSpec((NSA_QB, hd), lambda b, i: (b * nq + i, 0)),
        out_shape=jax.ShapeDtypeStruct((batch * t_len, hd), MXU_DTYPE),
        scratch_shapes=[pltpu.VMEM((NSA_H, 1, NSA_QB), jnp.float32),
                        pltpu.VMEM((NSA_H, 1, NSA_QB), jnp.float32),
                        pltpu.VMEM((NSA_H, NSA_D, NSA_QB), jnp.float32)],
        compiler_params=pltpu.CompilerParams(dimension_semantics=("parallel", "arbitrary"),
                                             vmem_limit_bytes=VMEM_LIMIT_BYTES),
    )(tab.astype(jnp.float32), qn, kcb, vcb, ksb, vst, kwb, vwt, bias_tiles, proj)


def _mdot(a, b):
    return jnp.dot(a.astype(MXU_DTYPE), b.astype(MXU_DTYPE), preferred_element_type=jnp.float32)


def _mdot_nt(a, b):
    return lax.dot_general(a.astype(MXU_DTYPE), b.astype(MXU_DTYPE), (((1,), (1,)), ((), ())),
                           preferred_element_type=jnp.float32)


def _mdot_tn(a, b):
    return lax.dot_general(a.astype(MXU_DTYPE), b.astype(MXU_DTYPE), (((0,), (0,)), ((), ())),
                           preferred_element_type=jnp.float32)


def _shift_rows(x, prev, s):
    xs = pltpu.roll(x, s, axis=0)
    ps = pltpu.roll(prev, s, axis=0)
    row8 = lax.broadcasted_iota(jnp.int32, prev.shape, 0)
    head = jnp.where(row8 < s, ps, xs[0:8])
    return jnp.concatenate([head, xs[8:]], axis=0)


def _gdn_prep_kernel(x_ref, prev_ref, first_ref, ab_ref, cw_ref, alog_ref, dtb_ref,
                     q_ref, k_ref, v_ref, gb_ref, *, tiles_per_batch, t_valid):
    r = pl.program_id(0)
    tm = x_ref.shape[0]
    tile = r % tiles_per_batch
    prev = jnp.where(tile == 0, first_ref[0], prev_ref[...])
    x = x_ref[...]
    acc = x * cw_ref[GDN_CONV - 1:GDN_CONV, :]
    for s in range(1, GDN_CONV):
        acc = acc + _shift_rows(x, prev, s) * cw_ref[GDN_CONV - 1 - s:GDN_CONV - s, :]
    c = acc * jax.nn.sigmoid(acc)

    def l2(t):
        return t * lax.rsqrt(jnp.sum(t * t, axis=-1, keepdims=True) + 1e-6)

    for h in range(GDN_KH):
        sl = slice(h * GDN_DK, (h + 1) * GDN_DK)
        q_ref[:, sl] = l2(c[:, sl]) * (GDN_DK ** -0.5)
        k_ref[:, sl] = l2(c[:, GDN_QK + h * GDN_DK:GDN_QK + (h + 1) * GDN_DK])
    v_ref[...] = c[:, 2 * GDN_QK:]
    ab = ab_ref[...]
    z = ab + dtb_ref[...]
    softplus = jnp.maximum(z, 0.0) + jnp.log(1.0 + jnp.exp(-jnp.abs(z)))
    gate = -jnp.exp(alog_ref[...]) * softplus
    lane = lax.broadcasted_iota(jnp.int32, ab.shape, 1)
    pos = tile * tm + lax.broadcasted_iota(jnp.int32, ab.shape, 0)
    gb = jnp.where(lane < GDN_VH, gate, jnp.where(lane < 2 * GDN_VH, jax.nn.sigmoid(ab), 0.0))
    gb_ref[...] = jnp.where(pos < t_valid, gb, 0.0)


def _gdn_prep(xin, first, ab, ab_col, conv_w, a_log, dt_bias, batch, t_pad, t_valid):
    m = batch * t_pad
    tm = _pick(t_pad, (512, 256, 128, 64))
    tpb = t_pad // tm
    f32 = jnp.float32
    lanes = lambda v: jnp.zeros((1, 128), f32).at[0, :GDN_VH].set(v.astype(f32))
    return pl.pallas_call(
        functools.partial(_gdn_prep_kernel, tiles_per_batch=tpb, t_valid=t_valid),
        grid=(m // tm,),
        in_specs=[pl.BlockSpec((tm, GDN_CONV_CH), lambda r: (r, 0)),
                  pl.BlockSpec((8, GDN_CONV_CH), lambda r: (jnp.maximum(r * (tm // 8) - 1, 0), 0)),
                  pl.BlockSpec((1, 8, GDN_CONV_CH), lambda r: (r // tpb, 0, 0)),
                  pl.BlockSpec((tm, 128), lambda r: (r, ab_col)),
                  pl.BlockSpec((GDN_CONV, GDN_CONV_CH), lambda r: (0, 0)),
                  pl.BlockSpec((1, 128), lambda r: (0, 0)),
                  pl.BlockSpec((1, 128), lambda r: (0, 0))],
        out_specs=[pl.BlockSpec((tm, GDN_QK), lambda r: (r, 0)),
                   pl.BlockSpec((tm, GDN_QK), lambda r: (r, 0)),
                   pl.BlockSpec((tm, GDN_V), lambda r: (r, 0)),
                   pl.BlockSpec((tm, 128), lambda r: (r, 0))],
        out_shape=[jax.ShapeDtypeStruct((m, GDN_QK), f32), jax.ShapeDtypeStruct((m, GDN_QK), f32),
                   jax.ShapeDtypeStruct((m, GDN_V), f32), jax.ShapeDtypeStruct((m, 128), f32)],
        compiler_params=pltpu.CompilerParams(dimension_semantics=("arbitrary",),
                                             vmem_limit_bytes=VMEM_LIMIT_BYTES),
    )(xin, xin, first, ab, conv_w.astype(f32), lanes(a_log), lanes(dt_bias))


def _cumsum_rows(x):
    row = lax.broadcasted_iota(jnp.int32, x.shape, 0)
    s = 1
    while s < x.shape[0]:
        x = x + jnp.where(row >= s, pltpu.roll(x, s, axis=0), 0.0)
        s *= 2
    return x


def _split3(a):
    hi = a.astype(jnp.bfloat16)
    lo = (a - hi.astype(jnp.float32)).astype(jnp.bfloat16)
    return hi, lo


def _dot3(a, b):
    if MXU_DTYPE != jnp.bfloat16:
        return jnp.dot(a, b, preferred_element_type=jnp.float32)
    ah, al = _split3(a)
    bh, bl = _split3(b)
    d = lambda x, y: jnp.dot(x, y, preferred_element_type=jnp.float32)
    return d(ah, bh) + (d(ah, bl) + d(al, bh))


def _unit_lower_inverses(lmats, row, col):
    eye = jnp.where(row == col, 1.0, 0.0)
    blk = lax.shift_right_logical(row, 3) == lax.shift_right_logical(col, 3)
    lds = [jnp.where(blk, l, 0.0) for l in lmats]
    xs = [eye - ld for ld in lds]
    ps = [_dot3(ld, ld) for ld in lds]
    for step in range(2):
        xs = [x + _dot3(x, p) for x, p in zip(xs, ps)]
        if step < 1:
            ps = [_dot3(p, p) for p in ps]
    size = 8
    while size < lmats[0].shape[0]:
        sh = size.bit_length() - 1
        inner = lax.shift_right_logical(row, sh) == lax.shift_right_logical(col, sh)
        outer = lax.shift_right_logical(row, sh + 1) == lax.shift_right_logical(col, sh + 1)
        coffs = [jnp.where(outer, jnp.where(inner, 0.0, l), 0.0) for l in lmats]
        ts = [_dot3(x, c) for x, c in zip(xs, coffs)]
        xs = [x - _dot3(t, x) for x, t in zip(xs, ts)]
        size *= 2
    return xs


def _gdn_chunk_kernel(q_ref, k_ref, v_ref, gb_ref, z_ref, s0_ref, nw_ref, y_ref, s_ref):
    f32 = jnp.float32
    cn = q_ref.shape[0]

    @pl.when(pl.program_id(1) == 0)
    def _():
        s_ref[...] = s0_ref[...]

    gb = gb_ref[...]
    gcum = _cumsum_rows(gb)
    gcum_t = _pad_rows(gcum, 128).T
    row = lax.broadcasted_iota(jnp.int32, (cn, cn), 0)
    col = lax.broadcasted_iota(jnp.int32, (cn, cn), 1)
    rep = GDN_VH // GDN_KH
    heads = range(GDN_VH)
    qs = [q_ref[:, kh * GDN_DK:(kh + 1) * GDN_DK] for kh in range(GDN_KH)]
    ks = [k_ref[:, kh * GDN_DK:(kh + 1) * GDN_DK] for kh in range(GDN_KH)]
    kk = [_mdot_nt(k, k) for k in ks]
    qk = [_mdot_nt(q, k) for q, k in zip(qs, ks)]
    gcols = [gcum[:, h:h + 1] for h in heads]
    diffs = [gcols[h] - gcum_t[h:h + 1, :cn] for h in heads]
    egs = [jnp.exp(g) for g in gcols]
    betas = [gb[:, GDN_VH + h:GDN_VH + h + 1] for h in heads]
    lmats = [betas[h] * kk[h // rep] * jnp.exp(jnp.where(col < row, diffs[h], -jnp.inf)) for h in heads]
    ainvs = _unit_lower_inverses(lmats, row, col)
    sols = [_dot3(ainvs[h], jnp.concatenate([betas[h] * v_ref[:, h * GDN_DV:(h + 1) * GDN_DV],
                                             (betas[h] * egs[h]) * ks[h // rep]], axis=1)) for h in heads]
    atts = [qk[h // rep] * jnp.exp(jnp.where(col <= row, diffs[h], -jnp.inf)) for h in heads]
    ss = [s_ref[0, h] for h in heads]
    us = [sols[h][:, :GDN_DV] - _mdot(sols[h][:, GDN_DV:], ss[h]) for h in heads]
    os_ = [egs[h] * _mdot(qs[h // rep], ss[h]) + _mdot(atts[h], us[h]) for h in heads]
    for h in heads:
        gl = gcum[cn - 1:cn, h:h + 1]
        s_ref[0, h] = jnp.exp(gl) * ss[h] + _mdot_tn(ks[h // rep] * jnp.exp(gl - gcols[h]), us[h])
    for h in heads:
        o = os_[h]
        on = o * lax.rsqrt(jnp.mean(o * o, axis=-1, keepdims=True) + RMS_EPS) * nw_ref[...]
        z = z_ref[:, h * GDN_DV:(h + 1) * GDN_DV]
        y_ref[:, h * GDN_DV:(h + 1) * GDN_DV] = (on * (z * jax.nn.sigmoid(z))).astype(y_ref.dtype)


def _gdn_chunks(q, k, v, gb, zsrc, z_col, s0, norm_w, batch, t_pad):
    cn = GDN_CHUNK
    nc = t_pad // cn
    rowblk = lambda w, c=0: pl.BlockSpec((cn, w), lambda b, i, c=c: (b * nc + i, c))
    st = pl.BlockSpec((1, GDN_VH, GDN_DK, GDN_DV), lambda b, i: (b, 0, 0, 0))
    return pl.pallas_call(
        _gdn_chunk_kernel,
        grid=(batch, nc),
        in_specs=[rowblk(GDN_QK), rowblk(GDN_QK), rowblk(GDN_V), rowblk(128), rowblk(GDN_V, z_col), st,
                  pl.BlockSpec((1, GDN_DV), lambda b, i: (0, 0))],
        out_specs=[rowblk(GDN_V), st],
        out_shape=[jax.ShapeDtypeStruct((batch * t_pad, GDN_V), MXU_DTYPE),
                   jax.ShapeDtypeStruct((batch, GDN_VH, GDN_DK, GDN_DV), jnp.float32)],
        compiler_params=pltpu.CompilerParams(dimension_semantics=("parallel", "arbitrary"),
                                             vmem_limit_bytes=VMEM_LIMIT_BYTES),
    )(q, k, v, gb, zsrc, s0.astype(jnp.float32), norm_w.reshape(1, GDN_DV).astype(jnp.float32))


OD_Q = 0
OD_K = ML_H * ML_DK
OD_V = 2 * ML_H * ML_DK
OD_O = OD_V + ML_H * ML_DV
OD_SMALL = OD_O + ML_H * ML_DV
OD_N = OD_SMALL + 128


def _cummax_rows(x):
    row = lax.broadcasted_iota(jnp.int32, x.shape, 0)
    s = 1
    while s < x.shape[0]:
        x = jnp.maximum(x, jnp.where(row >= s, pltpu.roll(x, s, axis=0), -jnp.inf))
        s *= 2
    return x


def _round_mxu(x):
    return x.astype(MXU_DTYPE).astype(jnp.float32)


def _mlstm_chunk_kernel(q_ref, k_ref, v_ref, og_ref, if_ref, bif_ref, nw_ref, c0_ref, n0_ref, m0_ref,
                        y_ref, c_ref, n_ref, m_ref, *, t_valid):
    f32 = jnp.float32
    cn = q_ref.shape[0]
    ci = pl.program_id(1)

    @pl.when(ci == 0)
    def _():
        c_ref[...] = c0_ref[...]
        n_ref[...] = n0_ref[...]
        m_ref[...] = m0_ref[...]

    pre = if_ref[...] + bif_ref[...]
    pos = ci * cn + lax.broadcasted_iota(jnp.int32, pre.shape, 0)
    live = pos < t_valid
    lf = jnp.where(live, jnp.minimum(pre, 0.0) - jnp.log(1.0 + jnp.exp(-jnp.abs(pre))), 0.0)
    fcum = _cumsum_rows(lf)
    a_all = jnp.where(live, pre, NEG) - pltpu.roll(fcum, 128 - ML_H, axis=1)
    amax = _cummax_rows(a_all)
    a_t = _pad_rows(a_all, 128).T
    row = lax.broadcasted_iota(jnp.int32, (cn, cn), 0)
    col = lax.broadcasted_iota(jnp.int32, (cn, cn), 1)
    m_all = m_ref[0]
    heads = range(ML_H)
    qs = [q_ref[:, h * ML_DK:(h + 1) * ML_DK] * (ML_DK ** -0.5) for h in heads]
    ks = [k_ref[:, h * ML_DK:(h + 1) * ML_DK] for h in heads]
    vs = [v_ref[:, h * ML_DV:(h + 1) * ML_DV] for h in heads]
    m_prevs = [m_all[:, h:h + 1] for h in heads]
    fcs = [fcum[:, ML_H + h:ML_H + h + 1] for h in heads]
    mts = [fcs[h] + jnp.maximum(m_prevs[h], amax[:, h:h + 1]) for h in heads]
    qks = [_mdot_nt(qs[h], ks[h]) for h in heads]
    ss = [qks[h] * jnp.exp(jnp.where(col <= row, a_t[h:h + 1, :cn] + (fcs[h] - mts[h]), -jnp.inf)) for h in heads]
    dec0s = [jnp.exp(fcs[h] + m_prevs[h] - mts[h]) for h in heads]
    cms = [c_ref[0, h] for h in heads]
    nvs = [n_ref[0, h:h + 1, :] for h in heads]
    nums = [dec0s[h] * _mdot(qs[h], cms[h]) + _mdot(ss[h], vs[h]) for h in heads]
    dens = [dec0s[h] * jnp.sum(_round_mxu(qs[h]) * _round_mxu(nvs[h]), axis=-1, keepdims=True)
            + jnp.sum(ss[h], axis=-1, keepdims=True) for h in heads]
    hcs = [nums[h] / jnp.maximum(jnp.abs(dens[h]), jnp.exp(-mts[h])) for h in heads]
    m_new = [mts[h][cn - 1:cn] for h in heads]
    for h in heads:
        f_end = fcs[h][cn - 1:cn]
        w = jnp.exp(a_all[:, h:h + 1] + (f_end - m_new[h]))
        dc = jnp.exp(f_end + m_prevs[h] - m_new[h])
        c_ref[0, h] = dc * cms[h] + _mdot_tn(w * ks[h], vs[h])
        n_ref[0, h:h + 1, :] = dc * nvs[h] + jnp.sum(_round_mxu(w) * _round_mxu(ks[h]), axis=0, keepdims=True)
    for h in heads:
        hc = hcs[h]
        hn = hc * lax.rsqrt(jnp.mean(hc * hc, axis=-1, keepdims=True) + RMS_EPS) \
            * nw_ref[:, h * ML_DV:(h + 1) * ML_DV]
        og = og_ref[:, h * ML_DV:(h + 1) * ML_DV]
        y_ref[:, h * ML_DV:(h + 1) * ML_DV] = (hn * jax.nn.sigmoid(og)).astype(y_ref.dtype)
    lane = lax.broadcasted_iota(jnp.int32, m_all.shape, 1)
    out = m_all
    for h in range(ML_H):
        out = jnp.where(lane == h, m_new[h], out)
    m_ref[0] = out


def _mlstm_chunks(proj, b_if, mnorm, c0, n0, m0, batch, t_pad, t_valid):
    cn = ML_CHUNK
    nc = t_pad // cn
    f32 = jnp.float32
    blk = lambda w, c: pl.BlockSpec((cn, w), lambda b, i, c=c: (b * nc + i, c))
    const = lambda shape: pl.BlockSpec(shape, lambda b, i: (0,) * len(shape))
    st_c = pl.BlockSpec((1, ML_H, ML_DK, ML_DV), lambda b, i: (b, 0, 0, 0))
    st_n = pl.BlockSpec((1, ML_H, ML_DK), lambda b, i: (b, 0, 0))
    st_m = pl.BlockSpec((1, 1, 128), lambda b, i: (b, 0, 0))
    bif = jnp.zeros((1, 128), f32).at[0, :2 * ML_H].set(b_if.astype(f32))
    m0p = jnp.zeros((batch, 1, 128), f32).at[:, 0, :ML_H].set(m0.astype(f32))
    hq, hv = ML_H * ML_DK, ML_H * ML_DV
    y, c, n, m = pl.pallas_call(
        functools.partial(_mlstm_chunk_kernel, t_valid=t_valid),
        grid=(batch, nc),
        in_specs=[blk(hq, OD_Q // hq), blk(hq, OD_K // hq), blk(hv, OD_V // hv), blk(hv, OD_O // hv),
                  blk(128, OD_SMALL // 128), const((1, 128)), const((1, hv)), st_c, st_n, st_m],
        out_specs=[blk(hv, 0), st_c, st_n, st_m],
        out_shape=[jax.ShapeDtypeStruct((batch * t_pad, hv), MXU_DTYPE),
                   jax.ShapeDtypeStruct((batch, ML_H, ML_DK, ML_DV), f32),
                   jax.ShapeDtypeStruct((batch, ML_H, ML_DK), f32),
                   jax.ShapeDtypeStruct((batch, 1, 128), f32)],
        compiler_params=pltpu.CompilerParams(dimension_semantics=("parallel", "arbitrary"),
                                             vmem_limit_bytes=VMEM_LIMIT_BYTES),
    )(proj, proj, proj, proj, proj, bif, mnorm.reshape(1, hv).astype(f32), c0.astype(f32), n0.astype(f32), m0p)
    return y, c, n, m[:, 0, :ML_H]


def _permute_odd_w(w_in):
    pad = jnp.zeros((w_in.shape[0], OD_N - w_in.shape[1]), w_in.dtype)
    return jnp.concatenate([w_in, pad], axis=1).astype(MXU_DTYPE)


DEC_PAGES_PER_STEP = 16


def _dec_compress_kernel(pt_ref, *refs, pp):
    kc_refs, vc_refs = refs[:pp], refs[pp:2 * pp]
    wck_ref, wcv_ref, g1_ref, kcb_ref, vcb_ref = refs[2 * pp:]
    per_page = PAGE_SIZE // NSA_BLK

    def compress(page_refs, w_ref, g):
        w = w_ref[:, g * NSA_D:(g + 1) * NSA_D]
        rows = [jnp.sum(r[pl.ds(g, PAGE_SIZE, stride=NSA_G), :].reshape(per_page, NSA_BLK, NSA_D) * w[None], axis=1)
                for r in page_refs]
        return jnp.concatenate(rows, axis=0)

    for g in range(NSA_G):
        sl = slice(g * NSA_D, (g + 1) * NSA_D)
        kcb_ref[0, :, sl] = _group_rms(compress(kc_refs, wck_ref, g), g1_ref[...])
        vcb_ref[0, :, sl] = compress(vc_refs, wcv_ref, g)


def _dec_compress(page_table, pool_kc, pool_vc, layer, w_ck, w_cv, g1):
    nseq, npages = page_table.shape
    pp = DEC_PAGES_PER_STEP
    per_page = PAGE_SIZE // NSA_BLK
    nphys = pool_kc.shape[1]
    base = layer * nphys
    kc = pool_kc.reshape(-1, NSA_D)
    vc = pool_vc.reshape(-1, NSA_D)
    page = lambda j: pl.BlockSpec((PAGE_SIZE * NSA_G, NSA_D), lambda b, i, pt, j=j: (base + pt[b, i * pp + j], 0))
    const = lambda shape: pl.BlockSpec(shape, lambda b, i, pt: (0,) * len(shape))
    out = pl.BlockSpec((1, pp * per_page, NSA_KV), lambda b, i, pt: (b, i, 0))
    nb = npages * per_page
    return pl.pallas_call(
        functools.partial(_dec_compress_kernel, pp=pp),
        grid_spec=pltpu.PrefetchScalarGridSpec(
            num_scalar_prefetch=1, grid=(nseq, npages // pp),
            in_specs=[page(j) for j in range(pp)] * 2 + [const((NSA_BLK, NSA_KV))] * 2 + [const((1, NSA_D))],
            out_specs=[out, out]),
        out_shape=[jax.ShapeDtypeStruct((nseq, nb, NSA_KV), jnp.float32)] * 2,
        compiler_params=pltpu.CompilerParams(dimension_semantics=("parallel", "arbitrary"),
                                             vmem_limit_bytes=VMEM_LIMIT_BYTES),
    )(page_table, *([kc] * pp), *([vc] * pp), w_ck.reshape(NSA_BLK, NSA_KV), w_cv.reshape(NSA_BLK, NSA_KV),
      g1.reshape(1, NSA_D))


def _rows_by_head(head_row, fn, nheads):
    out = fn(0)
    out = jnp.broadcast_to(out, (nheads, out.shape[1]))
    for h in range(1, nheads):
        out = jnp.where(head_row == h, fn(h), out)
    return out


def _dec_select_kernel(tab_ref, q_ref, kcb_ref, vcb_ref, oc_ref, idx_ref, *, qpos):
    f32 = jnp.float32
    nbp = kcb_ref.shape[1]
    q = q_ref[0]
    head_row = lax.broadcasted_iota(jnp.int32, (NSA_H, 1), 0)
    blk = lax.broadcasted_iota(jnp.int32, (1, nbp), 1)
    blkf = blk.astype(f32)
    dist = qpos - (blk * NSA_BLK + NSA_BLK - 1)
    ok = dist >= 0
    cur = qpos // NSA_BLK
    lg = jnp.zeros((NSA_H, nbp), f32)
    for g in range(NSA_G):
        lg_g = _mdot_nt(q, kcb_ref[0, :, g * NSA_D:(g + 1) * NSA_D])
        lg = jnp.where(head_row // NSA_HG == g, lg_g, lg)
    lg = lg + _rows_by_head(head_row, lambda h: _bias_of_dist(dist, tab_ref, h), NSA_H)
    lg = jnp.where(ok, lg, NEG)
    e = jnp.exp(lg - jnp.max(lg, axis=-1, keepdims=True))
    p = jnp.where(ok, e / jnp.sum(e, axis=-1, keepdims=True), 0.0)
    oc = jnp.zeros((NSA_H, NSA_D), f32)
    lane = lax.broadcasted_iota(jnp.int32, (1, 128), 1)
    for g in range(NSA_G):
        in_g = head_row // NSA_HG == g
        oc = jnp.where(in_g, _mdot(p, vcb_ref[0, :, g * NSA_D:(g + 1) * NSA_D]), oc)
        score = jnp.sum(jnp.where(in_g, p, 0.0), axis=0, keepdims=True)
        score = jnp.where(blk < cur, score, -1.0)
        idx = jnp.full((1, 128), -1.0, f32)
        for r in range(NSA_TOPK):
            mx = jnp.max(score, axis=-1, keepdims=True)
            first = jnp.min(jnp.where(score == mx, blkf, float(nbp)), axis=-1, keepdims=True)
            idx = jnp.where(lane == r, jnp.where(mx >= 0.0, first, -1.0), idx)
            score = jnp.where(blkf == first, -2.0, score)
        idx_ref[0, g:g + 1, :] = idx.astype(jnp.int32)
    oc_ref[0] = oc


def _dec_select(tab, qh, kcb, vcb, qpos):
    nseq = qh.shape[0]
    nbp = kcb.shape[1]
    seq = lambda shape: pl.BlockSpec((1,) + shape, lambda b: (b, 0, 0))
    return pl.pallas_call(
        functools.partial(_dec_select_kernel, qpos=qpos),
        grid=(nseq,),
        in_specs=[pl.BlockSpec(memory_space=pltpu.SMEM), seq((NSA_H, NSA_D)), seq((nbp, NSA_KV)),
                  seq((nbp, NSA_KV))],
        out_specs=[seq((NSA_H, NSA_D)), seq((NSA_G, 128))],
        out_shape=[jax.ShapeDtypeStruct((nseq, NSA_H, NSA_D), jnp.float32),
                   jax.ShapeDtypeStruct((nseq, NSA_G, 128), jnp.int32)],
        compiler_params=pltpu.CompilerParams(dimension_semantics=("parallel",),
                                             vmem_limit_bytes=VMEM_LIMIT_BYTES),
    )(tab.astype(jnp.float32), qh, kcb, vcb)


def _dec_attend_kernel(pt_ref, idx_ref, tab_ref, q_ref, *refs, qpos):
    f32 = jnp.float32
    ks_refs, vs_refs = refs[:NSA_TOPK], refs[NSA_TOPK:2 * NSA_TOPK]
    ksn_ref, vsn_ref, kwb_ref, vwb_ref, kwn_ref, vwn_ref, oc_ref, ng_ref, y_ref = refs[2 * NSA_TOPK:]
    b = pl.program_id(0)
    g = pl.program_id(1)
    cur = qpos // NSA_BLK
    head_row = lax.broadcasted_iota(jnp.int32, (NSA_HG, 1), 0)
    q4 = q_ref[0, pl.ds(g * NSA_HG, NSA_HG), :]
    q4r = _round_mxu(q4.astype(f32))
    new_row = b * NSA_G + g
    tab0 = _rows_by_head(head_row, lambda h: jnp.full((1, 1), tab_ref[0, g * NSA_HG + h], f32), NSA_HG)

    def attend(keys, vals, dist, ok, k_new, v_new):
        lg = _mdot_nt(q4, keys)
        lg = lg + _rows_by_head(head_row, lambda h: _bias_of_dist(dist, tab_ref, g * NSA_HG + h), NSA_HG)
        lg = jnp.where(ok, lg, NEG)
        lg_new = jnp.sum(q4r * _round_mxu(k_new), axis=-1, keepdims=True) + tab0
        m = jnp.maximum(jnp.max(lg, axis=-1, keepdims=True), lg_new)
        e = jnp.exp(lg - m)
        e_new = jnp.exp(lg_new - m)
        den = jnp.sum(e, axis=-1, keepdims=True) + e_new
        return _mdot(e / den, vals) + _round_mxu(e_new / den) * _round_mxu(v_new)

    pieces, oks = [], []
    jrow = lax.broadcasted_iota(jnp.int32, (1, NSA_BLK), 1)
    for s in range(NSA_TOPK):
        blk_id = idx_ref[b, g, s]
        valid = jnp.logical_and(blk_id >= 0, blk_id < cur)
        pos = jnp.maximum(blk_id, 0) * NSA_BLK + jrow
        pieces.append(qpos - pos)
        oks.append(jnp.where(valid, 1.0, 0.0) * jnp.where(pos <= qpos, 1.0, 0.0))
    dist_s = jnp.concatenate(pieces, axis=1)
    ok_s = jnp.concatenate(oks, axis=1) > 0.5
    mine = lambda r: r[pl.ds(g, NSA_BLK, stride=NSA_G), :]
    k_sel = jnp.concatenate([mine(r) for r in ks_refs], axis=0)
    v_sel = jnp.concatenate([mine(r) for r in vs_refs], axis=0)
    o_s = attend(k_sel, v_sel, dist_s, ok_s, ksn_ref[pl.ds(new_row, 1), :], vsn_ref[pl.ds(new_row, 1), :])

    wb = kwb_ref.shape[1]
    dist_w = wb - lax.broadcasted_iota(jnp.int32, (1, wb), 1)
    ok_w = dist_w < NSA_WIN
    o_w = attend(kwb_ref[0], vwb_ref[0], dist_w, ok_w, kwn_ref[pl.ds(new_row, 1), :], vwn_ref[pl.ds(new_row, 1), :])

    gates = jax.nn.sigmoid(ng_ref[pl.ds(b, 1), :])
    lane = lax.broadcasted_iota(jnp.int32, (1, 128), 1)

    def gate(branch):
        def one(h):
            c = 2 * GDN_VH + (g * NSA_HG + h) * 3 + branch
            return jnp.sum(jnp.where(lane == c, gates, 0.0), axis=-1, keepdims=True)
        return _rows_by_head(head_row, one, NSA_HG)

    oc4 = oc_ref[0, pl.ds(g * NSA_HG, NSA_HG), :]
    y_ref[0, pl.ds(g * NSA_HG, NSA_HG), :] = oc4 * gate(0) + o_s * gate(1) + o_w * gate(2)


def _dec_attend(tab, page_table, idx, qh, pool_ks, pool_vs, layer, ksn, vsn, buf_kw, buf_vw, kwn, vwn, oc, ng,
                qpos):
    nseq = qh.shape[0]
    base = layer * pool_ks.shape[1]
    wb = buf_kw.shape[1]
    ks = pool_ks.reshape(-1, NSA_D)
    vs = pool_vs.reshape(-1, NSA_D)
    per_page = PAGE_SIZE // NSA_BLK

    def sel(s):
        def index(b, g, pt, ix, s=s):
            blk_id = jnp.maximum(ix[b, g, s], 0)
            return ((base + pt[b, blk_id // per_page]) * per_page + blk_id % per_page, 0)
        return pl.BlockSpec((NSA_BLK * NSA_G, NSA_D), index)

    whole = lambda a: pl.BlockSpec(a.shape, lambda b, g, pt, ix: (0,) * a.ndim)
    heads = pl.BlockSpec((1, NSA_H, NSA_D), lambda b, g, pt, ix: (b, 0, 0))
    win = pl.BlockSpec((1, wb, NSA_D), lambda b, g, pt, ix: (b, 0, g))
    rows = lambda a: a.reshape(nseq * NSA_G, NSA_D)
    new = [rows(ksn), rows(vsn), rows(kwn), rows(vwn)]
    return pl.pallas_call(
        functools.partial(_dec_attend_kernel, qpos=qpos),
        grid_spec=pltpu.PrefetchScalarGridSpec(
            num_scalar_prefetch=2, grid=(nseq, NSA_G),
            in_specs=[pl.BlockSpec(memory_space=pltpu.SMEM), heads] + [sel(s) for s in range(NSA_TOPK)] * 2
                     + [whole(new[0]), whole(new[1]), win, win, whole(new[2]), whole(new[3]), heads, whole(ng)],
            out_specs=heads),
        out_shape=jax.ShapeDtypeStruct((nseq, NSA_H, NSA_D), jnp.float32),
        compiler_params=pltpu.CompilerParams(dimension_semantics=("parallel", "arbitrary"),
                                             vmem_limit_bytes=VMEM_LIMIT_BYTES),
    )(page_table, idx, tab.astype(jnp.float32), qh, *([ks] * NSA_TOPK), *([vs] * NSA_TOPK), new[0], new[1],
      buf_kw.reshape(nseq, wb, NSA_KV), buf_vw.reshape(nseq, wb, NSA_KV), new[2], new[3], oc, ng)


def _nsa_decode(proj, tab, qk_gain, w_ck, w_cv, caches, nseq):
    assert PAST_LEN % PAGE_SIZE == 0 and DEC_SEQ == 1
    pad = jnp.pad(proj, ((0, 512 - nseq), (0, 0)))
    qn, ksn, kwn, _, _, _, _, _, _ = _nsa_prep(pad, qk_gain, w_ck, w_cv)
    qh = qn[:nseq].reshape(nseq, NSA_H, NSA_D).astype(jnp.float32)
    kv = lambda i: proj[:, EV_KV + i * NSA_KV:EV_KV + (i + 1) * NSA_KV]
    layer = caches['layer']
    kcb, vcb = _dec_compress(caches['page_table'], caches['kc'], caches['vc'], layer, w_ck, w_cv, qk_gain[1])
    oc, idx = _dec_select(tab, qh, kcb, vcb, PAST_LEN)
    ng = proj[:, EV_SMALL:EV_SMALL + 128]
    y = _dec_attend(tab, caches['page_table'], idx, qh, caches['ks'], caches['vs'], layer, ksn[:nseq], kv(3),
                    caches['kw'], caches['vw'], kwn[:nseq], kv(5), oc, ng, PAST_LEN)
    return y.reshape(nseq, NSA_H * NSA_D), ksn[:nseq], kwn[:nseq]


def _rms_cast_kernel(x_ref, g_ref, o_ref):
    x = x_ref[...]
    y = x * lax.rsqrt(jnp.mean(x * x, axis=-1, keepdims=True) + RMS_EPS) * g_ref[...]
    o_ref[...] = y.astype(o_ref.dtype)


def _rms_cast(h, g):
    m, d = h.shape
    tm = _pick(m, (1024, 256, 128))
    return pl.pallas_call(
        _rms_cast_kernel,
        grid=(m // tm,),
        in_specs=[pl.BlockSpec((tm, d), lambda i: (i, 0)), pl.BlockSpec((1, d), lambda i: (0, 0))],
        out_specs=pl.BlockSpec((tm, d), lambda i: (i, 0)),
        out_shape=jax.ShapeDtypeStruct((m, d), MXU_DTYPE),
        compiler_params=pltpu.CompilerParams(dimension_semantics=("parallel",),
                                             vmem_limit_bytes=VMEM_LIMIT_BYTES),
    )(h, g.reshape(1, d).astype(jnp.float32))


def _mm2_kernel(xa_ref, xb_ref, w_ref, r_ref, o_ref):
    ka = xa_ref.shape[1]
    y = jnp.dot(xa_ref[...], w_ref[:ka, :].astype(MXU_DTYPE), preferred_element_type=jnp.float32)
    y = y + jnp.dot(xb_ref[...], w_ref[ka:, :].astype(MXU_DTYPE), preferred_element_type=jnp.float32)
    o_ref[...] = y + r_ref[...]


def _matmul2_res(xa, xb, w, res, layer, tiles=None):
    m, ka = xa.shape
    kb = xb.shape[1]
    n = w.shape[-1]
    tm = _pick(m, (tiles[0],) if tiles else (512, 256, 128))
    tn = _pick(n, (tiles[1],) if tiles else (512, 256, 128))
    return pl.pallas_call(
        _mm2_kernel,
        grid=(m // tm, n // tn),
        in_specs=[pl.BlockSpec((tm, ka), lambda i, j: (i, 0)), pl.BlockSpec((tm, kb), lambda i, j: (i, 0)),
                  pl.BlockSpec((None, ka + kb, tn), lambda i, j: (layer, 0, j)),
                  pl.BlockSpec((tm, tn), lambda i, j: (i, j))],
        out_specs=pl.BlockSpec((tm, tn), lambda i, j: (i, j)),
        out_shape=jax.ShapeDtypeStruct((m, n), jnp.float32),
        compiler_params=pltpu.CompilerParams(dimension_semantics=("parallel", "parallel"),
                                             vmem_limit_bytes=VMEM_LIMIT_BYTES),
    )(xa, xb, w, res)


LAYER_TILES = (
    dict(inp=(2048, 640), out=(2048, 512), mlp=((2048, 512), (2048, 512))),
    dict(inp=(2048, 896), out=(2048, 512), mlp=((2048, 512), (1024, 512, 4096))),
    dict(inp=(2048, 1152), out=(2048, 512), mlp=((1024, 1024), (1024, 1024))),
    dict(inp=(2048, 896), out=(2048, 512), mlp=((2048, 512), (2048, 512, 1024))),
)


def _mlp_block(h, g, w1, w2, layer, tiles):
    mid = _matmul(_rms_cast(h, g), w1, act='relu2', out_dtype=MXU_DTYPE, tiles=tiles[0], layer=layer)
    return _matmul(mid, w2, res=h, tiles=tiles[1], layer=layer)


def _pad_time(x, t_pad):
    b, d = x.shape
    return jnp.zeros((b, t_pad, d), x.dtype).at[:, 0, :].set(x).reshape(b * t_pad, d)


def _even_layer(h, g_mix, w_in, w_out, conv_w, a_log, dt_bias, gnorm, qk_gain, w_ck, w_cv, tab, bias_tiles,
                batch, t_len, conv_st, s_st, decode, tiles):
    f32 = jnp.float32
    proj = _matmul(_rms_cast(h, g_mix), w_in, tiles=tiles['inp'])
    t_pad = t_len if decode is None else GDN_CHUNK
    pp = proj if decode is None else _pad_time(proj, t_pad)
    first = jnp.pad(conv_st.astype(f32), ((0, 0), (8 - (GDN_CONV - 1), 0), (0, 0)))
    q, k, v, gb = _gdn_prep(pp, first, pp, EV_SMALL // 128, conv_w, a_log, dt_bias, batch, t_pad, t_len)
    ya, s_new = _gdn_chunks(q, k, v, gb, pp, EV_Z // GDN_V, s_st, gnorm, batch, t_pad)
    p3 = proj.reshape(batch, t_len, EV_N)
    xin_tail = jnp.concatenate([conv_st.astype(f32), p3[..., :GDN_CONV_CH]], axis=1)[:, -(GDN_CONV - 1):]
    hd = lambda t: t.reshape(batch, t_len, NSA_G, NSA_D)
    kv = lambda i: hd(p3[..., EV_KV + i * NSA_KV:EV_KV + (i + 1) * NSA_KV])
    if decode is None:
        qn, ksn, kwn, ksb, vsb, kwb, vwb, kcb, vcb = _nsa_prep(proj, qk_gain, w_ck, w_cv)
        yb = _nsa_prompt_attn(tab, bias_tiles, proj, qn, kcb, vcb, ksb, vsb, kwb, vwb, batch, t_len)
        nw = min(NSA_WIN, t_len)
        kw_out, vw_out = hd(kwn)[:, -nw:], kv(5)[:, -nw:]
        ks_out = hd(ksn)
    else:
        ya = ya.reshape(batch, t_pad, GDN_V)[:, 0]
        yb, ksn, kwn = _nsa_decode(proj, tab, qk_gain, w_ck, w_cv, decode, batch)
        yb = yb.astype(MXU_DTYPE)
        ks_out = hd(ksn)
        kw_out = jnp.concatenate([decode['kw'][:, 1:], hd(kwn)], axis=1)
        vw_out = jnp.concatenate([decode['vw'][:, 1:], kv(5)], axis=1)
    h = _matmul2_res(ya, yb, w_out[0], h, w_out[1], tiles=tiles['out'])
    return h, (kv(0), kv(1), ks_out, kv(3), kw_out, vw_out, xin_tail, s_new)


def _odd_layer(h, g_mix, w_in, w_out, b_if, mnorm, batch, t_len, c0, n0, m0, decode, tiles):
    proj = _matmul(_rms_cast(h, g_mix), w_in, tiles=tiles['inp'])
    t_pad = t_len if not decode else ML_CHUNK
    pp = proj if not decode else _pad_time(proj, t_pad)
    y, c, n, m = _mlstm_chunks(pp, b_if, mnorm, c0, n0, m0, batch, t_pad, t_len)
    if decode:
        y = y.reshape(batch, t_pad, ODD_MIX)[:, 0]
    return _matmul(y, w_out[0], res=h, tiles=tiles['out'], layer=w_out[1]), (c, n, m)


def kernel(x_prompt, x_sample, cache_kc, cache_vc, cache_ks, cache_vs, cache_kw, cache_vw,
           state_gdn_conv, state_gdn_s, state_ml_c, state_ml_n, state_ml_m, page_table,
           rel_bias, norm_mix, norm_mlp, w_in_even, w_out_even, gdn_conv_w, gdn_a_log,
           gdn_dt_bias, gdn_norm_w, nsa_qk_gain, nsa_w_ck, nsa_w_cv, w_in_odd, w_out_odd,
           ml_b_if, ml_norm_w, w_ff1, w_ff2):
    f32 = jnp.float32
    bp, tp, d = x_prompt.shape
    bs, ts, _ = x_sample.shape
    hp = x_prompt.reshape(bp * tp, d)
    hs = x_sample.reshape(bs * ts, d)
    bias_tiles = _bias_tiles(rel_bias)
    ev_p, ev_s, od_p, od_s = [], [], [], []
    for l in range(DEPTH):
        j = l // 2
        tiles = LAYER_TILES[l]
        if l % 2 == 0:
            w_in = _permute_even_w(w_in_even[j])
            args = (w_in, (w_out_even, j), gdn_conv_w[j], gdn_a_log[j], gdn_dt_bias[j], gdn_norm_w[j],
                    nsa_qk_gain[j], nsa_w_ck[j], nsa_w_cv[j], rel_bias, bias_tiles)
            hp, stp = _even_layer(hp, norm_mix[l], *args, bp, tp,
                                  jnp.zeros((bp, GDN_CONV - 1, GDN_CONV_CH), f32),
                                  jnp.zeros((bp, GDN_VH, GDN_DK, GDN_DV), f32), None, tiles)
            caches = dict(kc=cache_kc, vc=cache_vc, ks=cache_ks, vs=cache_vs, layer=j, kw=cache_kw[j],
                          vw=cache_vw[j], page_table=page_table)
            hs, sts = _even_layer(hs, norm_mix[l], *args, bs, ts, state_gdn_conv[j], state_gdn_s[j], caches,
                                  tiles)
            ev_p.append(stp)
            ev_s.append(sts)
        else:
            w_in = _permute_odd_w(w_in_odd[j])
            hp, stp = _odd_layer(hp, norm_mix[l], w_in, (w_out_odd, j), ml_b_if[j], ml_norm_w[j], bp, tp,
                                 jnp.zeros((bp, ML_H, ML_DK, ML_DV), f32), jnp.zeros((bp, ML_H, ML_DK), f32),
                                 jnp.zeros((bp, ML_H), f32), False, tiles)
            hs, sts = _odd_layer(hs, norm_mix[l], w_in, (w_out_odd, j), ml_b_if[j], ml_norm_w[j], bs, ts,
                                 state_ml_c[j], state_ml_n[j], state_ml_m[j], True, tiles)
            od_p.append(stp)
            od_s.append(sts)
        hp = _mlp_block(hp, norm_mlp[l], w_ff1, w_ff2, l, tiles['mlp'])
        hs = _mlp_block(hs, norm_mlp[l], w_ff1, w_ff2, l, tiles['mlp'])

    kc_p, vc_p, ks_p, vs_p, kw_p, vw_p, conv_p, gdn_p = [jnp.stack(a) for a in zip(*ev_p)]
    kc_s, vc_s, ks_s, vs_s, kw_s, vw_s, conv_s, gdn_s = [jnp.stack(a) for a in zip(*ev_s)]
    mlc_p, mln_p, mlm_p = [jnp.stack(a) for a in zip(*od_p)]
    mlc_s, mln_s, mlm_s = [jnp.stack(a) for a in zip(*od_s)]
    return (hp.reshape(bp, tp, d), hs.reshape(bs, ts, d), kc_p, kc_s, vc_p, vc_s, ks_p, ks_s, vs_p, vs_s,
            kw_p, kw_s, vw_p, vw_s, conv_p, conv_s, gdn_p, gdn_s, mlc_p, mlc_s, mln_p, mln_s, mlm_p, mlm_s)
```

```python
import functools
import math

import jax
import jax.numpy as jnp
import numpy as np
from jax import lax
from jax.experimental import pallas as pl
from jax.experimental.pallas import tpu as pltpu

DEPTH = 4
DEC_SEQ = 1
PAST_LEN = 16384
PAGE_SIZE = 128
RMS_EPS = 1e-6
NEG = -1e30

GDN_KH = 4
GDN_VH = 8
GDN_DK = 128
GDN_DV = 128
GDN_CONV = 4
GDN_CHUNK = 64
GDN_QK = GDN_KH * GDN_DK
GDN_V = GDN_VH * GDN_DV
GDN_CONV_CH = 2 * GDN_QK + GDN_V

NSA_H = 8
NSA_G = 2
NSA_HG = NSA_H // NSA_G
NSA_D = 128
NSA_BLK = 64
NSA_TOPK = 15
NSA_WIN = 512
NSA_QB = 128
NSA_KV = NSA_G * NSA_D

REL_BUCKETS = 32
REL_EXACT = 16
REL_MAX_DIST = 2048

ML_H = 8
ML_DK = 128
ML_DV = 256
ML_CHUNK = 64

ODD_MIX = ML_H * ML_DV

VMEM_LIMIT_BYTES = 56 * 1024 * 1024
MXU_DTYPE = jnp.bfloat16


def _mm_kernel(x_ref, w_ref, *rest, nk, act, has_res):
    rest = list(rest)
    r_ref = rest.pop(0) if has_res else None
    o_ref = rest.pop(0)

    def finish(y):
        if act == 'relu2':
            y = jnp.square(jnp.maximum(y, 0.0))
        if has_res:
            y = y + r_ref[...]
        o_ref[...] = y.astype(o_ref.dtype)

    d = jnp.dot(x_ref[...], w_ref[...].astype(MXU_DTYPE), preferred_element_type=jnp.float32)
    if nk == 1:
        finish(d)
        return
    acc_ref = rest.pop(0)
    k = pl.program_id(2)

    @pl.when(k == 0)
    def _():
        acc_ref[...] = d

    @pl.when(k > 0)
    def _():
        acc_ref[...] += d

    @pl.when(k == nk - 1)
    def _():
        finish(acc_ref[...])


def _pick(n, prefs):
    for p in prefs:
        if n % p == 0:
            return p
    return n


def _matmul(x, w, *, act=None, res=None, out_dtype=jnp.float32, tiles=None, layer=None):
    m, kdim = x.shape
    n = w.shape[-1]
    tm = _pick(m, (tiles[0],) if tiles else (512, 256, 128))
    tn = _pick(n, (tiles[1],) if tiles else (512, 896, 384, 256, 128))
    tk = _pick(kdim, (tiles[2],) if tiles and len(tiles) > 2 else (2048, 1024, 512))
    nk = kdim // tk
    if layer is None:
        w_spec = pl.BlockSpec((tk, tn), lambda i, j, k: (k, j))
    else:
        w_spec = pl.BlockSpec((None, tk, tn), lambda i, j, k: (layer, k, j))
    in_specs = [pl.BlockSpec((tm, tk), lambda i, j, k: (i, k)), w_spec]
    args = [x, w]
    if res is not None:
        in_specs.append(pl.BlockSpec((tm, tn), lambda i, j, k: (i, j)))
        args.append(res)
    return pl.pallas_call(
        functools.partial(_mm_kernel, nk=nk, act=act, has_res=res is not None),
        grid=(m // tm, n // tn, nk),
        in_specs=in_specs,
        out_specs=pl.BlockSpec((tm, tn), lambda i, j, k: (i, j)),
        out_shape=jax.ShapeDtypeStruct((m, n), out_dtype),
        scratch_shapes=[pltpu.VMEM((tm, tn), jnp.float32)] if nk > 1 else [],
        compiler_params=pltpu.CompilerParams(
            dimension_semantics=("parallel", "parallel", "arbitrary"),
            vmem_limit_bytes=VMEM_LIMIT_BYTES),
    )(*args)


EV_CONV = 0
EV_Z = GDN_CONV_CH
EV_NQ = EV_Z + GDN_V
EV_KV = EV_NQ + NSA_H * NSA_D
EV_SMALL = EV_KV + 6 * NSA_KV
EV_N = EV_SMALL + 128
EV_ORIG_SMALL = 2 * GDN_QK + 2 * GDN_V
EV_ORIG_N = EV_ORIG_SMALL + 2 * GDN_VH + NSA_H * NSA_D + 6 * NSA_KV + 3 * NSA_H


def _permute_even_w(w_in):
    ab = w_in[:, EV_ORIG_SMALL:EV_ORIG_SMALL + 2 * GDN_VH]
    big = w_in[:, EV_ORIG_SMALL + 2 * GDN_VH:EV_ORIG_N - 3 * NSA_H]
    ng = w_in[:, EV_ORIG_N - 3 * NSA_H:]
    pad = jnp.zeros((w_in.shape[0], EV_N - EV_ORIG_N), w_in.dtype)
    return jnp.concatenate([w_in[:, :EV_ORIG_SMALL], big, ab, ng, pad], axis=1).astype(MXU_DTYPE)


def _bucket_thresholds():
    n = np.arange(0, 4 * REL_MAX_DIST)
    nf = np.maximum(n, REL_EXACT).astype(np.float32)
    large = REL_EXACT + (np.log(nf / np.float32(REL_EXACT)) / np.float32(math.log(REL_MAX_DIST / REL_EXACT))
                         * np.float32(REL_BUCKETS - REL_EXACT)).astype(np.int32)
    b = np.where(n < REL_EXACT, n, np.minimum(large, REL_BUCKETS - 1))
    return tuple(int(np.argmax(b >= k)) for k in range(REL_BUCKETS))


BUCKET_THR = _bucket_thresholds()
BIAS_TILES = 14
assert BUCKET_THR[-1] <= (BIAS_TILES - 1) * 128 - 127


def _bias_of_dist(dist, tab_ref, head):
    val = jnp.full(dist.shape, tab_ref[0, head], jnp.float32)
    for k in range(1, REL_BUCKETS):
        val = jnp.where(dist >= BUCKET_THR[k], tab_ref[k, head], val)
    return val


def _bias_tiles_kernel(tab_ref, o_ref):
    h = pl.program_id(0)
    d = pl.program_id(1)
    i = lax.broadcasted_iota(jnp.int32, (128, 128), 1)
    j = lax.broadcasted_iota(jnp.int32, (128, 128), 0)
    dist = d * 128 + i - j
    val = jnp.full(dist.shape, tab_ref[0, h], jnp.float32)
    for k in range(1, REL_BUCKETS):
        val = jnp.where(dist >= BUCKET_THR[k], tab_ref[k, h], val)
    o_ref[0, 0] = val


def _bias_tiles(tab):
    return pl.pallas_call(
        _bias_tiles_kernel,
        grid=(NSA_H, BIAS_TILES),
        in_specs=[pl.BlockSpec(memory_space=pltpu.SMEM)],
        out_specs=pl.BlockSpec((1, 1, 128, 128), lambda h, d: (h, d, 0, 0)),
        out_shape=jax.ShapeDtypeStruct((NSA_H, BIAS_TILES, 128, 128), jnp.float32),
    )(tab.astype(jnp.float32))


def _group_rms(x, gain_row, scale=1.0):
    ms = jnp.mean(x * x, axis=-1, keepdims=True)
    y = x * lax.rsqrt(ms + RMS_EPS) * gain_row
    return y * scale if scale != 1.0 else y


def _nsa_prep_kernel(nq_ref, kc_ref, vc_ref, ks_ref, vs_ref, kw_ref, vw_ref, gain_ref, wck_ref, wcv_ref,
                     qn_ref, ksn_ref, kwn_ref, ksb_ref, vst_ref, kwb_ref, vwt_ref, kcb_ref, vcb_ref):
    tm = nq_ref.shape[0]
    g0 = gain_ref[0:1, :]
    g1 = gain_ref[1:2, :]
    g2 = gain_ref[2:3, :]
    g3 = gain_ref[3:4, :]
    for h in range(NSA_H):
        sl = slice(h * NSA_D, (h + 1) * NSA_D)
        qn_ref[:, sl] = _group_rms(nq_ref[:, sl], g0, NSA_D ** -0.5).astype(qn_ref.dtype)
    for g in range(NSA_G):
        sl = slice(g * NSA_D, (g + 1) * NSA_D)
        ksn = _group_rms(ks_ref[:, sl], g2)
        kwn = _group_rms(kw_ref[:, sl], g3)
        ksn_ref[:, sl] = ksn
        kwn_ref[:, sl] = kwn
        ksb_ref[:, sl] = ksn.astype(ksb_ref.dtype)
        kwb_ref[:, sl] = kwn.astype(kwb_ref.dtype)
    for src, dst in ((vs_ref, vst_ref), (vw_ref, vwt_ref)):
        vt = src[...].T
        tk = dst.shape[2]
        for c in range(tm // tk):
            dst[c] = vt[:, c * tk:(c + 1) * tk].astype(dst.dtype)
    nblk = tm // NSA_BLK
    kc3 = kc_ref[...].reshape(nblk, NSA_BLK, NSA_KV)
    vc3 = vc_ref[...].reshape(nblk, NSA_BLK, NSA_KV)
    kcb = jnp.sum(kc3 * wck_ref[...][None], axis=1)
    vcb = jnp.sum(vc3 * wcv_ref[...][None], axis=1)
    for g in range(NSA_G):
        sl = slice(g * NSA_D, (g + 1) * NSA_D)
        kcb_ref[:, sl] = _group_rms(kcb[:, sl], g1)
    vcb_ref[...] = vcb


def _nsa_prep(proj, gain, w_ck, w_cv):
    m = proj.shape[0]
    tm = 512
    kvb = EV_KV // NSA_KV
    f32, bf16 = jnp.float32, MXU_DTYPE

    def kv_spec(i):
        return pl.BlockSpec((tm, NSA_KV), lambda r, i=i: (r, kvb + i))

    row = lambda w: pl.BlockSpec((tm, w), lambda r: (r, 0))
    slabs = lambda tk: pl.BlockSpec((tm // tk, NSA_KV, tk), lambda r: (r, 0, 0))
    full = lambda a: pl.BlockSpec(a.shape, lambda r: (0,) * a.ndim)
    wck = w_ck.reshape(NSA_BLK, NSA_KV)
    wcv = w_cv.reshape(NSA_BLK, NSA_KV)
    outs = pl.pallas_call(
        _nsa_prep_kernel,
        grid=(m // tm,),
        in_specs=[pl.BlockSpec((tm, NSA_H * NSA_D), lambda r: (r, EV_NQ // (NSA_H * NSA_D)))]
                 + [kv_spec(i) for i in range(6)] + [full(gain), full(wck), full(wcv)],
        out_specs=[row(NSA_H * NSA_D), row(NSA_KV), row(NSA_KV), row(NSA_KV), slabs(NSA_TK), row(NSA_KV),
                   slabs(NSA_QB)] + [pl.BlockSpec((tm // NSA_BLK, NSA_KV), lambda r: (r, 0))] * 2,
        out_shape=[jax.ShapeDtypeStruct((m, NSA_H * NSA_D), bf16),
                   jax.ShapeDtypeStruct((m, NSA_KV), f32), jax.ShapeDtypeStruct((m, NSA_KV), f32),
                   jax.ShapeDtypeStruct((m, NSA_KV), bf16), jax.ShapeDtypeStruct((m // NSA_TK, NSA_KV, NSA_TK), bf16),
                   jax.ShapeDtypeStruct((m, NSA_KV), bf16), jax.ShapeDtypeStruct((m // NSA_QB, NSA_KV, NSA_QB), bf16)]
                  + [jax.ShapeDtypeStruct((m // NSA_BLK, NSA_KV), f32)] * 2,
        compiler_params=pltpu.CompilerParams(dimension_semantics=("parallel",),
                                             vmem_limit_bytes=VMEM_LIMIT_BYTES),
    )(proj, proj, proj, proj, proj, proj, proj, gain, wck, wcv)
    return outs


NSA_TK = 256


def _pad_rows(x, rows):
    if x.shape[0] == rows:
        return x
    return jnp.concatenate([x, jnp.zeros((rows - x.shape[0],) + x.shape[1:], x.dtype)], axis=0)


def _nsa_attn_kernel(tab_ref, q_ref, kcb_ref, vcb_ref, ks_ref, vst_ref, kw_ref, vwt_ref, bias_ref, ng_ref,
                     o_ref, m_scr, l_scr, acc_scr):
    f32, bf16 = jnp.float32, MXU_DTYPE
    qb = pl.program_id(1)
    nb = kcb_ref.shape[0]
    qn = NSA_QB
    gates = jax.nn.sigmoid(ng_ref[...])
    tsub = NSA_TK // qn
    shift = NSA_BLK.bit_length() - 1

    heads = range(NSA_H)
    gls = [slice(g * NSA_D, (g + 1) * NSA_D) for g in range(NSA_G)]
    qs = [q_ref[:, hh * NSA_D:(hh + 1) * NSA_D] for hh in heads]
    qts = [q.astype(f32).T.astype(bf16) for q in qs]

    blk = lax.broadcasted_iota(jnp.int32, (nb, qn), 0)
    blkf = blk.astype(f32)
    qpos = qb * qn + lax.broadcasted_iota(jnp.int32, (nb, qn), 1)
    dist_c = qpos - (blk * NSA_BLK + NSA_BLK - 1)
    ok_c = dist_c >= 0
    cur = lax.shift_right_logical(qpos, shift)
    o_c, scores = [], []
    for g in range(NSA_G):
        q2 = jnp.concatenate(qs[g * NSA_HG:(g + 1) * NSA_HG], axis=0)
        kcb = kcb_ref[:, gls[g]].astype(bf16)
        vcb = _pad_rows(vcb_ref[:, gls[g]], qn).astype(bf16)
        lgt = lax.dot_general(kcb, q2, (((1,), (1,)), ((), ())), preferred_element_type=f32)
        score = jnp.zeros((nb, qn), f32)
        for h in range(NSA_HG):
            lg = lgt[:, h * qn:(h + 1) * qn] + _bias_of_dist(dist_c, tab_ref, g * NSA_HG + h)
            lg = jnp.where(ok_c, lg, NEG)
            mx = jnp.max(lg, axis=0, keepdims=True)
            e = jnp.exp(lg - mx)
            p = jnp.where(ok_c, e / jnp.sum(e, axis=0, keepdims=True), 0.0)
            score = score + p
            p_t = _pad_rows(p, qn).T.astype(bf16)
            o_c.append(jnp.dot(p_t, vcb, preferred_element_type=f32))
        scores.append(jnp.where(blk < cur, score, -1.0))

    sels = [jnp.zeros((nb, qn), f32) for _ in range(NSA_G)]
    for _ in range(min(NSA_TOPK, nb)):
        for g in range(NSA_G):
            mx = jnp.max(scores[g], axis=0, keepdims=True)
            first = jnp.min(jnp.where(scores[g] == mx, blkf, float(nb)), axis=0, keepdims=True)
            pick = blkf == first
            sels[g] = jnp.where(pick, jnp.where(mx >= 0.0, 1.0, 0.0), sels[g])
            scores[g] = jnp.where(pick, -2.0, scores[g])
    sel_ps = [_pad_rows(jnp.where(blk == cur, 1.0, s), qn).astype(bf16) for s in sels]

    def reset():
        m_scr[...] = jnp.full(m_scr.shape, NEG, f32)
        l_scr[...] = jnp.zeros(l_scr.shape, f32)
        acc_scr[...] = jnp.zeros(acc_scr.shape, f32)

    def update(hh, lg, vt):
        m_old = m_scr[hh]
        m_new = jnp.maximum(m_old, jnp.max(lg, axis=0, keepdims=True))
        p = jnp.exp(lg - m_new)
        alpha = jnp.exp(m_old - m_new)
        l_scr[hh] = alpha * l_scr[hh] + jnp.sum(p, axis=0, keepdims=True)
        acc_scr[hh] = alpha * acc_scr[hh] + jnp.dot(vt, p.astype(bf16), preferred_element_type=f32)
        m_scr[hh] = m_new

    def result(hh):
        return (acc_scr[hh] / l_scr[hh]).T

    reset()
    n_tiles = qb // tsub + 1
    tok = qb * qn + lax.broadcasted_iota(jnp.int32, (NSA_TK, qn), 1)
    key = lax.broadcasted_iota(jnp.int32, (NSA_TK, qn), 0)
    eblk = lax.broadcasted_iota(jnp.int32, (NSA_TK, qn), 1)

    def body(i, carry):
        kt = n_tiles - 1 - i
        k0 = pl.multiple_of(kt * NSA_TK, NSA_TK)
        expand = jnp.where(eblk == lax.shift_right_logical(k0 + key, shift), 1.0, 0.0).astype(bf16)
        causal = k0 + key <= tok
        for g in range(NSA_G):
            k = ks_ref[pl.ds(k0, NSA_TK), gls[g]]
            vt = vst_ref[kt, gls[g], :]
            member = jnp.dot(expand, sel_ps[g], preferred_element_type=f32)
            keep = jnp.where(causal, member, 0.0) > 0.5
            for h in range(NSA_HG):
                hh = g * NSA_HG + h
                tiles = [bias_ref[hh, jnp.clip(qb - (kt * tsub + j), 0, BIAS_TILES - 1)] for j in range(tsub)]
                lg = jnp.dot(k, qts[hh], preferred_element_type=f32) + jnp.concatenate(tiles, axis=0)
                update(hh, jnp.where(keep, lg, NEG), vt)
        return carry

    lax.fori_loop(0, n_tiles, body, 0)
    o_s = [result(hh) for hh in heads]

    reset()
    nwin = NSA_WIN // qn + 1
    wi = lax.broadcasted_iota(jnp.int32, (qn, qn), 1)
    wj = lax.broadcasted_iota(jnp.int32, (qn, qn), 0)
    for d in range(nwin):
        sub = qb - d
        subc = jnp.maximum(sub, 0)
        for g in range(NSA_G):
            k = kw_ref[pl.ds(pl.multiple_of(subc * qn, qn), qn), gls[g]]
            vt = vwt_ref[subc, gls[g], :]
            for h in range(NSA_HG):
                hh = g * NSA_HG + h
                x = jnp.dot(k, qts[hh], preferred_element_type=f32) + bias_ref[hh, d]
                if d == 0:
                    x = jnp.where(wj <= wi, x, NEG)
                else:
                    if d == nwin - 1:
                        x = jnp.where(wj > wi, x, NEG)
                    x = jnp.where(sub >= 0, x, NEG)
                update(hh, x, vt)
    o_w = [result(hh) for hh in heads]

    for hh in heads:
        c = 2 * GDN_VH + hh * 3
        y = o_c[hh] * gates[:, c:c + 1] + o_s[hh] * gates[:, c + 1:c + 2] + o_w[hh] * gates[:, c + 2:c + 3]
        o_ref[:, hh * NSA_D:(hh + 1) * NSA_D] = y.astype(o_ref.dtype)


def _nsa_prompt_attn(tab, bias_tiles, proj, qn, kcb, vcb, ksb, vst, kwb, vwt, batch, t_len):
    nq = t_len // NSA_QB
    nb = t_len // NSA_BLK
    hd = NSA_H * NSA_D
    per_b = lambda w: pl.BlockSpec((t_len, w), lambda b, i: (b, 0))
    tiles_b = lambda tk: pl.BlockSpec((t_len // tk, NSA_KV, tk), lambda b, i: (b, 0, 0))
    return pl.pallas_call(
        _nsa_attn_kernel,
        grid=(batch, nq),
        in_specs=[pl.BlockSpec(memory_space=pltpu.SMEM),
                  pl.BlockSpec((NSA_QB, hd), lambda b, i: (b * nq + i, 0)),
                  pl.BlockSpec((nb, NSA_KV), lambda b, i: (b, 0)),
                  pl.BlockSpec((nb, NSA_KV), lambda b, i: (b, 0)),
                  per_b(NSA_KV), tiles_b(NSA_TK), per_b(NSA_KV), tiles_b(NSA_QB),
                  pl.BlockSpec(bias_tiles.shape, lambda b, i: (0, 0, 0, 0)),
                  pl.BlockSpec((NSA_QB, 128), lambda b, i: (b * nq + i, EV_SMALL // 128 + 0))],
        out_specs=pl.BlockSpec((NSA_QB, hd), lambda b, i: (b * nq + i, 0)),
        out_shape=jax.ShapeDtypeStruct((batch * t_len, hd), MXU_DTYPE),
        scratch_shapes=[pltpu.VMEM((NSA_H, 1, NSA_QB), jnp.float32),
                        pltpu.VMEM((NSA_H, 1, NSA_QB), jnp.float32),
                        pltpu.VMEM((NSA_H, NSA_D, NSA_QB), jnp.float32)],
        compiler_params=pltpu.CompilerParams(dimension_semantics=("parallel", "arbitrary"),
                                             vmem_limit_bytes=VMEM_LIMIT_BYTES),
    )(tab.astype(jnp.float32), qn, kcb, vcb, ksb, vst, kwb, vwt, bias_tiles, proj)


def _mdot(a, b):
    return jnp.dot(a.astype(MXU_DTYPE), b.astype(MXU_DTYPE), preferred_element_type=jnp.float32)


def _mdot_nt(a, b):
    return lax.dot_general(a.astype(MXU_DTYPE), b.astype(MXU_DTYPE), (((1,), (1,)), ((), ())),
                           preferred_element_type=jnp.float32)


def _mdot_tn(a, b):
    return lax.dot_general(a.astype(MXU_DTYPE), b.astype(MXU_DTYPE), (((0,), (0,)), ((), ())),
                           preferred_element_type=jnp.float32)


def _shift_rows(x, prev, s):
    xs = pltpu.roll(x, s, axis=0)
    ps = pltpu.roll(prev, s, axis=0)
    row8 = lax.broadcasted_iota(jnp.int32, prev.shape, 0)
    head = jnp.where(row8 < s, ps, xs[0:8])
    return jnp.concatenate([head, xs[8:]], axis=0)


def _gdn_prep_kernel(x_ref, prev_ref, first_ref, ab_ref, cw_ref, alog_ref, dtb_ref,
                     q_ref, k_ref, v_ref, gb_ref, *, tiles_per_batch, t_valid):
    r = pl.program_id(0)
    tm = x_ref.shape[0]
    tile = r % tiles_per_batch
    prev = jnp.where(tile == 0, first_ref[0], prev_ref[...])
    x = x_ref[...]
    acc = x * cw_ref[GDN_CONV - 1:GDN_CONV, :]
    for s in range(1, GDN_CONV):
        acc = acc + _shift_rows(x, prev, s) * cw_ref[GDN_CONV - 1 - s:GDN_CONV - s, :]
    c = acc * jax.nn.sigmoid(acc)

    def l2(t):
        return t * lax.rsqrt(jnp.sum(t * t, axis=-1, keepdims=True) + 1e-6)

    for h in range(GDN_KH):
        sl = slice(h * GDN_DK, (h + 1) * GDN_DK)
        q_ref[:, sl] = l2(c[:, sl]) * (GDN_DK ** -0.5)
        k_ref[:, sl] = l2(c[:, GDN_QK + h * GDN_DK:GDN_QK + (h + 1) * GDN_DK])
    v_ref[...] = c[:, 2 * GDN_QK:]
    ab = ab_ref[...]
    z = ab + dtb_ref[...]
    softplus = jnp.maximum(z, 0.0) + jnp.log(1.0 + jnp.exp(-jnp.abs(z)))
    gate = -jnp.exp(alog_ref[...]) * softplus
    lane = lax.broadcasted_iota(jnp.int32, ab.shape, 1)
    pos = tile * tm + lax.broadcasted_iota(jnp.int32, ab.shape, 0)
    gb = jnp.where(lane < GDN_VH, gate, jnp.where(lane < 2 * GDN_VH, jax.nn.sigmoid(ab), 0.0))
    gb_ref[...] = jnp.where(pos < t_valid, gb, 0.0)


def _gdn_prep(xin, first, ab, ab_col, conv_w, a_log, dt_bias, batch, t_pad, t_valid):
    m = batch * t_pad
    tm = _pick(t_pad, (512, 256, 128, 64))
    tpb = t_pad // tm
    f32 = jnp.float32
    lanes = lambda v: jnp.zeros((1, 128), f32).at[0, :GDN_VH].set(v.astype(f32))
    return pl.pallas_call(
        functools.partial(_gdn_prep_kernel, tiles_per_batch=tpb, t_valid=t_valid),
        grid=(m // tm,),
        in_specs=[pl.BlockSpec((tm, GDN_CONV_CH), lambda r: (r, 0)),
                  pl.BlockSpec((8, GDN_CONV_CH), lambda r: (jnp.maximum(r * (tm // 8) - 1, 0), 0)),
                  pl.BlockSpec((1, 8, GDN_CONV_CH), lambda r: (r // tpb, 0, 0)),
                  pl.BlockSpec((tm, 128), lambda r: (r, ab_col)),
                  pl.BlockSpec((GDN_CONV, GDN_CONV_CH), lambda r: (0, 0)),
                  pl.BlockSpec((1, 128), lambda r: (0, 0)),
                  pl.BlockSpec((1, 128), lambda r: (0, 0))],
        out_specs=[pl.BlockSpec((tm, GDN_QK), lambda r: (r, 0)),
                   pl.BlockSpec((tm, GDN_QK), lambda r: (r, 0)),
                   pl.BlockSpec((tm, GDN_V), lambda r: (r, 0)),
                   pl.BlockSpec((tm, 128), lambda r: (r, 0))],
        out_shape=[jax.ShapeDtypeStruct((m, GDN_QK), f32), jax.ShapeDtypeStruct((m, GDN_QK), f32),
                   jax.ShapeDtypeStruct((m, GDN_V), f32), jax.ShapeDtypeStruct((m, 128), f32)],
        compiler_params=pltpu.CompilerParams(dimension_semantics=("arbitrary",),
                                             vmem_limit_bytes=VMEM_LIMIT_BYTES),
    )(xin, xin, first, ab, conv_w.astype(f32), lanes(a_log), lanes(dt_bias))


def _cumsum_rows(x):
    row = lax.broadcasted_iota(jnp.int32, x.shape, 0)
    s = 1
    while s < x.shape[0]:
        x = x + jnp.where(row >= s, pltpu.roll(x, s, axis=0), 0.0)
        s *= 2
    return x


def _split3(a):
    hi = a.astype(jnp.bfloat16)
    lo = (a - hi.astype(jnp.float32)).astype(jnp.bfloat16)
    return hi, lo


def _dot3(a, b):
    if MXU_DTYPE != jnp.bfloat16:
        return jnp.dot(a, b, preferred_element_type=jnp.float32)
    ah, al = _split3(a)
    bh, bl = _split3(b)
    d = lambda x, y: jnp.dot(x, y, preferred_element_type=jnp.float32)
    return d(ah, bh) + (d(ah, bl) + d(al, bh))


def _unit_lower_inverses(lmats, row, col):
    eye = jnp.where(row == col, 1.0, 0.0)
    blk = lax.shift_right_logical(row, 3) == lax.shift_right_logical(col, 3)
    lds = [jnp.where(blk, l, 0.0) for l in lmats]
    xs = [eye - ld for ld in lds]
    ps = [_dot3(ld, ld) for ld in lds]
    for step in range(2):
        xs = [x + _dot3(x, p) for x, p in zip(xs, ps)]
        if step < 1:
            ps = [_dot3(p, p) for p in ps]
    size = 8
    while size < lmats[0].shape[0]:
        sh = size.bit_length() - 1
        inner = lax.shift_right_logical(row, sh) == lax.shift_right_logical(col, sh)
        outer = lax.shift_right_logical(row, sh + 1) == lax.shift_right_logical(col, sh + 1)
        coffs = [jnp.where(outer, jnp.where(inner, 0.0, l), 0.0) for l in lmats]
        ts = [_dot3(x, c) for x, c in zip(xs, coffs)]
        xs = [x - _dot3(t, x) for x, t in zip(xs, ts)]
        size *= 2
    return xs


def _gdn_chunk_kernel(q_ref, k_ref, v_ref, gb_ref, z_ref, s0_ref, nw_ref, y_ref, s_ref):
    f32 = jnp.float32
    cn = q_ref.shape[0]

    @pl.when(pl.program_id(1) == 0)
    def _():
        s_ref[...] = s0_ref[...]

    gb = gb_ref[...]
    gcum = _cumsum_rows(gb)
    gcum_t = _pad_rows(gcum, 128).T
    row = lax.broadcasted_iota(jnp.int32, (cn, cn), 0)
    col = lax.broadcasted_iota(jnp.int32, (cn, cn), 1)
    rep = GDN_VH // GDN_KH
    heads = range(GDN_VH)
    qs = [q_ref[:, kh * GDN_DK:(kh + 1) * GDN_DK] for kh in range(GDN_KH)]
    ks = [k_ref[:, kh * GDN_DK:(kh + 1) * GDN_DK] for kh in range(GDN_KH)]
    kk = [_mdot_nt(k, k) for k in ks]
    qk = [_mdot_nt(q, k) for q, k in zip(qs, ks)]
    gcols = [gcum[:, h:h + 1] for h in heads]
    diffs = [gcols[h] - gcum_t[h:h + 1, :cn] for h in heads]
    egs = [jnp.exp(g) for g in gcols]
    betas = [gb[:, GDN_VH + h:GDN_VH + h + 1] for h in heads]
    lmats = [betas[h] * kk[h // rep] * jnp.exp(jnp.where(col < row, diffs[h], -jnp.inf)) for h in heads]
    ainvs = _unit_lower_inverses(lmats, row, col)
    sols = [_dot3(ainvs[h], jnp.concatenate([betas[h] * v_ref[:, h * GDN_DV:(h + 1) * GDN_DV],
                                             (betas[h] * egs[h]) * ks[h // rep]], axis=1)) for h in heads]
    atts = [qk[h // rep] * jnp.exp(jnp.where(col <= row, diffs[h], -jnp.inf)) for h in heads]
    ss = [s_ref[0, h] for h in heads]
    us = [sols[h][:, :GDN_DV] - _mdot(sols[h][:, GDN_DV:], ss[h]) for h in heads]
    os_ = [egs[h] * _mdot(qs[h // rep], ss[h]) + _mdot(atts[h], us[h]) for h in heads]
    for h in heads:
        gl = gcum[cn - 1:cn, h:h + 1]
        s_ref[0, h] = jnp.exp(gl) * ss[h] + _mdot_tn(ks[h // rep] * jnp.exp(gl - gcols[h]), us[h])
    for h in heads:
        o = os_[h]
        on = o * lax.rsqrt(jnp.mean(o * o, axis=-1, keepdims=True) + RMS_EPS) * nw_ref[...]
        z = z_ref[:, h * GDN_DV:(h + 1) * GDN_DV]
        y_ref[:, h * GDN_DV:(h + 1) * GDN_DV] = (on * (z * jax.nn.sigmoid(z))).astype(y_ref.dtype)


def _gdn_chunks(q, k, v, gb, zsrc, z_col, s0, norm_w, batch, t_pad):
    cn = GDN_CHUNK
    nc = t_pad // cn
    rowblk = lambda w, c=0: pl.BlockSpec((cn, w), lambda b, i, c=c: (b * nc + i, c))
    st = pl.BlockSpec((1, GDN_VH, GDN_DK, GDN_DV), lambda b, i: (b, 0, 0, 0))
    return pl.pallas_call(
        _gdn_chunk_kernel,
        grid=(batch, nc),
        in_specs=[rowblk(GDN_QK), rowblk(GDN_QK), rowblk(GDN_V), rowblk(128), rowblk(GDN_V, z_col), st,
                  pl.BlockSpec((1, GDN_DV), lambda b, i: (0, 0))],
        out_specs=[rowblk(GDN_V), st],
        out_shape=[jax.ShapeDtypeStruct((batch * t_pad, GDN_V), MXU_DTYPE),
                   jax.ShapeDtypeStruct((batch, GDN_VH, GDN_DK, GDN_DV), jnp.float32)],
        compiler_params=pltpu.CompilerParams(dimension_semantics=("parallel", "arbitrary"),
                                             vmem_limit_bytes=VMEM_LIMIT_BYTES),
    )(q, k, v, gb, zsrc, s0.astype(jnp.float32), norm_w.reshape(1, GDN_DV).astype(jnp.float32))


OD_Q = 0
OD_K = ML_H * ML_DK
OD_V = 2 * ML_H * ML_DK
OD_O = OD_V + ML_H * ML_DV
OD_SMALL = OD_O + ML_H * ML_DV
OD_N = OD_SMALL + 128


def _cummax_rows(x):
    row = lax.broadcasted_iota(jnp.int32, x.shape, 0)
    s = 1
    while s < x.shape[0]:
        x = jnp.maximum(x, jnp.where(row >= s, pltpu.roll(x, s, axis=0), -jnp.inf))
        s *= 2
    return x


def _round_mxu(x):
    return x.astype(MXU_DTYPE).astype(jnp.float32)


def _mlstm_chunk_kernel(q_ref, k_ref, v_ref, og_ref, if_ref, bif_ref, nw_ref, c0_ref, n0_ref, m0_ref,
                        y_ref, c_ref, n_ref, m_ref, *, t_valid):
    f32 = jnp.float32
    cn = q_ref.shape[0]
    ci = pl.program_id(1)

    @pl.when(ci == 0)
    def _():
        c_ref[...] = c0_ref[...]
        n_ref[...] = n0_ref[...]
        m_ref[...] = m0_ref[...]

    pre = if_ref[...] + bif_ref[...]
    pos = ci * cn + lax.broadcasted_iota(jnp.int32, pre.shape, 0)
    live = pos < t_valid
    lf = jnp.where(live, jnp.minimum(pre, 0.0) - jnp.log(1.0 + jnp.exp(-jnp.abs(pre))), 0.0)
    fcum = _cumsum_rows(lf)
    a_all = jnp.where(live, pre, NEG) - pltpu.roll(fcum, 128 - ML_H, axis=1)
    amax = _cummax_rows(a_all)
    a_t = _pad_rows(a_all, 128).T
    row = lax.broadcasted_iota(jnp.int32, (cn, cn), 0)
    col = lax.broadcasted_iota(jnp.int32, (cn, cn), 1)
    m_all = m_ref[0]
    heads = range(ML_H)
    qs = [q_ref[:, h * ML_DK:(h + 1) * ML_DK] * (ML_DK ** -0.5) for h in heads]
    ks = [k_ref[:, h * ML_DK:(h + 1) * ML_DK] for h in heads]
    vs = [v_ref[:, h * ML_DV:(h + 1) * ML_DV] for h in heads]
    m_prevs = [m_all[:, h:h + 1] for h in heads]
    fcs = [fcum[:, ML_H + h:ML_H + h + 1] for h in heads]
    mts = [fcs[h] + jnp.maximum(m_prevs[h], amax[:, h:h + 1]) for h in heads]
    qks = [_mdot_nt(qs[h], ks[h]) for h in heads]
    ss = [qks[h] * jnp.exp(jnp.where(col <= row, a_t[h:h + 1, :cn] + (fcs[h] - mts[h]), -jnp.inf)) for h in heads]
    dec0s = [jnp.exp(fcs[h] + m_prevs[h] - mts[h]) for h in heads]
    cms = [c_ref[0, h] for h in heads]
    nvs = [n_ref[0, h:h + 1, :] for h in heads]
    nums = [dec0s[h] * _mdot(qs[h], cms[h]) + _mdot(ss[h], vs[h]) for h in heads]
    dens = [dec0s[h] * jnp.sum(_round_mxu(qs[h]) * _round_mxu(nvs[h]), axis=-1, keepdims=True)
            + jnp.sum(ss[h], axis=-1, keepdims=True) for h in heads]
    hcs = [nums[h] / jnp.maximum(jnp.abs(dens[h]), jnp.exp(-mts[h])) for h in heads]
    m_new = [mts[h][cn - 1:cn] for h in heads]
    for h in heads:
        f_end = fcs[h][cn - 1:cn]
        w = jnp.exp(a_all[:, h:h + 1] + (f_end - m_new[h]))
        dc = jnp.exp(f_end + m_prevs[h] - m_new[h])
        c_ref[0, h] = dc * cms[h] + _mdot_tn(w * ks[h], vs[h])
        n_ref[0, h:h + 1, :] = dc * nvs[h] + jnp.sum(_round_mxu(w) * _round_mxu(ks[h]), axis=0, keepdims=True)
    for h in heads:
        hc = hcs[h]
        hn = hc * lax.rsqrt(jnp.mean(hc * hc, axis=-1, keepdims=True) + RMS_EPS) \
            * nw_ref[:, h * ML_DV:(h + 1) * ML_DV]
        og = og_ref[:, h * ML_DV:(h + 1) * ML_DV]
        y_ref[:, h * ML_DV:(h + 1) * ML_DV] = (hn * jax.nn.sigmoid(og)).astype(y_ref.dtype)
    lane = lax.broadcasted_iota(jnp.int32, m_all.shape, 1)
    out = m_all
    for h in range(ML_H):
        out = jnp.where(lane == h, m_new[h], out)
    m_ref[0] = out


def _mlstm_chunks(proj, b_if, mnorm, c0, n0, m0, batch, t_pad, t_valid):
    cn = ML_CHUNK
    nc = t_pad // cn
    f32 = jnp.float32
    blk = lambda w, c: pl.BlockSpec((cn, w), lambda b, i, c=c: (b * nc + i, c))
    const = lambda shape: pl.BlockSpec(shape, lambda b, i: (0,) * len(shape))
    st_c = pl.BlockSpec((1, ML_H, ML_DK, ML_DV), lambda b, i: (b, 0, 0, 0))
    st_n = pl.BlockSpec((1, ML_H, ML_DK), lambda b, i: (b, 0, 0))
    st_m = pl.BlockSpec((1, 1, 128), lambda b, i: (b, 0, 0))
    bif = jnp.zeros((1, 128), f32).at[0, :2 * ML_H].set(b_if.astype(f32))
    m0p = jnp.zeros((batch, 1, 128), f32).at[:, 0, :ML_H].set(m0.astype(f32))
    hq, hv = ML_H * ML_DK, ML_H * ML_DV
    y, c, n, m = pl.pallas_call(
        functools.partial(_mlstm_chunk_kernel, t_valid=t_valid),
        grid=(batch, nc),
        in_specs=[blk(hq, OD_Q // hq), blk(hq, OD_K // hq), blk(hv, OD_V // hv), blk(hv, OD_O // hv),
                  blk(128, OD_SMALL // 128), const((1, 128)), const((1, hv)), st_c, st_n, st_m],
        out_specs=[blk(hv, 0), st_c, st_n, st_m],
        out_shape=[jax.ShapeDtypeStruct((batch * t_pad, hv), MXU_DTYPE),
                   jax.ShapeDtypeStruct((batch, ML_H, ML_DK, ML_DV), f32),
                   jax.ShapeDtypeStruct((batch, ML_H, ML_DK), f32),
                   jax.ShapeDtypeStruct((batch, 1, 128), f32)],
        compiler_params=pltpu.CompilerParams(dimension_semantics=("parallel", "arbitrary"),
                                             vmem_limit_bytes=VMEM_LIMIT_BYTES),
    )(proj, proj, proj, proj, proj, bif, mnorm.reshape(1, hv).astype(f32), c0.astype(f32), n0.astype(f32), m0p)
    return y, c, n, m[:, 0, :ML_H]


def _permute_odd_w(w_in):
    pad = jnp.zeros((w_in.shape[0], OD_N - w_in.shape[1]), w_in.dtype)
    return jnp.concatenate([w_in, pad], axis=1).astype(MXU_DTYPE)


DEC_PAGES_PER_STEP = 16


def _dec_compress_kernel(pt_ref, *refs, pp):
    kc_refs, vc_refs = refs[:pp], refs[pp:2 * pp]
    wck_ref, wcv_ref, g1_ref, kcb_ref, vcb_ref = refs[2 * pp:]
    per_page = PAGE_SIZE // NSA_BLK

    def compress(page_refs, w_ref, g):
        w = w_ref[:, g * NSA_D:(g + 1) * NSA_D]
        rows = [jnp.sum(r[pl.ds(g, PAGE_SIZE, stride=NSA_G), :].reshape(per_page, NSA_BLK, NSA_D) * w[None], axis=1)
                for r in page_refs]
        return jnp.concatenate(rows, axis=0)

    for g in range(NSA_G):
        sl = slice(g * NSA_D, (g + 1) * NSA_D)
        kcb_ref[0, :, sl] = _group_rms(compress(kc_refs, wck_ref, g), g1_ref[...])
        vcb_ref[0, :, sl] = compress(vc_refs, wcv_ref, g)


def _dec_compress(page_table, pool_kc, pool_vc, layer, w_ck, w_cv, g1):
    nseq, npages = page_table.shape
    pp = math.gcd(DEC_PAGES_PER_STEP, npages)
    per_page = PAGE_SIZE // NSA_BLK
    nphys = pool_kc.shape[1]
    base = layer * nphys
    kc = pool_kc.reshape(-1, NSA_D)
    vc = pool_vc.reshape(-1, NSA_D)
    page = lambda j: pl.BlockSpec((PAGE_SIZE * NSA_G, NSA_D), lambda b, i, pt, j=j: (base + pt[b, i * pp + j], 0))
    const = lambda shape: pl.BlockSpec(shape, lambda b, i, pt: (0,) * len(shape))
    out = pl.BlockSpec((1, pp * per_page, NSA_KV), lambda b, i, pt: (b, i, 0))
    nb = npages * per_page
    return pl.pallas_call(
        functools.partial(_dec_compress_kernel, pp=pp),
        grid_spec=pltpu.PrefetchScalarGridSpec(
            num_scalar_prefetch=1, grid=(nseq, npages // pp),
            in_specs=[page(j) for j in range(pp)] * 2 + [const((NSA_BLK, NSA_KV))] * 2 + [const((1, NSA_D))],
            out_specs=[out, out]),
        out_shape=[jax.ShapeDtypeStruct((nseq, nb, NSA_KV), jnp.float32)] * 2,
        compiler_params=pltpu.CompilerParams(dimension_semantics=("parallel", "arbitrary"),
                                             vmem_limit_bytes=VMEM_LIMIT_BYTES),
    )(page_table, *([kc] * pp), *([vc] * pp), w_ck.reshape(NSA_BLK, NSA_KV), w_cv.reshape(NSA_BLK, NSA_KV),
      g1.reshape(1, NSA_D))


def _rows_by_head(head_row, fn, nheads):
    out = fn(0)
    out = jnp.broadcast_to(out, (nheads, out.shape[1]))
    for h in range(1, nheads):
        out = jnp.where(head_row == h, fn(h), out)
    return out


def _dec_select_kernel(tab_ref, q_ref, kcb_ref, vcb_ref, oc_ref, idx_ref, *, qpos):
    f32 = jnp.float32
    nbp = kcb_ref.shape[1]
    q = q_ref[0]
    head_row = lax.broadcasted_iota(jnp.int32, (NSA_H, 1), 0)
    blk = lax.broadcasted_iota(jnp.int32, (1, nbp), 1)
    blkf = blk.astype(f32)
    dist = qpos - (blk * NSA_BLK + NSA_BLK - 1)
    ok = dist >= 0
    cur = qpos // NSA_BLK
    lg = jnp.zeros((NSA_H, nbp), f32)
    for g in range(NSA_G):
        lg_g = _mdot_nt(q, kcb_ref[0, :, g * NSA_D:(g + 1) * NSA_D])
        lg = jnp.where(head_row // NSA_HG == g, lg_g, lg)
    lg = lg + _rows_by_head(head_row, lambda h: _bias_of_dist(dist, tab_ref, h), NSA_H)
    lg = jnp.where(ok, lg, NEG)
    e = jnp.exp(lg - jnp.max(lg, axis=-1, keepdims=True))
    p = jnp.where(ok, e / jnp.sum(e, axis=-1, keepdims=True), 0.0)
    oc = jnp.zeros((NSA_H, NSA_D), f32)
    lane = lax.broadcasted_iota(jnp.int32, (1, 128), 1)
    for g in range(NSA_G):
        in_g = head_row // NSA_HG == g
        oc = jnp.where(in_g, _mdot(p, vcb_ref[0, :, g * NSA_D:(g + 1) * NSA_D]), oc)
        score = jnp.sum(jnp.where(in_g, p, 0.0), axis=0, keepdims=True)
        score = jnp.where(blk < cur, score, -1.0)
        idx = jnp.full((1, 128), -1.0, f32)
        for r in range(NSA_TOPK):
            mx = jnp.max(score, axis=-1, keepdims=True)
            first = jnp.min(jnp.where(score == mx, blkf, float(nbp)), axis=-1, keepdims=True)
            idx = jnp.where(lane == r, jnp.where(mx >= 0.0, first, -1.0), idx)
            score = jnp.where(blkf == first, -2.0, score)
        idx_ref[0, g:g + 1, :] = idx.astype(jnp.int32)
    oc_ref[0] = oc


def _dec_select(tab, qh, kcb, vcb, qpos):
    nseq = qh.shape[0]
    nbp = kcb.shape[1]
    seq = lambda shape: pl.BlockSpec((1,) + shape, lambda b: (b, 0, 0))
    return pl.pallas_call(
        functools.partial(_dec_select_kernel, qpos=qpos),
        grid=(nseq,),
        in_specs=[pl.BlockSpec(memory_space=pltpu.SMEM), seq((NSA_H, NSA_D)), seq((nbp, NSA_KV)),
                  seq((nbp, NSA_KV))],
        out_specs=[seq((NSA_H, NSA_D)), seq((NSA_G, 128))],
        out_shape=[jax.ShapeDtypeStruct((nseq, NSA_H, NSA_D), jnp.float32),
                   jax.ShapeDtypeStruct((nseq, NSA_G, 128), jnp.int32)],
        compiler_params=pltpu.CompilerParams(dimension_semantics=("parallel",),
                                             vmem_limit_bytes=VMEM_LIMIT_BYTES),
    )(tab.astype(jnp.float32), qh, kcb, vcb)


def _dec_attend_kernel(pt_ref, idx_ref, tab_ref, q_ref, *refs, qpos):
    f32 = jnp.float32
    ks_refs, vs_refs = refs[:NSA_TOPK], refs[NSA_TOPK:2 * NSA_TOPK]
    ksn_ref, vsn_ref, kwb_ref, vwb_ref, kwn_ref, vwn_ref, oc_ref, ng_ref, y_ref = refs[2 * NSA_TOPK:]
    b = pl.program_id(0)
    g = pl.program_id(1)
    cur = qpos // NSA_BLK
    head_row = lax.broadcasted_iota(jnp.int32, (NSA_HG, 1), 0)
    q4 = q_ref[0, pl.ds(g * NSA_HG, NSA_HG), :]
    q4r = _round_mxu(q4.astype(f32))
    new_row = b * NSA_G + g
    tab0 = _rows_by_head(head_row, lambda h: jnp.full((1, 1), tab_ref[0, g * NSA_HG + h], f32), NSA_HG)

    def attend(keys, vals, dist, ok, k_new, v_new):
        lg = _mdot_nt(q4, keys)
        lg = lg + _rows_by_head(head_row, lambda h: _bias_of_dist(dist, tab_ref, g * NSA_HG + h), NSA_HG)
        lg = jnp.where(ok, lg, NEG)
        lg_new = jnp.sum(q4r * _round_mxu(k_new), axis=-1, keepdims=True) + tab0
        m = jnp.maximum(jnp.max(lg, axis=-1, keepdims=True), lg_new)
        e = jnp.exp(lg - m)
        e_new = jnp.exp(lg_new - m)
        den = jnp.sum(e, axis=-1, keepdims=True) + e_new
        return _mdot(e / den, vals) + _round_mxu(e_new / den) * _round_mxu(v_new)

    pieces, oks = [], []
    jrow = lax.broadcasted_iota(jnp.int32, (1, NSA_BLK), 1)
    for s in range(NSA_TOPK):
        blk_id = idx_ref[b, g, s]
        valid = jnp.logical_and(blk_id >= 0, blk_id < cur)
        pos = jnp.maximum(blk_id, 0) * NSA_BLK + jrow
        pieces.append(qpos - pos)
        oks.append(jnp.where(valid, 1.0, 0.0) * jnp.where(pos <= qpos, 1.0, 0.0))
    dist_s = jnp.concatenate(pieces, axis=1)
    ok_s = jnp.concatenate(oks, axis=1) > 0.5
    mine = lambda r: r[pl.ds(g, NSA_BLK, stride=NSA_G), :]
    k_sel = jnp.concatenate([mine(r) for r in ks_refs], axis=0)
    v_sel = jnp.concatenate([mine(r) for r in vs_refs], axis=0)
    o_s = attend(k_sel, v_sel, dist_s, ok_s, ksn_ref[pl.ds(new_row, 1), :], vsn_ref[pl.ds(new_row, 1), :])

    wb = kwb_ref.shape[1]
    dist_w = wb - lax.broadcasted_iota(jnp.int32, (1, wb), 1)
    ok_w = dist_w < NSA_WIN
    o_w = attend(kwb_ref[0], vwb_ref[0], dist_w, ok_w, kwn_ref[pl.ds(new_row, 1), :], vwn_ref[pl.ds(new_row, 1), :])

    gates = jax.nn.sigmoid(ng_ref[pl.ds(b, 1), :])
    lane = lax.broadcasted_iota(jnp.int32, (1, 128), 1)

    def gate(branch):
        def one(h):
            c = 2 * GDN_VH + (g * NSA_HG + h) * 3 + branch
            return jnp.sum(jnp.where(lane == c, gates, 0.0), axis=-1, keepdims=True)
        return _rows_by_head(head_row, one, NSA_HG)

    oc4 = oc_ref[0, pl.ds(g * NSA_HG, NSA_HG), :]
    y_ref[0, pl.ds(g * NSA_HG, NSA_HG), :] = oc4 * gate(0) + o_s * gate(1) + o_w * gate(2)


def _dec_attend(tab, page_table, idx, qh, pool_ks, pool_vs, layer, ksn, vsn, buf_kw, buf_vw, kwn, vwn, oc, ng,
                qpos):
    nseq = qh.shape[0]
    base = layer * pool_ks.shape[1]
    wb = buf_kw.shape[1]
    ks = pool_ks.reshape(-1, NSA_D)
    vs = pool_vs.reshape(-1, NSA_D)
    per_page = PAGE_SIZE // NSA_BLK

    def sel(s):
        def index(b, g, pt, ix, s=s):
            blk_id = jnp.maximum(ix[b, g, s], 0)
            return ((base + pt[b, blk_id // per_page]) * per_page + blk_id % per_page, 0)
        return pl.BlockSpec((NSA_BLK * NSA_G, NSA_D), index)

    whole = lambda a: pl.BlockSpec(a.shape, lambda b, g, pt, ix: (0,) * a.ndim)
    heads = pl.BlockSpec((1, NSA_H, NSA_D), lambda b, g, pt, ix: (b, 0, 0))
    win = pl.BlockSpec((1, wb, NSA_D), lambda b, g, pt, ix: (b, 0, g))
    rows = lambda a: a.reshape(nseq * NSA_G, NSA_D)
    new = [rows(ksn), rows(vsn), rows(kwn), rows(vwn)]
    return pl.pallas_call(
        functools.partial(_dec_attend_kernel, qpos=qpos),
        grid_spec=pltpu.PrefetchScalarGridSpec(
            num_scalar_prefetch=2, grid=(nseq, NSA_G),
            in_specs=[pl.BlockSpec(memory_space=pltpu.SMEM), heads] + [sel(s) for s in range(NSA_TOPK)] * 2
                     + [whole(new[0]), whole(new[1]), win, win, whole(new[2]), whole(new[3]), heads, whole(ng)],
            out_specs=heads),
        out_shape=jax.ShapeDtypeStruct((nseq, NSA_H, NSA_D), jnp.float32),
        compiler_params=pltpu.CompilerParams(dimension_semantics=("parallel", "arbitrary"),
                                             vmem_limit_bytes=VMEM_LIMIT_BYTES),
    )(page_table, idx, tab.astype(jnp.float32), qh, *([ks] * NSA_TOPK), *([vs] * NSA_TOPK), new[0], new[1],
      buf_kw.reshape(nseq, wb, NSA_KV), buf_vw.reshape(nseq, wb, NSA_KV), new[2], new[3], oc, ng)


def _nsa_decode(proj, tab, qk_gain, w_ck, w_cv, caches, nseq):
    assert PAST_LEN % PAGE_SIZE == 0 and DEC_SEQ == 1
    pad = jnp.pad(proj, ((0, 512 - nseq), (0, 0)))
    qn, ksn, kwn, _, _, _, _, _, _ = _nsa_prep(pad, qk_gain, w_ck, w_cv)
    qh = qn[:nseq].reshape(nseq, NSA_H, NSA_D).astype(jnp.float32)
    kv = lambda i: proj[:, EV_KV + i * NSA_KV:EV_KV + (i + 1) * NSA_KV]
    layer = caches['layer']
    kcb, vcb = _dec_compress(caches['page_table'], caches['kc'], caches['vc'], layer, w_ck, w_cv, qk_gain[1])
    oc, idx = _dec_select(tab, qh, kcb, vcb, PAST_LEN)
    ng = proj[:, EV_SMALL:EV_SMALL + 128]
    y = _dec_attend(tab, caches['page_table'], idx, qh, caches['ks'], caches['vs'], layer, ksn[:nseq], kv(3),
                    caches['kw'], caches['vw'], kwn[:nseq], kv(5), oc, ng, PAST_LEN)
    return y.reshape(nseq, NSA_H * NSA_D), ksn[:nseq], kwn[:nseq]


def _rms_cast_kernel(x_ref, g_ref, o_ref):
    x = x_ref[...]
    y = x * lax.rsqrt(jnp.mean(x * x, axis=-1, keepdims=True) + RMS_EPS) * g_ref[...]
    o_ref[...] = y.astype(o_ref.dtype)


def _rms_cast(h, g):
    m, d = h.shape
    tm = _pick(m, (1024, 256, 128))
    return pl.pallas_call(
        _rms_cast_kernel,
        grid=(m // tm,),
        in_specs=[pl.BlockSpec((tm, d), lambda i: (i, 0)), pl.BlockSpec((1, d), lambda i: (0, 0))],
        out_specs=pl.BlockSpec((tm, d), lambda i: (i, 0)),
        out_shape=jax.ShapeDtypeStruct((m, d), MXU_DTYPE),
        compiler_params=pltpu.CompilerParams(dimension_semantics=("parallel",),
                                             vmem_limit_bytes=VMEM_LIMIT_BYTES),
    )(h, g.reshape(1, d).astype(jnp.float32))


def _mm2_kernel(xa_ref, xb_ref, w_ref, r_ref, o_ref):
    ka = xa_ref.shape[1]
    y = jnp.dot(xa_ref[...], w_ref[:ka, :].astype(MXU_DTYPE), preferred_element_type=jnp.float32)
    y = y + jnp.dot(xb_ref[...], w_ref[ka:, :].astype(MXU_DTYPE), preferred_element_type=jnp.float32)
    o_ref[...] = y + r_ref[...]


def _matmul2_res(xa, xb, w, res, layer, tiles=None):
    m, ka = xa.shape
    kb = xb.shape[1]
    n = w.shape[-1]
    tm = _pick(m, (tiles[0],) if tiles else (512, 256, 128))
    tn = _pick(n, (tiles[1],) if tiles else (512, 256, 128))
    return pl.pallas_call(
        _mm2_kernel,
        grid=(m // tm, n // tn),
        in_specs=[pl.BlockSpec((tm, ka), lambda i, j: (i, 0)), pl.BlockSpec((tm, kb), lambda i, j: (i, 0)),
                  pl.BlockSpec((None, ka + kb, tn), lambda i, j: (layer, 0, j)),
                  pl.BlockSpec((tm, tn), lambda i, j: (i, j))],
        out_specs=pl.BlockSpec((tm, tn), lambda i, j: (i, j)),
        out_shape=jax.ShapeDtypeStruct((m, n), jnp.float32),
        compiler_params=pltpu.CompilerParams(dimension_semantics=("parallel", "parallel"),
                                             vmem_limit_bytes=VMEM_LIMIT_BYTES),
    )(xa, xb, w, res)


LAYER_TILES = (
    dict(inp=(2048, 1152), out=(2048, 512), mlp=((2048, 512), (1024, 1024))),
    dict(inp=(2048, 896), out=(2048, 512), mlp=((2048, 512), (1024, 1024))),
    dict(inp=(2048, 1152), out=(2048, 512), mlp=((2048, 512), (1024, 1024))),
    dict(inp=(2048, 896), out=(2048, 512), mlp=((2048, 512), (1024, 1024))),
)


def _mlp_block(h, g, w1, w2, layer, tiles):
    mid = _matmul(_rms_cast(h, g), w1, act='relu2', out_dtype=MXU_DTYPE, tiles=tiles[0], layer=layer)
    return _matmul(mid, w2, res=h, tiles=tiles[1], layer=layer)


def _pad_time(x, t_pad):
    b, d = x.shape
    return jnp.zeros((b, t_pad, d), x.dtype).at[:, 0, :].set(x).reshape(b * t_pad, d)


def _even_layer(h, g_mix, w_in, w_out, conv_w, a_log, dt_bias, gnorm, qk_gain, w_ck, w_cv, tab, bias_tiles,
                batch, t_len, conv_st, s_st, decode, tiles):
    f32 = jnp.float32
    proj = _matmul(_rms_cast(h, g_mix), w_in, tiles=tiles['inp'])
    t_pad = t_len if decode is None else GDN_CHUNK
    pp = proj if decode is None else _pad_time(proj, t_pad)
    first = jnp.pad(conv_st.astype(f32), ((0, 0), (8 - (GDN_CONV - 1), 0), (0, 0)))
    q, k, v, gb = _gdn_prep(pp, first, pp, EV_SMALL // 128, conv_w, a_log, dt_bias, batch, t_pad, t_len)
    ya, s_new = _gdn_chunks(q, k, v, gb, pp, EV_Z // GDN_V, s_st, gnorm, batch, t_pad)
    p3 = proj.reshape(batch, t_len, EV_N)
    xin_tail = jnp.concatenate([conv_st.astype(f32), p3[..., :GDN_CONV_CH]], axis=1)[:, -(GDN_CONV - 1):]
    hd = lambda t: t.reshape(batch, t_len, NSA_G, NSA_D)
    kv = lambda i: hd(p3[..., EV_KV + i * NSA_KV:EV_KV + (i + 1) * NSA_KV])
    if decode is None:
        qn, ksn, kwn, ksb, vsb, kwb, vwb, kcb, vcb = _nsa_prep(proj, qk_gain, w_ck, w_cv)
        yb = _nsa_prompt_attn(tab, bias_tiles, proj, qn, kcb, vcb, ksb, vsb, kwb, vwb, batch, t_len)
        nw = min(NSA_WIN, t_len)
        kw_out, vw_out = hd(kwn)[:, -nw:], kv(5)[:, -nw:]
        ks_out = hd(ksn)
    else:
        ya = ya.reshape(batch, t_pad, GDN_V)[:, 0]
        yb, ksn, kwn = _nsa_decode(proj, tab, qk_gain, w_ck, w_cv, decode, batch)
        yb = yb.astype(MXU_DTYPE)
        ks_out = hd(ksn)
        kw_out = jnp.concatenate([decode['kw'][:, 1:], hd(kwn)], axis=1)
        vw_out = jnp.concatenate([decode['vw'][:, 1:], kv(5)], axis=1)
    h = _matmul2_res(ya, yb, w_out[0], h, w_out[1], tiles=tiles['out'])
    return h, (kv(0), kv(1), ks_out, kv(3), kw_out, vw_out, xin_tail, s_new)


def _odd_layer(h, g_mix, w_in, w_out, b_if, mnorm, batch, t_len, c0, n0, m0, decode, tiles):
    proj = _matmul(_rms_cast(h, g_mix), w_in, tiles=tiles['inp'])
    t_pad = t_len if not decode else ML_CHUNK
    pp = proj if not decode else _pad_time(proj, t_pad)
    y, c, n, m = _mlstm_chunks(pp, b_if, mnorm, c0, n0, m0, batch, t_pad, t_len)
    if decode:
        y = y.reshape(batch, t_pad, ODD_MIX)[:, 0]
    return _matmul(y, w_out[0], res=h, tiles=tiles['out'], layer=w_out[1]), (c, n, m)


def kernel(x_prompt, x_sample, cache_kc, cache_vc, cache_ks, cache_vs, cache_kw, cache_vw,
           state_gdn_conv, state_gdn_s, state_ml_c, state_ml_n, state_ml_m, page_table,
           rel_bias, norm_mix, norm_mlp, w_in_even, w_out_even, gdn_conv_w, gdn_a_log,
           gdn_dt_bias, gdn_norm_w, nsa_qk_gain, nsa_w_ck, nsa_w_cv, w_in_odd, w_out_odd,
           ml_b_if, ml_norm_w, w_ff1, w_ff2):
    f32 = jnp.float32
    bp, tp, d = x_prompt.shape
    bs, ts, _ = x_sample.shape
    hp = x_prompt.reshape(bp * tp, d)
    hs = x_sample.reshape(bs * ts, d)
    bias_tiles = _bias_tiles(rel_bias)
    ev_p, ev_s, od_p, od_s = [], [], [], []
    for l in range(DEPTH):
        j = l // 2
        tiles = LAYER_TILES[l]
        if l % 2 == 0:
            w_in = _permute_even_w(w_in_even[j])
            args = (w_in, (w_out_even, j), gdn_conv_w[j], gdn_a_log[j], gdn_dt_bias[j], gdn_norm_w[j],
                    nsa_qk_gain[j], nsa_w_ck[j], nsa_w_cv[j], rel_bias, bias_tiles)
            hp, stp = _even_layer(hp, norm_mix[l], *args, bp, tp,
                                  jnp.zeros((bp, GDN_CONV - 1, GDN_CONV_CH), f32),
                                  jnp.zeros((bp, GDN_VH, GDN_DK, GDN_DV), f32), None, tiles)
            caches = dict(kc=cache_kc, vc=cache_vc, ks=cache_ks, vs=cache_vs, layer=j, kw=cache_kw[j],
                          vw=cache_vw[j], page_table=page_table)
            hs, sts = _even_layer(hs, norm_mix[l], *args, bs, ts, state_gdn_conv[j], state_gdn_s[j], caches,
                                  tiles)
            ev_p.append(stp)
            ev_s.append(sts)
        else:
            w_in = _permute_odd_w(w_in_odd[j])
            hp, stp = _odd_layer(hp, norm_mix[l], w_in, (w_out_odd, j), ml_b_if[j], ml_norm_w[j], bp, tp,
                                 jnp.zeros((bp, ML_H, ML_DK, ML_DV), f32), jnp.zeros((bp, ML_H, ML_DK), f32),
                                 jnp.zeros((bp, ML_H), f32), False, tiles)
            hs, sts = _odd_layer(hs, norm_mix[l], w_in, (w_out_odd, j), ml_b_if[j], ml_norm_w[j], bs, ts,
                                 state_ml_c[j], state_ml_n[j], state_ml_m[j], True, tiles)
            od_p.append(stp)
            od_s.append(sts)
        hp = _mlp_block(hp, norm_mlp[l], w_ff1, w_ff2, l, tiles['mlp'])
        hs = _mlp_block(hs, norm_mlp[l], w_ff1, w_ff2, l, tiles['mlp'])

    kc_p, vc_p, ks_p, vs_p, kw_p, vw_p, conv_p, gdn_p = [jnp.stack(a) for a in zip(*ev_p)]
    kc_s, vc_s, ks_s, vs_s, kw_s, vw_s, conv_s, gdn_s = [jnp.stack(a) for a in zip(*ev_s)]
    mlc_p, mln_p, mlm_p = [jnp.stack(a) for a in zip(*od_p)]
    mlc_s, mln_s, mlm_s = [jnp.stack(a) for a in zip(*od_s)]
    return (hp.reshape(bp, tp, d), hs.reshape(bs, ts, d), kc_p, kc_s, vc_p, vc_s, ks_p, ks_s, vs_p, vs_s,
            kw_p, kw_s, vw_p, vw_s, conv_p, conv_s, gdn_p, gdn_s, mlc_p, mlc_s, mln_p, mln_s, mlm_p, mlm_s)
```

```python
import functools
import math

import jax
import jax.numpy as jnp
import numpy as np
from jax import lax
from jax.experimental import pallas as pl
from jax.experimental.pallas import tpu as pltpu

DEPTH = 4
DEC_SEQ = 1
PAST_LEN = 16384
PAGE_SIZE = 128
RMS_EPS = 1e-6
NEG = -1e30

GDN_KH = 4
GDN_VH = 8
GDN_DK = 128
GDN_DV = 128
GDN_CONV = 4
GDN_CHUNK = 64
GDN_QK = GDN_KH * GDN_DK
GDN_V = GDN_VH * GDN_DV
GDN_CONV_CH = 2 * GDN_QK + GDN_V

NSA_H = 8
NSA_G = 2
NSA_HG = NSA_H // NSA_G
NSA_D = 128
NSA_BLK = 64
NSA_TOPK = 15
NSA_WIN = 512
NSA_QB = 128
NSA_KV = NSA_G * NSA_D

REL_BUCKETS = 32
REL_EXACT = 16
REL_MAX_DIST = 2048

ML_H = 8
ML_DK = 128
ML_DV = 256
ML_CHUNK = 64

ODD_MIX = ML_H * ML_DV

VMEM_LIMIT_BYTES = 56 * 1024 * 1024
MXU_DTYPE = jnp.bfloat16


def _mm_kernel(x_ref, w_ref, *rest, nk, act, has_res):
    rest = list(rest)
    r_ref = rest.pop(0) if has_res else None
    o_ref = rest.pop(0)

    def finish(y):
        if act == 'relu2':
            y = jnp.square(jnp.maximum(y, 0.0))
        if has_res:
            y = y + r_ref[...]
        o_ref[...] = y.astype(o_ref.dtype)

    d = jnp.dot(x_ref[...], w_ref[...].astype(MXU_DTYPE), preferred_element_type=jnp.float32)
    if nk == 1:
        finish(d)
        return
    acc_ref = rest.pop(0)
    k = pl.program_id(2)

    @pl.when(k == 0)
    def _():
        acc_ref[...] = d

    @pl.when(k > 0)
    def _():
        acc_ref[...] += d

    @pl.when(k == nk - 1)
    def _():
        finish(acc_ref[...])


def _pick(n, prefs):
    for p in prefs:
        if n % p == 0:
            return p
    return n


def _matmul(x, w, *, act=None, res=None, out_dtype=jnp.float32, tiles=None, layer=None):
    m, kdim = x.shape
    n = w.shape[-1]
    tm = _pick(m, (tiles[0],) if tiles else (512, 256, 128))
    tn = _pick(n, (tiles[1],) if tiles else (512, 896, 384, 256, 128))
    tk = _pick(kdim, (tiles[2],) if tiles and len(tiles) > 2 else (2048, 1024, 512))
    nk = kdim // tk
    if layer is None:
        w_spec = pl.BlockSpec((tk, tn), lambda i, j, k: (k, j))
    else:
        w_spec = pl.BlockSpec((None, tk, tn), lambda i, j, k: (layer, k, j))
    in_specs = [pl.BlockSpec((tm, tk), lambda i, j, k: (i, k)), w_spec]
    args = [x, w]
    if res is not None:
        in_specs.append(pl.BlockSpec((tm, tn), lambda i, j, k: (i, j)))
        args.append(res)
    return pl.pallas_call(
        functools.partial(_mm_kernel, nk=nk, act=act, has_res=res is not None),
        grid=(m // tm, n // tn, nk),
        in_specs=in_specs,
        out_specs=pl.BlockSpec((tm, tn), lambda i, j, k: (i, j)),
        out_shape=jax.ShapeDtypeStruct((m, n), out_dtype),
        scratch_shapes=[pltpu.VMEM((tm, tn), jnp.float32)] if nk > 1 else [],
        compiler_params=pltpu.CompilerParams(
            dimension_semantics=("parallel", "parallel", "arbitrary"),
            vmem_limit_bytes=VMEM_LIMIT_BYTES),
    )(*args)


EV_CONV = 0
EV_Z = GDN_CONV_CH
EV_NQ = EV_Z + GDN_V
EV_KV = EV_NQ + NSA_H * NSA_D
EV_SMALL = EV_KV + 6 * NSA_KV
EV_N = EV_SMALL + 128
EV_ORIG_SMALL = 2 * GDN_QK + 2 * GDN_V
EV_ORIG_N = EV_ORIG_SMALL + 2 * GDN_VH + NSA_H * NSA_D + 6 * NSA_KV + 3 * NSA_H


def _permute_even_w(w_in):
    ab = w_in[:, EV_ORIG_SMALL:EV_ORIG_SMALL + 2 * GDN_VH]
    big = w_in[:, EV_ORIG_SMALL + 2 * GDN_VH:EV_ORIG_N - 3 * NSA_H]
    ng = w_in[:, EV_ORIG_N - 3 * NSA_H:]
    pad = jnp.zeros((w_in.shape[0], EV_N - EV_ORIG_N), w_in.dtype)
    return jnp.concatenate([w_in[:, :EV_ORIG_SMALL], big, ab, ng, pad], axis=1).astype(MXU_DTYPE)


def _bucket_thresholds():
    n = np.arange(0, 4 * REL_MAX_DIST)
    nf = np.maximum(n, REL_EXACT).astype(np.float32)
    large = REL_EXACT + (np.log(nf / np.float32(REL_EXACT)) / np.float32(math.log(REL_MAX_DIST / REL_EXACT))
                         * np.float32(REL_BUCKETS - REL_EXACT)).astype(np.int32)
    b = np.where(n < REL_EXACT, n, np.minimum(large, REL_BUCKETS - 1))
    return tuple(int(np.argmax(b >= k)) for k in range(REL_BUCKETS))


BUCKET_THR = _bucket_thresholds()
BIAS_TILES = 14
assert BUCKET_THR[-1] <= (BIAS_TILES - 1) * 128 - 127


def _bias_of_dist(dist, tab_ref, head):
    val = jnp.full(dist.shape, tab_ref[0, head], jnp.float32)
    for k in range(1, REL_BUCKETS):
        val = jnp.where(dist >= BUCKET_THR[k], tab_ref[k, head], val)
    return val


def _bias_tiles_kernel(tab_ref, o_ref):
    h = pl.program_id(0)
    d = pl.program_id(1)
    i = lax.broadcasted_iota(jnp.int32, (128, 128), 1)
    j = lax.broadcasted_iota(jnp.int32, (128, 128), 0)
    dist = d * 128 + i - j
    val = jnp.full(dist.shape, tab_ref[0, h], jnp.float32)
    for k in range(1, REL_BUCKETS):
        val = jnp.where(dist >= BUCKET_THR[k], tab_ref[k, h], val)
    o_ref[0, 0] = val


def _bias_tiles(tab):
    return pl.pallas_call(
        _bias_tiles_kernel,
        grid=(NSA_H, BIAS_TILES),
        in_specs=[pl.BlockSpec(memory_space=pltpu.SMEM)],
        out_specs=pl.BlockSpec((1, 1, 128, 128), lambda h, d: (h, d, 0, 0)),
        out_shape=jax.ShapeDtypeStruct((NSA_H, BIAS_TILES, 128, 128), jnp.float32),
    )(tab.astype(jnp.float32))


def _group_rms(x, gain_row, scale=1.0):
    ms = jnp.mean(x * x, axis=-1, keepdims=True)
    y = x * lax.rsqrt(ms + RMS_EPS) * gain_row
    return y * scale if scale != 1.0 else y


def _nsa_prep_kernel(nq_ref, kc_ref, vc_ref, ks_ref, vs_ref, kw_ref, vw_ref, gain_ref, wck_ref, wcv_ref,
                     qn_ref, ksn_ref, kwn_ref, ksb_ref, vst_ref, kwb_ref, vwt_ref, kcb_ref, vcb_ref):
    tm = nq_ref.shape[0]
    g0 = gain_ref[0:1, :]
    g1 = gain_ref[1:2, :]
    g2 = gain_ref[2:3, :]
    g3 = gain_ref[3:4, :]
    for h in range(NSA_H):
        sl = slice(h * NSA_D, (h + 1) * NSA_D)
        qn_ref[:, sl] = _group_rms(nq_ref[:, sl], g0, NSA_D ** -0.5).astype(qn_ref.dtype)
    for g in range(NSA_G):
        sl = slice(g * NSA_D, (g + 1) * NSA_D)
        ksn = _group_rms(ks_ref[:, sl], g2)
        kwn = _group_rms(kw_ref[:, sl], g3)
        ksn_ref[:, sl] = ksn
        kwn_ref[:, sl] = kwn
        ksb_ref[:, sl] = ksn.astype(ksb_ref.dtype)
        kwb_ref[:, sl] = kwn.astype(kwb_ref.dtype)
    for src, dst in ((vs_ref, vst_ref), (vw_ref, vwt_ref)):
        vt = src[...].T
        tk = dst.shape[2]
        for c in range(tm // tk):
            dst[c] = vt[:, c * tk:(c + 1) * tk].astype(dst.dtype)
    nblk = tm // NSA_BLK
    kc3 = kc_ref[...].reshape(nblk, NSA_BLK, NSA_KV)
    vc3 = vc_ref[...].reshape(nblk, NSA_BLK, NSA_KV)
    kcb = jnp.sum(kc3 * wck_ref[...][None], axis=1)
    vcb = jnp.sum(vc3 * wcv_ref[...][None], axis=1)
    for g in range(NSA_G):
        sl = slice(g * NSA_D, (g + 1) * NSA_D)
        kcb_ref[:, sl] = _group_rms(kcb[:, sl], g1)
    vcb_ref[...] = vcb


def _nsa_prep(proj, gain, w_ck, w_cv):
    m = proj.shape[0]
    tm = 512
    kvb = EV_KV // NSA_KV
    f32, bf16 = jnp.float32, MXU_DTYPE

    def kv_spec(i):
        return pl.BlockSpec((tm, NSA_KV), lambda r, i=i: (r, kvb + i))

    row = lambda w: pl.BlockSpec((tm, w), lambda r: (r, 0))
    slabs = lambda tk: pl.BlockSpec((tm // tk, NSA_KV, tk), lambda r: (r, 0, 0))
    full = lambda a: pl.BlockSpec(a.shape, lambda r: (0,) * a.ndim)
    wck = w_ck.reshape(NSA_BLK, NSA_KV)
    wcv = w_cv.reshape(NSA_BLK, NSA_KV)
    outs = pl.pallas_call(
        _nsa_prep_kernel,
        grid=(m // tm,),
        in_specs=[pl.BlockSpec((tm, NSA_H * NSA_D), lambda r: (r, EV_NQ // (NSA_H * NSA_D)))]
                 + [kv_spec(i) for i in range(6)] + [full(gain), full(wck), full(wcv)],
        out_specs=[row(NSA_H * NSA_D), row(NSA_KV), row(NSA_KV), row(NSA_KV), slabs(NSA_TK), row(NSA_KV),
                   slabs(NSA_QB)] + [pl.BlockSpec((tm // NSA_BLK, NSA_KV), lambda r: (r, 0))] * 2,
        out_shape=[jax.ShapeDtypeStruct((m, NSA_H * NSA_D), bf16),
                   jax.ShapeDtypeStruct((m, NSA_KV), f32), jax.ShapeDtypeStruct((m, NSA_KV), f32),
                   jax.ShapeDtypeStruct((m, NSA_KV), bf16), jax.ShapeDtypeStruct((m // NSA_TK, NSA_KV, NSA_TK), bf16),
                   jax.ShapeDtypeStruct((m, NSA_KV), bf16), jax.ShapeDtypeStruct((m // NSA_QB, NSA_KV, NSA_QB), bf16)]
                  + [jax.ShapeDtypeStruct((m // NSA_BLK, NSA_KV), f32)] * 2,
        compiler_params=pltpu.CompilerParams(dimension_semantics=("parallel",),
                                             vmem_limit_bytes=VMEM_LIMIT_BYTES),
    )(proj, proj, proj, proj, proj, proj, proj, gain, wck, wcv)
    return outs


NSA_TK = 256


def _pad_rows(x, rows):
    if x.shape[0] == rows:
        return x
    return jnp.concatenate([x, jnp.zeros((rows - x.shape[0],) + x.shape[1:], x.dtype)], axis=0)


def _nsa_attn_kernel(tab_ref, q_ref, kcb_ref, vcb_ref, ks_ref, vst_ref, kw_ref, vwt_ref, bias_ref, ng_ref,
                     o_ref, m_scr, l_scr, acc_scr):
    f32, bf16 = jnp.float32, MXU_DTYPE
    qb = pl.program_id(1)
    nb = kcb_ref.shape[0]
    qn = NSA_QB
    gates = jax.nn.sigmoid(ng_ref[...])
    tsub = NSA_TK // qn
    shift = NSA_BLK.bit_length() - 1

    heads = range(NSA_H)
    gls = [slice(g * NSA_D, (g + 1) * NSA_D) for g in range(NSA_G)]
    qs = [q_ref[:, hh * NSA_D:(hh + 1) * NSA_D] for hh in heads]
    qts = [q.astype(f32).T.astype(bf16) for q in qs]

    blk = lax.broadcasted_iota(jnp.int32, (nb, qn), 0)
    blkf = blk.astype(f32)
    qpos = qb * qn + lax.broadcasted_iota(jnp.int32, (nb, qn), 1)
    dist_c = qpos - (blk * NSA_BLK + NSA_BLK - 1)
    ok_c = dist_c >= 0
    cur = lax.shift_right_logical(qpos, shift)
    o_c, scores = [], []
    for g in range(NSA_G):
        q2 = jnp.concatenate(qs[g * NSA_HG:(g + 1) * NSA_HG], axis=0)
        kcb = kcb_ref[:, gls[g]].astype(bf16)
        vcb = _pad_rows(vcb_ref[:, gls[g]], qn).astype(bf16)
        lgt = lax.dot_general(kcb, q2, (((1,), (1,)), ((), ())), preferred_element_type=f32)
        score = jnp.zeros((nb, qn), f32)
        for h in range(NSA_HG):
            lg = lgt[:, h * qn:(h + 1) * qn] + _bias_of_dist(dist_c, tab_ref, g * NSA_HG + h)
            lg = jnp.where(ok_c, lg, NEG)
            mx = jnp.max(lg, axis=0, keepdims=True)
            e = jnp.exp(lg - mx)
            p = jnp.where(ok_c, e / jnp.sum(e, axis=0, keepdims=True), 0.0)
            score = score + p
            p_t = _pad_rows(p, qn).T.astype(bf16)
            o_c.append(jnp.dot(p_t, vcb, preferred_element_type=f32))
        scores.append(jnp.where(blk < cur, score, -1.0))

    sels = [jnp.zeros((nb, qn), f32) for _ in range(NSA_G)]
    for _ in range(min(NSA_TOPK, nb)):
        for g in range(NSA_G):
            mx = jnp.max(scores[g], axis=0, keepdims=True)
            first = jnp.min(jnp.where(scores[g] == mx, blkf, float(nb)), axis=0, keepdims=True)
            pick = blkf == first
            sels[g] = jnp.where(pick, jnp.where(mx >= 0.0, 1.0, 0.0), sels[g])
            scores[g] = jnp.where(pick, -2.0, scores[g])
    sel_ps = [_pad_rows(jnp.where(blk == cur, 1.0, s), qn).astype(bf16) for s in sels]

    def reset():
        m_scr[...] = jnp.full(m_scr.shape, NEG, f32)
        l_scr[...] = jnp.zeros(l_scr.shape, f32)
        acc_scr[...] = jnp.zeros(acc_scr.shape, f32)

    def update(hh, lg, vt):
        m_old = m_scr[hh]
        m_new = jnp.maximum(m_old, jnp.max(lg, axis=0, keepdims=True))
        p = jnp.exp(lg - m_new)
        alpha = jnp.exp(m_old - m_new)
        l_scr[hh] = alpha * l_scr[hh] + jnp.sum(p, axis=0, keepdims=True)
        acc_scr[hh] = alpha * acc_scr[hh] + jnp.dot(vt, p.astype(bf16), preferred_element_type=f32)
        m_scr[hh] = m_new

    def result(hh):
        return (acc_scr[hh] / l_scr[hh]).T

    reset()
    n_tiles = qb // tsub + 1
    tok = qb * qn + lax.broadcasted_iota(jnp.int32, (NSA_TK, qn), 1)
    key = lax.broadcasted_iota(jnp.int32, (NSA_TK, qn), 0)
    eblk = lax.broadcasted_iota(jnp.int32, (NSA_TK, qn), 1)

    def body(i, carry):
        kt = n_tiles - 1 - i
        k0 = pl.multiple_of(kt * NSA_TK, NSA_TK)
        expand = jnp.where(eblk == lax.shift_right_logical(k0 + key, shift), 1.0, 0.0).astype(bf16)
        causal = k0 + key <= tok
        for g in range(NSA_G):
            k = ks_ref[pl.ds(k0, NSA_TK), gls[g]]
            vt = vst_ref[kt, gls[g], :]
            member = jnp.dot(expand, sel_ps[g], preferred_element_type=f32)
            keep = jnp.where(causal, member, 0.0) > 0.5
            for h in range(NSA_HG):
                hh = g * NSA_HG + h
                tiles = [bias_ref[hh, jnp.clip(qb - (kt * tsub + j), 0, BIAS_TILES - 1)] for j in range(tsub)]
                lg = jnp.dot(k, qts[hh], preferred_element_type=f32) + jnp.concatenate(tiles, axis=0)
                update(hh, jnp.where(keep, lg, NEG), vt)
        return carry

    lax.fori_loop(0, n_tiles, body, 0)
    o_s = [result(hh) for hh in heads]

    reset()
    nwin = NSA_WIN // qn + 1
    wi = lax.broadcasted_iota(jnp.int32, (qn, qn), 1)
    wj = lax.broadcasted_iota(jnp.int32, (qn, qn), 0)
    for d in range(nwin):
        sub = qb - d
        subc = jnp.maximum(sub, 0)
        for g in range(NSA_G):
            k = kw_ref[pl.ds(pl.multiple_of(subc * qn, qn), qn), gls[g]]
            vt = vwt_ref[subc, gls[g], :]
            for h in range(NSA_HG):
                hh = g * NSA_HG + h
                x = jnp.dot(k, qts[hh], preferred_element_type=f32) + bias_ref[hh, d]
                if d == 0:
                    x = jnp.where(wj <= wi, x, NEG)
                else:
                    if d == nwin - 1:
                        x = jnp.where(wj > wi, x, NEG)
                    x = jnp.where(sub >= 0, x, NEG)
                update(hh, x, vt)
    o_w = [result(hh) for hh in heads]

    for hh in heads:
        c = 2 * GDN_VH + hh * 3
        y = o_c[hh] * gates[:, c:c + 1] + o_s[hh] * gates[:, c + 1:c + 2] + o_w[hh] * gates[:, c + 2:c + 3]
        o_ref[:, hh * NSA_D:(hh + 1) * NSA_D] = y.astype(o_ref.dtype)


def _nsa_prompt_attn(tab, bias_tiles, proj, qn, kcb, vcb, ksb, vst, kwb, vwt, batch, t_len):
    nq = t_len // NSA_QB
    nb = t_len // NSA_BLK
    hd = NSA_H * NSA_D
    per_b = lambda w: pl.BlockSpec((t_len, w), lambda b, i: (b, 0))
    tiles_b = lambda tk: pl.BlockSpec((t_len // tk, NSA_KV, tk), lambda b, i: (b, 0, 0))
    return pl.pallas_call(
        _nsa_attn_kernel,
        grid=(batch, nq),
        in_specs=[pl.BlockSpec(memory_space=pltpu.SMEM),
                  pl.BlockSpec((NSA_QB, hd), lambda b, i: (b * nq + i, 0)),
                  pl.BlockSpec((nb, NSA_KV), lambda b, i: (b, 0)),
                  pl.BlockSpec((nb, NSA_KV), lambda b, i: (b, 0)),
                  per_b(NSA_KV), tiles_b(NSA_TK), per_b(NSA_KV), tiles_b(NSA_QB),
                  pl.BlockSpec(bias_tiles.shape, lambda b, i: (0, 0, 0, 0)),
                  pl.BlockSpec((NSA_QB, 128), lambda b, i: (b * nq + i, EV_SMALL // 128 + 0))],
        out_specs=pl.BlockSpec((NSA_QB, hd), lambda b, i: (b * nq + i, 0)),
        out_shape=jax.ShapeDtypeStruct((batch * t_len, hd), MXU_DTYPE),
        scratch_shapes=[pltpu.VMEM((NSA_H, 1, NSA_QB), jnp.float32),
                        pltpu.VMEM((NSA_H, 1, NSA_QB), jnp.float32),
                        pltpu.VMEM((NSA_H, NSA_D, NSA_QB), jnp.float32)],
        compiler_params=pltpu.CompilerParams(dimension_semantics=("parallel", "arbitrary"),
                                             vmem_limit_bytes=VMEM_LIMIT_BYTES),
    )(tab.astype(jnp.float32), qn, kcb, vcb, ksb, vst, kwb, vwt, bias_tiles, proj)


def _mdot(a, b):
    return jnp.dot(a.astype(MXU_DTYPE), b.astype(MXU_DTYPE), preferred_element_type=jnp.float32)


def _mdot_nt(a, b):
    return lax.dot_general(a.astype(MXU_DTYPE), b.astype(MXU_DTYPE), (((1,), (1,)), ((), ())),
                           preferred_element_type=jnp.float32)


def _mdot_tn(a, b):
    return lax.dot_general(a.astype(MXU_DTYPE), b.astype(MXU_DTYPE), (((0,), (0,)), ((), ())),
                           preferred_element_type=jnp.float32)


def _shift_rows(x, prev, s):
    xs = pltpu.roll(x, s, axis=0)
    ps = pltpu.roll(prev, s, axis=0)
    row8 = lax.broadcasted_iota(jnp.int32, prev.shape, 0)
    head = jnp.where(row8 < s, ps, xs[0:8])
    return jnp.concatenate([head, xs[8:]], axis=0)


def _gdn_prep_kernel(x_ref, prev_ref, first_ref, ab_ref, cw_ref, alog_ref, dtb_ref,
                     q_ref, k_ref, v_ref, gb_ref, *, tiles_per_batch, t_valid):
    r = pl.program_id(0)
    tm = x_ref.shape[0]
    tile = r % tiles_per_batch
    prev = jnp.where(tile == 0, first_ref[0], prev_ref[...])
    x = x_ref[...]
    acc = x * cw_ref[GDN_CONV - 1:GDN_CONV, :]
    for s in range(1, GDN_CONV):
        acc = acc + _shift_rows(x, prev, s) * cw_ref[GDN_CONV - 1 - s:GDN_CONV - s, :]
    c = acc * jax.nn.sigmoid(acc)

    def l2(t):
        return t * lax.rsqrt(jnp.sum(t * t, axis=-1, keepdims=True) + 1e-6)

    for h in range(GDN_KH):
        sl = slice(h * GDN_DK, (h + 1) * GDN_DK)
        q_ref[:, sl] = l2(c[:, sl]) * (GDN_DK ** -0.5)
        k_ref[:, sl] = l2(c[:, GDN_QK + h * GDN_DK:GDN_QK + (h + 1) * GDN_DK])
    v_ref[...] = c[:, 2 * GDN_QK:]
    ab = ab_ref[...]
    z = ab + dtb_ref[...]
    softplus = jnp.maximum(z, 0.0) + jnp.log(1.0 + jnp.exp(-jnp.abs(z)))
    gate = -jnp.exp(alog_ref[...]) * softplus
    lane = lax.broadcasted_iota(jnp.int32, ab.shape, 1)
    pos = tile * tm + lax.broadcasted_iota(jnp.int32, ab.shape, 0)
    gb = jnp.where(lane < GDN_VH, gate, jnp.where(lane < 2 * GDN_VH, jax.nn.sigmoid(ab), 0.0))
    gb_ref[...] = jnp.where(pos < t_valid, gb, 0.0)


def _gdn_prep(xin, first, ab, ab_col, conv_w, a_log, dt_bias, batch, t_pad, t_valid):
    m = batch * t_pad
    tm = _pick(t_pad, (512, 256, 128, 64))
    tpb = t_pad // tm
    f32 = jnp.float32
    lanes = lambda v: jnp.zeros((1, 128), f32).at[0, :GDN_VH].set(v.astype(f32))
    return pl.pallas_call(
        functools.partial(_gdn_prep_kernel, tiles_per_batch=tpb, t_valid=t_valid),
        grid=(m // tm,),
        in_specs=[pl.BlockSpec((tm, GDN_CONV_CH), lambda r: (r, 0)),
                  pl.BlockSpec((8, GDN_CONV_CH), lambda r: (jnp.maximum(r * (tm // 8) - 1, 0), 0)),
                  pl.BlockSpec((1, 8, GDN_CONV_CH), lambda r: (r // tpb, 0, 0)),
                  pl.BlockSpec((tm, 128), lambda r: (r, ab_col)),
                  pl.BlockSpec((GDN_CONV, GDN_CONV_CH), lambda r: (0, 0)),
                  pl.BlockSpec((1, 128), lambda r: (0, 0)),
                  pl.BlockSpec((1, 128), lambda r: (0, 0))],
        out_specs=[pl.BlockSpec((tm, GDN_QK), lambda r: (r, 0)),
                   pl.BlockSpec((tm, GDN_QK), lambda r: (r, 0)),
                   pl.BlockSpec((tm, GDN_V), lambda r: (r, 0)),
                   pl.BlockSpec((tm, 128), lambda r: (r, 0))],
        out_shape=[jax.ShapeDtypeStruct((m, GDN_QK), f32), jax.ShapeDtypeStruct((m, GDN_QK), f32),
                   jax.ShapeDtypeStruct((m, GDN_V), f32), jax.ShapeDtypeStruct((m, 128), f32)],
        compiler_params=pltpu.CompilerParams(dimension_semantics=("arbitrary",),
                                             vmem_limit_bytes=VMEM_LIMIT_BYTES),
    )(xin, xin, first, ab, conv_w.astype(f32), lanes(a_log), lanes(dt_bias))


def _cumsum_rows(x):
    row = lax.broadcasted_iota(jnp.int32, x.shape, 0)
    s = 1
    while s < x.shape[0]:
        x = x + jnp.where(row >= s, pltpu.roll(x, s, axis=0), 0.0)
        s *= 2
    return x


def _split3(a):
    hi = a.astype(jnp.bfloat16)
    lo = (a - hi.astype(jnp.float32)).astype(jnp.bfloat16)
    return hi, lo


def _dot3(a, b):
    if MXU_DTYPE != jnp.bfloat16:
        return jnp.dot(a, b, preferred_element_type=jnp.float32)
    ah, al = _split3(a)
    bh, bl = _split3(b)
    d = lambda x, y: jnp.dot(x, y, preferred_element_type=jnp.float32)
    return d(ah, bh) + (d(ah, bl) + d(al, bh))


def _unit_lower_inverses(lmats, row, col):
    eye = jnp.where(row == col, 1.0, 0.0)
    blk = lax.shift_right_logical(row, 3) == lax.shift_right_logical(col, 3)
    lds = [jnp.where(blk, l, 0.0) for l in lmats]
    xs = [eye - ld for ld in lds]
    ps = [_dot3(ld, ld) for ld in lds]
    for step in range(2):
        xs = [x + _dot3(x, p) for x, p in zip(xs, ps)]
        if step < 1:
            ps = [_dot3(p, p) for p in ps]
    size = 8
    while size < lmats[0].shape[0]:
        sh = size.bit_length() - 1
        inner = lax.shift_right_logical(row, sh) == lax.shift_right_logical(col, sh)
        outer = lax.shift_right_logical(row, sh + 1) == lax.shift_right_logical(col, sh + 1)
        coffs = [jnp.where(outer, jnp.where(inner, 0.0, l), 0.0) for l in lmats]
        ts = [_dot3(x, c) for x, c in zip(xs, coffs)]
        xs = [x - _dot3(t, x) for x, t in zip(xs, ts)]
        size *= 2
    return xs


CHUNKS_PER_STEP = 2


def _gdn_chunk_kernel(q_ref, k_ref, v_ref, gb_ref, z_ref, s0_ref, nw_ref, y_ref, s_ref):
    cn = GDN_CHUNK

    @pl.when(pl.program_id(1) == 0)
    def _():
        s_ref[...] = s0_ref[...]

    row = lax.broadcasted_iota(jnp.int32, (cn, cn), 0)
    col = lax.broadcasted_iota(jnp.int32, (cn, cn), 1)
    rep = GDN_VH // GDN_KH
    heads = range(GDN_VH)

    def one_chunk(rows):
        gb = gb_ref[rows, :]
        gcum = _cumsum_rows(gb)
        gcum_t = _pad_rows(gcum, 128).T
        qs = [q_ref[rows, kh * GDN_DK:(kh + 1) * GDN_DK] for kh in range(GDN_KH)]
        ks = [k_ref[rows, kh * GDN_DK:(kh + 1) * GDN_DK] for kh in range(GDN_KH)]
        kk = [_mdot_nt(k, k) for k in ks]
        qk = [_mdot_nt(q, k) for q, k in zip(qs, ks)]
        gcols = [gcum[:, h:h + 1] for h in heads]
        diffs = [gcols[h] - gcum_t[h:h + 1, :cn] for h in heads]
        egs = [jnp.exp(g) for g in gcols]
        betas = [gb[:, GDN_VH + h:GDN_VH + h + 1] for h in heads]
        lmats = [betas[h] * kk[h // rep] * jnp.exp(jnp.where(col < row, diffs[h], -jnp.inf)) for h in heads]
        ainvs = _unit_lower_inverses(lmats, row, col)
        sols = [_dot3(ainvs[h], jnp.concatenate([betas[h] * v_ref[rows, h * GDN_DV:(h + 1) * GDN_DV],
                                                 (betas[h] * egs[h]) * ks[h // rep]], axis=1)) for h in heads]
        atts = [qk[h // rep] * jnp.exp(jnp.where(col <= row, diffs[h], -jnp.inf)) for h in heads]
        ss = [s_ref[0, h] for h in heads]
        us = [sols[h][:, :GDN_DV] - _mdot(sols[h][:, GDN_DV:], ss[h]) for h in heads]
        os_ = [egs[h] * _mdot(qs[h // rep], ss[h]) + _mdot(atts[h], us[h]) for h in heads]
        for h in heads:
            gl = gcum[cn - 1:cn, h:h + 1]
            s_ref[0, h] = jnp.exp(gl) * ss[h] + _mdot_tn(ks[h // rep] * jnp.exp(gl - gcols[h]), us[h])
        for h in heads:
            o = os_[h]
            on = o * lax.rsqrt(jnp.mean(o * o, axis=-1, keepdims=True) + RMS_EPS) * nw_ref[...]
            z = z_ref[rows, h * GDN_DV:(h + 1) * GDN_DV]
            y_ref[rows, h * GDN_DV:(h + 1) * GDN_DV] = (on * (z * jax.nn.sigmoid(z))).astype(y_ref.dtype)

    for sub in range(q_ref.shape[0] // cn):
        one_chunk(slice(sub * cn, (sub + 1) * cn))


def _gdn_chunks(q, k, v, gb, zsrc, z_col, s0, norm_w, batch, t_pad):
    cn = GDN_CHUNK * math.gcd(CHUNKS_PER_STEP, t_pad // GDN_CHUNK)
    nc = t_pad // cn
    rowblk = lambda w, c=0: pl.BlockSpec((cn, w), lambda b, i, c=c: (b * nc + i, c))
    st = pl.BlockSpec((1, GDN_VH, GDN_DK, GDN_DV), lambda b, i: (b, 0, 0, 0))
    return pl.pallas_call(
        _gdn_chunk_kernel,
        grid=(batch, nc),
        in_specs=[rowblk(GDN_QK), rowblk(GDN_QK), rowblk(GDN_V), rowblk(128), rowblk(GDN_V, z_col), st,
                  pl.BlockSpec((1, GDN_DV), lambda b, i: (0, 0))],
        out_specs=[rowblk(GDN_V), st],
        out_shape=[jax.ShapeDtypeStruct((batch * t_pad, GDN_V), MXU_DTYPE),
                   jax.ShapeDtypeStruct((batch, GDN_VH, GDN_DK, GDN_DV), jnp.float32)],
        compiler_params=pltpu.CompilerParams(dimension_semantics=("parallel", "arbitrary"),
                                             vmem_limit_bytes=VMEM_LIMIT_BYTES),
    )(q, k, v, gb, zsrc, s0.astype(jnp.float32), norm_w.reshape(1, GDN_DV).astype(jnp.float32))


OD_Q = 0
OD_K = ML_H * ML_DK
OD_V = 2 * ML_H * ML_DK
OD_O = OD_V + ML_H * ML_DV
OD_SMALL = OD_O + ML_H * ML_DV
OD_N = OD_SMALL + 128


def _cummax_rows(x):
    row = lax.broadcasted_iota(jnp.int32, x.shape, 0)
    s = 1
    while s < x.shape[0]:
        x = jnp.maximum(x, jnp.where(row >= s, pltpu.roll(x, s, axis=0), -jnp.inf))
        s *= 2
    return x


def _round_mxu(x):
    return x.astype(MXU_DTYPE).astype(jnp.float32)


def _mlstm_chunk_kernel(q_ref, k_ref, v_ref, og_ref, if_ref, bif_ref, nw_ref, c0_ref, n0_ref, m0_ref,
                        y_ref, c_ref, n_ref, m_ref, *, t_valid):
    cn = ML_CHUNK
    n_sub = q_ref.shape[0] // cn
    ci = pl.program_id(1)

    @pl.when(ci == 0)
    def _():
        c_ref[...] = c0_ref[...]
        n_ref[...] = n0_ref[...]
        m_ref[...] = m0_ref[...]

    row = lax.broadcasted_iota(jnp.int32, (cn, cn), 0)
    col = lax.broadcasted_iota(jnp.int32, (cn, cn), 1)
    heads = range(ML_H)

    def one_chunk(sub):
        rows = slice(sub * cn, (sub + 1) * cn)
        pre = if_ref[rows, :] + bif_ref[...]
        pos = (ci * n_sub + sub) * cn + lax.broadcasted_iota(jnp.int32, pre.shape, 0)
        live = pos < t_valid
        lf = jnp.where(live, jnp.minimum(pre, 0.0) - jnp.log(1.0 + jnp.exp(-jnp.abs(pre))), 0.0)
        fcum = _cumsum_rows(lf)
        a_all = jnp.where(live, pre, NEG) - pltpu.roll(fcum, 128 - ML_H, axis=1)
        amax = _cummax_rows(a_all)
        a_t = _pad_rows(a_all, 128).T
        m_all = m_ref[0]
        qs = [q_ref[rows, h * ML_DK:(h + 1) * ML_DK] * (ML_DK ** -0.5) for h in heads]
        ks = [k_ref[rows, h * ML_DK:(h + 1) * ML_DK] for h in heads]
        vs = [v_ref[rows, h * ML_DV:(h + 1) * ML_DV] for h in heads]
        m_prevs = [m_all[:, h:h + 1] for h in heads]
        fcs = [fcum[:, ML_H + h:ML_H + h + 1] for h in heads]
        mts = [fcs[h] + jnp.maximum(m_prevs[h], amax[:, h:h + 1]) for h in heads]
        qks = [_mdot_nt(qs[h], ks[h]) for h in heads]
        ss = [qks[h] * jnp.exp(jnp.where(col <= row, a_t[h:h + 1, :cn] + (fcs[h] - mts[h]), -jnp.inf))
              for h in heads]
        dec0s = [jnp.exp(fcs[h] + m_prevs[h] - mts[h]) for h in heads]
        cms = [c_ref[0, h] for h in heads]
        nvs = [n_ref[0, h:h + 1, :] for h in heads]
        nums = [dec0s[h] * _mdot(qs[h], cms[h]) + _mdot(ss[h], vs[h]) for h in heads]
        dens = [dec0s[h] * jnp.sum(_round_mxu(qs[h]) * _round_mxu(nvs[h]), axis=-1, keepdims=True)
                + jnp.sum(ss[h], axis=-1, keepdims=True) for h in heads]
        hcs = [nums[h] / jnp.maximum(jnp.abs(dens[h]), jnp.exp(-mts[h])) for h in heads]
        m_new = [mts[h][cn - 1:cn] for h in heads]
        for h in heads:
            f_end = fcs[h][cn - 1:cn]
            w = jnp.exp(a_all[:, h:h + 1] + (f_end - m_new[h]))
            dc = jnp.exp(f_end + m_prevs[h] - m_new[h])
            c_ref[0, h] = dc * cms[h] + _mdot_tn(w * ks[h], vs[h])
            n_ref[0, h:h + 1, :] = dc * nvs[h] + jnp.sum(_round_mxu(w) * _round_mxu(ks[h]), axis=0, keepdims=True)
        for h in heads:
            hc = hcs[h]
            hn = hc * lax.rsqrt(jnp.mean(hc * hc, axis=-1, keepdims=True) + RMS_EPS) \
                * nw_ref[:, h * ML_DV:(h + 1) * ML_DV]
            og = og_ref[rows, h * ML_DV:(h + 1) * ML_DV]
            y_ref[rows, h * ML_DV:(h + 1) * ML_DV] = (hn * jax.nn.sigmoid(og)).astype(y_ref.dtype)
        lane = lax.broadcasted_iota(jnp.int32, m_all.shape, 1)
        out = m_all
        for h in heads:
            out = jnp.where(lane == h, m_new[h], out)
        m_ref[0] = out

    for sub in range(n_sub):
        one_chunk(sub)


def _mlstm_chunks(proj, b_if, mnorm, c0, n0, m0, batch, t_pad, t_valid):
    cn = ML_CHUNK * math.gcd(CHUNKS_PER_STEP, t_pad // ML_CHUNK)
    nc = t_pad // cn
    f32 = jnp.float32
    blk = lambda w, c: pl.BlockSpec((cn, w), lambda b, i, c=c: (b * nc + i, c))
    const = lambda shape: pl.BlockSpec(shape, lambda b, i: (0,) * len(shape))
    st_c = pl.BlockSpec((1, ML_H, ML_DK, ML_DV), lambda b, i: (b, 0, 0, 0))
    st_n = pl.BlockSpec((1, ML_H, ML_DK), lambda b, i: (b, 0, 0))
    st_m = pl.BlockSpec((1, 1, 128), lambda b, i: (b, 0, 0))
    bif = jnp.zeros((1, 128), f32).at[0, :2 * ML_H].set(b_if.astype(f32))
    m0p = jnp.zeros((batch, 1, 128), f32).at[:, 0, :ML_H].set(m0.astype(f32))
    hq, hv = ML_H * ML_DK, ML_H * ML_DV
    y, c, n, m = pl.pallas_call(
        functools.partial(_mlstm_chunk_kernel, t_valid=t_valid),
        grid=(batch, nc),
        in_specs=[blk(hq, OD_Q // hq), blk(hq, OD_K // hq), blk(hv, OD_V // hv), blk(hv, OD_O // hv),
                  blk(128, OD_SMALL // 128), const((1, 128)), const((1, hv)), st_c, st_n, st_m],
        out_specs=[blk(hv, 0), st_c, st_n, st_m],
        out_shape=[jax.ShapeDtypeStruct((batch * t_pad, hv), MXU_DTYPE),
                   jax.ShapeDtypeStruct((batch, ML_H, ML_DK, ML_DV), f32),
                   jax.ShapeDtypeStruct((batch, ML_H, ML_DK), f32),
                   jax.ShapeDtypeStruct((batch, 1, 128), f32)],
        compiler_params=pltpu.CompilerParams(dimension_semantics=("parallel", "arbitrary"),
                                             vmem_limit_bytes=VMEM_LIMIT_BYTES),
    )(proj, proj, proj, proj, proj, bif, mnorm.reshape(1, hv).astype(f32), c0.astype(f32), n0.astype(f32), m0p)
    return y, c, n, m[:, 0, :ML_H]


def _permute_odd_w(w_in):
    pad = jnp.zeros((w_in.shape[0], OD_N - w_in.shape[1]), w_in.dtype)
    return jnp.concatenate([w_in, pad], axis=1).astype(MXU_DTYPE)


DEC_PAGES_PER_STEP = 16


def _dec_compress_kernel(pt_ref, *refs, pp):
    kc_refs, vc_refs = refs[:pp], refs[pp:2 * pp]
    wck_ref, wcv_ref, g1_ref, kcb_ref, vcb_ref = refs[2 * pp:]
    per_page = PAGE_SIZE // NSA_BLK

    def compress(page_refs, w_ref, g):
        w = w_ref[:, g * NSA_D:(g + 1) * NSA_D]
        rows = [jnp.sum(r[pl.ds(g, PAGE_SIZE, stride=NSA_G), :].reshape(per_page, NSA_BLK, NSA_D) * w[None], axis=1)
                for r in page_refs]
        return jnp.concatenate(rows, axis=0)

    for g in range(NSA_G):
        sl = slice(g * NSA_D, (g + 1) * NSA_D)
        kcb_ref[0, :, sl] = _group_rms(compress(kc_refs, wck_ref, g), g1_ref[...])
        vcb_ref[0, :, sl] = compress(vc_refs, wcv_ref, g)


def _dec_compress(page_table, pool_kc, pool_vc, layer, w_ck, w_cv, g1):
    nseq, npages = page_table.shape
    pp = math.gcd(DEC_PAGES_PER_STEP, npages)
    per_page = PAGE_SIZE // NSA_BLK
    nphys = pool_kc.shape[1]
    base = layer * nphys
    kc = pool_kc.reshape(-1, NSA_D)
    vc = pool_vc.reshape(-1, NSA_D)
    page = lambda j: pl.BlockSpec((PAGE_SIZE * NSA_G, NSA_D), lambda b, i, pt, j=j: (base + pt[b, i * pp + j], 0))
    const = lambda shape: pl.BlockSpec(shape, lambda b, i, pt: (0,) * len(shape))
    out = pl.BlockSpec((1, pp * per_page, NSA_KV), lambda b, i, pt: (b, i, 0))
    nb = npages * per_page
    return pl.pallas_call(
        functools.partial(_dec_compress_kernel, pp=pp),
        grid_spec=pltpu.PrefetchScalarGridSpec(
            num_scalar_prefetch=1, grid=(nseq, npages // pp),
            in_specs=[page(j) for j in range(pp)] * 2 + [const((NSA_BLK, NSA_KV))] * 2 + [const((1, NSA_D))],
            out_specs=[out, out]),
        out_shape=[jax.ShapeDtypeStruct((nseq, nb, NSA_KV), jnp.float32)] * 2,
        compiler_params=pltpu.CompilerParams(dimension_semantics=("parallel", "arbitrary"),
                                             vmem_limit_bytes=VMEM_LIMIT_BYTES),
    )(page_table, *([kc] * pp), *([vc] * pp), w_ck.reshape(NSA_BLK, NSA_KV), w_cv.reshape(NSA_BLK, NSA_KV),
      g1.reshape(1, NSA_D))


def _rows_by_head(head_row, fn, nheads):
    out = fn(0)
    out = jnp.broadcast_to(out, (nheads, out.shape[1]))
    for h in range(1, nheads):
        out = jnp.where(head_row == h, fn(h), out)
    return out


def _dec_select_kernel(tab_ref, q_ref, kcb_ref, vcb_ref, oc_ref, idx_ref, *, qpos):
    f32 = jnp.float32
    nbp = kcb_ref.shape[1]
    q = q_ref[0]
    head_row = lax.broadcasted_iota(jnp.int32, (NSA_H, 1), 0)
    blk = lax.broadcasted_iota(jnp.int32, (1, nbp), 1)
    blkf = blk.astype(f32)
    dist = qpos - (blk * NSA_BLK + NSA_BLK - 1)
    ok = dist >= 0
    cur = qpos // NSA_BLK
    lg = jnp.zeros((NSA_H, nbp), f32)
    for g in range(NSA_G):
        lg_g = _mdot_nt(q, kcb_ref[0, :, g * NSA_D:(g + 1) * NSA_D])
        lg = jnp.where(head_row // NSA_HG == g, lg_g, lg)
    lg = lg + _rows_by_head(head_row, lambda h: _bias_of_dist(dist, tab_ref, h), NSA_H)
    lg = jnp.where(ok, lg, NEG)
    e = jnp.exp(lg - jnp.max(lg, axis=-1, keepdims=True))
    p = jnp.where(ok, e / jnp.sum(e, axis=-1, keepdims=True), 0.0)
    oc = jnp.zeros((NSA_H, NSA_D), f32)
    lane = lax.broadcasted_iota(jnp.int32, (1, 128), 1)
    for g in range(NSA_G):
        in_g = head_row // NSA_HG == g
        oc = jnp.where(in_g, _mdot(p, vcb_ref[0, :, g * NSA_D:(g + 1) * NSA_D]), oc)
        score = jnp.sum(jnp.where(in_g, p, 0.0), axis=0, keepdims=True)
        score = jnp.where(blk < cur, score, -1.0)
        idx = jnp.full((1, 128), -1.0, f32)
        for r in range(NSA_TOPK):
            mx = jnp.max(score, axis=-1, keepdims=True)
            first = jnp.min(jnp.where(score == mx, blkf, float(nbp)), axis=-1, keepdims=True)
            idx = jnp.where(lane == r, jnp.where(mx >= 0.0, first, -1.0), idx)
            score = jnp.where(blkf == first, -2.0, score)
        idx_ref[0, g:g + 1, :] = idx.astype(jnp.int32)
    oc_ref[0] = oc


def _dec_select(tab, qh, kcb, vcb, qpos):
    nseq = qh.shape[0]
    nbp = kcb.shape[1]
    seq = lambda shape: pl.BlockSpec((1,) + shape, lambda b: (b, 0, 0))
    return pl.pallas_call(
        functools.partial(_dec_select_kernel, qpos=qpos),
        grid=(nseq,),
        in_specs=[pl.BlockSpec(memory_space=pltpu.SMEM), seq((NSA_H, NSA_D)), seq((nbp, NSA_KV)),
                  seq((nbp, NSA_KV))],
        out_specs=[seq((NSA_H, NSA_D)), seq((NSA_G, 128))],
        out_shape=[jax.ShapeDtypeStruct((nseq, NSA_H, NSA_D), jnp.float32),
                   jax.ShapeDtypeStruct((nseq, NSA_G, 128), jnp.int32)],
        compiler_params=pltpu.CompilerParams(dimension_semantics=("parallel",),
                                             vmem_limit_bytes=VMEM_LIMIT_BYTES),
    )(tab.astype(jnp.float32), qh, kcb, vcb)


def _dec_attend_kernel(pt_ref, idx_ref, tab_ref, q_ref, *refs, qpos):
    f32 = jnp.float32
    ks_refs, vs_refs = refs[:NSA_TOPK], refs[NSA_TOPK:2 * NSA_TOPK]
    ksn_ref, vsn_ref, kwb_ref, vwb_ref, kwn_ref, vwn_ref, oc_ref, ng_ref, y_ref = refs[2 * NSA_TOPK:]
    b = pl.program_id(0)
    g = pl.program_id(1)
    cur = qpos // NSA_BLK
    head_row = lax.broadcasted_iota(jnp.int32, (NSA_HG, 1), 0)
    q4 = q_ref[0, pl.ds(g * NSA_HG, NSA_HG), :]
    q4r = _round_mxu(q4.astype(f32))
    new_row = b * NSA_G + g
    tab0 = _rows_by_head(head_row, lambda h: jnp.full((1, 1), tab_ref[0, g * NSA_HG + h], f32), NSA_HG)

    def attend(keys, vals, dist, ok, k_new, v_new):
        lg = _mdot_nt(q4, keys)
        lg = lg + _rows_by_head(head_row, lambda h: _bias_of_dist(dist, tab_ref, g * NSA_HG + h), NSA_HG)
        lg = jnp.where(ok, lg, NEG)
        lg_new = jnp.sum(q4r * _round_mxu(k_new), axis=-1, keepdims=True) + tab0
        m = jnp.maximum(jnp.max(lg, axis=-1, keepdims=True), lg_new)
        e = jnp.exp(lg - m)
        e_new = jnp.exp(lg_new - m)
        den = jnp.sum(e, axis=-1, keepdims=True) + e_new
        return _mdot(e / den, vals) + _round_mxu(e_new / den) * _round_mxu(v_new)

    pieces, oks = [], []
    jrow = lax.broadcasted_iota(jnp.int32, (1, NSA_BLK), 1)
    for s in range(NSA_TOPK):
        blk_id = idx_ref[b, g, s]
        valid = jnp.logical_and(blk_id >= 0, blk_id < cur)
        pos = jnp.maximum(blk_id, 0) * NSA_BLK + jrow
        pieces.append(qpos - pos)
        oks.append(jnp.where(valid, 1.0, 0.0) * jnp.where(pos <= qpos, 1.0, 0.0))
    dist_s = jnp.concatenate(pieces, axis=1)
    ok_s = jnp.concatenate(oks, axis=1) > 0.5
    mine = lambda r: r[pl.ds(g, NSA_BLK, stride=NSA_G), :]
    k_sel = jnp.concatenate([mine(r) for r in ks_refs], axis=0)
    v_sel = jnp.concatenate([mine(r) for r in vs_refs], axis=0)
    o_s = attend(k_sel, v_sel, dist_s, ok_s, ksn_ref[pl.ds(new_row, 1), :], vsn_ref[pl.ds(new_row, 1), :])

    wb = kwb_ref.shape[1]
    dist_w = wb - lax.broadcasted_iota(jnp.int32, (1, wb), 1)
    ok_w = dist_w < NSA_WIN
    o_w = attend(kwb_ref[0], vwb_ref[0], dist_w, ok_w, kwn_ref[pl.ds(new_row, 1), :], vwn_ref[pl.ds(new_row, 1), :])

    gates = jax.nn.sigmoid(ng_ref[pl.ds(b, 1), :])
    lane = lax.broadcasted_iota(jnp.int32, (1, 128), 1)

    def gate(branch):
        def one(h):
            c = 2 * GDN_VH + (g * NSA_HG + h) * 3 + branch
            return jnp.sum(jnp.where(lane == c, gates, 0.0), axis=-1, keepdims=True)
        return _rows_by_head(head_row, one, NSA_HG)

    oc4 = oc_ref[0, pl.ds(g * NSA_HG, NSA_HG), :]
    y_ref[0, pl.ds(g * NSA_HG, NSA_HG), :] = oc4 * gate(0) + o_s * gate(1) + o_w * gate(2)


def _dec_attend(tab, page_table, idx, qh, pool_ks, pool_vs, layer, ksn, vsn, buf_kw, buf_vw, kwn, vwn, oc, ng,
                qpos):
    nseq = qh.shape[0]
    base = layer * pool_ks.shape[1]
    wb = buf_kw.shape[1]
    ks = pool_ks.reshape(-1, NSA_D)
    vs = pool_vs.reshape(-1, NSA_D)
    per_page = PAGE_SIZE // NSA_BLK

    def sel(s):
        def index(b, g, pt, ix, s=s):
            blk_id = jnp.maximum(ix[b, g, s], 0)
            return ((base + pt[b, blk_id // per_page]) * per_page + blk_id % per_page, 0)
        return pl.BlockSpec((NSA_BLK * NSA_G, NSA_D), index)

    whole = lambda a: pl.BlockSpec(a.shape, lambda b, g, pt, ix: (0,) * a.ndim)
    heads = pl.BlockSpec((1, NSA_H, NSA_D), lambda b, g, pt, ix: (b, 0, 0))
    win = pl.BlockSpec((1, wb, NSA_D), lambda b, g, pt, ix: (b, 0, g))
    rows = lambda a: a.reshape(nseq * NSA_G, NSA_D)
    new = [rows(ksn), rows(vsn), rows(kwn), rows(vwn)]
    return pl.pallas_call(
        functools.partial(_dec_attend_kernel, qpos=qpos),
        grid_spec=pltpu.PrefetchScalarGridSpec(
            num_scalar_prefetch=2, grid=(nseq, NSA_G),
            in_specs=[pl.BlockSpec(memory_space=pltpu.SMEM), heads] + [sel(s) for s in range(NSA_TOPK)] * 2
                     + [whole(new[0]), whole(new[1]), win, win, whole(new[2]), whole(new[3]), heads, whole(ng)],
            out_specs=heads),
        out_shape=jax.ShapeDtypeStruct((nseq, NSA_H, NSA_D), jnp.float32),
        compiler_params=pltpu.CompilerParams(dimension_semantics=("parallel", "arbitrary"),
                                             vmem_limit_bytes=VMEM_LIMIT_BYTES),
    )(page_table, idx, tab.astype(jnp.float32), qh, *([ks] * NSA_TOPK), *([vs] * NSA_TOPK), new[0], new[1],
      buf_kw.reshape(nseq, wb, NSA_KV), buf_vw.reshape(nseq, wb, NSA_KV), new[2], new[3], oc, ng)


def _nsa_decode(proj, tab, qk_gain, w_ck, w_cv, caches, nseq):
    assert PAST_LEN % PAGE_SIZE == 0 and DEC_SEQ == 1
    pad = jnp.pad(proj, ((0, 512 - nseq), (0, 0)))
    qn, ksn, kwn, _, _, _, _, _, _ = _nsa_prep(pad, qk_gain, w_ck, w_cv)
    qh = qn[:nseq].reshape(nseq, NSA_H, NSA_D).astype(jnp.float32)
    kv = lambda i: proj[:, EV_KV + i * NSA_KV:EV_KV + (i + 1) * NSA_KV]
    layer = caches['layer']
    kcb, vcb = _dec_compress(caches['page_table'], caches['kc'], caches['vc'], layer, w_ck, w_cv, qk_gain[1])
    oc, idx = _dec_select(tab, qh, kcb, vcb, PAST_LEN)
    ng = proj[:, EV_SMALL:EV_SMALL + 128]
    y = _dec_attend(tab, caches['page_table'], idx, qh, caches['ks'], caches['vs'], layer, ksn[:nseq], kv(3),
                    caches['kw'], caches['vw'], kwn[:nseq], kv(5), oc, ng, PAST_LEN)
    return y.reshape(nseq, NSA_H * NSA_D), ksn[:nseq], kwn[:nseq]


def _rms_cast_kernel(x_ref, g_ref, o_ref):
    x = x_ref[...]
    y = x * lax.rsqrt(jnp.mean(x * x, axis=-1, keepdims=True) + RMS_EPS) * g_ref[...]
    o_ref[...] = y.astype(o_ref.dtype)


def _rms_cast(h, g):
    m, d = h.shape
    tm = _pick(m, (1024, 256, 128))
    return pl.pallas_call(
        _rms_cast_kernel,
        grid=(m // tm,),
        in_specs=[pl.BlockSpec((tm, d), lambda i: (i, 0)), pl.BlockSpec((1, d), lambda i: (0, 0))],
        out_specs=pl.BlockSpec((tm, d), lambda i: (i, 0)),
        out_shape=jax.ShapeDtypeStruct((m, d), MXU_DTYPE),
        compiler_params=pltpu.CompilerParams(dimension_semantics=("parallel",),
                                             vmem_limit_bytes=VMEM_LIMIT_BYTES),
    )(h, g.reshape(1, d).astype(jnp.float32))


def _mm2_kernel(xa_ref, xb_ref, w_ref, r_ref, o_ref):
    ka = xa_ref.shape[1]
    y = jnp.dot(xa_ref[...], w_ref[:ka, :].astype(MXU_DTYPE), preferred_element_type=jnp.float32)
    y = y + jnp.dot(xb_ref[...], w_ref[ka:, :].astype(MXU_DTYPE), preferred_element_type=jnp.float32)
    o_ref[...] = y + r_ref[...]


def _matmul2_res(xa, xb, w, res, layer, tiles=None):
    m, ka = xa.shape
    kb = xb.shape[1]
    n = w.shape[-1]
    tm = _pick(m, (tiles[0],) if tiles else (512, 256, 128))
    tn = _pick(n, (tiles[1],) if tiles else (512, 256, 128))
    return pl.pallas_call(
        _mm2_kernel,
        grid=(m // tm, n // tn),
        in_specs=[pl.BlockSpec((tm, ka), lambda i, j: (i, 0)), pl.BlockSpec((tm, kb), lambda i, j: (i, 0)),
                  pl.BlockSpec((None, ka + kb, tn), lambda i, j: (layer, 0, j)),
                  pl.BlockSpec((tm, tn), lambda i, j: (i, j))],
        out_specs=pl.BlockSpec((tm, tn), lambda i, j: (i, j)),
        out_shape=jax.ShapeDtypeStruct((m, n), jnp.float32),
        compiler_params=pltpu.CompilerParams(dimension_semantics=("parallel", "parallel"),
                                             vmem_limit_bytes=VMEM_LIMIT_BYTES),
    )(xa, xb, w, res)


LAYER_TILES = (
    dict(inp=(2048, 1152), out=(2048, 512), mlp=((2048, 512), (1024, 1024))),
    dict(inp=(2048, 896), out=(2048, 512), mlp=((2048, 512), (1024, 1024))),
    dict(inp=(2048, 1152), out=(2048, 512), mlp=((2048, 512), (1024, 1024))),
    dict(inp=(2048, 896), out=(2048, 512), mlp=((2048, 512), (1024, 1024))),
)


def _mlp_block(h, g, w1, w2, layer, tiles):
    mid = _matmul(_rms_cast(h, g), w1, act='relu2', out_dtype=MXU_DTYPE, tiles=tiles[0], layer=layer)
    return _matmul(mid, w2, res=h, tiles=tiles[1], layer=layer)


def _pad_time(x, t_pad):
    b, d = x.shape
    return jnp.zeros((b, t_pad, d), x.dtype).at[:, 0, :].set(x).reshape(b * t_pad, d)


def _even_layer(h, g_mix, w_in, w_out, conv_w, a_log, dt_bias, gnorm, qk_gain, w_ck, w_cv, tab, bias_tiles,
                batch, t_len, conv_st, s_st, decode, tiles):
    f32 = jnp.float32
    proj = _matmul(_rms_cast(h, g_mix), w_in, tiles=tiles['inp'])
    t_pad = t_len if decode is None else GDN_CHUNK
    pp = proj if decode is None else _pad_time(proj, t_pad)
    first = jnp.pad(conv_st.astype(f32), ((0, 0), (8 - (GDN_CONV - 1), 0), (0, 0)))
    q, k, v, gb = _gdn_prep(pp, first, pp, EV_SMALL // 128, conv_w, a_log, dt_bias, batch, t_pad, t_len)
    ya, s_new = _gdn_chunks(q, k, v, gb, pp, EV_Z // GDN_V, s_st, gnorm, batch, t_pad)
    p3 = proj.reshape(batch, t_len, EV_N)
    xin_tail = jnp.concatenate([conv_st.astype(f32), p3[..., :GDN_CONV_CH]], axis=1)[:, -(GDN_CONV - 1):]
    hd = lambda t: t.reshape(batch, t_len, NSA_G, NSA_D)
    kv = lambda i: hd(p3[..., EV_KV + i * NSA_KV:EV_KV + (i + 1) * NSA_KV])
    if decode is None:
        qn, ksn, kwn, ksb, vsb, kwb, vwb, kcb, vcb = _nsa_prep(proj, qk_gain, w_ck, w_cv)
        yb = _nsa_prompt_attn(tab, bias_tiles, proj, qn, kcb, vcb, ksb, vsb, kwb, vwb, batch, t_len)
        nw = min(NSA_WIN, t_len)
        kw_out, vw_out = hd(kwn)[:, -nw:], kv(5)[:, -nw:]
        ks_out = hd(ksn)
    else:
        ya = ya.reshape(batch, t_pad, GDN_V)[:, 0]
        yb, ksn, kwn = _nsa_decode(proj, tab, qk_gain, w_ck, w_cv, decode, batch)
        yb = yb.astype(MXU_DTYPE)
        ks_out = hd(ksn)
        kw_out = jnp.concatenate([decode['kw'][:, 1:], hd(kwn)], axis=1)
        vw_out = jnp.concatenate([decode['vw'][:, 1:], kv(5)], axis=1)
    h = _matmul2_res(ya, yb, w_out[0], h, w_out[1], tiles=tiles['out'])
    return h, (kv(0), kv(1), ks_out, kv(3), kw_out, vw_out, xin_tail, s_new)


def _odd_layer(h, g_mix, w_in, w_out, b_if, mnorm, batch, t_len, c0, n0, m0, decode, tiles):
    proj = _matmul(_rms_cast(h, g_mix), w_in, tiles=tiles['inp'])
    t_pad = t_len if not decode else ML_CHUNK
    pp = proj if not decode else _pad_time(proj, t_pad)
    y, c, n, m = _mlstm_chunks(pp, b_if, mnorm, c0, n0, m0, batch, t_pad, t_len)
    if decode:
        y = y.reshape(batch, t_pad, ODD_MIX)[:, 0]
    return _matmul(y, w_out[0], res=h, tiles=tiles['out'], layer=w_out[1]), (c, n, m)


def kernel(x_prompt, x_sample, cache_kc, cache_vc, cache_ks, cache_vs, cache_kw, cache_vw,
           state_gdn_conv, state_gdn_s, state_ml_c, state_ml_n, state_ml_m, page_table,
           rel_bias, norm_mix, norm_mlp, w_in_even, w_out_even, gdn_conv_w, gdn_a_log,
           gdn_dt_bias, gdn_norm_w, nsa_qk_gain, nsa_w_ck, nsa_w_cv, w_in_odd, w_out_odd,
           ml_b_if, ml_norm_w, w_ff1, w_ff2):
    f32 = jnp.float32
    bp, tp, d = x_prompt.shape
    bs, ts, _ = x_sample.shape
    hp = x_prompt.reshape(bp * tp, d)
    hs = x_sample.reshape(bs * ts, d)
    bias_tiles = _bias_tiles(rel_bias)
    ev_p, ev_s, od_p, od_s = [], [], [], []
    for l in range(DEPTH):
        j = l // 2
        tiles = LAYER_TILES[l]
        if l % 2 == 0:
            w_in = _permute_even_w(w_in_even[j])
            args = (w_in, (w_out_even, j), gdn_conv_w[j], gdn_a_log[j], gdn_dt_bias[j], gdn_norm_w[j],
                    nsa_qk_gain[j], nsa_w_ck[j], nsa_w_cv[j], rel_bias, bias_tiles)
            hp, stp = _even_layer(hp, norm_mix[l], *args, bp, tp,
                                  jnp.zeros((bp, GDN_CONV - 1, GDN_CONV_CH), f32),
                                  jnp.zeros((bp, GDN_VH, GDN_DK, GDN_DV), f32), None, tiles)
            caches = dict(kc=cache_kc, vc=cache_vc, ks=cache_ks, vs=cache_vs, layer=j, kw=cache_kw[j],
                          vw=cache_vw[j], page_table=page_table)
            hs, sts = _even_layer(hs, norm_mix[l], *args, bs, ts, state_gdn_conv[j], state_gdn_s[j], caches,
                                  tiles)
            ev_p.append(stp)
            ev_s.append(sts)
        else:
            w_in = _permute_odd_w(w_in_odd[j])
            hp, stp = _odd_layer(hp, norm_mix[l], w_in, (w_out_odd, j), ml_b_if[j], ml_norm_w[j], bp, tp,
                                 jnp.zeros((bp, ML_H, ML_DK, ML_DV), f32), jnp.zeros((bp, ML_H, ML_DK), f32),
                                 jnp.zeros((bp, ML_H), f32), False, tiles)
            hs, sts = _odd_layer(hs, norm_mix[l], w_in, (w_out_odd, j), ml_b_if[j], ml_norm_w[j], bs, ts,
                                 state_ml_c[j], state_ml_n[j], state_ml_m[j], True, tiles)
            od_p.append(stp)
            od_s.append(sts)
        hp = _mlp_block(hp, norm_mlp[l], w_ff1, w_ff2, l, tiles['mlp'])
        hs = _mlp_block(hs, norm_mlp[l], w_ff1, w_ff2, l, tiles['mlp'])

    kc_p, vc_p, ks_p, vs_p, kw_p, vw_p, conv_p, gdn_p = [jnp.stack(a) for a in zip(*ev_p)]
    kc_s, vc_s, ks_s, vs_s, kw_s, vw_s, conv_s, gdn_s = [jnp.stack(a) for a in zip(*ev_s)]
    mlc_p, mln_p, mlm_p = [jnp.stack(a) for a in zip(*od_p)]
    mlc_s, mln_s, mlm_s = [jnp.stack(a) for a in zip(*od_s)]
    return (hp.reshape(bp, tp, d), hs.reshape(bs, ts, d), kc_p, kc_s, vc_p, vc_s, ks_p, ks_s, vs_p, vs_s,
            kw_p, kw_s, vw_p, vw_s, conv_p, conv_s, gdn_p, gdn_s, mlc_p, mlc_s, mln_p, mln_s, mlm_p, mlm_s)
```
